```python
import jax, jax.numpy as jnp
from jax import lax
import numpy as np

D_MODEL = 2048
BATCH = 4
SEQ = 2048
DEPTH = 1

CHUNK = 128
SGU_GROUPS = 8
SGU_GROUP_DIM = D_MODEL // 16
SGU_WIDTH = SGU_GROUPS * SGU_GROUP_DIM
FOX_HEADS = 8
HEAD_DIM = 128
FOX_WIDTH = FOX_HEADS * HEAD_DIM
Q_BLOCK = 128
N_GROUPS = 4
EXPERTS_PER_GROUP = 8
N_EXPERTS = N_GROUPS * EXPERTS_PER_GROUP
TOP_K = 2
D_EXPERT = D_MODEL // 4
EXPERT_BLOCK = 128
EPS = 1e-6

OFF_U = 0
OFF_V = OFF_U + SGU_WIDTH
OFF_Q = OFF_V + SGU_WIDTH
OFF_K = OFF_Q + FOX_WIDTH
OFF_VA = OFF_K + FOX_WIDTH
OFF_F = OFF_VA + FOX_WIDTH
OFF_GATE = OFF_F + FOX_HEADS
IN_COLS = OFF_GATE + 2 * D_MODEL

kernel_name = 'hybrid_sgu_fox_hmoe_block'


def rms_norm(x, g):
    xf = x.astype(jnp.float32)
    y = xf * lax.rsqrt(jnp.mean(xf * xf, axis=-1, keepdims=True) + EPS)
    return (y * g.astype(jnp.float32)).astype(x.dtype)


def layer_norm(x, g, b):
    xf = x.astype(jnp.float32)
    mu = jnp.mean(xf, axis=-1, keepdims=True)
    var = jnp.mean(jnp.square(xf - mu), axis=-1, keepdims=True)
    y = (xf - mu) * lax.rsqrt(var + EPS)
    return (y * g.astype(jnp.float32) + b.astype(jnp.float32)).astype(x.dtype)


def chunked_sgu(u, v, ln_g, ln_b, w_s, b_s):
    B, S, _ = u.shape
    nc = S // CHUNK
    v = layer_norm(v, ln_g, ln_b)
    vg = v.reshape(B, nc, CHUNK, SGU_GROUPS, SGU_GROUP_DIM)
    causal = jnp.tril(jnp.ones((CHUNK, CHUNK), dtype=bool))
    w = jnp.where(causal[None], w_s, jnp.zeros_like(w_s))
    s = jnp.einsum('gts,bcsgd->bctgd', w, vg) + b_s.T[None, None, :, :, None]
    return u * s.reshape(B, S, SGU_WIDTH)


def forgetting_attention(q, k, v, f_logit, q_g, k_g):
    B, S, H, Dh = q.shape
    q = rms_norm(q, q_g)
    k = rms_norm(k, k_g)
    log_f = jax.nn.log_sigmoid(f_logit.astype(jnp.float32))
    c = jnp.cumsum(log_f, axis=1)
    c_k = c.transpose(0, 2, 1)
    nb = S // Q_BLOCK
    qb = q.reshape(B, nb, Q_BLOCK, H, Dh).transpose(1, 0, 2, 3, 4)
    cqb = c.reshape(B, nb, Q_BLOCK, H).transpose(1, 0, 3, 2)
    kpos = jnp.arange(S)
    scale = HEAD_DIM ** -0.5

    def block(args):
        qi, cqi, i = args
        s = jnp.einsum('bqhd,bkhd->bhqk', qi, k).astype(jnp.float32) * scale
        s = s + cqi[..., :, None] - c_k[:, :, None, :]
        qpos = i * Q_BLOCK + jnp.arange(Q_BLOCK)
        s = jnp.where(qpos[:, None] >= kpos[None, :], s, -jnp.inf)
        p = jax.nn.softmax(s, axis=-1)
        return jnp.einsum('bhqk,bkhd->bqhd', p.astype(v.dtype), v)

    o = lax.map(block, (qb, cqb, jnp.arange(nb)))
    return o.transpose(1, 0, 2, 3, 4).reshape(B, S, H * Dh)


def hierarchical_moe(h, w_rg, b_rg, w_re, b_re, w_g, w_u, w_d):
    B, S, D = h.shape
    T = B * S
    xt = h.reshape(T, D)
    g_logits = (xt @ w_rg + b_rg).astype(jnp.float32)
    g_prob = jax.nn.softmax(g_logits, axis=-1)
    grp = jnp.argmax(g_logits, axis=-1)
    p_grp = jnp.take_along_axis(g_prob, grp[:, None], axis=1)
    e_logits = (xt @ w_re + b_re).astype(jnp.float32).reshape(T, N_GROUPS, EXPERTS_PER_GROUP)
    e_logits = jnp.take_along_axis(e_logits, grp[:, None, None], axis=1)[:, 0]
    top_v, top_i = lax.top_k(e_logits, TOP_K)
    w_top = jax.nn.softmax(top_v, axis=-1) * p_grp
    eid = grp[:, None] * EXPERTS_PER_GROUP + top_i
    A = T * TOP_K
    eid_f = eid.reshape(A)
    tok_f = jnp.repeat(jnp.arange(T, dtype=jnp.int32), TOP_K)
    gate_f = w_top.reshape(A)
    order = jnp.argsort(eid_f)
    e_sorted = eid_f[order]
    counts = jnp.bincount(eid_f, length=N_EXPERTS)
    starts = jnp.cumsum(counts) - counts
    padded = ((counts + EXPERT_BLOCK - 1) // EXPERT_BLOCK) * EXPERT_BLOCK
    pad_ends = jnp.cumsum(padded)
    pad_starts = pad_ends - padded
    dest = pad_starts[e_sorted] + (jnp.arange(A) - starts[e_sorted])
    P = A + N_EXPERTS * EXPERT_BLOCK
    NB = P // EXPERT_BLOCK
    buf_tok = jnp.full((P,), T, dtype=jnp.int32).at[dest].set(tok_f[order])
    buf_gate = jnp.zeros((P,), jnp.float32).at[dest].set(gate_f[order])
    blk_e = jnp.minimum(jnp.searchsorted(pad_ends, jnp.arange(NB) * EXPERT_BLOCK, side='right'), N_EXPERTS - 1)
    x_pad = jnp.concatenate([xt, jnp.zeros((1, D), xt.dtype)], axis=0)
    xb = x_pad[buf_tok].reshape(NB, EXPERT_BLOCK, D)

    def expert_block(args):
        xi, e = args
        return (jax.nn.silu(xi @ w_g[e]) * (xi @ w_u[e])) @ w_d[e]

    yb = lax.map(expert_block, (xb, blk_e)).reshape(P, D)
    y = jnp.zeros((T + 1, D), jnp.float32).at[buf_tok].add(yb.astype(jnp.float32) * buf_gate[:, None])[:T]
    return y.astype(h.dtype).reshape(B, S, D)


def setup_inputs(seed: int = 0) -> dict:
    key = jax.random.key(seed)
    ks = jax.random.split(key, 24)
    f32 = jnp.float32
    nrm = lambda k, shape, s: jax.random.normal(k, shape, f32) * s
    L = DEPTH
    return {
        'x': jax.random.normal(ks[0], (BATCH, SEQ, D_MODEL), f32),
        'norm1_g': 1.0 + nrm(ks[1], (L, D_MODEL), 0.05),
        'w_in': nrm(ks[2], (L, D_MODEL, IN_COLS), D_MODEL ** -0.5),
        'b_gate': nrm(ks[3], (L, 2 * D_MODEL), 0.1),
        'b_forget': 2.0 + nrm(ks[4], (L, FOX_HEADS), 0.5),
        'sgu_ln_g': 1.0 + nrm(ks[5], (L, SGU_WIDTH), 0.05),
        'sgu_ln_b': nrm(ks[6], (L, SGU_WIDTH), 0.05),
        'w_spatial': nrm(ks[7], (L, SGU_GROUPS, CHUNK, CHUNK), CHUNK ** -0.5),
        'b_spatial': 1.0 + nrm(ks[8], (L, SGU_GROUPS, CHUNK), 0.1),
        'q_norm_g': 1.0 + nrm(ks[9], (L, HEAD_DIM), 0.05),
        'k_norm_g': 1.0 + nrm(ks[10], (L, HEAD_DIM), 0.05),
        'w_proj_sgu': nrm(ks[11], (L, SGU_WIDTH, D_MODEL), SGU_WIDTH ** -0.5),
        'w_proj_fox': nrm(ks[12], (L, FOX_WIDTH, D_MODEL), FOX_WIDTH ** -0.5),
        'w_out': nrm(ks[13], (L, D_MODEL, D_MODEL), D_MODEL ** -0.5),
        'norm2_g': 1.0 + nrm(ks[14], (L, D_MODEL), 0.05),
        'w_router_group': nrm(ks[15], (L, D_MODEL, N_GROUPS), D_MODEL ** -0.5),
        'b_router_group': nrm(ks[16], (L, N_GROUPS), 0.01),
        'w_router_expert': nrm(ks[17], (L, D_MODEL, N_EXPERTS), D_MODEL ** -0.5),
        'b_router_expert': nrm(ks[18], (L, N_EXPERTS), 0.01),
        'w_expert_gate': nrm(ks[19], (L, N_EXPERTS, D_MODEL, D_EXPERT), D_MODEL ** -0.5),
        'w_expert_up': nrm(ks[20], (L, N_EXPERTS, D_MODEL, D_EXPERT), D_MODEL ** -0.5),
        'w_expert_down': nrm(ks[21], (L, N_EXPERTS, D_EXPERT, D_MODEL), D_EXPERT ** -0.5),
    }


def reference(x, norm1_g, w_in, b_gate, b_forget, sgu_ln_g, sgu_ln_b, w_spatial, b_spatial,
              q_norm_g, k_norm_g, w_proj_sgu, w_proj_fox, w_out, norm2_g,
              w_router_group, b_router_group, w_router_expert, b_router_expert,
              w_expert_gate, w_expert_up, w_expert_down):
    B, S, D = x.shape
    for l in range(DEPTH):
        h = rms_norm(x, norm1_g[l])
        z = h @ w_in[l]
        u = jax.nn.gelu(z[..., OFF_U:OFF_V])
        v = jax.nn.gelu(z[..., OFF_V:OFF_Q])
        q = z[..., OFF_Q:OFF_K].reshape(B, S, FOX_HEADS, HEAD_DIM)
        k = z[..., OFF_K:OFF_VA].reshape(B, S, FOX_HEADS, HEAD_DIM)
        va = z[..., OFF_VA:OFF_F].reshape(B, S, FOX_HEADS, HEAD_DIM)
        f_logit = z[..., OFF_F:OFF_GATE] + b_forget[l]
        gates = jax.nn.sigmoid(z[..., OFF_GATE:] + b_gate[l])
        g_sgu = gates[..., :D]
        g_fox = gates[..., D:]
        y_sgu = chunked_sgu(u, v, sgu_ln_g[l], sgu_ln_b[l], w_spatial[l], b_spatial[l]) @ w_proj_sgu[l]
        y_fox = forgetting_attention(q, k, va, f_logit, q_norm_g[l], k_norm_g[l]) @ w_proj_fox[l]
        x = x + (g_sgu * y_sgu + g_fox * y_fox) @ w_out[l]
        h2 = rms_norm(x, norm2_g[l])
        x = x + hierarchical_moe(h2, w_router_group[l], b_router_group[l], w_router_expert[l],
                                 b_router_expert[l], w_expert_gate[l], w_expert_up[l], w_expert_down[l])
    return x
```

```python
import functools

import jax
import jax.numpy as jnp
from jax import lax
from jax.experimental import pallas as pl
from jax.experimental.pallas import tpu as pltpu

F32 = jnp.float32
BF16 = jnp.bfloat16
I32 = jnp.int32

D_MODEL = 2048
CHUNK = 128
SGU_GROUPS = 8
SGU_WIDTH = 1024
FOX_HEADS = 8
HEAD_DIM = 128
FOX_WIDTH = 1024
N_GROUPS = 4
EXPERTS_PER_GROUP = 8
N_EXPERTS = 32
D_EXPERT = 512
EXPERT_BLOCK = 128
EPS = 1e-6

OFF_F = 5 * 1024
OFF_GATE = OFF_F + FOX_HEADS

LANES = 128
VMEM_LIMIT = 56 * 1024 * 1024

SEC = 1024
N_MAIN_SEC = 5
N_GATE_SEC = 4
TM_IN = 512
TQ = 256
TM_MIX = 256
RCH = 256
TM_CMB = 256
ATTN_SCALE = HEAD_DIM ** -0.5


def _sigmoid(x):
    return 1.0 / (1.0 + jnp.exp(-x))


def _log_sigmoid(x):
    return jnp.minimum(x, 0.0) - jnp.log1p(jnp.exp(-jnp.abs(x)))


def _inproj_kernel(x_ref, g1_ref, w_ref, wf_ref, bf_ref, bg_ref, lng_ref, lnb_ref, wsp_ref,
                   bsp_ref, qg_ref, kg_ref,
                   asgu_ref, q_ref, k_ref, va_ref, gates_ref, lf_ref,
                   h_s, u_s):
    j = pl.program_id(1)
    tm = x_ref.shape[0]

    @pl.when(j == 0)
    def _():
        x = x_ref[...]
        ms = jnp.mean(x * x, axis=-1, keepdims=True)
        hb = ((x * lax.rsqrt(ms + EPS)) * g1_ref[...]).astype(BF16)
        h_s[...] = hb
        f = jnp.dot(hb, wf_ref[...], preferred_element_type=F32) + bf_ref[...]
        lf_ref[...] = _log_sigmoid(f)

    z = jnp.dot(h_s[...], w_ref[...], preferred_element_type=F32)

    @pl.when(j == 0)
    def _():
        u_s[...] = jax.nn.gelu(z)

    @pl.when(j == 1)
    def _():
        v = jax.nn.gelu(z)
        mu = jnp.mean(v, axis=-1, keepdims=True)
        vc = v - mu
        var = jnp.mean(vc * vc, axis=-1, keepdims=True)
        vn = ((vc * lax.rsqrt(var + EPS)) * lng_ref[...] + lnb_ref[...]).astype(BF16)
        row = lax.broadcasted_iota(I32, (CHUNK, CHUNK), 0)
        col = lax.broadcasted_iota(I32, (CHUNK, CHUNK), 1)
        causal = row >= col
        for g in range(SGU_GROUPS):
            wg = jnp.where(causal, wsp_ref[g], 0.0).astype(BF16)
            bcol = bsp_ref[:, g:g + 1]
            gs = slice(g * LANES, (g + 1) * LANES)
            for c in range(tm // CHUNK):
                rs = slice(c * CHUNK, (c + 1) * CHUNK)
                s = jnp.dot(wg, vn[rs, gs], preferred_element_type=F32) + bcol
                asgu_ref[rs, gs] = (u_s[rs, gs] * s).astype(BF16)

    def _head_norm(gain_ref, out_ref, scale):
        for h in range(FOX_HEADS):
            hs = slice(h * HEAD_DIM, (h + 1) * HEAD_DIM)
            zh = z[:, hs]
            ms = jnp.mean(zh * zh, axis=-1, keepdims=True)
            y = (zh * lax.rsqrt(ms + EPS)) * gain_ref[...]
            if scale != 1.0:
                y = y * scale
            out_ref[:, hs] = y.astype(BF16)

    @pl.when(j == 2)
    def _():
        _head_norm(qg_ref, q_ref, ATTN_SCALE)

    @pl.when(j == 3)
    def _():
        _head_norm(kg_ref, k_ref, 1.0)

    @pl.when(j == 4)
    def _():
        va_ref[...] = z.astype(BF16)

    @pl.when(j >= N_MAIN_SEC)
    def _():
        gates_ref[...] = _sigmoid(z + bg_ref[...]).astype(BF16)


def _inproj(x2, g1, w_all, w_f, b_f, b_gate, ln_g, ln_b, w_sp, b_sp_t, q_g, k_g):
    t = x2.shape[0]
    n_sec = N_MAIN_SEC + N_GATE_SEC
    gate_blk = lambda i, j: (i, jnp.maximum(j - N_MAIN_SEC, 0))
    row_blk = lambda i, j: (i, 0)
    const2 = lambda i, j: (0, 0)
    out_shape = (
        jax.ShapeDtypeStruct((t, SGU_WIDTH), BF16),
        jax.ShapeDtypeStruct((t, FOX_WIDTH), BF16),
        jax.ShapeDtypeStruct((t, FOX_WIDTH), BF16),
        jax.ShapeDtypeStruct((t, FOX_WIDTH), BF16),
        jax.ShapeDtypeStruct((t, 2 * D_MODEL), BF16),
        jax.ShapeDtypeStruct((t, LANES), F32),
    )
    return pl.pallas_call(
        _inproj_kernel,
        out_shape=out_shape,
        grid=(t // TM_IN, n_sec),
        in_specs=[
            pl.BlockSpec((TM_IN, D_MODEL), row_blk),
            pl.BlockSpec((1, D_MODEL), const2),
            pl.BlockSpec((D_MODEL, SEC), lambda i, j: (0, j)),
            pl.BlockSpec((D_MODEL, LANES), const2),
            pl.BlockSpec((1, LANES), const2),
            pl.BlockSpec((1, SEC), lambda i, j: (0, jnp.maximum(j - N_MAIN_SEC, 0))),
            pl.BlockSpec((1, SGU_WIDTH), const2),
            pl.BlockSpec((1, SGU_WIDTH), const2),
            pl.BlockSpec((SGU_GROUPS, CHUNK, CHUNK), lambda i, j: (0, 0, 0)),
            pl.BlockSpec((CHUNK, SGU_GROUPS), const2),
            pl.BlockSpec((1, HEAD_DIM), const2),
            pl.BlockSpec((1, HEAD_DIM), const2),
        ],
        out_specs=(
            pl.BlockSpec((TM_IN, SGU_WIDTH), row_blk),
            pl.BlockSpec((TM_IN, FOX_WIDTH), row_blk),
            pl.BlockSpec((TM_IN, FOX_WIDTH), row_blk),
            pl.BlockSpec((TM_IN, FOX_WIDTH), row_blk),
            pl.BlockSpec((TM_IN, SEC), gate_blk),
            pl.BlockSpec((TM_IN, LANES), row_blk),
        ),
        scratch_shapes=[pltpu.VMEM((TM_IN, D_MODEL), BF16), pltpu.VMEM((TM_IN, SEC), F32)],
        compiler_params=pltpu.CompilerParams(
            dimension_semantics=("arbitrary", "arbitrary"), vmem_limit_bytes=VMEM_LIMIT),
        name="inproj",
    )(x2, g1, w_all, w_f, b_f, b_gate, ln_g, ln_b, w_sp, b_sp_t, q_g, k_g)


def _split3(x):
    x0 = x.astype(BF16)
    r1 = x - x0.astype(F32)
    x1 = r1.astype(BF16)
    r2 = r1 - x1.astype(F32)
    return x0, x1, r2.astype(BF16)


def _cumsum_kernel(lf_ref, negc_ref, c_s):
    seq = lf_ref.shape[0]
    row = lax.broadcasted_iota(I32, (CHUNK, CHUNK), 0)
    col = lax.broadcasted_iota(I32, (CHUNK, CHUNK), 1)
    tri = jnp.where(row >= col, 1.0, 0.0).astype(BF16)
    carry = jnp.zeros((1, LANES), F32)
    for r in range(seq // CHUNK):
        rs = slice(r * CHUNK, (r + 1) * CHUNK)
        x0, x1, x2 = _split3(lf_ref[rs, :])
        cs = (jnp.dot(tri, x0, preferred_element_type=F32)
              + jnp.dot(tri, x1, preferred_element_type=F32)
              + jnp.dot(tri, x2, preferred_element_type=F32)) + carry
        carry = cs[CHUNK - 1:CHUNK, :]
        c_s[rs, :] = cs
    ct = c_s[...].T
    for kb in range(seq // TQ):
        negc_ref[0, :, kb, :] = -ct[0:FOX_HEADS, kb * TQ:(kb + 1) * TQ]


def _forget_cumsum(lf, batch, seq):
    return pl.pallas_call(
        _cumsum_kernel,
        out_shape=jax.ShapeDtypeStruct((batch, FOX_HEADS, seq // TQ, TQ), F32),
        grid=(batch,),
        in_specs=[pl.BlockSpec((seq, LANES), lambda b: (b, 0))],
        out_specs=pl.BlockSpec((1, FOX_HEADS, seq // TQ, TQ), lambda b: (b, 0, 0, 0)),
        scratch_shapes=[pltpu.VMEM((seq, LANES), F32)],
        compiler_params=pltpu.CompilerParams(
            dimension_semantics=("arbitrary",), vmem_limit_bytes=VMEM_LIMIT),
        name="forget_cumsum",
    )(lf)


def _attn_kernel(q_ref, k_ref, v_ref, negc_ref, o_ref):
    seq = q_ref.shape[0]
    nt = (((1,), (1,)), ((), ()))

    def scores(q, kb):
        k0 = pl.multiple_of(kb * TQ, TQ)
        s = lax.dot_general(q, k_ref[pl.ds(k0, TQ), :], nt, preferred_element_type=F32)
        return s + negc_ref[0, 0, pl.ds(kb, 1), :], k0

    def update(carry, s, k0):
        m, l, acc = carry
        m_new = jnp.maximum(m, jnp.max(s, axis=-1, keepdims=True))
        p = jnp.exp(s - m_new)
        alpha = jnp.exp(m - m_new)
        l = alpha * l + jnp.sum(p, axis=-1, keepdims=True)
        acc = alpha * acc + jnp.dot(p.astype(BF16), v_ref[pl.ds(k0, TQ), :],
                                    preferred_element_type=F32)
        return m_new, l, acc

    row = lax.broadcasted_iota(I32, (TQ, TQ), 0)
    col = lax.broadcasted_iota(I32, (TQ, TQ), 1)
    causal = row >= col

    def q_body(qi, _):
        q0 = pl.multiple_of(qi * TQ, TQ)
        q = q_ref[pl.ds(q0, TQ), :]

        def kv_body(kb, carry):
            s, k0 = scores(q, kb)
            return update(carry, s, k0)

        init = (jnp.full((TQ, 1), -jnp.inf, F32), jnp.zeros((TQ, 1), F32),
                jnp.zeros((TQ, HEAD_DIM), F32))
        carry = lax.fori_loop(0, qi, kv_body, init)
        s, k0 = scores(q, qi)
        _, l, acc = update(carry, jnp.where(causal, s, -jnp.inf), k0)
        o_ref[pl.ds(q0, TQ), :] = (acc / l).astype(BF16)
        return 0

    lax.fori_loop(0, seq // TQ, q_body, 0)


def _attention(q, k, va, negc, batch, seq):
    t = q.shape[0]
    blk = pl.BlockSpec((seq, HEAD_DIM), lambda b, h: (b, h))
    return pl.pallas_call(
        _attn_kernel,
        out_shape=jax.ShapeDtypeStruct((t, FOX_WIDTH), BF16),
        grid=(batch, FOX_HEADS),
        in_specs=[blk, blk, blk,
                  pl.BlockSpec((1, 1, seq // TQ, TQ), lambda b, h: (b, h, 0, 0))],
        out_specs=blk,
        compiler_params=pltpu.CompilerParams(
            dimension_semantics=("arbitrary", "arbitrary"), vmem_limit_bytes=VMEM_LIMIT),
        name="fox_attention",
    )(q, k, va, negc)


def _mix_kernel(as_ref, af_ref, g_ref, x_ref, wps_ref, wpf_ref, wo_ref, g2_ref, wr_ref, br_ref,
                x1_ref, lg_ref):
    ys = jnp.dot(as_ref[...], wps_ref[...], preferred_element_type=F32)
    yf = jnp.dot(af_ref[...], wpf_ref[...], preferred_element_type=F32)
    m = (g_ref[:, :D_MODEL].astype(F32) * ys + g_ref[:, D_MODEL:].astype(F32) * yf).astype(BF16)
    x1 = x_ref[...] + jnp.dot(m, wo_ref[...], preferred_element_type=F32)
    x1_ref[...] = x1
    ms = jnp.mean(x1 * x1, axis=-1, keepdims=True)
    h2 = ((x1 * lax.rsqrt(ms + EPS)) * g2_ref[...]).astype(BF16)
    lg_ref[...] = jnp.dot(h2, wr_ref[...], preferred_element_type=F32) + br_ref[...]


def _mix(a_sgu, a_fox, gates, x2, wps, wpf, wo, g2, w_r, b_r):
    t = x2.shape[0]
    row = lambda i: (i, 0)
    const = lambda i: (0, 0)
    resident = functools.partial(pl.BlockSpec, index_map=const, pipeline_mode=pl.Buffered(1))
    return pl.pallas_call(
        _mix_kernel,
        out_shape=(jax.ShapeDtypeStruct((t, D_MODEL), F32),
                   jax.ShapeDtypeStruct((t, LANES), F32)),
        grid=(t // TM_MIX,),
        in_specs=[
            pl.BlockSpec((TM_MIX, SGU_WIDTH), row),
            pl.BlockSpec((TM_MIX, FOX_WIDTH), row),
            pl.BlockSpec((TM_MIX, 2 * D_MODEL), row),
            pl.BlockSpec((TM_MIX, D_MODEL), row),
            resident((SGU_WIDTH, D_MODEL)),
            resident((FOX_WIDTH, D_MODEL)),
            resident((D_MODEL, D_MODEL)),
            pl.BlockSpec((1, D_MODEL), const),
            resident((D_MODEL, LANES)),
            pl.BlockSpec((1, LANES), const),
        ],
        out_specs=(pl.BlockSpec((TM_MIX, D_MODEL), row), pl.BlockSpec((TM_MIX, LANES), row)),
        compiler_params=pltpu.CompilerParams(
            dimension_semantics=("arbitrary",), vmem_limit_bytes=VMEM_LIMIT),
        name="mix",
    )(a_sgu, a_fox, gates, x2, wps, wpf, wo, g2, w_r, b_r)


def _route_kernel(lg_ref, mi_ref, mf_ref, bm_ref, em_ref):
    t = lg_ref.shape[0]
    n_chunks = t // RCH
    lane_i = lax.broadcasted_iota(I32, (RCH, LANES), 1)
    lane = lane_i.astype(F32)
    lane_grp = ((lane_i - N_GROUPS) >> 3).astype(F32)
    is_grp = lane_i < N_GROUPS
    is_exp = (lane_i >= N_GROUPS) & (lane_i < N_GROUPS + N_EXPERTS)
    r_i = lax.broadcasted_iota(I32, (RCH, RCH), 0)
    c_i = lax.broadcasted_iota(I32, (RCH, RCH), 1)
    strict_lower = jnp.where(r_i > c_i, 1.0, 0.0).astype(BF16)
    neg_inf = -jnp.inf

    def first_max(vals):
        vmax = jnp.max(vals, axis=-1, keepdims=True)
        idx = jnp.min(jnp.where(vals == vmax, lane, float(LANES)), axis=-1, keepdims=True)
        return vmax, idx

    def pick(table, idx):
        return jnp.sum(jnp.where(lane == idx, table, 0.0), axis=-1, keepdims=True)

    def pack(cols):
        out = jnp.zeros((RCH, LANES), F32)
        for n, c in enumerate(cols):
            out = jnp.where(lane_i == n, c, out)
        return out

    def pass1(ci, counts):
        r0 = pl.multiple_of(ci * RCH, RCH)
        lg = lg_ref[pl.ds(r0, RCH), :]
        gmax, grp = first_max(jnp.where(is_grp, lg, neg_inf))
        p_grp = 1.0 / jnp.sum(jnp.where(is_grp, jnp.exp(lg - gmax), 0.0), axis=-1, keepdims=True)
        el = jnp.where(is_exp & (lane_grp == grp), lg, neg_inf)
        v1, i1 = first_max(el)
        v2, i2 = first_max(jnp.where(lane == i1, neg_inf, el))
        e21 = jnp.exp(v2 - v1)
        w1 = p_grp / (1.0 + e21)
        w2 = p_grp * e21 / (1.0 + e21)
        e1 = i1 - float(N_GROUPS)
        e2 = i2 - float(N_GROUPS)
        hot = jnp.where((lane == e1) | (lane == e2), 1.0, 0.0)
        before = jnp.dot(strict_lower, hot.astype(BF16), preferred_element_type=F32) + counts
        mi_ref[pl.ds(r0, RCH), :] = pack([e1, e2, pick(before, e1), pick(before, e2)]).astype(I32)
        mf_ref[pl.ds(r0, RCH), :] = pack([w1, w2])
        return counts + jnp.sum(hot, axis=0, keepdims=True)

    counts = lax.fori_loop(0, n_chunks, pass1, jnp.zeros((1, LANES), F32))

    nblk = jnp.floor((counts + float(EXPERT_BLOCK - 1)) * (1.0 / EXPERT_BLOCK))
    u_r = lax.broadcasted_iota(I32, (LANES, LANES), 0)
    u_c = lax.broadcasted_iota(I32, (LANES, LANES), 1)
    strict_upper = jnp.where(u_r < u_c, 1.0, 0.0).astype(BF16)
    bstart = jnp.dot(jnp.broadcast_to(nblk, (8, LANES)).astype(BF16), strict_upper,
                     preferred_element_type=F32)[0:1, :]
    bend = bstart + nblk

    def pass2(ci, _):
        r0 = pl.multiple_of(ci * RCH, RCH)
        mi = mi_ref[pl.ds(r0, RCH), :].astype(F32)
        cols = [jnp.sum(jnp.where(lane_i == n, mi, 0.0), axis=-1, keepdims=True) for n in range(4)]
        d1 = pick(bstart, cols[0]) * float(EXPERT_BLOCK) + cols[2]
        d2 = pick(bstart, cols[1]) * float(EXPERT_BLOCK) + cols[3]
        mi_ref[pl.ds(r0, RCH), :] = pack(cols + [d1, d2]).astype(I32)
        return 0

    lax.fori_loop(0, n_chunks, pass2, 0)

    nb_rows = bm_ref.shape[0]
    b_col = lax.broadcasted_iota(I32, (nb_rows, LANES), 0).astype(F32)
    b_lane = lax.broadcasted_iota(I32, (nb_rows, LANES), 1)
    done = jnp.where((bend <= b_col) & (b_lane < N_EXPERTS), 1.0, 0.0)
    blk_e = jnp.minimum(jnp.sum(done, axis=-1, keepdims=True), float(N_EXPERTS - 1))
    n_used = jnp.sum(jnp.where(b_lane[0:1, :] == N_EXPERTS - 1, bend, 0.0), axis=-1, keepdims=True)
    bm_ref[...] = jnp.where(b_lane == 0, blk_e, jnp.where(b_lane == 1, n_used, 0.0)).astype(I32)

    e_row = lax.broadcasted_iota(I32, (8, LANES), 0)
    em_ref[...] = jnp.where(e_row == 0, counts,
                            jnp.where(e_row == 1, bstart, jnp.where(e_row == 2, nblk, 0.0))).astype(I32)


def _route(logits, nb_rows):
    t = logits.shape[0]
    full = lambda shape: pl.BlockSpec(shape, lambda: (0, 0))
    return pl.pallas_call(
        _route_kernel,
        out_shape=(jax.ShapeDtypeStruct((t, LANES), I32),
                   jax.ShapeDtypeStruct((t, LANES), F32),
                   jax.ShapeDtypeStruct((nb_rows, LANES), I32),
                   jax.ShapeDtypeStruct((8, LANES), I32)),
        in_specs=[full((t, LANES))],
        out_specs=(full((t, LANES)), full((t, LANES)), full((nb_rows, LANES)), full((8, LANES))),
        compiler_params=pltpu.CompilerParams(vmem_limit_bytes=VMEM_LIMIT),
        name="route",
    )(logits)


DISPATCH_UNROLL = 8


def _dispatch_kernel(d1_ref, d2_ref, cnt_ref, bstart_ref, nblk_ref, nused_ref, x_ref, xs_ref, sem):
    t = x_ref.shape[0]
    nb = xs_ref.shape[0] // EXPERT_BLOCK

    def row_copy(src_row, dst_row):
        return pltpu.make_async_copy(x_ref.at[pl.ds(src_row, 1), :],
                                     xs_ref.at[pl.ds(dst_row, 1), :], sem)

    def block_copy(dst_block):
        dst_row = pl.multiple_of(dst_block * EXPERT_BLOCK, EXPERT_BLOCK)
        return pltpu.make_async_copy(x_ref.at[pl.ds(0, EXPERT_BLOCK), :],
                                     xs_ref.at[pl.ds(dst_row, EXPERT_BLOCK), :], sem)

    def tok_body(tok, _):
        row_copy(tok, d1_ref[tok]).start()
        row_copy(tok, d2_ref[tok]).start()
        return 0

    lax.fori_loop(0, t, tok_body, 0, unroll=DISPATCH_UNROLL)

    def pad_expert(e, n_pad):
        lo = bstart_ref[e] * EXPERT_BLOCK + cnt_ref[e]
        hi = (bstart_ref[e] + nblk_ref[e]) * EXPERT_BLOCK

        def pad_body(r, _):
            row_copy(0, r).start()
            return 0

        lax.fori_loop(lo, hi, pad_body, 0)
        return n_pad + (hi - lo)

    n_pad = lax.fori_loop(0, N_EXPERTS, pad_expert, 0)

    n_used = nused_ref[0]

    def fill_body(b, _):
        block_copy(b).start()
        return 0

    lax.fori_loop(n_used, nb, fill_body, 0)

    def bulk_wait(_, c):
        block_copy(0).wait()
        return c

    lax.fori_loop(0, (2 * t) // EXPERT_BLOCK + (nb - n_used), bulk_wait, 0)

    def row_wait(_, c):
        row_copy(0, 0).wait()
        return c

    lax.fori_loop(0, n_pad, row_wait, 0)


def _dispatch(dest1, dest2, cnt, bstart, nblk, n_used, x1, n_rows):
    smem = pl.BlockSpec(memory_space=pltpu.SMEM)
    hbm = pl.BlockSpec(memory_space=pl.ANY)
    return pl.pallas_call(
        _dispatch_kernel,
        out_shape=jax.ShapeDtypeStruct((n_rows, D_MODEL), F32),
        in_specs=[smem, smem, smem, smem, smem, smem, hbm],
        out_specs=hbm,
        scratch_shapes=[pltpu.SemaphoreType.DMA(())],
        name="dispatch",
    )(dest1, dest2, cnt, bstart, nblk, n_used, x1)


def _experts_kernel(blk_e_ref, nused_ref, x_ref, g2_ref, wg_ref, wu_ref, wd_ref, y_ref,
                    wg_s, wu_s, wd_s):
    b = pl.program_id(0)
    used = b < nused_ref[0]
    new_expert = (b == 0) | (blk_e_ref[b] != blk_e_ref[jnp.maximum(b - 1, 0)])

    @pl.when(used & new_expert)
    def _():
        wg_s[...] = wg_ref[0].astype(BF16)
        wu_s[...] = wu_ref[0].astype(BF16)
        wd_s[...] = wd_ref[0].astype(BF16)

    @pl.when(used)
    def _():
        x = x_ref[...]
        ms = jnp.mean(x * x, axis=-1, keepdims=True)
        h = ((x * lax.rsqrt(ms + EPS)) * g2_ref[...]).astype(BF16)
        a = jnp.dot(h, wg_s[...], preferred_element_type=F32)
        u = jnp.dot(h, wu_s[...], preferred_element_type=F32)
        mid = ((a * _sigmoid(a)) * u).astype(BF16)
        y_ref[...] = jnp.dot(mid, wd_s[...], preferred_element_type=F32)

    @pl.when(jnp.logical_not(used))
    def _():
        y_ref[...] = jnp.zeros_like(y_ref)


def _experts(blk_e, n_used, xs, g2, w_g, w_u, w_d):
    n_rows = xs.shape[0]
    nb = n_rows // EXPERT_BLOCK

    def row_blk(b, blk_e_ref, nused_ref):
        return (jnp.minimum(b, nused_ref[0] - 1), 0)

    def w_blk(b, blk_e_ref, nused_ref):
        return (blk_e_ref[b], 0, 0)

    grid_spec = pltpu.PrefetchScalarGridSpec(
        num_scalar_prefetch=2,
        grid=(nb,),
        in_specs=[
            pl.BlockSpec((EXPERT_BLOCK, D_MODEL), row_blk),
            pl.BlockSpec((1, D_MODEL), lambda b, e, n: (0, 0)),
            pl.BlockSpec((1, D_MODEL, D_EXPERT), w_blk),
            pl.BlockSpec((1, D_MODEL, D_EXPERT), w_blk),
            pl.BlockSpec((1, D_EXPERT, D_MODEL), w_blk),
        ],
        out_specs=pl.BlockSpec((EXPERT_BLOCK, D_MODEL), lambda b, e, n: (b, 0)),
        scratch_shapes=[pltpu.VMEM((D_MODEL, D_EXPERT), BF16),
                        pltpu.VMEM((D_MODEL, D_EXPERT), BF16),
                        pltpu.VMEM((D_EXPERT, D_MODEL), BF16)],
    )
    return pl.pallas_call(
        _experts_kernel,
        out_shape=jax.ShapeDtypeStruct((n_rows, D_MODEL), F32),
        grid_spec=grid_spec,
        compiler_params=pltpu.CompilerParams(
            dimension_semantics=("arbitrary",), vmem_limit_bytes=VMEM_LIMIT),
        name="experts",
    )(blk_e, n_used, xs, g2, w_g, w_u, w_d)


COMBINE_UNROLL = 8


def _combine_kernel(d1_ref, d2_ref, x1_ref, w_ref, y_ref, o_ref, gbuf, sem):
    i = pl.program_id(0)
    n = pl.num_programs(0)
    tm = x1_ref.shape[0]

    def issue(tile, slot):
        def body(r, _):
            tok = tile * tm + r
            pltpu.make_async_copy(y_ref.at[pl.ds(d1_ref[tok], 1), :],
                                  gbuf.at[slot, pl.ds(r, 1), :], sem.at[slot]).start()
            pltpu.make_async_copy(y_ref.at[pl.ds(d2_ref[tok], 1), :],
                                  gbuf.at[slot, pl.ds(tm + r, 1), :], sem.at[slot]).start()
            return 0

        lax.fori_loop(0, tm, body, 0, unroll=COMBINE_UNROLL)

    slot = i % 2

    @pl.when(i == 0)
    def _():
        issue(0, 0)

    @pl.when(i + 1 < n)
    def _():
        issue(i + 1, 1 - slot)

    pltpu.make_async_copy(y_ref.at[pl.ds(0, 2 * tm), :], gbuf.at[slot], sem.at[slot]).wait()
    w = w_ref[...]
    g = gbuf[slot]
    o_ref[...] = x1_ref[...] + (w[:, 0:1] * g[0:tm, :] + w[:, 1:2] * g[tm:2 * tm, :])


def _combine(dest1, dest2, x1, w, yb):
    t = x1.shape[0]
    grid_spec = pltpu.PrefetchScalarGridSpec(
        num_scalar_prefetch=2,
        grid=(t // TM_CMB,),
        in_specs=[
            pl.BlockSpec((TM_CMB, D_MODEL), lambda i, a, b: (i, 0)),
            pl.BlockSpec((TM_CMB, LANES), lambda i, a, b: (i, 0)),
            pl.BlockSpec(memory_space=pl.ANY),
        ],
        out_specs=pl.BlockSpec((TM_CMB, D_MODEL), lambda i, a, b: (i, 0)),
        scratch_shapes=[pltpu.VMEM((2, 2 * TM_CMB, D_MODEL), F32),
                        pltpu.SemaphoreType.DMA((2,))],
    )
    return pl.pallas_call(
        _combine_kernel,
        out_shape=jax.ShapeDtypeStruct((t, D_MODEL), F32),
        grid_spec=grid_spec,
        compiler_params=pltpu.CompilerParams(
            dimension_semantics=("arbitrary",), vmem_limit_bytes=VMEM_LIMIT),
        name="combine",
    )(dest1, dest2, x1, w, yb)


def kernel(x, norm1_g, w_in, b_gate, b_forget, sgu_ln_g, sgu_ln_b, w_spatial, b_spatial, q_norm_g, k_norm_g, w_proj_sgu, w_proj_fox, w_out, norm2_g, w_router_group, b_router_group, w_router_expert, b_router_expert, w_expert_gate, w_expert_up, w_expert_down):
    batch, seq, d = x.shape
    t = batch * seq
    l = 0
    x2 = x.reshape(t, d)

    w_all = jnp.concatenate([w_in[l][:, :OFF_F], w_in[l][:, OFF_GATE:]], axis=1).astype(BF16)
    w_f = jnp.pad(w_in[l][:, OFF_F:OFF_GATE], ((0, 0), (0, LANES - FOX_HEADS))).astype(BF16)
    b_f = jnp.pad(b_forget[l], (0, LANES - FOX_HEADS)).reshape(1, LANES)
    n_r = N_GROUPS + N_EXPERTS
    w_r = jnp.pad(jnp.concatenate([w_router_group[l], w_router_expert[l]], axis=1),
                  ((0, 0), (0, LANES - n_r))).astype(BF16)
    b_r = jnp.pad(jnp.concatenate([b_router_group[l], b_router_expert[l]]),
                  (0, LANES - n_r)).reshape(1, LANES)

    a_sgu, q, k, va, gates, lf = _inproj(
        x2, norm1_g[l].reshape(1, d), w_all, w_f, b_f, b_gate[l].reshape(1, 2 * d),
        sgu_ln_g[l].reshape(1, SGU_WIDTH), sgu_ln_b[l].reshape(1, SGU_WIDTH),
        w_spatial[l], b_spatial[l].T, q_norm_g[l].reshape(1, HEAD_DIM),
        k_norm_g[l].reshape(1, HEAD_DIM))
    negc = _forget_cumsum(lf, batch, seq)
    a_fox = _attention(q, k, va, negc, batch, seq)
    x1, logits = _mix(a_sgu, a_fox, gates, x2, w_proj_sgu[l].astype(BF16),
                      w_proj_fox[l].astype(BF16), w_out[l].astype(BF16),
                      norm2_g[l].reshape(1, d), w_r, b_r)

    n_assign = 2 * t
    n_rows = n_assign + N_EXPERTS * EXPERT_BLOCK
    nb = n_rows // EXPERT_BLOCK
    nb_rows = -(-nb // 8) * 8
    meta_i, meta_f, bmeta, emeta = _route(logits, nb_rows)
    dest1, dest2 = meta_i[:, 4], meta_i[:, 5]
    n_used = bmeta[0:1, 1]
    xs = _dispatch(dest1, dest2, emeta[0, :N_EXPERTS], emeta[1, :N_EXPERTS], emeta[2, :N_EXPERTS],
                   n_used, x1, n_rows)
    yb = _experts(bmeta[:nb, 0], n_used, xs, norm2_g[l].reshape(1, d),
                  w_expert_gate[l], w_expert_up[l], w_expert_down[l])
    out = _combine(dest1, dest2, x1, meta_f, yb)
    return out.reshape(batch, seq, d)
```

```python
import functools

import jax
import jax.numpy as jnp
from jax import lax
from jax.experimental import pallas as pl
from jax.experimental.pallas import tpu as pltpu

F32 = jnp.float32
BF16 = jnp.bfloat16
I32 = jnp.int32

D_MODEL = 2048
CHUNK = 128
SGU_GROUPS = 8
SGU_WIDTH = 1024
FOX_HEADS = 8
HEAD_DIM = 128
FOX_WIDTH = 1024
N_GROUPS = 4
EXPERTS_PER_GROUP = 8
N_EXPERTS = 32
D_EXPERT = 512
EXPERT_BLOCK = 128
EPS = 1e-6

OFF_F = 5 * 1024
OFF_GATE = OFF_F + FOX_HEADS

LANES = 128
VMEM_LIMIT = 56 * 1024 * 1024

SEC = 1024
N_MAIN_SEC = 5
N_GATE_SEC = 4
TM_IN = 512
TQ = 256
TM_MIX = 256
RCH = 256
TM_CMB = 256
ATTN_SCALE = HEAD_DIM ** -0.5


def _sigmoid(x):
    return 1.0 / (1.0 + jnp.exp(-x))


def _log_sigmoid(x):
    return jnp.minimum(x, 0.0) - jnp.log1p(jnp.exp(-jnp.abs(x)))


def _inproj_kernel(x_ref, g1_ref, w_ref, wf_ref, bf_ref, bg_ref, lng_ref, lnb_ref, wsp_ref,
                   bsp_ref, qg_ref, kg_ref,
                   asgu_ref, q_ref, kt_ref, va_ref, gates_ref, lf_ref,
                   h_s, u_s):
    j = pl.program_id(1)
    tm = x_ref.shape[0]

    @pl.when(j == 0)
    def _():
        x = x_ref[...]
        ms = jnp.mean(x * x, axis=-1, keepdims=True)
        hb = ((x * lax.rsqrt(ms + EPS)) * g1_ref[...]).astype(BF16)
        h_s[...] = hb
        f = jnp.dot(hb, wf_ref[...], preferred_element_type=F32) + bf_ref[...]
        lf_ref[...] = _log_sigmoid(f)

    z = jnp.dot(h_s[...], w_ref[...], preferred_element_type=F32)

    @pl.when(j == 0)
    def _():
        u_s[...] = jax.nn.gelu(z)

    @pl.when(j == 1)
    def _():
        v = jax.nn.gelu(z)
        mu = jnp.mean(v, axis=-1, keepdims=True)
        vc = v - mu
        var = jnp.mean(vc * vc, axis=-1, keepdims=True)
        vn = ((vc * lax.rsqrt(var + EPS)) * lng_ref[...] + lnb_ref[...]).astype(BF16)
        row = lax.broadcasted_iota(I32, (CHUNK, CHUNK), 0)
        col = lax.broadcasted_iota(I32, (CHUNK, CHUNK), 1)
        causal = row >= col
        for g in range(SGU_GROUPS):
            wg = jnp.where(causal, wsp_ref[g], 0.0).astype(BF16)
            bcol = bsp_ref[:, g:g + 1]
            gs = slice(g * LANES, (g + 1) * LANES)
            for c in range(tm // CHUNK):
                rs = slice(c * CHUNK, (c + 1) * CHUNK)
                s = jnp.dot(wg, vn[rs, gs], preferred_element_type=F32) + bcol
                asgu_ref[rs, gs] = (u_s[rs, gs] * s).astype(BF16)

    def _head_norm(h, gain_ref):
        zh = z[:, h * HEAD_DIM:(h + 1) * HEAD_DIM]
        ms = jnp.mean(zh * zh, axis=-1, keepdims=True)
        return (zh * lax.rsqrt(ms + EPS)) * gain_ref[...]

    @pl.when(j == 2)
    def _():
        for h in range(FOX_HEADS):
            q_ref[:, h * HEAD_DIM:(h + 1) * HEAD_DIM] = (_head_norm(h, qg_ref) * ATTN_SCALE).astype(BF16)

    @pl.when(j == 3)
    def _():
        for h in range(FOX_HEADS):
            kt_ref[h * HEAD_DIM:(h + 1) * HEAD_DIM, :] = _head_norm(h, kg_ref).T.astype(BF16)

    @pl.when(j == 4)
    def _():
        va_ref[...] = z.astype(BF16)

    @pl.when(j >= N_MAIN_SEC)
    def _():
        gates_ref[...] = _sigmoid(z + bg_ref[...]).astype(BF16)


def _inproj(x2, g1, w_all, w_f, b_f, b_gate, ln_g, ln_b, w_sp, b_sp_t, q_g, k_g):
    t = x2.shape[0]
    n_sec = N_MAIN_SEC + N_GATE_SEC
    gate_blk = lambda i, j: (i, jnp.maximum(j - N_MAIN_SEC, 0))
    row_blk = lambda i, j: (i, 0)
    const2 = lambda i, j: (0, 0)
    out_shape = (
        jax.ShapeDtypeStruct((t, SGU_WIDTH), BF16),
        jax.ShapeDtypeStruct((t, FOX_WIDTH), BF16),
        jax.ShapeDtypeStruct((FOX_WIDTH, t), BF16),
        jax.ShapeDtypeStruct((t, FOX_WIDTH), BF16),
        jax.ShapeDtypeStruct((t, 2 * D_MODEL), BF16),
        jax.ShapeDtypeStruct((t, LANES), F32),
    )
    return pl.pallas_call(
        _inproj_kernel,
        out_shape=out_shape,
        grid=(t // TM_IN, n_sec),
        in_specs=[
            pl.BlockSpec((TM_IN, D_MODEL), row_blk),
            pl.BlockSpec((1, D_MODEL), const2),
            pl.BlockSpec((D_MODEL, SEC), lambda i, j: (0, j)),
            pl.BlockSpec((D_MODEL, LANES), const2),
            pl.BlockSpec((1, LANES), const2),
            pl.BlockSpec((1, SEC), lambda i, j: (0, jnp.maximum(j - N_MAIN_SEC, 0))),
            pl.BlockSpec((1, SGU_WIDTH), const2),
            pl.BlockSpec((1, SGU_WIDTH), const2),
            pl.BlockSpec((SGU_GROUPS, CHUNK, CHUNK), lambda i, j: (0, 0, 0)),
            pl.BlockSpec((CHUNK, SGU_GROUPS), const2),
            pl.BlockSpec((1, HEAD_DIM), const2),
            pl.BlockSpec((1, HEAD_DIM), const2),
        ],
        out_specs=(
            pl.BlockSpec((TM_IN, SGU_WIDTH), row_blk),
            pl.BlockSpec((TM_IN, FOX_WIDTH), row_blk),
            pl.BlockSpec((FOX_WIDTH, TM_IN), lambda i, j: (0, i)),
            pl.BlockSpec((TM_IN, FOX_WIDTH), row_blk),
            pl.BlockSpec((TM_IN, SEC), gate_blk),
            pl.BlockSpec((TM_IN, LANES), row_blk),
        ),
        scratch_shapes=[pltpu.VMEM((TM_IN, D_MODEL), BF16), pltpu.VMEM((TM_IN, SEC), F32)],
        compiler_params=pltpu.CompilerParams(
            dimension_semantics=("arbitrary", "arbitrary"), vmem_limit_bytes=VMEM_LIMIT),
        name="inproj",
    )(x2, g1, w_all, w_f, b_f, b_gate, ln_g, ln_b, w_sp, b_sp_t, q_g, k_g)


def _split3(x):
    x0 = x.astype(BF16)
    r1 = x - x0.astype(F32)
    x1 = r1.astype(BF16)
    r2 = r1 - x1.astype(F32)
    return x0, x1, r2.astype(BF16)


def _cumsum_kernel(lf_ref, negc_ref, c_s):
    seq = lf_ref.shape[0]
    row = lax.broadcasted_iota(I32, (CHUNK, CHUNK), 0)
    col = lax.broadcasted_iota(I32, (CHUNK, CHUNK), 1)
    tri = jnp.where(row >= col, 1.0, 0.0).astype(BF16)
    carry = jnp.zeros((1, LANES), F32)
    for r in range(seq // CHUNK):
        rs = slice(r * CHUNK, (r + 1) * CHUNK)
        x0, x1, x2 = _split3(lf_ref[rs, :])
        cs = (jnp.dot(tri, x0, preferred_element_type=F32)
              + jnp.dot(tri, x1, preferred_element_type=F32)
              + jnp.dot(tri, x2, preferred_element_type=F32)) + carry
        carry = cs[CHUNK - 1:CHUNK, :]
        c_s[rs, :] = cs
    ct = c_s[...].T
    negc_ref[0] = -ct[0:FOX_HEADS, :]


def _forget_cumsum(lf, batch, seq):
    return pl.pallas_call(
        _cumsum_kernel,
        out_shape=jax.ShapeDtypeStruct((batch, FOX_HEADS, seq), F32),
        grid=(batch,),
        in_specs=[pl.BlockSpec((seq, LANES), lambda b: (b, 0))],
        out_specs=pl.BlockSpec((1, FOX_HEADS, seq), lambda b: (b, 0, 0)),
        scratch_shapes=[pltpu.VMEM((seq, LANES), F32)],
        compiler_params=pltpu.CompilerParams(
            dimension_semantics=("arbitrary",), vmem_limit_bytes=VMEM_LIMIT),
        name="forget_cumsum",
    )(lf)


def _attn_kernel(q_ref, kt_ref, v_ref, negc_ref, o_ref):
    seq = q_ref.shape[0]
    row = lax.broadcasted_iota(I32, (TQ, TQ), 0)
    col = lax.broadcasted_iota(I32, (TQ, TQ), 1)
    causal = row >= col
    for qi in range(seq // TQ):
        k0 = qi * TQ
        q = q_ref[k0:k0 + TQ, :]
        s_d = jnp.dot(q, kt_ref[:, k0:k0 + TQ], preferred_element_type=F32)
        s_d = jnp.where(causal, s_d + negc_ref[0, 0, :, k0:k0 + TQ], -jnp.inf)
        m = jnp.max(s_d, axis=-1, keepdims=True)
        if qi > 0:
            s_o = jnp.dot(q, kt_ref[:, 0:k0], preferred_element_type=F32) + negc_ref[0, 0, :, 0:k0]
            m = jnp.maximum(m, jnp.max(s_o, axis=-1, keepdims=True))
        p_d = jnp.exp(s_d - m)
        l = jnp.sum(p_d, axis=-1, keepdims=True)
        acc = jnp.dot(p_d.astype(BF16), v_ref[k0:k0 + TQ, :], preferred_element_type=F32)
        if qi > 0:
            p_o = jnp.exp(s_o - m)
            l = l + jnp.sum(p_o, axis=-1, keepdims=True)
            acc = acc + jnp.dot(p_o.astype(BF16), v_ref[0:k0, :], preferred_element_type=F32)
        o_ref[k0:k0 + TQ, :] = (acc * (1.0 / l)).astype(BF16)


def _attention(q, kt, va, negc, batch, seq):
    t = q.shape[0]
    blk = pl.BlockSpec((seq, HEAD_DIM), lambda b, h: (b, h))
    return pl.pallas_call(
        _attn_kernel,
        out_shape=jax.ShapeDtypeStruct((t, FOX_WIDTH), BF16),
        grid=(batch, FOX_HEADS),
        in_specs=[blk,
                  pl.BlockSpec((HEAD_DIM, seq), lambda b, h: (h, b)),
                  blk,
                  pl.BlockSpec((1, 1, 1, seq), lambda b, h: (b, h, 0, 0))],
        out_specs=blk,
        compiler_params=pltpu.CompilerParams(
            dimension_semantics=("arbitrary", "arbitrary"), vmem_limit_bytes=VMEM_LIMIT),
        name="fox_attention",
    )(q, kt, va, negc)


def _mix_kernel(as_ref, af_ref, g_ref, x_ref, wps_ref, wpf_ref, wo_ref, g2_ref, wr_ref, br_ref,
                x1_ref, lg_ref):
    ys = jnp.dot(as_ref[...], wps_ref[...], preferred_element_type=F32)
    yf = jnp.dot(af_ref[...], wpf_ref[...], preferred_element_type=F32)
    m = (g_ref[:, :D_MODEL].astype(F32) * ys + g_ref[:, D_MODEL:].astype(F32) * yf).astype(BF16)
    x1 = x_ref[...] + jnp.dot(m, wo_ref[...], preferred_element_type=F32)
    x1_ref[...] = x1
    ms = jnp.mean(x1 * x1, axis=-1, keepdims=True)
    h2 = ((x1 * lax.rsqrt(ms + EPS)) * g2_ref[...]).astype(BF16)
    lg_ref[...] = jnp.dot(h2, wr_ref[...], preferred_element_type=F32) + br_ref[...]


def _mix(a_sgu, a_fox, gates, x2, wps, wpf, wo, g2, w_r, b_r):
    t = x2.shape[0]
    row = lambda i: (i, 0)
    const = lambda i: (0, 0)
    resident = functools.partial(pl.BlockSpec, index_map=const, pipeline_mode=pl.Buffered(1))
    return pl.pallas_call(
        _mix_kernel,
        out_shape=(jax.ShapeDtypeStruct((t, D_MODEL), F32),
                   jax.ShapeDtypeStruct((t, LANES), F32)),
        grid=(t // TM_MIX,),
        in_specs=[
            pl.BlockSpec((TM_MIX, SGU_WIDTH), row),
            pl.BlockSpec((TM_MIX, FOX_WIDTH), row),
            pl.BlockSpec((TM_MIX, 2 * D_MODEL), row),
            pl.BlockSpec((TM_MIX, D_MODEL), row),
            resident((SGU_WIDTH, D_MODEL)),
            resident((FOX_WIDTH, D_MODEL)),
            resident((D_MODEL, D_MODEL)),
            pl.BlockSpec((1, D_MODEL), const),
            resident((D_MODEL, LANES)),
            pl.BlockSpec((1, LANES), const),
        ],
        out_specs=(pl.BlockSpec((TM_MIX, D_MODEL), row), pl.BlockSpec((TM_MIX, LANES), row)),
        compiler_params=pltpu.CompilerParams(
            dimension_semantics=("arbitrary",), vmem_limit_bytes=VMEM_LIMIT),
        name="mix",
    )(a_sgu, a_fox, gates, x2, wps, wpf, wo, g2, w_r, b_r)


def _route_kernel(lg_ref, mi_ref, mf_ref, bm_ref, em_ref):
    t = lg_ref.shape[0]
    n_chunks = t // RCH
    lane_i = lax.broadcasted_iota(I32, (RCH, LANES), 1)
    lane = lane_i.astype(F32)
    lane_grp = ((lane_i - N_GROUPS) >> 3).astype(F32)
    is_grp = lane_i < N_GROUPS
    is_exp = (lane_i >= N_GROUPS) & (lane_i < N_GROUPS + N_EXPERTS)
    r_i = lax.broadcasted_iota(I32, (RCH, RCH), 0)
    c_i = lax.broadcasted_iota(I32, (RCH, RCH), 1)
    strict_lower = jnp.where(r_i > c_i, 1.0, 0.0).astype(BF16)
    neg_inf = -jnp.inf

    def first_max(vals):
        vmax = jnp.max(vals, axis=-1, keepdims=True)
        idx = jnp.min(jnp.where(vals == vmax, lane, float(LANES)), axis=-1, keepdims=True)
        return vmax, idx

    def pick(table, idx):
        return jnp.sum(jnp.where(lane == idx, table, 0.0), axis=-1, keepdims=True)

    def pack(cols):
        out = jnp.zeros((RCH, LANES), F32)
        for n, c in enumerate(cols):
            out = jnp.where(lane_i == n, c, out)
        return out

    def pass1(ci, counts):
        r0 = pl.multiple_of(ci * RCH, RCH)
        lg = lg_ref[pl.ds(r0, RCH), :]
        gmax, grp = first_max(jnp.where(is_grp, lg, neg_inf))
        p_grp = 1.0 / jnp.sum(jnp.where(is_grp, jnp.exp(lg - gmax), 0.0), axis=-1, keepdims=True)
        el = jnp.where(is_exp & (lane_grp == grp), lg, neg_inf)
        v1, i1 = first_max(el)
        v2, i2 = first_max(jnp.where(lane == i1, neg_inf, el))
        e21 = jnp.exp(v2 - v1)
        w1 = p_grp / (1.0 + e21)
        w2 = p_grp * e21 / (1.0 + e21)
        e1 = i1 - float(N_GROUPS)
        e2 = i2 - float(N_GROUPS)
        hot = jnp.where((lane == e1) | (lane == e2), 1.0, 0.0)
        before = jnp.dot(strict_lower, hot.astype(BF16), preferred_element_type=F32) + counts
        mi_ref[pl.ds(r0, RCH), :] = pack([e1, e2, pick(before, e1), pick(before, e2)]).astype(I32)
        mf_ref[pl.ds(r0, RCH), :] = pack([w1, w2])
        return counts + jnp.sum(hot, axis=0, keepdims=True)

    counts = lax.fori_loop(0, n_chunks, pass1, jnp.zeros((1, LANES), F32))

    nblk = jnp.floor((counts + float(EXPERT_BLOCK - 1)) * (1.0 / EXPERT_BLOCK))
    u_r = lax.broadcasted_iota(I32, (LANES, LANES), 0)
    u_c = lax.broadcasted_iota(I32, (LANES, LANES), 1)
    strict_upper = jnp.where(u_r < u_c, 1.0, 0.0).astype(BF16)
    bstart = jnp.dot(jnp.broadcast_to(nblk, (8, LANES)).astype(BF16), strict_upper,
                     preferred_element_type=F32)[0:1, :]
    bend = bstart + nblk

    def pass2(ci, _):
        r0 = pl.multiple_of(ci * RCH, RCH)
        mi = mi_ref[pl.ds(r0, RCH), :].astype(F32)
        cols = [jnp.sum(jnp.where(lane_i == n, mi, 0.0), axis=-1, keepdims=True) for n in range(4)]
        d1 = pick(bstart, cols[0]) * float(EXPERT_BLOCK) + cols[2]
        d2 = pick(bstart, cols[1]) * float(EXPERT_BLOCK) + cols[3]
        mi_ref[pl.ds(r0, RCH), :] = pack(cols + [d1, d2]).astype(I32)
        return 0

    lax.fori_loop(0, n_chunks, pass2, 0)

    nb_rows = bm_ref.shape[0]
    b_col = lax.broadcasted_iota(I32, (nb_rows, LANES), 0).astype(F32)
    b_lane = lax.broadcasted_iota(I32, (nb_rows, LANES), 1)
    done = jnp.where((bend <= b_col) & (b_lane < N_EXPERTS), 1.0, 0.0)
    blk_e = jnp.minimum(jnp.sum(done, axis=-1, keepdims=True), float(N_EXPERTS - 1))
    n_used = jnp.sum(jnp.where(b_lane[0:1, :] == N_EXPERTS - 1, bend, 0.0), axis=-1, keepdims=True)
    bm_ref[...] = jnp.where(b_lane == 0, blk_e, jnp.where(b_lane == 1, n_used, 0.0)).astype(I32)

    e_row = lax.broadcasted_iota(I32, (8, LANES), 0)
    em_ref[...] = jnp.where(e_row == 0, counts,
                            jnp.where(e_row == 1, bstart, jnp.where(e_row == 2, nblk, 0.0))).astype(I32)


def _route(logits, nb_rows):
    t = logits.shape[0]
    full = lambda shape: pl.BlockSpec(shape, lambda: (0, 0))
    return pl.pallas_call(
        _route_kernel,
        out_shape=(jax.ShapeDtypeStruct((t, LANES), I32),
                   jax.ShapeDtypeStruct((t, LANES), F32),
                   jax.ShapeDtypeStruct((nb_rows, LANES), I32),
                   jax.ShapeDtypeStruct((8, LANES), I32)),
        in_specs=[full((t, LANES))],
        out_specs=(full((t, LANES)), full((t, LANES)), full((nb_rows, LANES)), full((8, LANES))),
        compiler_params=pltpu.CompilerParams(vmem_limit_bytes=VMEM_LIMIT),
        name="route",
    )(logits)


DISPATCH_UNROLL = 8
TM_DSP = 256


def _dispatch_kernel(d1_ref, d2_ref, cnt_ref, bstart_ref, nblk_ref, nused_ref, x_ref, xs_ref, sem):
    i = pl.program_id(0)
    tm = x_ref.shape[0]
    nb = xs_ref.shape[0] // EXPERT_BLOCK

    def row_copy(src_row, dst_row):
        return pltpu.make_async_copy(x_ref.at[pl.ds(src_row, 1), :],
                                     xs_ref.at[pl.ds(dst_row, 1), :], sem)

    def block_copy(dst_block):
        dst_row = pl.multiple_of(dst_block * EXPERT_BLOCK, EXPERT_BLOCK)
        return pltpu.make_async_copy(x_ref.at[pl.ds(0, EXPERT_BLOCK), :],
                                     xs_ref.at[pl.ds(dst_row, EXPERT_BLOCK), :], sem)

    def tok_body(r, _):
        tok = i * tm + r
        row_copy(r, d1_ref[tok]).start()
        row_copy(r, d2_ref[tok]).start()
        return 0

    lax.fori_loop(0, tm, tok_body, 0, unroll=DISPATCH_UNROLL)

    @pl.when(i == 0)
    def _():
        def pad_expert(e, n_pad):
            lo = bstart_ref[e] * EXPERT_BLOCK + cnt_ref[e]
            hi = (bstart_ref[e] + nblk_ref[e]) * EXPERT_BLOCK

            def pad_body(r, _):
                row_copy(0, r).start()
                return 0

            lax.fori_loop(lo, hi, pad_body, 0)
            return n_pad + (hi - lo)

        n_pad = lax.fori_loop(0, N_EXPERTS, pad_expert, 0)
        n_used = nused_ref[0]

        def fill_body(b, _):
            block_copy(b).start()
            return 0

        lax.fori_loop(n_used, nb, fill_body, 0)

        def block_wait(_, c):
            block_copy(0).wait()
            return c

        lax.fori_loop(n_used, nb, block_wait, 0)

        def row_wait(_, c):
            row_copy(0, 0).wait()
            return c

        lax.fori_loop(0, n_pad, row_wait, 0)

    for _ in range((2 * tm) // EXPERT_BLOCK):
        block_copy(0).wait()


def _dispatch(dest1, dest2, cnt, bstart, nblk, n_used, x1, n_rows):
    t = x1.shape[0]
    grid_spec = pltpu.PrefetchScalarGridSpec(
        num_scalar_prefetch=6,
        grid=(t // TM_DSP,),
        in_specs=[pl.BlockSpec((TM_DSP, D_MODEL), lambda i, *_: (i, 0))],
        out_specs=pl.BlockSpec(memory_space=pl.ANY),
        scratch_shapes=[pltpu.SemaphoreType.DMA(())],
    )
    return pl.pallas_call(
        _dispatch_kernel,
        out_shape=jax.ShapeDtypeStruct((n_rows, D_MODEL), F32),
        grid_spec=grid_spec,
        compiler_params=pltpu.CompilerParams(
            dimension_semantics=("arbitrary",), vmem_limit_bytes=VMEM_LIMIT),
        name="dispatch",
    )(dest1, dest2, cnt, bstart, nblk, n_used, x1)


def _experts_kernel(blk_e_ref, nused_ref, x_ref, g2_ref, wg_ref, wu_ref, wd_ref, y_ref,
                    wg_s, wu_s, wd_s):
    b = pl.program_id(0)
    used = b < nused_ref[0]
    new_expert = (b == 0) | (blk_e_ref[b] != blk_e_ref[jnp.maximum(b - 1, 0)])

    @pl.when(used & new_expert)
    def _():
        wg_s[...] = wg_ref[0].astype(BF16)
        wu_s[...] = wu_ref[0].astype(BF16)
        wd_s[...] = wd_ref[0].astype(BF16)

    @pl.when(used)
    def _():
        x = x_ref[...]
        ms = jnp.mean(x * x, axis=-1, keepdims=True)
        h = ((x * lax.rsqrt(ms + EPS)) * g2_ref[...]).astype(BF16)
        a = jnp.dot(h, wg_s[...], preferred_element_type=F32)
        u = jnp.dot(h, wu_s[...], preferred_element_type=F32)
        mid = ((a * _sigmoid(a)) * u).astype(BF16)
        y_ref[...] = jnp.dot(mid, wd_s[...], preferred_element_type=F32)

    @pl.when(jnp.logical_not(used))
    def _():
        y_ref[...] = jnp.zeros_like(y_ref)


def _experts(blk_e, n_used, xs, g2, w_g, w_u, w_d):
    n_rows = xs.shape[0]
    nb = n_rows // EXPERT_BLOCK

    def row_blk(b, blk_e_ref, nused_ref):
        return (jnp.minimum(b, nused_ref[0] - 1), 0)

    def w_blk(b, blk_e_ref, nused_ref):
        return (blk_e_ref[b], 0, 0)

    grid_spec = pltpu.PrefetchScalarGridSpec(
        num_scalar_prefetch=2,
        grid=(nb,),
        in_specs=[
            pl.BlockSpec((EXPERT_BLOCK, D_MODEL), row_blk),
            pl.BlockSpec((1, D_MODEL), lambda b, e, n: (0, 0)),
            pl.BlockSpec((1, D_MODEL, D_EXPERT), w_blk),
            pl.BlockSpec((1, D_MODEL, D_EXPERT), w_blk),
            pl.BlockSpec((1, D_EXPERT, D_MODEL), w_blk),
        ],
        out_specs=pl.BlockSpec((EXPERT_BLOCK, D_MODEL), lambda b, e, n: (b, 0)),
        scratch_shapes=[pltpu.VMEM((D_MODEL, D_EXPERT), BF16),
                        pltpu.VMEM((D_MODEL, D_EXPERT), BF16),
                        pltpu.VMEM((D_EXPERT, D_MODEL), BF16)],
    )
    return pl.pallas_call(
        _experts_kernel,
        out_shape=jax.ShapeDtypeStruct((n_rows, D_MODEL), F32),
        grid_spec=grid_spec,
        compiler_params=pltpu.CompilerParams(
            dimension_semantics=("arbitrary",), vmem_limit_bytes=VMEM_LIMIT),
        name="experts",
    )(blk_e, n_used, xs, g2, w_g, w_u, w_d)


COMBINE_UNROLL = 8


def _combine_kernel(d1_ref, d2_ref, x1_ref, w_ref, y_ref, o_ref, gbuf, sem):
    i = pl.program_id(0)
    n = pl.num_programs(0)
    tm = x1_ref.shape[0]

    def issue(tile, slot):
        def body(r, _):
            tok = tile * tm + r
            pltpu.make_async_copy(y_ref.at[pl.ds(d1_ref[tok], 1), :],
                                  gbuf.at[slot, pl.ds(r, 1), :], sem.at[slot]).start()
            pltpu.make_async_copy(y_ref.at[pl.ds(d2_ref[tok], 1), :],
                                  gbuf.at[slot, pl.ds(tm + r, 1), :], sem.at[slot]).start()
            return 0

        lax.fori_loop(0, tm, body, 0, unroll=COMBINE_UNROLL)

    slot = i % 2

    @pl.when(i == 0)
    def _():
        issue(0, 0)

    @pl.when(i + 1 < n)
    def _():
        issue(i + 1, 1 - slot)

    pltpu.make_async_copy(y_ref.at[pl.ds(0, 2 * tm), :], gbuf.at[slot], sem.at[slot]).wait()
    w = w_ref[...]
    g = gbuf[slot]
    o_ref[...] = x1_ref[...] + (w[:, 0:1] * g[0:tm, :] + w[:, 1:2] * g[tm:2 * tm, :])


def _combine(dest1, dest2, x1, w, yb):
    t = x1.shape[0]
    grid_spec = pltpu.PrefetchScalarGridSpec(
        num_scalar_prefetch=2,
        grid=(t // TM_CMB,),
        in_specs=[
            pl.BlockSpec((TM_CMB, D_MODEL), lambda i, a, b: (i, 0)),
            pl.BlockSpec((TM_CMB, LANES), lambda i, a, b: (i, 0)),
            pl.BlockSpec(memory_space=pl.ANY),
        ],
        out_specs=pl.BlockSpec((TM_CMB, D_MODEL), lambda i, a, b: (i, 0)),
        scratch_shapes=[pltpu.VMEM((2, 2 * TM_CMB, D_MODEL), F32),
                        pltpu.SemaphoreType.DMA((2,))],
    )
    return pl.pallas_call(
        _combine_kernel,
        out_shape=jax.ShapeDtypeStruct((t, D_MODEL), F32),
        grid_spec=grid_spec,
        compiler_params=pltpu.CompilerParams(
            dimension_semantics=("arbitrary",), vmem_limit_bytes=VMEM_LIMIT),
        name="combine",
    )(dest1, dest2, x1, w, yb)


def kernel(x, norm1_g, w_in, b_gate, b_forget, sgu_ln_g, sgu_ln_b, w_spatial, b_spatial, q_norm_g, k_norm_g, w_proj_sgu, w_proj_fox, w_out, norm2_g, w_router_group, b_router_group, w_router_expert, b_router_expert, w_expert_gate, w_expert_up, w_expert_down):
    batch, seq, d = x.shape
    t = batch * seq
    l = 0
    x2 = x.reshape(t, d)

    w_all = jnp.concatenate([w_in[l][:, :OFF_F], w_in[l][:, OFF_GATE:]], axis=1).astype(BF16)
    w_f = jnp.pad(w_in[l][:, OFF_F:OFF_GATE], ((0, 0), (0, LANES - FOX_HEADS))).astype(BF16)
    b_f = jnp.pad(b_forget[l], (0, LANES - FOX_HEADS)).reshape(1, LANES)
    n_r = N_GROUPS + N_EXPERTS
    w_r = jnp.pad(jnp.concatenate([w_router_group[l], w_router_expert[l]], axis=1),
                  ((0, 0), (0, LANES - n_r))).astype(BF16)
    b_r = jnp.pad(jnp.concatenate([b_router_group[l], b_router_expert[l]]),
                  (0, LANES - n_r)).reshape(1, LANES)

    a_sgu, q, kt, va, gates, lf = _inproj(
        x2, norm1_g[l].reshape(1, d), w_all, w_f, b_f, b_gate[l].reshape(1, 2 * d),
        sgu_ln_g[l].reshape(1, SGU_WIDTH), sgu_ln_b[l].reshape(1, SGU_WIDTH),
        w_spatial[l], b_spatial[l].T, q_norm_g[l].reshape(1, HEAD_DIM),
        k_norm_g[l].reshape(1, HEAD_DIM))
    negc = _forget_cumsum(lf, batch, seq).reshape(batch, FOX_HEADS, 1, seq)
    a_fox = _attention(q, kt, va, negc, batch, seq)
    x1, logits = _mix(a_sgu, a_fox, gates, x2, w_proj_sgu[l].astype(BF16),
                      w_proj_fox[l].astype(BF16), w_out[l].astype(BF16),
                      norm2_g[l].reshape(1, d), w_r, b_r)

    n_assign = 2 * t
    n_rows = n_assign + N_EXPERTS * EXPERT_BLOCK
    nb = n_rows // EXPERT_BLOCK
    nb_rows = -(-nb // 8) * 8
    meta_i, meta_f, bmeta, emeta = _route(logits, nb_rows)
    dest1, dest2 = meta_i[:, 4], meta_i[:, 5]
    n_used = bmeta[0:1, 1]
    xs = _dispatch(dest1, dest2, emeta[0, :N_EXPERTS], emeta[1, :N_EXPERTS], emeta[2, :N_EXPERTS],
                   n_used, x1, n_rows)
    yb = _experts(bmeta[:nb, 0], n_used, xs, norm2_g[l].reshape(1, d),
                  w_expert_gate[l], w_expert_up[l], w_expert_down[l])
    out = _combine(dest1, dest2, x1, meta_f, yb)
    return out.reshape(batch, seq, d)
```

```python
import functools

import jax
import jax.numpy as jnp
from jax import lax
from jax.experimental import pallas as pl
from jax.experimental.pallas import tpu as pltpu

F32 = jnp.float32
BF16 = jnp.bfloat16
I32 = jnp.int32

D_MODEL = 2048
CHUNK = 128
SGU_GROUPS = 8
SGU_WIDTH = 1024
FOX_HEADS = 8
HEAD_DIM = 128
FOX_WIDTH = 1024
N_GROUPS = 4
EXPERTS_PER_GROUP = 8
N_EXPERTS = 32
D_EXPERT = 512
EXPERT_BLOCK = 128
EPS = 1e-6

OFF_F = 5 * 1024
OFF_GATE = OFF_F + FOX_HEADS

LANES = 128
VMEM_LIMIT = 56 * 1024 * 1024

SEC = 1024
N_MAIN_SEC = 5
N_GATE_SEC = 4
TM_IN = 512
TQ = 256
TM_MIX = 256
RCH = 256
TM_CMB = 256
ATTN_SCALE = HEAD_DIM ** -0.5


def _sigmoid(x):
    return 1.0 / (1.0 + jnp.exp(-x))


def _log_sigmoid(x):
    return jnp.minimum(x, 0.0) - jnp.log1p(jnp.exp(-jnp.abs(x)))


def _inproj_kernel(x_ref, g1_ref, w_ref, wf_ref, bf_ref, bg_ref, lng_ref, lnb_ref, wsp_ref,
                   bsp_ref, qg_ref, kg_ref,
                   asgu_ref, q_ref, kt_ref, va_ref, gates_ref, lf_ref,
                   h_s, u_s):
    j = pl.program_id(1)
    tm = x_ref.shape[0]

    @pl.when(j == 0)
    def _():
        x = x_ref[...]
        ms = jnp.mean(x * x, axis=-1, keepdims=True)
        hb = ((x * lax.rsqrt(ms + EPS)) * g1_ref[...]).astype(BF16)
        h_s[...] = hb
        f = jnp.dot(hb, wf_ref[...], preferred_element_type=F32) + bf_ref[...]
        lf_ref[...] = _log_sigmoid(f)

    def section():
        return jnp.dot(h_s[...], w_ref[...], preferred_element_type=F32)

    @pl.when(j == 0)
    def _():
        u_s[...] = jax.nn.gelu(section())

    @pl.when(j == 1)
    def _():
        v = jax.nn.gelu(section())
        mu = jnp.mean(v, axis=-1, keepdims=True)
        vc = v - mu
        var = jnp.mean(vc * vc, axis=-1, keepdims=True)
        vn = ((vc * lax.rsqrt(var + EPS)) * lng_ref[...] + lnb_ref[...]).astype(BF16)
        row = lax.broadcasted_iota(I32, (CHUNK, CHUNK), 0)
        col = lax.broadcasted_iota(I32, (CHUNK, CHUNK), 1)
        causal = row >= col
        for g in range(SGU_GROUPS):
            wg = jnp.where(causal, wsp_ref[g], 0.0).astype(BF16)
            bcol = bsp_ref[:, g:g + 1]
            gs = slice(g * LANES, (g + 1) * LANES)
            for c in range(tm // CHUNK):
                rs = slice(c * CHUNK, (c + 1) * CHUNK)
                s = jnp.dot(wg, vn[rs, gs], preferred_element_type=F32) + bcol
                asgu_ref[rs, gs] = (u_s[rs, gs] * s).astype(BF16)

    def _head_norm(z, h, gain_ref):
        zh = z[:, h * HEAD_DIM:(h + 1) * HEAD_DIM]
        ms = jnp.mean(zh * zh, axis=-1, keepdims=True)
        return (zh * lax.rsqrt(ms + EPS)) * gain_ref[...]

    @pl.when(j == 2)
    def _():
        z = section()
        for h in range(FOX_HEADS):
            q_ref[:, h * HEAD_DIM:(h + 1) * HEAD_DIM] = (
                _head_norm(z, h, qg_ref) * ATTN_SCALE).astype(BF16)

    @pl.when(j == 3)
    def _():
        z = section()
        for h in range(FOX_HEADS):
            kt_ref[h * HEAD_DIM:(h + 1) * HEAD_DIM, :] = _head_norm(z, h, kg_ref).T.astype(BF16)

    @pl.when(j == 4)
    def _():
        va_ref[...] = section().astype(BF16)

    @pl.when(j >= N_MAIN_SEC)
    def _():
        gates_ref[...] = _sigmoid(section() + bg_ref[...]).astype(BF16)


def _inproj(x2, g1, w_all, w_f, b_f, b_gate, ln_g, ln_b, w_sp, b_sp_t, q_g, k_g):
    t = x2.shape[0]
    n_sec = N_MAIN_SEC + N_GATE_SEC
    gate_blk = lambda i, j: (i, jnp.maximum(j - N_MAIN_SEC, 0))
    row_blk = lambda i, j: (i, 0)
    const2 = lambda i, j: (0, 0)
    out_shape = (
        jax.ShapeDtypeStruct((t, SGU_WIDTH), BF16),
        jax.ShapeDtypeStruct((t, FOX_WIDTH), BF16),
        jax.ShapeDtypeStruct((FOX_WIDTH, t), BF16),
        jax.ShapeDtypeStruct((t, FOX_WIDTH), BF16),
        jax.ShapeDtypeStruct((t, 2 * D_MODEL), BF16),
        jax.ShapeDtypeStruct((t, LANES), F32),
    )
    return pl.pallas_call(
        _inproj_kernel,
        out_shape=out_shape,
        grid=(t // TM_IN, n_sec),
        in_specs=[
            pl.BlockSpec((TM_IN, D_MODEL), row_blk),
            pl.BlockSpec((1, D_MODEL), const2),
            pl.BlockSpec((D_MODEL, SEC), lambda i, j: (0, j)),
            pl.BlockSpec((D_MODEL, LANES), const2),
            pl.BlockSpec((1, LANES), const2),
            pl.BlockSpec((1, SEC), lambda i, j: (0, jnp.maximum(j - N_MAIN_SEC, 0))),
            pl.BlockSpec((1, SGU_WIDTH), const2),
            pl.BlockSpec((1, SGU_WIDTH), const2),
            pl.BlockSpec((SGU_GROUPS, CHUNK, CHUNK), lambda i, j: (0, 0, 0)),
            pl.BlockSpec((CHUNK, SGU_GROUPS), const2),
            pl.BlockSpec((1, HEAD_DIM), const2),
            pl.BlockSpec((1, HEAD_DIM), const2),
        ],
        out_specs=(
            pl.BlockSpec((TM_IN, SGU_WIDTH), row_blk),
            pl.BlockSpec((TM_IN, FOX_WIDTH), row_blk),
            pl.BlockSpec((FOX_WIDTH, TM_IN), lambda i, j: (0, i)),
            pl.BlockSpec((TM_IN, FOX_WIDTH), row_blk),
            pl.BlockSpec((TM_IN, SEC), gate_blk),
            pl.BlockSpec((TM_IN, LANES), row_blk),
        ),
        scratch_shapes=[pltpu.VMEM((TM_IN, D_MODEL), BF16), pltpu.VMEM((TM_IN, SEC), F32)],
        compiler_params=pltpu.CompilerParams(
            dimension_semantics=("arbitrary", "arbitrary"), vmem_limit_bytes=VMEM_LIMIT),
        name="inproj",
    )(x2, g1, w_all, w_f, b_f, b_gate, ln_g, ln_b, w_sp, b_sp_t, q_g, k_g)


def _split3(x):
    x0 = x.astype(BF16)
    r1 = x - x0.astype(F32)
    x1 = r1.astype(BF16)
    r2 = r1 - x1.astype(F32)
    return x0, x1, r2.astype(BF16)


def _cumsum_kernel(lf_ref, negc_ref, c_s):
    seq = lf_ref.shape[0]
    row = lax.broadcasted_iota(I32, (CHUNK, CHUNK), 0)
    col = lax.broadcasted_iota(I32, (CHUNK, CHUNK), 1)
    tri = jnp.where(row >= col, 1.0, 0.0).astype(BF16)
    carry = jnp.zeros((1, LANES), F32)
    for r in range(seq // CHUNK):
        rs = slice(r * CHUNK, (r + 1) * CHUNK)
        x0, x1, x2 = _split3(lf_ref[rs, :])
        cs = (jnp.dot(tri, x0, preferred_element_type=F32)
              + jnp.dot(tri, x1, preferred_element_type=F32)
              + jnp.dot(tri, x2, preferred_element_type=F32)) + carry
        carry = cs[CHUNK - 1:CHUNK, :]
        c_s[rs, :] = cs
    ct = c_s[...].T
    negc_ref[0] = -ct[0:FOX_HEADS, :]


def _forget_cumsum(lf, batch, seq):
    return pl.pallas_call(
        _cumsum_kernel,
        out_shape=jax.ShapeDtypeStruct((batch, FOX_HEADS, seq), F32),
        grid=(batch,),
        in_specs=[pl.BlockSpec((seq, LANES), lambda b: (b, 0))],
        out_specs=pl.BlockSpec((1, FOX_HEADS, seq), lambda b: (b, 0, 0)),
        scratch_shapes=[pltpu.VMEM((seq, LANES), F32)],
        compiler_params=pltpu.CompilerParams(
            dimension_semantics=("arbitrary",), vmem_limit_bytes=VMEM_LIMIT),
        name="forget_cumsum",
    )(lf)


def _attn_kernel(q_ref, kt_ref, v_ref, negc_ref, o_ref):
    seq = q_ref.shape[0]
    row = lax.broadcasted_iota(I32, (TQ, TQ), 0)
    col = lax.broadcasted_iota(I32, (TQ, TQ), 1)
    causal = row >= col
    for qi in range(seq // TQ):
        k0 = qi * TQ
        q = q_ref[k0:k0 + TQ, :]
        s_d = jnp.dot(q, kt_ref[:, k0:k0 + TQ], preferred_element_type=F32)
        s_d = jnp.where(causal, s_d + negc_ref[0, 0, :, k0:k0 + TQ], -jnp.inf)
        m = jnp.max(s_d, axis=-1, keepdims=True)
        if qi > 0:
            s_o = jnp.dot(q, kt_ref[:, 0:k0], preferred_element_type=F32) + negc_ref[0, 0, :, 0:k0]
            m = jnp.maximum(m, jnp.max(s_o, axis=-1, keepdims=True))
        p_d = jnp.exp(s_d - m)
        l = jnp.sum(p_d, axis=-1, keepdims=True)
        acc = jnp.dot(p_d.astype(BF16), v_ref[k0:k0 + TQ, :], preferred_element_type=F32)
        if qi > 0:
            p_o = jnp.exp(s_o - m)
            l = l + jnp.sum(p_o, axis=-1, keepdims=True)
            acc = acc + jnp.dot(p_o.astype(BF16), v_ref[0:k0, :], preferred_element_type=F32)
        o_ref[k0:k0 + TQ, :] = (acc * (1.0 / l)).astype(BF16)


def _attention(q, kt, va, negc, batch, seq):
    t = q.shape[0]
    blk = pl.BlockSpec((seq, HEAD_DIM), lambda b, h: (b, h))
    return pl.pallas_call(
        _attn_kernel,
        out_shape=jax.ShapeDtypeStruct((t, FOX_WIDTH), BF16),
        grid=(batch, FOX_HEADS),
        in_specs=[blk,
                  pl.BlockSpec((HEAD_DIM, seq), lambda b, h: (h, b)),
                  blk,
                  pl.BlockSpec((1, 1, 1, seq), lambda b, h: (b, h, 0, 0))],
        out_specs=blk,
        compiler_params=pltpu.CompilerParams(
            dimension_semantics=("arbitrary", "arbitrary"), vmem_limit_bytes=VMEM_LIMIT),
        name="fox_attention",
    )(q, kt, va, negc)


def _mix_kernel(as_ref, af_ref, g_ref, x_ref, wps_ref, wpf_ref, wo_ref, g2_ref, wr_ref, br_ref,
                x1_ref, lg_ref):
    ys = jnp.dot(as_ref[...], wps_ref[...], preferred_element_type=F32)
    yf = jnp.dot(af_ref[...], wpf_ref[...], preferred_element_type=F32)
    m = (g_ref[:, :D_MODEL].astype(F32) * ys + g_ref[:, D_MODEL:].astype(F32) * yf).astype(BF16)
    x1 = x_ref[...] + jnp.dot(m, wo_ref[...], preferred_element_type=F32)
    x1_ref[...] = x1
    ms = jnp.mean(x1 * x1, axis=-1, keepdims=True)
    h2 = ((x1 * lax.rsqrt(ms + EPS)) * g2_ref[...]).astype(BF16)
    lg_ref[...] = jnp.dot(h2, wr_ref[...], preferred_element_type=F32) + br_ref[...]


def _mix(a_sgu, a_fox, gates, x2, wps, wpf, wo, g2, w_r, b_r):
    t = x2.shape[0]
    row = lambda i: (i, 0)
    const = lambda i: (0, 0)
    resident = functools.partial(pl.BlockSpec, index_map=const, pipeline_mode=pl.Buffered(1))
    return pl.pallas_call(
        _mix_kernel,
        out_shape=(jax.ShapeDtypeStruct((t, D_MODEL), F32),
                   jax.ShapeDtypeStruct((t, LANES), F32)),
        grid=(t // TM_MIX,),
        in_specs=[
            pl.BlockSpec((TM_MIX, SGU_WIDTH), row),
            pl.BlockSpec((TM_MIX, FOX_WIDTH), row),
            pl.BlockSpec((TM_MIX, 2 * D_MODEL), row),
            pl.BlockSpec((TM_MIX, D_MODEL), row),
            resident((SGU_WIDTH, D_MODEL)),
            resident((FOX_WIDTH, D_MODEL)),
            resident((D_MODEL, D_MODEL)),
            pl.BlockSpec((1, D_MODEL), const),
            resident((D_MODEL, LANES)),
            pl.BlockSpec((1, LANES), const),
        ],
        out_specs=(pl.BlockSpec((TM_MIX, D_MODEL), row), pl.BlockSpec((TM_MIX, LANES), row)),
        compiler_params=pltpu.CompilerParams(
            dimension_semantics=("arbitrary",), vmem_limit_bytes=VMEM_LIMIT),
        name="mix",
    )(a_sgu, a_fox, gates, x2, wps, wpf, wo, g2, w_r, b_r)


def _route_kernel(lg_ref, dt_ref, mf_ref, bm_ref, em_ref, mi_ref):
    t = lg_ref.shape[0]
    n_chunks = t // RCH
    lane_i = lax.broadcasted_iota(I32, (RCH, LANES), 1)
    lane = lane_i.astype(F32)
    lane_grp = ((lane_i - N_GROUPS) >> 3).astype(F32)
    is_grp = lane_i < N_GROUPS
    is_exp = (lane_i >= N_GROUPS) & (lane_i < N_GROUPS + N_EXPERTS)
    r_i = lax.broadcasted_iota(I32, (RCH, RCH), 0)
    c_i = lax.broadcasted_iota(I32, (RCH, RCH), 1)
    strict_lower = jnp.where(r_i > c_i, 1.0, 0.0).astype(BF16)
    neg_inf = -jnp.inf

    def first_max(vals):
        vmax = jnp.max(vals, axis=-1, keepdims=True)
        idx = jnp.min(jnp.where(vals == vmax, lane, float(LANES)), axis=-1, keepdims=True)
        return vmax, idx

    def pick(table, idx):
        return jnp.sum(jnp.where(lane == idx, table, 0.0), axis=-1, keepdims=True)

    def pack(cols):
        out = jnp.zeros((RCH, LANES), F32)
        for n, c in enumerate(cols):
            out = jnp.where(lane_i == n, c, out)
        return out

    def pass1(ci, counts):
        r0 = pl.multiple_of(ci * RCH, RCH)
        lg = lg_ref[pl.ds(r0, RCH), :]
        gmax, grp = first_max(jnp.where(is_grp, lg, neg_inf))
        p_grp = 1.0 / jnp.sum(jnp.where(is_grp, jnp.exp(lg - gmax), 0.0), axis=-1, keepdims=True)
        el = jnp.where(is_exp & (lane_grp == grp), lg, neg_inf)
        v1, i1 = first_max(el)
        v2, i2 = first_max(jnp.where(lane == i1, neg_inf, el))
        e21 = jnp.exp(v2 - v1)
        w1 = p_grp / (1.0 + e21)
        w2 = p_grp * e21 / (1.0 + e21)
        e1 = i1 - float(N_GROUPS)
        e2 = i2 - float(N_GROUPS)
        hot = jnp.where((lane == e1) | (lane == e2), 1.0, 0.0)
        before = jnp.dot(strict_lower, hot.astype(BF16), preferred_element_type=F32) + counts
        mi_ref[pl.ds(r0, RCH), :] = pack([e1, e2, pick(before, e1), pick(before, e2)]).astype(I32)
        mf_ref[pl.ds(r0, RCH), :] = pack([w1, w2])
        return counts + jnp.sum(hot, axis=0, keepdims=True)

    counts = lax.fori_loop(0, n_chunks, pass1, jnp.zeros((1, LANES), F32))

    nblk = jnp.floor((counts + float(EXPERT_BLOCK - 1)) * (1.0 / EXPERT_BLOCK))
    u_r = lax.broadcasted_iota(I32, (LANES, LANES), 0)
    u_c = lax.broadcasted_iota(I32, (LANES, LANES), 1)
    strict_upper = jnp.where(u_r < u_c, 1.0, 0.0).astype(BF16)
    bstart = jnp.dot(jnp.broadcast_to(nblk, (8, LANES)).astype(BF16), strict_upper,
                     preferred_element_type=F32)[0:1, :]
    bend = bstart + nblk

    def pass2(ci, _):
        r0 = pl.multiple_of(ci * RCH, RCH)
        mi = mi_ref[pl.ds(r0, RCH), :].astype(F32)
        cols = [jnp.sum(jnp.where(lane_i == n, mi, 0.0), axis=-1, keepdims=True) for n in range(4)]
        d1 = pick(bstart, cols[0]) * float(EXPERT_BLOCK) + cols[2]
        d2 = pick(bstart, cols[1]) * float(EXPERT_BLOCK) + cols[3]
        dt_ref[ci] = pack([d1, d2]).T[0:8, :].astype(I32)
        return 0

    lax.fori_loop(0, n_chunks, pass2, 0)

    nb_rows = bm_ref.shape[0]
    b_col = lax.broadcasted_iota(I32, (nb_rows, LANES), 0).astype(F32)
    b_lane = lax.broadcasted_iota(I32, (nb_rows, LANES), 1)
    done = jnp.where((bend <= b_col) & (b_lane < N_EXPERTS), 1.0, 0.0)
    blk_e = jnp.minimum(jnp.sum(done, axis=-1, keepdims=True), float(N_EXPERTS - 1))
    n_used = jnp.sum(jnp.where(b_lane[0:1, :] == N_EXPERTS - 1, bend, 0.0), axis=-1, keepdims=True)
    bm_ref[...] = jnp.where(b_lane == 0, blk_e, jnp.where(b_lane == 1, n_used, 0.0)).astype(I32)

    e_row = lax.broadcasted_iota(I32, (8, LANES), 0)
    em_ref[...] = jnp.where(e_row == 0, counts,
                            jnp.where(e_row == 1, bstart, jnp.where(e_row == 2, nblk, 0.0))).astype(I32)


def _route(logits, nb_rows):
    t = logits.shape[0]
    full = lambda shape: pl.BlockSpec(shape, lambda: (0,) * len(shape))
    return pl.pallas_call(
        _route_kernel,
        out_shape=(jax.ShapeDtypeStruct((t // RCH, 8, RCH), I32),
                   jax.ShapeDtypeStruct((t, LANES), F32),
                   jax.ShapeDtypeStruct((nb_rows, LANES), I32),
                   jax.ShapeDtypeStruct((8, LANES), I32)),
        in_specs=[full((t, LANES))],
        out_specs=(full((t // RCH, 8, RCH)), full((t, LANES)), full((nb_rows, LANES)),
                   full((8, LANES))),
        scratch_shapes=[pltpu.VMEM((t, LANES), I32)],
        compiler_params=pltpu.CompilerParams(vmem_limit_bytes=VMEM_LIMIT),
        name="route",
    )(logits)


DISPATCH_UNROLL = 8
TM_DSP = 256


def _dispatch_kernel(d1_ref, d2_ref, cnt_ref, bstart_ref, nblk_ref, nused_ref, x_ref, xs_ref, sem):
    i = pl.program_id(0)
    tm = x_ref.shape[0]
    nb = xs_ref.shape[0] // EXPERT_BLOCK

    def row_copy(src_row, dst_row):
        return pltpu.make_async_copy(x_ref.at[pl.ds(src_row, 1), :],
                                     xs_ref.at[pl.ds(dst_row, 1), :], sem)

    def block_copy(dst_block):
        dst_row = pl.multiple_of(dst_block * EXPERT_BLOCK, EXPERT_BLOCK)
        return pltpu.make_async_copy(x_ref.at[pl.ds(0, EXPERT_BLOCK), :],
                                     xs_ref.at[pl.ds(dst_row, EXPERT_BLOCK), :], sem)

    def tok_body(r, _):
        tok = i * tm + r
        row_copy(r, d1_ref[tok]).start()
        row_copy(r, d2_ref[tok]).start()
        return 0

    lax.fori_loop(0, tm, tok_body, 0, unroll=DISPATCH_UNROLL)

    @pl.when(i == 0)
    def _():
        def pad_expert(e, n_pad):
            lo = bstart_ref[e] * EXPERT_BLOCK + cnt_ref[e]
            hi = (bstart_ref[e] + nblk_ref[e]) * EXPERT_BLOCK

            def pad_body(r, _):
                row_copy(0, r).start()
                return 0

            lax.fori_loop(lo, hi, pad_body, 0)
            return n_pad + (hi - lo)

        n_pad = lax.fori_loop(0, N_EXPERTS, pad_expert, 0)
        n_used = nused_ref[0]

        def fill_body(b, _):
            block_copy(b).start()
            return 0

        lax.fori_loop(n_used, nb, fill_body, 0)

        def block_wait(_, c):
            block_copy(0).wait()
            return c

        lax.fori_loop(n_used, nb, block_wait, 0)

        def row_wait(_, c):
            row_copy(0, 0).wait()
            return c

        lax.fori_loop(0, n_pad, row_wait, 0)

    for _ in range((2 * tm) // EXPERT_BLOCK):
        block_copy(0).wait()


def _dispatch(dest1, dest2, cnt, bstart, nblk, n_used, x1, n_rows):
    t = x1.shape[0]
    grid_spec = pltpu.PrefetchScalarGridSpec(
        num_scalar_prefetch=6,
        grid=(t // TM_DSP,),
        in_specs=[pl.BlockSpec((TM_DSP, D_MODEL), lambda i, *_: (i, 0))],
        out_specs=pl.BlockSpec(memory_space=pl.ANY),
        scratch_shapes=[pltpu.SemaphoreType.DMA(())],
    )
    return pl.pallas_call(
        _dispatch_kernel,
        out_shape=jax.ShapeDtypeStruct((n_rows, D_MODEL), F32),
        grid_spec=grid_spec,
        compiler_params=pltpu.CompilerParams(
            dimension_semantics=("arbitrary",), vmem_limit_bytes=VMEM_LIMIT),
        name="dispatch",
    )(dest1, dest2, cnt, bstart, nblk, n_used, x1)


def _experts_kernel(blk_e_ref, nused_ref, nblk_ref, x_ref, g2_ref, wg_hbm, wu_hbm, wd_hbm, y_ref,
                    wg_f, wu_f, wd_f, wg_s, wu_s, wd_s, slot_ref, sem):
    b = pl.program_id(0)
    n_used = nused_ref[0]
    used = b < n_used
    e = blk_e_ref[b]
    new_expert = (b == 0) | (e != blk_e_ref[jnp.maximum(b - 1, 0)])

    def weight_copies(expert, slot):
        return (pltpu.make_async_copy(wg_hbm.at[expert], wg_f.at[slot], sem.at[slot]),
                pltpu.make_async_copy(wu_hbm.at[expert], wu_f.at[slot], sem.at[slot]),
                pltpu.make_async_copy(wd_hbm.at[expert], wd_f.at[slot], sem.at[slot]))

    @pl.when(b == 0)
    def _():
        slot_ref[0] = 0
        for cp in weight_copies(e, 0):
            cp.start()

    @pl.when(used & new_expert)
    def _():
        slot = slot_ref[0]
        b_next = b + nblk_ref[e]

        @pl.when(b_next < n_used)
        def _():
            for cp in weight_copies(blk_e_ref[jnp.minimum(b_next, n_used - 1)], 1 - slot):
                cp.start()

        for cp in weight_copies(e, slot):
            cp.wait()
        wg_s[...] = wg_f[slot].astype(BF16)
        wu_s[...] = wu_f[slot].astype(BF16)
        wd_s[...] = wd_f[slot].astype(BF16)
        slot_ref[0] = 1 - slot

    @pl.when(used)
    def _():
        x = x_ref[...]
        ms = jnp.mean(x * x, axis=-1, keepdims=True)
        h = ((x * lax.rsqrt(ms + EPS)) * g2_ref[...]).astype(BF16)
        a = jnp.dot(h, wg_s[...], preferred_element_type=F32)
        u = jnp.dot(h, wu_s[...], preferred_element_type=F32)
        mid = ((a * _sigmoid(a)) * u).astype(BF16)
        y_ref[...] = jnp.dot(mid, wd_s[...], preferred_element_type=F32)

    @pl.when(jnp.logical_not(used))
    def _():
        y_ref[...] = jnp.zeros_like(y_ref)


def _experts(blk_e, n_used, nblk, xs, g2, w_g, w_u, w_d):
    n_rows = xs.shape[0]
    nb = n_rows // EXPERT_BLOCK

    def row_blk(b, blk_e_ref, nused_ref, nblk_ref):
        return (jnp.minimum(b, nused_ref[0] - 1), 0)

    hbm = pl.BlockSpec(memory_space=pl.ANY)
    grid_spec = pltpu.PrefetchScalarGridSpec(
        num_scalar_prefetch=3,
        grid=(nb,),
        in_specs=[
            pl.BlockSpec((EXPERT_BLOCK, D_MODEL), row_blk),
            pl.BlockSpec((1, D_MODEL), lambda b, *_: (0, 0)),
            hbm, hbm, hbm,
        ],
        out_specs=pl.BlockSpec((EXPERT_BLOCK, D_MODEL), lambda b, *_: (b, 0)),
        scratch_shapes=[pltpu.VMEM((2, D_MODEL, D_EXPERT), F32),
                        pltpu.VMEM((2, D_MODEL, D_EXPERT), F32),
                        pltpu.VMEM((2, D_EXPERT, D_MODEL), F32),
                        pltpu.VMEM((D_MODEL, D_EXPERT), BF16),
                        pltpu.VMEM((D_MODEL, D_EXPERT), BF16),
                        pltpu.VMEM((D_EXPERT, D_MODEL), BF16),
                        pltpu.SMEM((1,), I32),
                        pltpu.SemaphoreType.DMA((2,))],
    )
    return pl.pallas_call(
        _experts_kernel,
        out_shape=jax.ShapeDtypeStruct((n_rows, D_MODEL), F32),
        grid_spec=grid_spec,
        compiler_params=pltpu.CompilerParams(
            dimension_semantics=("arbitrary",), vmem_limit_bytes=VMEM_LIMIT),
        name="experts",
    )(blk_e, n_used, nblk, xs, g2, w_g, w_u, w_d)


COMBINE_UNROLL = 8


def _combine_kernel(d1_ref, d2_ref, x1_ref, w_ref, y_ref, o_ref, gbuf, sem):
    i = pl.program_id(0)
    n = pl.num_programs(0)
    tm = x1_ref.shape[0]

    def issue(tile, slot):
        def body(r, _):
            tok = tile * tm + r
            pltpu.make_async_copy(y_ref.at[pl.ds(d1_ref[tok], 1), :],
                                  gbuf.at[slot, pl.ds(r, 1), :], sem.at[slot]).start()
            pltpu.make_async_copy(y_ref.at[pl.ds(d2_ref[tok], 1), :],
                                  gbuf.at[slot, pl.ds(tm + r, 1), :], sem.at[slot]).start()
            return 0

        lax.fori_loop(0, tm, body, 0, unroll=COMBINE_UNROLL)

    slot = i % 2

    @pl.when(i == 0)
    def _():
        issue(0, 0)

    @pl.when(i + 1 < n)
    def _():
        issue(i + 1, 1 - slot)

    pltpu.make_async_copy(y_ref.at[pl.ds(0, 2 * tm), :], gbuf.at[slot], sem.at[slot]).wait()
    w = w_ref[...]
    g = gbuf[slot]
    o_ref[...] = x1_ref[...] + (w[:, 0:1] * g[0:tm, :] + w[:, 1:2] * g[tm:2 * tm, :])


def _combine(dest1, dest2, x1, w, yb):
    t = x1.shape[0]
    grid_spec = pltpu.PrefetchScalarGridSpec(
        num_scalar_prefetch=2,
        grid=(t // TM_CMB,),
        in_specs=[
            pl.BlockSpec((TM_CMB, D_MODEL), lambda i, a, b: (i, 0)),
            pl.BlockSpec((TM_CMB, LANES), lambda i, a, b: (i, 0)),
            pl.BlockSpec(memory_space=pl.ANY),
        ],
        out_specs=pl.BlockSpec((TM_CMB, D_MODEL), lambda i, a, b: (i, 0)),
        scratch_shapes=[pltpu.VMEM((2, 2 * TM_CMB, D_MODEL), F32),
                        pltpu.SemaphoreType.DMA((2,))],
    )
    return pl.pallas_call(
        _combine_kernel,
        out_shape=jax.ShapeDtypeStruct((t, D_MODEL), F32),
        grid_spec=grid_spec,
        compiler_params=pltpu.CompilerParams(
            dimension_semantics=("arbitrary",), vmem_limit_bytes=VMEM_LIMIT),
        name="combine",
    )(dest1, dest2, x1, w, yb)


def kernel(x, norm1_g, w_in, b_gate, b_forget, sgu_ln_g, sgu_ln_b, w_spatial, b_spatial, q_norm_g, k_norm_g, w_proj_sgu, w_proj_fox, w_out, norm2_g, w_router_group, b_router_group, w_router_expert, b_router_expert, w_expert_gate, w_expert_up, w_expert_down):
    batch, seq, d = x.shape
    t = batch * seq
    l = 0
    x2 = x.reshape(t, d)

    w_all = jnp.concatenate([w_in[l][:, :OFF_F], w_in[l][:, OFF_GATE:]], axis=1).astype(BF16)
    w_f = jnp.pad(w_in[l][:, OFF_F:OFF_GATE], ((0, 0), (0, LANES - FOX_HEADS))).astype(BF16)
    b_f = jnp.pad(b_forget[l], (0, LANES - FOX_HEADS)).reshape(1, LANES)
    n_r = N_GROUPS + N_EXPERTS
    w_r = jnp.pad(jnp.concatenate([w_router_group[l], w_router_expert[l]], axis=1),
                  ((0, 0), (0, LANES - n_r))).astype(BF16)
    b_r = jnp.pad(jnp.concatenate([b_router_group[l], b_router_expert[l]]),
                  (0, LANES - n_r)).reshape(1, LANES)

    a_sgu, q, kt, va, gates, lf = _inproj(
        x2, norm1_g[l].reshape(1, d), w_all, w_f, b_f, b_gate[l].reshape(1, 2 * d),
        sgu_ln_g[l].reshape(1, SGU_WIDTH), sgu_ln_b[l].reshape(1, SGU_WIDTH),
        w_spatial[l], b_spatial[l].T, q_norm_g[l].reshape(1, HEAD_DIM),
        k_norm_g[l].reshape(1, HEAD_DIM))
    negc = _forget_cumsum(lf, batch, seq).reshape(batch, FOX_HEADS, 1, seq)
    a_fox = _attention(q, kt, va, negc, batch, seq)
    x1, logits = _mix(a_sgu, a_fox, gates, x2, w_proj_sgu[l].astype(BF16),
                      w_proj_fox[l].astype(BF16), w_out[l].astype(BF16),
                      norm2_g[l].reshape(1, d), w_r, b_r)

    n_assign = 2 * t
    n_rows = n_assign + N_EXPERTS * EXPERT_BLOCK
    nb = n_rows // EXPERT_BLOCK
    nb_rows = -(-nb // 8) * 8
    dest_t, meta_f, bmeta, emeta = _route(logits, nb_rows)
    dest1, dest2 = dest_t[:, 0, :].reshape(t), dest_t[:, 1, :].reshape(t)
    n_used = bmeta[0:1, 1]
    xs = _dispatch(dest1, dest2, emeta[0, :N_EXPERTS], emeta[1, :N_EXPERTS], emeta[2, :N_EXPERTS],
                   n_used, x1, n_rows)
    yb = _experts(bmeta[:nb, 0], n_used, emeta[2, :N_EXPERTS], xs, norm2_g[l].reshape(1, d),
                  w_expert_gate[l], w_expert_up[l], w_expert_down[l])
    out = _combine(dest1, dest2, x1, meta_f, yb)
    return out.reshape(batch, seq, d)
```

```python
import functools

import jax
import jax.numpy as jnp
from jax import lax
from jax.experimental import pallas as pl
from jax.experimental.pallas import tpu as pltpu

F32 = jnp.float32
BF16 = jnp.bfloat16
I32 = jnp.int32

D_MODEL = 2048
CHUNK = 128
SGU_GROUPS = 8
SGU_WIDTH = 1024
FOX_HEADS = 8
HEAD_DIM = 128
FOX_WIDTH = 1024
N_GROUPS = 4
EXPERTS_PER_GROUP = 8
N_EXPERTS = 32
D_EXPERT = 512
EXPERT_BLOCK = 128
EPS = 1e-6


LANES = 128
VMEM_LIMIT = 56 * 1024 * 1024

SEC = 1024
N_MAIN_SEC = 5
N_GATE_SEC = 4
TM_IN = 512
TQ = 256
TM_MIX = 256
RCH = 256
TM_CMB = 512
ATTN_SCALE = HEAD_DIM ** -0.5


def _sigmoid(x):
    return 1.0 / (1.0 + jnp.exp(-x))


def _log_sigmoid(x):
    return jnp.minimum(x, 0.0) - jnp.log1p(jnp.exp(-jnp.abs(x)))


def _wprep_kernel(a_ref, b_ref, w_ref, wf_ref):
    j = pl.program_id(0)

    @pl.when(j < N_MAIN_SEC)
    def _():
        w_ref[...] = a_ref[...].astype(BF16)

    @pl.when(j == N_MAIN_SEC)
    def _():
        wf_ref[...] = a_ref[:, 0:LANES].astype(BF16)

    @pl.when(j >= N_MAIN_SEC)
    def _():
        w_ref[...] = jnp.concatenate(
            [a_ref[:, FOX_HEADS:], b_ref[:, 0:FOX_HEADS]], axis=1).astype(BF16)


def _wprep(w_in2):
    d = w_in2.shape[0]
    n_sec = N_MAIN_SEC + N_GATE_SEC
    return pl.pallas_call(
        _wprep_kernel,
        out_shape=(jax.ShapeDtypeStruct((d, n_sec * SEC), BF16),
                   jax.ShapeDtypeStruct((d, LANES), BF16)),
        grid=(n_sec,),
        in_specs=[pl.BlockSpec((d, SEC), lambda j: (0, j)),
                  pl.BlockSpec((d, LANES), lambda j: (0, (j + 1) * (SEC // LANES)))],
        out_specs=(pl.BlockSpec((d, SEC), lambda j: (0, j)),
                   pl.BlockSpec((d, LANES), lambda j: (0, 0))),
        compiler_params=pltpu.CompilerParams(
            dimension_semantics=("arbitrary",), vmem_limit_bytes=VMEM_LIMIT),
        name="wprep",
    )(w_in2, w_in2)


def _inproj_kernel(x_ref, g1_ref, w_ref, wf_ref, bf_ref, bg_ref, lng_ref, lnb_ref, wsp_ref,
                   bsp_ref, qg_ref, kg_ref,
                   asgu_ref, q_ref, kt_ref, va_ref, gates_ref, lf_ref,
                   h_s, u_s):
    j = pl.program_id(1)
    tm = x_ref.shape[0]

    def section():
        return jnp.dot(h_s[...], w_ref[...], preferred_element_type=F32)

    @pl.when(j == 0)
    def _():
        x = x_ref[...]
        ms = jnp.mean(x * x, axis=-1, keepdims=True)
        hb = ((x * lax.rsqrt(ms + EPS)) * g1_ref[...]).astype(BF16)
        h_s[...] = hb
        f = jnp.dot(hb, wf_ref[...], preferred_element_type=F32) + bf_ref[...]
        lf_ref[...] = _log_sigmoid(f)
        u_s[...] = jax.nn.gelu(section())

    @pl.when(j == 1)
    def _():
        v = jax.nn.gelu(section())
        mu = jnp.mean(v, axis=-1, keepdims=True)
        vc = v - mu
        var = jnp.mean(vc * vc, axis=-1, keepdims=True)
        vn = ((vc * lax.rsqrt(var + EPS)) * lng_ref[...] + lnb_ref[...]).astype(BF16)
        row = lax.broadcasted_iota(I32, (CHUNK, CHUNK), 0)
        col = lax.broadcasted_iota(I32, (CHUNK, CHUNK), 1)
        causal = row >= col
        for g in range(SGU_GROUPS):
            wg = jnp.where(causal, wsp_ref[g], 0.0).astype(BF16)
            bcol = bsp_ref[:, g:g + 1]
            gs = slice(g * LANES, (g + 1) * LANES)
            for c in range(tm // CHUNK):
                rs = slice(c * CHUNK, (c + 1) * CHUNK)
                s = jnp.dot(wg, vn[rs, gs], preferred_element_type=F32) + bcol
                asgu_ref[rs, gs] = (u_s[rs, gs] * s).astype(BF16)

    def _head_norm(z, h, gain_ref):
        zh = z[:, h * HEAD_DIM:(h + 1) * HEAD_DIM]
        ms = jnp.mean(zh * zh, axis=-1, keepdims=True)
        return (zh * lax.rsqrt(ms + EPS)) * gain_ref[...]

    @pl.when(j == 2)
    def _():
        z = section()
        for h in range(FOX_HEADS):
            q_ref[:, h * HEAD_DIM:(h + 1) * HEAD_DIM] = (
                _head_norm(z, h, qg_ref) * ATTN_SCALE).astype(BF16)

    @pl.when(j == 3)
    def _():
        z = section()
        for h in range(FOX_HEADS):
            kt_ref[h * HEAD_DIM:(h + 1) * HEAD_DIM, :] = _head_norm(z, h, kg_ref).T.astype(BF16)

    @pl.when(j == 4)
    def _():
        va_ref[...] = section().astype(BF16)

    @pl.when(j >= N_MAIN_SEC)
    def _():
        gates_ref[...] = _sigmoid(section() + bg_ref[...]).astype(BF16)


def _inproj(x2, g1, w_all, w_f, b_f, b_gate, ln_g, ln_b, w_sp, b_sp_t, q_g, k_g):
    t = x2.shape[0]
    n_sec = N_MAIN_SEC + N_GATE_SEC
    gate_blk = lambda i, j: (i, jnp.maximum(j - N_MAIN_SEC, 0))
    row_blk = lambda i, j: (i, 0)
    const2 = lambda i, j: (0, 0)
    out_shape = (
        jax.ShapeDtypeStruct((t, SGU_WIDTH), BF16),
        jax.ShapeDtypeStruct((t, FOX_WIDTH), BF16),
        jax.ShapeDtypeStruct((FOX_WIDTH, t), BF16),
        jax.ShapeDtypeStruct((t, FOX_WIDTH), BF16),
        jax.ShapeDtypeStruct((t, 2 * D_MODEL), BF16),
        jax.ShapeDtypeStruct((t, LANES), F32),
    )
    return pl.pallas_call(
        _inproj_kernel,
        out_shape=out_shape,
        grid=(t // TM_IN, n_sec),
        in_specs=[
            pl.BlockSpec((TM_IN, D_MODEL), row_blk),
            pl.BlockSpec((1, D_MODEL), const2),
            pl.BlockSpec((D_MODEL, SEC), lambda i, j: (0, j)),
            pl.BlockSpec((D_MODEL, LANES), const2),
            pl.BlockSpec((1, LANES), const2),
            pl.BlockSpec((1, SEC), lambda i, j: (0, jnp.maximum(j - N_MAIN_SEC, 0))),
            pl.BlockSpec((1, SGU_WIDTH), const2),
            pl.BlockSpec((1, SGU_WIDTH), const2),
            pl.BlockSpec((SGU_GROUPS, CHUNK, CHUNK), lambda i, j: (0, 0, 0)),
            pl.BlockSpec((CHUNK, SGU_GROUPS), const2),
            pl.BlockSpec((1, HEAD_DIM), const2),
            pl.BlockSpec((1, HEAD_DIM), const2),
        ],
        out_specs=(
            pl.BlockSpec((TM_IN, SGU_WIDTH), row_blk),
            pl.BlockSpec((TM_IN, FOX_WIDTH), row_blk),
            pl.BlockSpec((FOX_WIDTH, TM_IN), lambda i, j: (0, i)),
            pl.BlockSpec((TM_IN, FOX_WIDTH), row_blk),
            pl.BlockSpec((TM_IN, SEC), gate_blk),
            pl.BlockSpec((TM_IN, LANES), row_blk),
        ),
        scratch_shapes=[pltpu.VMEM((TM_IN, D_MODEL), BF16), pltpu.VMEM((TM_IN, SEC), F32)],
        compiler_params=pltpu.CompilerParams(
            dimension_semantics=("arbitrary", "arbitrary"), vmem_limit_bytes=VMEM_LIMIT),
        name="inproj",
    )(x2, g1, w_all, w_f, b_f, b_gate, ln_g, ln_b, w_sp, b_sp_t, q_g, k_g)


def _split3(x):
    x0 = x.astype(BF16)
    r1 = x - x0.astype(F32)
    x1 = r1.astype(BF16)
    r2 = r1 - x1.astype(F32)
    return x0, x1, r2.astype(BF16)


def _cumsum_kernel(lf_ref, negc_ref, c_s):
    seq = lf_ref.shape[0]
    row = lax.broadcasted_iota(I32, (CHUNK, CHUNK), 0)
    col = lax.broadcasted_iota(I32, (CHUNK, CHUNK), 1)
    tri = jnp.where(row >= col, 1.0, 0.0).astype(BF16)
    carry = jnp.zeros((1, LANES), F32)
    for r in range(seq // CHUNK):
        rs = slice(r * CHUNK, (r + 1) * CHUNK)
        x0, x1, x2 = _split3(lf_ref[rs, :])
        cs = (jnp.dot(tri, x0, preferred_element_type=F32)
              + jnp.dot(tri, x1, preferred_element_type=F32)
              + jnp.dot(tri, x2, preferred_element_type=F32)) + carry
        carry = cs[CHUNK - 1:CHUNK, :]
        c_s[rs, :] = cs
    ct = c_s[...].T
    negc_ref[0] = -ct[0:FOX_HEADS, :]


def _forget_cumsum(lf, batch, seq):
    return pl.pallas_call(
        _cumsum_kernel,
        out_shape=jax.ShapeDtypeStruct((batch, FOX_HEADS, seq), F32),
        grid=(batch,),
        in_specs=[pl.BlockSpec((seq, LANES), lambda b: (b, 0))],
        out_specs=pl.BlockSpec((1, FOX_HEADS, seq), lambda b: (b, 0, 0)),
        scratch_shapes=[pltpu.VMEM((seq, LANES), F32)],
        compiler_params=pltpu.CompilerParams(
            dimension_semantics=("arbitrary",), vmem_limit_bytes=VMEM_LIMIT),
        name="forget_cumsum",
    )(lf)


def _attn_kernel(q_ref, kt_ref, v_ref, negc_ref, o_ref):
    seq = q_ref.shape[0]
    row = lax.broadcasted_iota(I32, (TQ, TQ), 0)
    col = lax.broadcasted_iota(I32, (TQ, TQ), 1)
    causal = row >= col
    for qi in range(seq // TQ):
        k0 = qi * TQ
        q = q_ref[k0:k0 + TQ, :]
        s_d = jnp.dot(q, kt_ref[:, k0:k0 + TQ], preferred_element_type=F32)
        s_d = jnp.where(causal, s_d + negc_ref[0, 0, :, k0:k0 + TQ], -jnp.inf)
        m = jnp.max(s_d, axis=-1, keepdims=True)
        if qi > 0:
            s_o = jnp.dot(q, kt_ref[:, 0:k0], preferred_element_type=F32) + negc_ref[0, 0, :, 0:k0]
            m = jnp.maximum(m, jnp.max(s_o, axis=-1, keepdims=True))
        p_d = jnp.exp(s_d - m)
        l = jnp.sum(p_d, axis=-1, keepdims=True)
        acc = jnp.dot(p_d.astype(BF16), v_ref[k0:k0 + TQ, :], preferred_element_type=F32)
        if qi > 0:
            p_o = jnp.exp(s_o - m)
            l = l + jnp.sum(p_o, axis=-1, keepdims=True)
            acc = acc + jnp.dot(p_o.astype(BF16), v_ref[0:k0, :], preferred_element_type=F32)
        o_ref[k0:k0 + TQ, :] = (acc * (1.0 / l)).astype(BF16)


def _attention(q, kt, va, negc, batch, seq):
    t = q.shape[0]
    blk = pl.BlockSpec((seq, HEAD_DIM), lambda b, h: (b, h))
    return pl.pallas_call(
        _attn_kernel,
        out_shape=jax.ShapeDtypeStruct((t, FOX_WIDTH), BF16),
        grid=(batch, FOX_HEADS),
        in_specs=[blk,
                  pl.BlockSpec((HEAD_DIM, seq), lambda b, h: (h, b)),
                  blk,
                  pl.BlockSpec((1, 1, 1, seq), lambda b, h: (b, h, 0, 0))],
        out_specs=blk,
        compiler_params=pltpu.CompilerParams(
            dimension_semantics=("arbitrary", "arbitrary"), vmem_limit_bytes=VMEM_LIMIT),
        name="fox_attention",
    )(q, kt, va, negc)


def _mix_kernel(as_ref, af_ref, g_ref, x_ref, wps_ref, wpf_ref, wo_ref, g2_ref, wr_ref, br_ref,
                x1_ref, lg_ref):
    ys = jnp.dot(as_ref[...], wps_ref[...], preferred_element_type=F32)
    yf = jnp.dot(af_ref[...], wpf_ref[...], preferred_element_type=F32)
    m = (g_ref[:, :D_MODEL].astype(F32) * ys + g_ref[:, D_MODEL:].astype(F32) * yf).astype(BF16)
    x1 = x_ref[...] + jnp.dot(m, wo_ref[...], preferred_element_type=F32)
    x1_ref[...] = x1
    ms = jnp.mean(x1 * x1, axis=-1, keepdims=True)
    h2 = ((x1 * lax.rsqrt(ms + EPS)) * g2_ref[...]).astype(BF16)
    lg_ref[...] = jnp.dot(h2, wr_ref[...], preferred_element_type=F32) + br_ref[...]


def _mix(a_sgu, a_fox, gates, x2, wps, wpf, wo, g2, w_r, b_r):
    t = x2.shape[0]
    row = lambda i: (i, 0)
    const = lambda i: (0, 0)
    resident = functools.partial(pl.BlockSpec, index_map=const, pipeline_mode=pl.Buffered(1))
    return pl.pallas_call(
        _mix_kernel,
        out_shape=(jax.ShapeDtypeStruct((t, D_MODEL), F32),
                   jax.ShapeDtypeStruct((t, LANES), F32)),
        grid=(t // TM_MIX,),
        in_specs=[
            pl.BlockSpec((TM_MIX, SGU_WIDTH), row),
            pl.BlockSpec((TM_MIX, FOX_WIDTH), row),
            pl.BlockSpec((TM_MIX, 2 * D_MODEL), row),
            pl.BlockSpec((TM_MIX, D_MODEL), row),
            resident((SGU_WIDTH, D_MODEL)),
            resident((FOX_WIDTH, D_MODEL)),
            resident((D_MODEL, D_MODEL)),
            pl.BlockSpec((1, D_MODEL), const),
            resident((D_MODEL, LANES)),
            pl.BlockSpec((1, LANES), const),
        ],
        out_specs=(pl.BlockSpec((TM_MIX, D_MODEL), row), pl.BlockSpec((TM_MIX, LANES), row)),
        compiler_params=pltpu.CompilerParams(
            dimension_semantics=("arbitrary",), vmem_limit_bytes=VMEM_LIMIT),
        name="mix",
    )(a_sgu, a_fox, gates, x2, wps, wpf, wo, g2, w_r, b_r)


def _route_kernel(lg_ref, dt_ref, mf_ref, bm_ref, em_ref, mi_ref):
    t = lg_ref.shape[0]
    n_chunks = t // RCH
    lane_i = lax.broadcasted_iota(I32, (RCH, LANES), 1)
    lane = lane_i.astype(F32)
    lane_grp = ((lane_i - N_GROUPS) >> 3).astype(F32)
    is_grp = lane_i < N_GROUPS
    is_exp = (lane_i >= N_GROUPS) & (lane_i < N_GROUPS + N_EXPERTS)
    r_i = lax.broadcasted_iota(I32, (RCH, RCH), 0)
    c_i = lax.broadcasted_iota(I32, (RCH, RCH), 1)
    strict_lower = jnp.where(r_i > c_i, 1.0, 0.0).astype(BF16)
    neg_inf = -jnp.inf

    def first_max(vals):
        vmax = jnp.max(vals, axis=-1, keepdims=True)
        idx = jnp.min(jnp.where(vals == vmax, lane, float(LANES)), axis=-1, keepdims=True)
        return vmax, idx

    def pick(table, idx):
        return jnp.sum(jnp.where(lane == idx, table, 0.0), axis=-1, keepdims=True)

    def pack(cols):
        out = jnp.zeros((RCH, LANES), F32)
        for n, c in enumerate(cols):
            out = jnp.where(lane_i == n, c, out)
        return out

    def pass1(ci, counts):
        r0 = pl.multiple_of(ci * RCH, RCH)
        lg = lg_ref[pl.ds(r0, RCH), :]
        gmax, grp = first_max(jnp.where(is_grp, lg, neg_inf))
        p_grp = 1.0 / jnp.sum(jnp.where(is_grp, jnp.exp(lg - gmax), 0.0), axis=-1, keepdims=True)
        el = jnp.where(is_exp & (lane_grp == grp), lg, neg_inf)
        v1, i1 = first_max(el)
        v2, i2 = first_max(jnp.where(lane == i1, neg_inf, el))
        e21 = jnp.exp(v2 - v1)
        w1 = p_grp / (1.0 + e21)
        w2 = p_grp * e21 / (1.0 + e21)
        e1 = i1 - float(N_GROUPS)
        e2 = i2 - float(N_GROUPS)
        hot = jnp.where((lane == e1) | (lane == e2), 1.0, 0.0)
        before = jnp.dot(strict_lower, hot.astype(BF16), preferred_element_type=F32) + counts
        mi_ref[pl.ds(r0, RCH), :] = pack([e1, e2, pick(before, e1), pick(before, e2)]).astype(I32)
        mf_ref[pl.ds(r0, RCH), :] = pack([w1, w2])
        return counts + jnp.sum(hot, axis=0, keepdims=True)

    counts = lax.fori_loop(0, n_chunks, pass1, jnp.zeros((1, LANES), F32))

    nblk = jnp.floor((counts + float(EXPERT_BLOCK - 1)) * (1.0 / EXPERT_BLOCK))
    u_r = lax.broadcasted_iota(I32, (LANES, LANES), 0)
    u_c = lax.broadcasted_iota(I32, (LANES, LANES), 1)
    strict_upper = jnp.where(u_r < u_c, 1.0, 0.0).astype(BF16)
    bstart = jnp.dot(jnp.broadcast_to(nblk, (8, LANES)).astype(BF16), strict_upper,
                     preferred_element_type=F32)[0:1, :]
    bend = bstart + nblk

    def pass2(ci, _):
        r0 = pl.multiple_of(ci * RCH, RCH)
        mi = mi_ref[pl.ds(r0, RCH), :].astype(F32)
        cols = [jnp.sum(jnp.where(lane_i == n, mi, 0.0), axis=-1, keepdims=True) for n in range(4)]
        d1 = pick(bstart, cols[0]) * float(EXPERT_BLOCK) + cols[2]
        d2 = pick(bstart, cols[1]) * float(EXPERT_BLOCK) + cols[3]
        dt_ref[ci] = pack([d1, d2]).T[0:8, :].astype(I32)
        return 0

    lax.fori_loop(0, n_chunks, pass2, 0)

    nb_rows = bm_ref.shape[0]
    b_col = lax.broadcasted_iota(I32, (nb_rows, LANES), 0).astype(F32)
    b_lane = lax.broadcasted_iota(I32, (nb_rows, LANES), 1)
    done = jnp.where((bend <= b_col) & (b_lane < N_EXPERTS), 1.0, 0.0)
    blk_e = jnp.minimum(jnp.sum(done, axis=-1, keepdims=True), float(N_EXPERTS - 1))
    n_used = jnp.sum(jnp.where(b_lane[0:1, :] == N_EXPERTS - 1, bend, 0.0), axis=-1, keepdims=True)
    bm_ref[...] = jnp.where(b_lane == 0, blk_e, jnp.where(b_lane == 1, n_used, 0.0)).astype(I32)

    e_row = lax.broadcasted_iota(I32, (8, LANES), 0)
    em_ref[...] = jnp.where(e_row == 0, counts,
                            jnp.where(e_row == 1, bstart, jnp.where(e_row == 2, nblk, 0.0))).astype(I32)


def _route(logits, nb_rows):
    t = logits.shape[0]
    full = lambda shape: pl.BlockSpec(shape, lambda: (0,) * len(shape))
    return pl.pallas_call(
        _route_kernel,
        out_shape=(jax.ShapeDtypeStruct((t // RCH, 8, RCH), I32),
                   jax.ShapeDtypeStruct((t, LANES), F32),
                   jax.ShapeDtypeStruct((nb_rows, LANES), I32),
                   jax.ShapeDtypeStruct((8, LANES), I32)),
        in_specs=[full((t, LANES))],
        out_specs=(full((t // RCH, 8, RCH)), full((t, LANES)), full((nb_rows, LANES)),
                   full((8, LANES))),
        scratch_shapes=[pltpu.VMEM((t, LANES), I32)],
        compiler_params=pltpu.CompilerParams(vmem_limit_bytes=VMEM_LIMIT),
        name="route",
    )(logits)


DISPATCH_UNROLL = 8
TM_DSP = 1024


def _dispatch_kernel(d1_ref, d2_ref, cnt_ref, bstart_ref, nblk_ref, nused_ref, x_ref, xs_ref, sem):
    i = pl.program_id(0)
    tm = x_ref.shape[0]
    nb = xs_ref.shape[0] // EXPERT_BLOCK

    def row_copy(src_row, dst_row):
        return pltpu.make_async_copy(x_ref.at[pl.ds(src_row, 1), :],
                                     xs_ref.at[pl.ds(dst_row, 1), :], sem)

    def block_copy(dst_block):
        dst_row = pl.multiple_of(dst_block * EXPERT_BLOCK, EXPERT_BLOCK)
        return pltpu.make_async_copy(x_ref.at[pl.ds(0, EXPERT_BLOCK), :],
                                     xs_ref.at[pl.ds(dst_row, EXPERT_BLOCK), :], sem)

    def tok_body(r, _):
        tok = i * tm + r
        row_copy(r, d1_ref[tok]).start()
        row_copy(r, d2_ref[tok]).start()
        return 0

    lax.fori_loop(0, tm, tok_body, 0, unroll=DISPATCH_UNROLL)

    @pl.when(i == 0)
    def _():
        def pad_expert(e, n_pad):
            lo = bstart_ref[e] * EXPERT_BLOCK + cnt_ref[e]
            hi = (bstart_ref[e] + nblk_ref[e]) * EXPERT_BLOCK

            def pad_body(r, _):
                row_copy(0, r).start()
                return 0

            lax.fori_loop(lo, hi, pad_body, 0)
            return n_pad + (hi - lo)

        n_pad = lax.fori_loop(0, N_EXPERTS, pad_expert, 0)
        n_used = nused_ref[0]

        def fill_body(b, _):
            block_copy(b).start()
            return 0

        lax.fori_loop(n_used, nb, fill_body, 0)

        def block_wait(_, c):
            block_copy(0).wait()
            return c

        lax.fori_loop(n_used, nb, block_wait, 0)

        def row_wait(_, c):
            row_copy(0, 0).wait()
            return c

        lax.fori_loop(0, n_pad, row_wait, 0)

    for _ in range((2 * tm) // EXPERT_BLOCK):
        block_copy(0).wait()


def _dispatch(dest1, dest2, cnt, bstart, nblk, n_used, x1, n_rows):
    t = x1.shape[0]
    grid_spec = pltpu.PrefetchScalarGridSpec(
        num_scalar_prefetch=6,
        grid=(t // TM_DSP,),
        in_specs=[pl.BlockSpec((TM_DSP, D_MODEL), lambda i, *_: (i, 0))],
        out_specs=pl.BlockSpec(memory_space=pl.ANY),
        scratch_shapes=[pltpu.SemaphoreType.DMA(())],
    )
    return pl.pallas_call(
        _dispatch_kernel,
        out_shape=jax.ShapeDtypeStruct((n_rows, D_MODEL), F32),
        grid_spec=grid_spec,
        compiler_params=pltpu.CompilerParams(
            dimension_semantics=("arbitrary",), vmem_limit_bytes=VMEM_LIMIT),
        name="dispatch",
    )(dest1, dest2, cnt, bstart, nblk, n_used, x1)


def _experts_kernel(blk_e_ref, nused_ref, nblk_ref, x_ref, g2_ref, wg_hbm, wu_hbm, wd_hbm, y_ref,
                    wg_f, wu_f, wd_f, wg_s, wu_s, wd_s, slot_ref, sem):
    b = pl.program_id(0)
    n_used = nused_ref[0]
    used = b < n_used
    e = blk_e_ref[b]
    new_expert = (b == 0) | (e != blk_e_ref[jnp.maximum(b - 1, 0)])

    def weight_copies(expert, slot):
        return (pltpu.make_async_copy(wg_hbm.at[expert], wg_f.at[slot], sem.at[slot]),
                pltpu.make_async_copy(wu_hbm.at[expert], wu_f.at[slot], sem.at[slot]),
                pltpu.make_async_copy(wd_hbm.at[expert], wd_f.at[slot], sem.at[slot]))

    @pl.when(b == 0)
    def _():
        slot_ref[0] = 0
        for cp in weight_copies(e, 0):
            cp.start()

    @pl.when(used & new_expert)
    def _():
        slot = slot_ref[0]
        b_next = b + nblk_ref[e]

        @pl.when(b_next < n_used)
        def _():
            for cp in weight_copies(blk_e_ref[jnp.minimum(b_next, n_used - 1)], 1 - slot):
                cp.start()

        for cp in weight_copies(e, slot):
            cp.wait()
        wg_s[...] = wg_f[slot].astype(BF16)
        wu_s[...] = wu_f[slot].astype(BF16)
        wd_s[...] = wd_f[slot].astype(BF16)
        slot_ref[0] = 1 - slot

    @pl.when(used)
    def _():
        x = x_ref[...]
        ms = jnp.mean(x * x, axis=-1, keepdims=True)
        h = ((x * lax.rsqrt(ms + EPS)) * g2_ref[...]).astype(BF16)
        a = jnp.dot(h, wg_s[...], preferred_element_type=F32)
        u = jnp.dot(h, wu_s[...], preferred_element_type=F32)
        mid = ((a * _sigmoid(a)) * u).astype(BF16)
        y_ref[...] = jnp.dot(mid, wd_s[...], preferred_element_type=F32)

    @pl.when(jnp.logical_not(used))
    def _():
        y_ref[...] = jnp.zeros_like(y_ref)


def _experts(blk_e, n_used, nblk, xs, g2, w_g, w_u, w_d):
    n_rows = xs.shape[0]
    nb = n_rows // EXPERT_BLOCK

    def row_blk(b, blk_e_ref, nused_ref, nblk_ref):
        return (jnp.minimum(b, nused_ref[0] - 1), 0)

    hbm = pl.BlockSpec(memory_space=pl.ANY)
    grid_spec = pltpu.PrefetchScalarGridSpec(
        num_scalar_prefetch=3,
        grid=(nb,),
        in_specs=[
            pl.BlockSpec((EXPERT_BLOCK, D_MODEL), row_blk),
            pl.BlockSpec((1, D_MODEL), lambda b, *_: (0, 0)),
            hbm, hbm, hbm,
        ],
        out_specs=pl.BlockSpec((EXPERT_BLOCK, D_MODEL), lambda b, *_: (b, 0)),
        scratch_shapes=[pltpu.VMEM((2, D_MODEL, D_EXPERT), F32),
                        pltpu.VMEM((2, D_MODEL, D_EXPERT), F32),
                        pltpu.VMEM((2, D_EXPERT, D_MODEL), F32),
                        pltpu.VMEM((D_MODEL, D_EXPERT), BF16),
                        pltpu.VMEM((D_MODEL, D_EXPERT), BF16),
                        pltpu.VMEM((D_EXPERT, D_MODEL), BF16),
                        pltpu.SMEM((1,), I32),
                        pltpu.SemaphoreType.DMA((2,))],
    )
    return pl.pallas_call(
        _experts_kernel,
        out_shape=jax.ShapeDtypeStruct((n_rows, D_MODEL), F32),
        grid_spec=grid_spec,
        compiler_params=pltpu.CompilerParams(
            dimension_semantics=("arbitrary",), vmem_limit_bytes=VMEM_LIMIT),
        name="experts",
    )(blk_e, n_used, nblk, xs, g2, w_g, w_u, w_d)


COMBINE_UNROLL = 8


def _combine_kernel(d1_ref, d2_ref, x1_ref, w_ref, y_ref, o_ref, gbuf, sem):
    i = pl.program_id(0)
    n = pl.num_programs(0)
    tm = x1_ref.shape[0]

    def issue(tile, slot):
        def body(r, _):
            tok = tile * tm + r
            pltpu.make_async_copy(y_ref.at[pl.ds(d1_ref[tok], 1), :],
                                  gbuf.at[slot, pl.ds(r, 1), :], sem.at[slot]).start()
            pltpu.make_async_copy(y_ref.at[pl.ds(d2_ref[tok], 1), :],
                                  gbuf.at[slot, pl.ds(tm + r, 1), :], sem.at[slot]).start()
            return 0

        lax.fori_loop(0, tm, body, 0, unroll=COMBINE_UNROLL)

    slot = i % 2

    @pl.when(i == 0)
    def _():
        issue(0, 0)

    @pl.when(i + 1 < n)
    def _():
        issue(i + 1, 1 - slot)

    pltpu.make_async_copy(y_ref.at[pl.ds(0, 2 * tm), :], gbuf.at[slot], sem.at[slot]).wait()
    w = w_ref[...]
    g = gbuf[slot]
    o_ref[...] = x1_ref[...] + (w[:, 0:1] * g[0:tm, :] + w[:, 1:2] * g[tm:2 * tm, :])


def _combine(dest1, dest2, x1, w, yb):
    t = x1.shape[0]
    grid_spec = pltpu.PrefetchScalarGridSpec(
        num_scalar_prefetch=2,
        grid=(t // TM_CMB,),
        in_specs=[
            pl.BlockSpec((TM_CMB, D_MODEL), lambda i, a, b: (i, 0)),
            pl.BlockSpec((TM_CMB, LANES), lambda i, a, b: (i, 0)),
            pl.BlockSpec(memory_space=pl.ANY),
        ],
        out_specs=pl.BlockSpec((TM_CMB, D_MODEL), lambda i, a, b: (i, 0)),
        scratch_shapes=[pltpu.VMEM((2, 2 * TM_CMB, D_MODEL), F32),
                        pltpu.SemaphoreType.DMA((2,))],
    )
    return pl.pallas_call(
        _combine_kernel,
        out_shape=jax.ShapeDtypeStruct((t, D_MODEL), F32),
        grid_spec=grid_spec,
        compiler_params=pltpu.CompilerParams(
            dimension_semantics=("arbitrary",), vmem_limit_bytes=VMEM_LIMIT),
        name="combine",
    )(dest1, dest2, x1, w, yb)


def kernel(x, norm1_g, w_in, b_gate, b_forget, sgu_ln_g, sgu_ln_b, w_spatial, b_spatial, q_norm_g, k_norm_g, w_proj_sgu, w_proj_fox, w_out, norm2_g, w_router_group, b_router_group, w_router_expert, b_router_expert, w_expert_gate, w_expert_up, w_expert_down):
    batch, seq, d = x.shape
    t = batch * seq
    l = 0
    x2 = x.reshape(t, d)

    w_all, w_f = _wprep(w_in[l])
    b_f =jnp.pad(b_forget[l], (0, LANES - FOX_HEADS)).reshape(1, LANES)
    n_r = N_GROUPS + N_EXPERTS
    w_r = jnp.pad(jnp.concatenate([w_router_group[l], w_router_expert[l]], axis=1),
                  ((0, 0), (0, LANES - n_r))).astype(BF16)
    b_r = jnp.pad(jnp.concatenate([b_router_group[l], b_router_expert[l]]),
                  (0, LANES - n_r)).reshape(1, LANES)

    a_sgu, q, kt, va, gates, lf = _inproj(
        x2, norm1_g[l].reshape(1, d), w_all, w_f, b_f, b_gate[l].reshape(1, 2 * d),
        sgu_ln_g[l].reshape(1, SGU_WIDTH), sgu_ln_b[l].reshape(1, SGU_WIDTH),
        w_spatial[l], b_spatial[l].T, q_norm_g[l].reshape(1, HEAD_DIM),
        k_norm_g[l].reshape(1, HEAD_DIM))
    negc = _forget_cumsum(lf, batch, seq).reshape(batch, FOX_HEADS, 1, seq)
    a_fox = _attention(q, kt, va, negc, batch, seq)
    x1, logits = _mix(a_sgu, a_fox, gates, x2, w_proj_sgu[l].astype(BF16),
                      w_proj_fox[l].astype(BF16), w_out[l].astype(BF16),
                      norm2_g[l].reshape(1, d), w_r, b_r)

    n_assign = 2 * t
    n_rows = n_assign + N_EXPERTS * EXPERT_BLOCK
    nb = n_rows // EXPERT_BLOCK
    nb_rows = -(-nb // 8) * 8
    dest_t, meta_f, bmeta, emeta = _route(logits, nb_rows)
    dest1, dest2 = dest_t[:, 0, :].reshape(t), dest_t[:, 1, :].reshape(t)
    n_used = bmeta[0:1, 1]
    xs = _dispatch(dest1, dest2, emeta[0, :N_EXPERTS], emeta[1, :N_EXPERTS], emeta[2, :N_EXPERTS],
                   n_used, x1, n_rows)
    yb = _experts(bmeta[:nb, 0], n_used, emeta[2, :N_EXPERTS], xs, norm2_g[l].reshape(1, d),
                  w_expert_gate[l], w_expert_up[l], w_expert_down[l])
    out = _combine(dest1, dest2, x1, meta_f, yb)
    return out.reshape(batch, seq, d)
```

```python
import functools

import jax
import jax.numpy as jnp
from jax import lax
from jax.experimental import pallas as pl
from jax.experimental.pallas import tpu as pltpu

F32 = jnp.float32
BF16 = jnp.bfloat16
I32 = jnp.int32

D_MODEL = 2048
CHUNK = 128
SGU_GROUPS = 8
SGU_WIDTH = 1024
FOX_HEADS = 8
HEAD_DIM = 128
FOX_WIDTH = 1024
N_GROUPS = 4
EXPERTS_PER_GROUP = 8
N_EXPERTS = 32
D_EXPERT = 512
EXPERT_BLOCK = 128
EPS = 1e-6

LANES = 128
VMEM_LIMIT = 56 * 1024 * 1024

SEC = 1024
N_MAIN_SEC = 5
N_GATE_SEC = 4
TM_NORM = 1024
TM_IN = 512
TQ = 256
TM_MIX = 256
RCH = 256
TM_CMB = 512
ATTN_SCALE = HEAD_DIM ** -0.5


def _sigmoid(x):
    return 1.0 / (1.0 + jnp.exp(-x))


def _log_sigmoid(x):
    return jnp.minimum(x, 0.0) - jnp.log1p(jnp.exp(-jnp.abs(x)))


def _norm1_kernel(x_ref, g1_ref, wf_ref, bf_ref, h_ref, lf_ref):
    x = x_ref[...]
    ms = jnp.mean(x * x, axis=-1, keepdims=True)
    hb = ((x * lax.rsqrt(ms + EPS)) * g1_ref[...]).astype(BF16)
    h_ref[...] = hb
    f = jnp.dot(hb, wf_ref[...], preferred_element_type=F32) + bf_ref[...]
    lf_ref[...] = _log_sigmoid(f)


def _norm1(x2, g1, w_f, b_f):
    t = x2.shape[0]
    row = lambda i: (i, 0)
    const = lambda i: (0, 0)
    return pl.pallas_call(
        _norm1_kernel,
        out_shape=(jax.ShapeDtypeStruct((t, D_MODEL), BF16),
                   jax.ShapeDtypeStruct((t, LANES), F32)),
        grid=(t // TM_NORM,),
        in_specs=[pl.BlockSpec((TM_NORM, D_MODEL), row),
                  pl.BlockSpec((1, D_MODEL), const),
                  pl.BlockSpec((D_MODEL, LANES), const),
                  pl.BlockSpec((1, LANES), const)],
        out_specs=(pl.BlockSpec((TM_NORM, D_MODEL), row), pl.BlockSpec((TM_NORM, LANES), row)),
        compiler_params=pltpu.CompilerParams(
            dimension_semantics=("arbitrary",), vmem_limit_bytes=VMEM_LIMIT),
        name="norm1",
    )(x2, g1, w_f, b_f)


N_SECTIONS = 1 + 3 + N_GATE_SEC
N_CHUNKS = N_MAIN_SEC + N_GATE_SEC
GATE_SECTION0 = 4
WCONV_ROWS = 256


def _inproj_kernel(h_ref, wt_hbm, bg_ref, lng_ref, lnb_ref, wsp_ref, bsp_ref, qg_ref, kg_ref,
                   asgu_ref, q_ref, kt_ref, va_ref, gates_ref,
                   stage, wb, sem):
    s = pl.program_id(0)
    i = pl.program_id(1)
    tm = h_ref.shape[0]

    def chunk_copy(c, slot):
        r0 = pl.multiple_of(c * SEC + jnp.where(c >= N_MAIN_SEC, FOX_HEADS, 0), 8)
        return pltpu.make_async_copy(wt_hbm.at[pl.ds(r0, SEC), :], stage.at[slot], sem.at[slot])

    def convert(slot, k):
        for p in range(SEC // WCONV_ROWS):
            rs = slice(p * WCONV_ROWS, (p + 1) * WCONV_ROWS)
            wb[k, :, rs] = stage[slot, rs, :].T.astype(BF16)

    @pl.when((s == 0) & (i == 0))
    def _():
        chunk_copy(0, 0).start()
        chunk_copy(1, 1).start()
        chunk_copy(0, 0).wait()
        convert(0, 0)
        chunk_copy(2, 0).start()
        chunk_copy(1, 1).wait()
        convert(1, 1)

    @pl.when((s > 0) & (i == 0))
    def _():
        c = s + 1
        slot = c % 2
        chunk_copy(c, slot).wait()

        @pl.when(c + 1 < N_CHUNKS)
        def _():
            chunk_copy(c + 1, 1 - slot).start()

        convert(slot, 0)

    def section(k=0):
        return jnp.dot(h_ref[...], wb[k], preferred_element_type=F32)

    @pl.when(s == 0)
    def _():
        u = jax.nn.gelu(section(0))
        v = jax.nn.gelu(section(1))
        mu = jnp.mean(v, axis=-1, keepdims=True)
        vc = v - mu
        var = jnp.mean(vc * vc, axis=-1, keepdims=True)
        vn = ((vc * lax.rsqrt(var + EPS)) * lng_ref[...] + lnb_ref[...]).astype(BF16)
        row = lax.broadcasted_iota(I32, (CHUNK, CHUNK), 0)
        col = lax.broadcasted_iota(I32, (CHUNK, CHUNK), 1)
        causal = row >= col
        for g in range(SGU_GROUPS):
            wg = jnp.where(causal, wsp_ref[g], 0.0).astype(BF16)
            bcol = bsp_ref[:, g:g + 1]
            gs = slice(g * LANES, (g + 1) * LANES)
            for c in range(tm // CHUNK):
                rs = slice(c * CHUNK, (c + 1) * CHUNK)
                sg = jnp.dot(wg, vn[rs, gs], preferred_element_type=F32) + bcol
                asgu_ref[rs, gs] = (u[rs, gs] * sg).astype(BF16)

    def _head_norm(z, h, gain_ref):
        zh = z[:, h * HEAD_DIM:(h + 1) * HEAD_DIM]
        ms = jnp.mean(zh * zh, axis=-1, keepdims=True)
        return (zh * lax.rsqrt(ms + EPS)) * gain_ref[...]

    @pl.when(s == 1)
    def _():
        z = section()
        for h in range(FOX_HEADS):
            q_ref[:, h * HEAD_DIM:(h + 1) * HEAD_DIM] = (
                _head_norm(z, h, qg_ref) * ATTN_SCALE).astype(BF16)

    @pl.when(s == 2)
    def _():
        z = section()
        for h in range(FOX_HEADS):
            kt_ref[h * HEAD_DIM:(h + 1) * HEAD_DIM, :] = _head_norm(z, h, kg_ref).T.astype(BF16)

    @pl.when(s == 3)
    def _():
        va_ref[...] = section().astype(BF16)

    @pl.when(s >= GATE_SECTION0)
    def _():
        gates_ref[...] = (0.5 * jnp.tanh(0.5 * (section() + bg_ref[...])) + 0.5).astype(BF16)


def _inproj(h, wt, b_gate, ln_g, ln_b, w_sp, b_sp_t, q_g, k_g):
    t = h.shape[0]
    n_i = t // TM_IN

    def active(sec):
        return lambda s, i: jnp.where(s < sec, 0, jnp.where(s > sec, n_i - 1, i))

    def gate_blk(s, i):
        g = jnp.clip(s - GATE_SECTION0, 0, N_GATE_SEC - 1)
        return (jnp.where(s < GATE_SECTION0, 0, i), g)

    const2 = lambda s, i: (0, 0)
    out_shape = (
        jax.ShapeDtypeStruct((t, SGU_WIDTH), BF16),
        jax.ShapeDtypeStruct((t, FOX_WIDTH), BF16),
        jax.ShapeDtypeStruct((FOX_WIDTH, t), BF16),
        jax.ShapeDtypeStruct((t, FOX_WIDTH), BF16),
        jax.ShapeDtypeStruct((t, 2 * D_MODEL), BF16),
    )
    return pl.pallas_call(
        _inproj_kernel,
        out_shape=out_shape,
        grid=(N_SECTIONS, n_i),
        in_specs=[
            pl.BlockSpec((TM_IN, D_MODEL), lambda s, i: (i, 0)),
            pl.BlockSpec(memory_space=pl.ANY),
            pl.BlockSpec((1, SEC), lambda s, i: (0, jnp.clip(s - GATE_SECTION0, 0, N_GATE_SEC - 1))),
            pl.BlockSpec((1, SGU_WIDTH), const2),
            pl.BlockSpec((1, SGU_WIDTH), const2),
            pl.BlockSpec((SGU_GROUPS, CHUNK, CHUNK), lambda s, i: (0, 0, 0)),
            pl.BlockSpec((CHUNK, SGU_GROUPS), const2),
            pl.BlockSpec((1, HEAD_DIM), const2),
            pl.BlockSpec((1, HEAD_DIM), const2),
        ],
        out_specs=(
            pl.BlockSpec((TM_IN, SGU_WIDTH), lambda s, i: (active(0)(s, i), 0)),
            pl.BlockSpec((TM_IN, FOX_WIDTH), lambda s, i: (active(1)(s, i), 0)),
            pl.BlockSpec((FOX_WIDTH, TM_IN), lambda s, i: (0, active(2)(s, i))),
            pl.BlockSpec((TM_IN, FOX_WIDTH), lambda s, i: (active(3)(s, i), 0)),
            pl.BlockSpec((TM_IN, SEC), gate_blk),
        ),
        scratch_shapes=[pltpu.VMEM((2, SEC, D_MODEL), F32),
                        pltpu.VMEM((2, D_MODEL, SEC), BF16),
                        pltpu.SemaphoreType.DMA((2,))],
        compiler_params=pltpu.CompilerParams(
            dimension_semantics=("arbitrary", "arbitrary"), vmem_limit_bytes=VMEM_LIMIT),
        name="inproj",
    )(h, wt, b_gate, ln_g, ln_b, w_sp, b_sp_t, q_g, k_g)


def _split3(x):
    x0 = x.astype(BF16)
    r1 = x - x0.astype(F32)
    x1 = r1.astype(BF16)
    r2 = r1 - x1.astype(F32)
    return x0, x1, r2.astype(BF16)


def _cumsum_kernel(lf_ref, negc_ref, c_s):
    seq = lf_ref.shape[0]
    row = lax.broadcasted_iota(I32, (CHUNK, CHUNK), 0)
    col = lax.broadcasted_iota(I32, (CHUNK, CHUNK), 1)
    tri = jnp.where(row >= col, 1.0, 0.0).astype(BF16)
    carry = jnp.zeros((1, LANES), F32)
    for r in range(seq // CHUNK):
        rs = slice(r * CHUNK, (r + 1) * CHUNK)
        x0, x1, x2 = _split3(lf_ref[rs, :])
        cs = (jnp.dot(tri, x0, preferred_element_type=F32)
              + jnp.dot(tri, x1, preferred_element_type=F32)
              + jnp.dot(tri, x2, preferred_element_type=F32)) + carry
        carry = cs[CHUNK - 1:CHUNK, :]
        c_s[rs, :] = cs
    ct = c_s[...].T
    negc_ref[0] = -ct[0:FOX_HEADS, :]


def _forget_cumsum(lf, batch, seq):
    return pl.pallas_call(
        _cumsum_kernel,
        out_shape=jax.ShapeDtypeStruct((batch, FOX_HEADS, seq), F32),
        grid=(batch,),
        in_specs=[pl.BlockSpec((seq, LANES), lambda b: (b, 0))],
        out_specs=pl.BlockSpec((1, FOX_HEADS, seq), lambda b: (b, 0, 0)),
        scratch_shapes=[pltpu.VMEM((seq, LANES), F32)],
        compiler_params=pltpu.CompilerParams(
            dimension_semantics=("arbitrary",), vmem_limit_bytes=VMEM_LIMIT),
        name="forget_cumsum",
    )(lf)


def _attn_kernel(q_ref, kt_ref, v_ref, negc_ref, o_ref):
    seq = q_ref.shape[0]
    row = lax.broadcasted_iota(I32, (TQ, TQ), 0)
    col = lax.broadcasted_iota(I32, (TQ, TQ), 1)
    causal = row >= col
    for qi in range(seq // TQ):
        k0 = qi * TQ
        q = q_ref[k0:k0 + TQ, :]
        s_d = jnp.dot(q, kt_ref[:, k0:k0 + TQ], preferred_element_type=F32)
        s_d = jnp.where(causal, s_d + negc_ref[0, 0, :, k0:k0 + TQ], -jnp.inf)
        m = jnp.max(s_d, axis=-1, keepdims=True)
        if qi > 0:
            s_o = jnp.dot(q, kt_ref[:, 0:k0], preferred_element_type=F32) + negc_ref[0, 0, :, 0:k0]
            m = jnp.maximum(m, jnp.max(s_o, axis=-1, keepdims=True))
        p_d = jnp.exp(s_d - m)
        l = jnp.sum(p_d, axis=-1, keepdims=True)
        acc = jnp.dot(p_d.astype(BF16), v_ref[k0:k0 + TQ, :], preferred_element_type=F32)
        if qi > 0:
            p_o = jnp.exp(s_o - m)
            l = l + jnp.sum(p_o, axis=-1, keepdims=True)
            acc = acc + jnp.dot(p_o.astype(BF16), v_ref[0:k0, :], preferred_element_type=F32)
        o_ref[k0:k0 + TQ, :] = (acc * (1.0 / l)).astype(BF16)


def _attention(q, kt, va, negc, batch, seq):
    t = q.shape[0]
    blk = pl.BlockSpec((seq, HEAD_DIM), lambda b, h: (b, h))
    return pl.pallas_call(
        _attn_kernel,
        out_shape=jax.ShapeDtypeStruct((t, FOX_WIDTH), BF16),
        grid=(batch, FOX_HEADS),
        in_specs=[blk,
                  pl.BlockSpec((HEAD_DIM, seq), lambda b, h: (h, b)),
                  blk,
                  pl.BlockSpec((1, 1, 1, seq), lambda b, h: (b, h, 0, 0))],
        out_specs=blk,
        compiler_params=pltpu.CompilerParams(
            dimension_semantics=("arbitrary", "arbitrary"), vmem_limit_bytes=VMEM_LIMIT),
        name="fox_attention",
    )(q, kt, va, negc)


def _mix_kernel(as_ref, af_ref, g_ref, x_ref, wps_ref, wpf_ref, wo_ref, g2_ref, wr_ref, br_ref,
                x1_ref, lg_ref):
    ys = jnp.dot(as_ref[...], wps_ref[...], preferred_element_type=F32)
    yf = jnp.dot(af_ref[...], wpf_ref[...], preferred_element_type=F32)
    m = (g_ref[:, :D_MODEL].astype(F32) * ys + g_ref[:, D_MODEL:].astype(F32) * yf).astype(BF16)
    x1 = x_ref[...] + jnp.dot(m, wo_ref[...], preferred_element_type=F32)
    x1_ref[...] = x1
    ms = jnp.mean(x1 * x1, axis=-1, keepdims=True)
    h2 = ((x1 * lax.rsqrt(ms + EPS)) * g2_ref[...]).astype(BF16)
    lg_ref[...] = jnp.dot(h2, wr_ref[...], preferred_element_type=F32) + br_ref[...]


def _mix(a_sgu, a_fox, gates, x2, wps, wpf, wo, g2, w_r, b_r):
    t = x2.shape[0]
    row = lambda i: (i, 0)
    const = lambda i: (0, 0)
    resident = functools.partial(pl.BlockSpec, index_map=const, pipeline_mode=pl.Buffered(1))
    return pl.pallas_call(
        _mix_kernel,
        out_shape=(jax.ShapeDtypeStruct((t, D_MODEL), F32),
                   jax.ShapeDtypeStruct((t, LANES), F32)),
        grid=(t // TM_MIX,),
        in_specs=[
            pl.BlockSpec((TM_MIX, SGU_WIDTH), row),
            pl.BlockSpec((TM_MIX, FOX_WIDTH), row),
            pl.BlockSpec((TM_MIX, 2 * D_MODEL), row),
            pl.BlockSpec((TM_MIX, D_MODEL), row),
            resident((SGU_WIDTH, D_MODEL)),
            resident((FOX_WIDTH, D_MODEL)),
            resident((D_MODEL, D_MODEL)),
            pl.BlockSpec((1, D_MODEL), const),
            resident((D_MODEL, LANES)),
            pl.BlockSpec((1, LANES), const),
        ],
        out_specs=(pl.BlockSpec((TM_MIX, D_MODEL), row), pl.BlockSpec((TM_MIX, LANES), row)),
        compiler_params=pltpu.CompilerParams(
            dimension_semantics=("arbitrary",), vmem_limit_bytes=VMEM_LIMIT),
        name="mix",
    )(a_sgu, a_fox, gates, x2, wps, wpf, wo, g2, w_r, b_r)


def _route_kernel(lg_ref, dt_ref, mf_ref, bm_ref, em_ref, mi_ref):
    t = lg_ref.shape[0]
    n_chunks = t // RCH
    lane_i = lax.broadcasted_iota(I32, (RCH, LANES), 1)
    lane = lane_i.astype(F32)
    lane_grp = ((lane_i - N_GROUPS) >> 3).astype(F32)
    is_grp = lane_i < N_GROUPS
    is_exp = (lane_i >= N_GROUPS) & (lane_i < N_GROUPS + N_EXPERTS)
    r_i = lax.broadcasted_iota(I32, (RCH, RCH), 0)
    c_i = lax.broadcasted_iota(I32, (RCH, RCH), 1)
    strict_lower = jnp.where(r_i > c_i, 1.0, 0.0).astype(BF16)
    neg_inf = -jnp.inf

    def first_max(vals):
        vmax = jnp.max(vals, axis=-1, keepdims=True)
        idx = jnp.min(jnp.where(vals == vmax, lane, float(LANES)), axis=-1, keepdims=True)
        return vmax, idx

    def pick(table, idx):
        return jnp.sum(jnp.where(lane == idx, table, 0.0), axis=-1, keepdims=True)

    def pack(cols):
        out = jnp.zeros((RCH, LANES), F32)
        for n, c in enumerate(cols):
            out = jnp.where(lane_i == n, c, out)
        return out

    def pass1(ci, counts):
        r0 = pl.multiple_of(ci * RCH, RCH)
        lg = lg_ref[pl.ds(r0, RCH), :]
        gmax, grp = first_max(jnp.where(is_grp, lg, neg_inf))
        p_grp = 1.0 / jnp.sum(jnp.where(is_grp, jnp.exp(lg - gmax), 0.0), axis=-1, keepdims=True)
        el = jnp.where(is_exp & (lane_grp == grp), lg, neg_inf)
        v1, i1 = first_max(el)
        v2, i2 = first_max(jnp.where(lane == i1, neg_inf, el))
        e21 = jnp.exp(v2 - v1)
        w1 = p_grp / (1.0 + e21)
        w2 = p_grp * e21 / (1.0 + e21)
        e1 = i1 - float(N_GROUPS)
        e2 = i2 - float(N_GROUPS)
        hot = jnp.where((lane == e1) | (lane == e2), 1.0, 0.0)
        before = jnp.dot(strict_lower, hot.astype(BF16), preferred_element_type=F32) + counts
        mi_ref[pl.ds(r0, RCH), :] = pack([e1, e2, pick(before, e1), pick(before, e2)]).astype(I32)
        mf_ref[pl.ds(r0, RCH), :] = pack([w1, w2])
        return counts + jnp.sum(hot, axis=0, keepdims=True)

    counts = lax.fori_loop(0, n_chunks, pass1, jnp.zeros((1, LANES), F32))

    nblk = jnp.floor((counts + float(EXPERT_BLOCK - 1)) * (1.0 / EXPERT_BLOCK))
    u_r = lax.broadcasted_iota(I32, (LANES, LANES), 0)
    u_c = lax.broadcasted_iota(I32, (LANES, LANES), 1)
    strict_upper = jnp.where(u_r < u_c, 1.0, 0.0).astype(BF16)
    bstart = jnp.dot(jnp.broadcast_to(nblk, (8, LANES)).astype(BF16), strict_upper,
                     preferred_element_type=F32)[0:1, :]
    bend = bstart + nblk

    def pass2(ci, _):
        r0 = pl.multiple_of(ci * RCH, RCH)
        mi = mi_ref[pl.ds(r0, RCH), :].astype(F32)
        cols = [jnp.sum(jnp.where(lane_i == n, mi, 0.0), axis=-1, keepdims=True) for n in range(4)]
        d1 = pick(bstart, cols[0]) * float(EXPERT_BLOCK) + cols[2]
        d2 = pick(bstart, cols[1]) * float(EXPERT_BLOCK) + cols[3]
        dt_ref[ci] = pack([d1, d2]).T[0:8, :].astype(I32)
        return 0

    lax.fori_loop(0, n_chunks, pass2, 0)

    nb_rows = bm_ref.shape[0]
    b_col = lax.broadcasted_iota(I32, (nb_rows, LANES), 0).astype(F32)
    b_lane = lax.broadcasted_iota(I32, (nb_rows, LANES), 1)
    done = jnp.where((bend <= b_col) & (b_lane < N_EXPERTS), 1.0, 0.0)
    blk_e = jnp.minimum(jnp.sum(done, axis=-1, keepdims=True), float(N_EXPERTS - 1))
    n_used = jnp.sum(jnp.where(b_lane[0:1, :] == N_EXPERTS - 1, bend, 0.0), axis=-1, keepdims=True)
    bm_ref[...] = jnp.where(b_lane == 0, blk_e, jnp.where(b_lane == 1, n_used, 0.0)).astype(I32)

    e_row = lax.broadcasted_iota(I32, (8, LANES), 0)
    em_ref[...] = jnp.where(e_row == 0, counts,
                            jnp.where(e_row == 1, bstart, jnp.where(e_row == 2, nblk, 0.0))).astype(I32)


def _route(logits, nb_rows):
    t = logits.shape[0]
    full = lambda shape: pl.BlockSpec(shape, lambda: (0,) * len(shape))
    return pl.pallas_call(
        _route_kernel,
        out_shape=(jax.ShapeDtypeStruct((t // RCH, 8, RCH), I32),
                   jax.ShapeDtypeStruct((t, LANES), F32),
                   jax.ShapeDtypeStruct((nb_rows, LANES), I32),
                   jax.ShapeDtypeStruct((8, LANES), I32)),
        in_specs=[full((t, LANES))],
        out_specs=(full((t // RCH, 8, RCH)), full((t, LANES)), full((nb_rows, LANES)),
                   full((8, LANES))),
        scratch_shapes=[pltpu.VMEM((t, LANES), I32)],
        compiler_params=pltpu.CompilerParams(vmem_limit_bytes=VMEM_LIMIT),
        name="route",
    )(logits)


DISPATCH_UNROLL = 8
TM_DSP = 1024


def _dispatch_kernel(d1_ref, d2_ref, cnt_ref, bstart_ref, nblk_ref, nused_ref, x_ref, xs_ref, sem):
    i = pl.program_id(0)
    tm = x_ref.shape[0]
    nb = xs_ref.shape[0] // EXPERT_BLOCK

    def row_copy(src_row, dst_row):
        return pltpu.make_async_copy(x_ref.at[pl.ds(src_row, 1), :],
                                     xs_ref.at[pl.ds(dst_row, 1), :], sem)

    def block_copy(dst_block):
        dst_row = pl.multiple_of(dst_block * EXPERT_BLOCK, EXPERT_BLOCK)
        return pltpu.make_async_copy(x_ref.at[pl.ds(0, EXPERT_BLOCK), :],
                                     xs_ref.at[pl.ds(dst_row, EXPERT_BLOCK), :], sem)

    def tok_body(r, _):
        tok = i * tm + r
        row_copy(r, d1_ref[tok]).start()
        row_copy(r, d2_ref[tok]).start()
        return 0

    lax.fori_loop(0, tm, tok_body, 0, unroll=DISPATCH_UNROLL)

    @pl.when(i == 0)
    def _():
        def pad_expert(e, n_pad):
            lo = bstart_ref[e] * EXPERT_BLOCK + cnt_ref[e]
            hi = (bstart_ref[e] + nblk_ref[e]) * EXPERT_BLOCK

            def pad_body(r, _):
                row_copy(0, r).start()
                return 0

            lax.fori_loop(lo, hi, pad_body, 0)
            return n_pad + (hi - lo)

        n_pad = lax.fori_loop(0, N_EXPERTS, pad_expert, 0)
        n_used = nused_ref[0]

        def fill_body(b, _):
            block_copy(b).start()
            return 0

        lax.fori_loop(n_used, nb, fill_body, 0)

        def block_wait(_, c):
            block_copy(0).wait()
            return c

        lax.fori_loop(n_used, nb, block_wait, 0)

        def row_wait(_, c):
            row_copy(0, 0).wait()
            return c

        lax.fori_loop(0, n_pad, row_wait, 0)

    for _ in range((2 * tm) // EXPERT_BLOCK):
        block_copy(0).wait()


def _dispatch(dest1, dest2, cnt, bstart, nblk, n_used, x1, n_rows):
    t = x1.shape[0]
    grid_spec = pltpu.PrefetchScalarGridSpec(
        num_scalar_prefetch=6,
        grid=(t // TM_DSP,),
        in_specs=[pl.BlockSpec((TM_DSP, D_MODEL), lambda i, *_: (i, 0))],
        out_specs=pl.BlockSpec(memory_space=pl.ANY),
        scratch_shapes=[pltpu.SemaphoreType.DMA(())],
    )
    return pl.pallas_call(
        _dispatch_kernel,
        out_shape=jax.ShapeDtypeStruct((n_rows, D_MODEL), F32),
        grid_spec=grid_spec,
        compiler_params=pltpu.CompilerParams(
            dimension_semantics=("arbitrary",), vmem_limit_bytes=VMEM_LIMIT),
        name="dispatch",
    )(dest1, dest2, cnt, bstart, nblk, n_used, x1)


def _experts_kernel(blk_e_ref, nused_ref, nblk_ref, x_ref, g2_ref, wg_hbm, wu_hbm, wd_hbm, y_ref,
                    wg_f, wu_f, wd_f, wg_s, wu_s, wd_s, slot_ref, sem):
    b = pl.program_id(0)
    n_used = nused_ref[0]
    used = b < n_used
    e = blk_e_ref[b]
    new_expert = (b == 0) | (e != blk_e_ref[jnp.maximum(b - 1, 0)])

    def weight_copies(expert, slot):
        return (pltpu.make_async_copy(wg_hbm.at[expert], wg_f.at[slot], sem.at[slot]),
                pltpu.make_async_copy(wu_hbm.at[expert], wu_f.at[slot], sem.at[slot]),
                pltpu.make_async_copy(wd_hbm.at[expert], wd_f.at[slot], sem.at[slot]))

    @pl.when(b == 0)
    def _():
        slot_ref[0] = 0
        for cp in weight_copies(e, 0):
            cp.start()

    @pl.when(used & new_expert)
    def _():
        slot = slot_ref[0]
        b_next = b + nblk_ref[e]

        @pl.when(b_next < n_used)
        def _():
            for cp in weight_copies(blk_e_ref[jnp.minimum(b_next, n_used - 1)], 1 - slot):
                cp.start()

        for cp in weight_copies(e, slot):
            cp.wait()
        wg_s[...] = wg_f[slot].astype(BF16)
        wu_s[...] = wu_f[slot].astype(BF16)
        wd_s[...] = wd_f[slot].astype(BF16)
        slot_ref[0] = 1 - slot

    @pl.when(used)
    def _():
        x = x_ref[...]
        ms = jnp.mean(x * x, axis=-1, keepdims=True)
        h = ((x * lax.rsqrt(ms + EPS)) * g2_ref[...]).astype(BF16)
        a = jnp.dot(h, wg_s[...], preferred_element_type=F32)
        u = jnp.dot(h, wu_s[...], preferred_element_type=F32)
        mid = ((a * _sigmoid(a)) * u).astype(BF16)
        y_ref[...] = jnp.dot(mid, wd_s[...], preferred_element_type=F32)

    @pl.when(jnp.logical_not(used))
    def _():
        y_ref[...] = jnp.zeros_like(y_ref)


def _experts(blk_e, n_used, nblk, xs, g2, w_g, w_u, w_d):
    n_rows = xs.shape[0]
    nb = n_rows // EXPERT_BLOCK

    def row_blk(b, blk_e_ref, nused_ref, nblk_ref):
        return (jnp.minimum(b, nused_ref[0] - 1), 0)

    hbm = pl.BlockSpec(memory_space=pl.ANY)
    grid_spec = pltpu.PrefetchScalarGridSpec(
        num_scalar_prefetch=3,
        grid=(nb,),
        in_specs=[
            pl.BlockSpec((EXPERT_BLOCK, D_MODEL), row_blk),
            pl.BlockSpec((1, D_MODEL), lambda b, *_: (0, 0)),
            hbm, hbm, hbm,
        ],
        out_specs=pl.BlockSpec((EXPERT_BLOCK, D_MODEL), lambda b, *_: (b, 0)),
        scratch_shapes=[pltpu.VMEM((2, D_MODEL, D_EXPERT), F32),
                        pltpu.VMEM((2, D_MODEL, D_EXPERT), F32),
                        pltpu.VMEM((2, D_EXPERT, D_MODEL), F32),
                        pltpu.VMEM((D_MODEL, D_EXPERT), BF16),
                        pltpu.VMEM((D_MODEL, D_EXPERT), BF16),
                        pltpu.VMEM((D_EXPERT, D_MODEL), BF16),
                        pltpu.SMEM((1,), I32),
                        pltpu.SemaphoreType.DMA((2,))],
    )
    return pl.pallas_call(
        _experts_kernel,
        out_shape=jax.ShapeDtypeStruct((n_rows, D_MODEL), F32),
        grid_spec=grid_spec,
        compiler_params=pltpu.CompilerParams(
            dimension_semantics=("arbitrary",), vmem_limit_bytes=VMEM_LIMIT),
        name="experts",
    )(blk_e, n_used, nblk, xs, g2, w_g, w_u, w_d)


COMBINE_UNROLL = 8


def _combine_kernel(d1_ref, d2_ref, x1_ref, w_ref, y_ref, o_ref, gbuf, sem):
    i = pl.program_id(0)
    n = pl.num_programs(0)
    tm = x1_ref.shape[0]

    def issue(tile, slot):
        def body(r, _):
            tok = tile * tm + r
            pltpu.make_async_copy(y_ref.at[pl.ds(d1_ref[tok], 1), :],
                                  gbuf.at[slot, pl.ds(r, 1), :], sem.at[slot]).start()
            pltpu.make_async_copy(y_ref.at[pl.ds(d2_ref[tok], 1), :],
                                  gbuf.at[slot, pl.ds(tm + r, 1), :], sem.at[slot]).start()
            return 0

        lax.fori_loop(0, tm, body, 0, unroll=COMBINE_UNROLL)

    slot = i % 2

    @pl.when(i == 0)
    def _():
        issue(0, 0)

    @pl.when(i + 1 < n)
    def _():
        issue(i + 1, 1 - slot)

    pltpu.make_async_copy(y_ref.at[pl.ds(0, 2 * tm), :], gbuf.at[slot], sem.at[slot]).wait()
    w = w_ref[...]
    g = gbuf[slot]
    o_ref[...] = x1_ref[...] + (w[:, 0:1] * g[0:tm, :] + w[:, 1:2] * g[tm:2 * tm, :])


def _combine(dest1, dest2, x1, w, yb):
    t = x1.shape[0]
    grid_spec = pltpu.PrefetchScalarGridSpec(
        num_scalar_prefetch=2,
        grid=(t // TM_CMB,),
        in_specs=[
            pl.BlockSpec((TM_CMB, D_MODEL), lambda i, a, b: (i, 0)),
            pl.BlockSpec((TM_CMB, LANES), lambda i, a, b: (i, 0)),
            pl.BlockSpec(memory_space=pl.ANY),
        ],
        out_specs=pl.BlockSpec((TM_CMB, D_MODEL), lambda i, a, b: (i, 0)),
        scratch_shapes=[pltpu.VMEM((2, 2 * TM_CMB, D_MODEL), F32),
                        pltpu.SemaphoreType.DMA((2,))],
    )
    return pl.pallas_call(
        _combine_kernel,
        out_shape=jax.ShapeDtypeStruct((t, D_MODEL), F32),
        grid_spec=grid_spec,
        compiler_params=pltpu.CompilerParams(
            dimension_semantics=("arbitrary",), vmem_limit_bytes=VMEM_LIMIT),
        name="combine",
    )(dest1, dest2, x1, w, yb)


def kernel(x, norm1_g, w_in, b_gate, b_forget, sgu_ln_g, sgu_ln_b, w_spatial, b_spatial, q_norm_g, k_norm_g, w_proj_sgu, w_proj_fox, w_out, norm2_g, w_router_group, b_router_group, w_router_expert, b_router_expert, w_expert_gate, w_expert_up, w_expert_down):
    batch, seq, d = x.shape
    t = batch * seq
    l = 0
    x2 = x.reshape(t, d)

    wt = jnp.swapaxes(w_in[l], 0, 1)
    off_f = N_MAIN_SEC * SEC
    w_f = jnp.pad(wt[off_f:off_f + FOX_HEADS].T, ((0, 0), (0, LANES - FOX_HEADS))).astype(BF16)
    b_f = jnp.pad(b_forget[l], (0, LANES - FOX_HEADS)).reshape(1, LANES)
    n_r = N_GROUPS + N_EXPERTS
    w_r = jnp.pad(jnp.concatenate([w_router_group[l], w_router_expert[l]], axis=1),
                  ((0, 0), (0, LANES - n_r))).astype(BF16)
    b_r = jnp.pad(jnp.concatenate([b_router_group[l], b_router_expert[l]]),
                  (0, LANES - n_r)).reshape(1, LANES)

    h, lf = _norm1(x2, norm1_g[l].reshape(1, d), w_f, b_f)
    a_sgu, q, kt, va, gates = _inproj(
        h, wt, b_gate[l].reshape(1, 2 * d),
        sgu_ln_g[l].reshape(1, SGU_WIDTH), sgu_ln_b[l].reshape(1, SGU_WIDTH),
        w_spatial[l], b_spatial[l].T, q_norm_g[l].reshape(1, HEAD_DIM),
        k_norm_g[l].reshape(1, HEAD_DIM))
    negc = _forget_cumsum(lf, batch, seq).reshape(batch, FOX_HEADS, 1, seq)
    a_fox = _attention(q, kt, va, negc, batch, seq)
    x1, logits = _mix(a_sgu, a_fox, gates, x2, w_proj_sgu[l].astype(BF16),
                      w_proj_fox[l].astype(BF16), w_out[l].astype(BF16),
                      norm2_g[l].reshape(1, d), w_r, b_r)

    n_assign = 2 * t
    n_rows = n_assign + N_EXPERTS * EXPERT_BLOCK
    nb = n_rows // EXPERT_BLOCK
    nb_rows = -(-nb // 8) * 8
    dest_t, meta_f, bmeta, emeta = _route(logits, nb_rows)
    dest1, dest2 = dest_t[:, 0, :].reshape(t), dest_t[:, 1, :].reshape(t)
    n_used = bmeta[0:1, 1]
    xs = _dispatch(dest1, dest2, emeta[0, :N_EXPERTS], emeta[1, :N_EXPERTS], emeta[2, :N_EXPERTS],
                   n_used, x1, n_rows)
    yb = _experts(bmeta[:nb, 0], n_used, emeta[2, :N_EXPERTS], xs, norm2_g[l].reshape(1, d),
                  w_expert_gate[l], w_expert_up[l], w_expert_down[l])
    out = _combine(dest1, dest2, x1, meta_f, yb)
    return out.reshape(batch, seq, d)
```

```python
import functools

import jax
import jax.numpy as jnp
from jax import lax
from jax.experimental import pallas as pl
from jax.experimental.pallas import tpu as pltpu

F32 = jnp.float32
BF16 = jnp.bfloat16
I32 = jnp.int32

D_MODEL = 2048
CHUNK = 128
SGU_GROUPS = 8
SGU_WIDTH = 1024
FOX_HEADS = 8
HEAD_DIM = 128
FOX_WIDTH = 1024
N_GROUPS = 4
EXPERTS_PER_GROUP = 8
N_EXPERTS = 32
D_EXPERT = 512
EXPERT_BLOCK = 128
EPS = 1e-6

LANES = 128
PREFETCH_PRIORITY = 1
VMEM_LIMIT = 56 * 1024 * 1024

SEC = 1024
N_MAIN_SEC = 5
N_GATE_SEC = 4
TM_NORM = 1024
TM_IN = 512
TQ = 256
TM_MIX = 256
RCH = 512
TM_CMB = 512
ATTN_SCALE = HEAD_DIM ** -0.5


def _sigmoid(x):
    return 1.0 / (1.0 + jnp.exp(-x))


def _log_sigmoid(x):
    return jnp.minimum(x, 0.0) - jnp.log1p(jnp.exp(-jnp.abs(x)))


def _norm1_kernel(x_ref, g1_ref, wf_ref, bf_ref, h_ref, lf_ref):
    x = x_ref[...]
    ms = jnp.mean(x * x, axis=-1, keepdims=True)
    hb = ((x * lax.rsqrt(ms + EPS)) * g1_ref[...]).astype(BF16)
    h_ref[...] = hb
    f = jnp.dot(hb, wf_ref[...], preferred_element_type=F32) + bf_ref[...]
    lf_ref[...] = _log_sigmoid(f)


def _norm1(x2, g1, w_f, b_f):
    t = x2.shape[0]
    row = lambda i: (i, 0)
    const = lambda i: (0, 0)
    return pl.pallas_call(
        _norm1_kernel,
        out_shape=(jax.ShapeDtypeStruct((t, D_MODEL), BF16),
                   jax.ShapeDtypeStruct((t, LANES), F32)),
        grid=(t // TM_NORM,),
        in_specs=[pl.BlockSpec((TM_NORM, D_MODEL), row),
                  pl.BlockSpec((1, D_MODEL), const),
                  pl.BlockSpec((D_MODEL, LANES), const),
                  pl.BlockSpec((1, LANES), const)],
        out_specs=(pl.BlockSpec((TM_NORM, D_MODEL), row), pl.BlockSpec((TM_NORM, LANES), row)),
        compiler_params=pltpu.CompilerParams(
            dimension_semantics=("arbitrary",), vmem_limit_bytes=VMEM_LIMIT),
        name="norm1",
    )(x2, g1, w_f, b_f)


N_SECTIONS = 1 + 3 + N_GATE_SEC
N_CHUNKS = N_MAIN_SEC + N_GATE_SEC
GATE_SECTION0 = 4
WCONV_ROWS = 256


def _inproj_kernel(h_ref, wt_hbm, bg_ref, lng_ref, lnb_ref, wsp_ref, bsp_ref, qg_ref, kg_ref,
                   asgu_ref, q_ref, kt_ref, va_ref, gates_ref,
                   stage, wb, sem):
    s = pl.program_id(0)
    i = pl.program_id(1)
    tm = h_ref.shape[0]

    def chunk_copy(c, slot):
        r0 = pl.multiple_of(c * SEC + jnp.where(c >= N_MAIN_SEC, FOX_HEADS, 0), 8)
        return pltpu.make_async_copy(wt_hbm.at[pl.ds(r0, SEC), :], stage.at[slot], sem.at[slot])

    def convert(slot, k):
        for p in range(SEC // WCONV_ROWS):
            rs = slice(p * WCONV_ROWS, (p + 1) * WCONV_ROWS)
            wb[k, :, rs] = stage[slot, rs, :].T.astype(BF16)

    @pl.when((s == 0) & (i == 0))
    def _():
        chunk_copy(0, 0).start()
        chunk_copy(1, 1).start()
        chunk_copy(0, 0).wait()
        convert(0, 0)
        chunk_copy(2, 0).start(priority=PREFETCH_PRIORITY)
        chunk_copy(1, 1).wait()
        convert(1, 1)

    @pl.when((s > 0) & (i == 0))
    def _():
        c = s + 1
        slot = c % 2
        chunk_copy(c, slot).wait()

        @pl.when(c + 1 < N_CHUNKS)
        def _():
            chunk_copy(c + 1, 1 - slot).start(priority=PREFETCH_PRIORITY)

        convert(slot, 0)

    def section(k=0):
        return jnp.dot(h_ref[...], wb[k], preferred_element_type=F32)

    @pl.when(s == 0)
    def _():
        u = jax.nn.gelu(section(0))
        v = jax.nn.gelu(section(1))
        mu = jnp.mean(v, axis=-1, keepdims=True)
        vc = v - mu
        var = jnp.mean(vc * vc, axis=-1, keepdims=True)
        vn = ((vc * lax.rsqrt(var + EPS)) * lng_ref[...] + lnb_ref[...]).astype(BF16)
        row = lax.broadcasted_iota(I32, (CHUNK, CHUNK), 0)
        col = lax.broadcasted_iota(I32, (CHUNK, CHUNK), 1)
        causal = row >= col
        for g in range(SGU_GROUPS):
            wg = jnp.where(causal, wsp_ref[g], 0.0).astype(BF16)
            bcol = bsp_ref[:, g:g + 1]
            gs = slice(g * LANES, (g + 1) * LANES)
            for c in range(tm // CHUNK):
                rs = slice(c * CHUNK, (c + 1) * CHUNK)
                sg = jnp.dot(wg, vn[rs, gs], preferred_element_type=F32) + bcol
                asgu_ref[rs, gs] = (u[rs, gs] * sg).astype(BF16)

    def _head_norm(z, h, gain_ref):
        zh = z[:, h * HEAD_DIM:(h + 1) * HEAD_DIM]
        ms = jnp.mean(zh * zh, axis=-1, keepdims=True)
        return (zh * lax.rsqrt(ms + EPS)) * gain_ref[...]

    @pl.when(s == 1)
    def _():
        z = section()
        for h in range(FOX_HEADS):
            q_ref[:, h * HEAD_DIM:(h + 1) * HEAD_DIM] = (
                _head_norm(z, h, qg_ref) * ATTN_SCALE).astype(BF16)

    @pl.when(s == 2)
    def _():
        z = section()
        for h in range(FOX_HEADS):
            kt_ref[h * HEAD_DIM:(h + 1) * HEAD_DIM, :] = _head_norm(z, h, kg_ref).T.astype(BF16)

    @pl.when(s == 3)
    def _():
        va_ref[...] = section().astype(BF16)

    @pl.when(s >= GATE_SECTION0)
    def _():
        gates_ref[...] = (0.5 * jnp.tanh(0.5 * (section() + bg_ref[...])) + 0.5).astype(BF16)


def _inproj(h, wt, b_gate, ln_g, ln_b, w_sp, b_sp_t, q_g, k_g):
    t = h.shape[0]
    n_i = t // TM_IN

    def active(sec):
        return lambda s, i: jnp.where(s < sec, 0, jnp.where(s > sec, n_i - 1, i))

    def gate_blk(s, i):
        g = jnp.clip(s - GATE_SECTION0, 0, N_GATE_SEC - 1)
        return (jnp.where(s < GATE_SECTION0, 0, i), g)

    const2 = lambda s, i: (0, 0)
    out_shape = (
        jax.ShapeDtypeStruct((t, SGU_WIDTH), BF16),
        jax.ShapeDtypeStruct((t, FOX_WIDTH), BF16),
        jax.ShapeDtypeStruct((FOX_WIDTH, t), BF16),
        jax.ShapeDtypeStruct((t, FOX_WIDTH), BF16),
        jax.ShapeDtypeStruct((t, 2 * D_MODEL), BF16),
    )
    return pl.pallas_call(
        _inproj_kernel,
        out_shape=out_shape,
        grid=(N_SECTIONS, n_i),
        in_specs=[
            pl.BlockSpec((TM_IN, D_MODEL), lambda s, i: (i, 0)),
            pl.BlockSpec(memory_space=pl.ANY),
            pl.BlockSpec((1, SEC), lambda s, i: (0, jnp.clip(s - GATE_SECTION0, 0, N_GATE_SEC - 1))),
            pl.BlockSpec((1, SGU_WIDTH), const2),
            pl.BlockSpec((1, SGU_WIDTH), const2),
            pl.BlockSpec((SGU_GROUPS, CHUNK, CHUNK), lambda s, i: (0, 0, 0)),
            pl.BlockSpec((CHUNK, SGU_GROUPS), const2),
            pl.BlockSpec((1, HEAD_DIM), const2),
            pl.BlockSpec((1, HEAD_DIM), const2),
        ],
        out_specs=(
            pl.BlockSpec((TM_IN, SGU_WIDTH), lambda s, i: (active(0)(s, i), 0)),
            pl.BlockSpec((TM_IN, FOX_WIDTH), lambda s, i: (active(1)(s, i), 0)),
            pl.BlockSpec((FOX_WIDTH, TM_IN), lambda s, i: (0, active(2)(s, i))),
            pl.BlockSpec((TM_IN, FOX_WIDTH), lambda s, i: (active(3)(s, i), 0)),
            pl.BlockSpec((TM_IN, SEC), gate_blk),
        ),
        scratch_shapes=[pltpu.VMEM((2, SEC, D_MODEL), F32),
                        pltpu.VMEM((2, D_MODEL, SEC), BF16),
                        pltpu.SemaphoreType.DMA((2,))],
        compiler_params=pltpu.CompilerParams(
            dimension_semantics=("arbitrary", "arbitrary"), vmem_limit_bytes=VMEM_LIMIT),
        name="inproj",
    )(h, wt, b_gate, ln_g, ln_b, w_sp, b_sp_t, q_g, k_g)


def _split3(x):
    x0 = x.astype(BF16)
    r1 = x - x0.astype(F32)
    x1 = r1.astype(BF16)
    r2 = r1 - x1.astype(F32)
    return x0, x1, r2.astype(BF16)


def _cumsum_kernel(lf_ref, negc_ref, c_s):
    seq = lf_ref.shape[0]
    row = lax.broadcasted_iota(I32, (CHUNK, CHUNK), 0)
    col = lax.broadcasted_iota(I32, (CHUNK, CHUNK), 1)
    tri = jnp.where(row >= col, 1.0, 0.0).astype(BF16)
    carry = jnp.zeros((1, LANES), F32)
    for r in range(seq // CHUNK):
        rs = slice(r * CHUNK, (r + 1) * CHUNK)
        x0, x1, x2 = _split3(lf_ref[rs, :])
        cs = (jnp.dot(tri, x0, preferred_element_type=F32)
              + jnp.dot(tri, x1, preferred_element_type=F32)
              + jnp.dot(tri, x2, preferred_element_type=F32)) + carry
        carry = cs[CHUNK - 1:CHUNK, :]
        c_s[rs, :] = cs
    ct = c_s[...].T
    negc_ref[0] = -ct[0:FOX_HEADS, :]


def _forget_cumsum(lf, batch, seq):
    return pl.pallas_call(
        _cumsum_kernel,
        out_shape=jax.ShapeDtypeStruct((batch, FOX_HEADS, seq), F32),
        grid=(batch,),
        in_specs=[pl.BlockSpec((seq, LANES), lambda b: (b, 0))],
        out_specs=pl.BlockSpec((1, FOX_HEADS, seq), lambda b: (b, 0, 0)),
        scratch_shapes=[pltpu.VMEM((seq, LANES), F32)],
        compiler_params=pltpu.CompilerParams(
            dimension_semantics=("arbitrary",), vmem_limit_bytes=VMEM_LIMIT),
        name="forget_cumsum",
    )(lf)


def _attn_kernel(q_ref, kt_ref, v_ref, negc_ref, o_ref):
    seq = q_ref.shape[0]
    row = lax.broadcasted_iota(I32, (TQ, TQ), 0)
    col = lax.broadcasted_iota(I32, (TQ, TQ), 1)
    causal = row >= col
    for qi in range(seq // TQ):
        k0 = qi * TQ
        q = q_ref[k0:k0 + TQ, :]
        s_d = jnp.dot(q, kt_ref[:, k0:k0 + TQ], preferred_element_type=F32)
        s_d = jnp.where(causal, s_d + negc_ref[0, 0, :, k0:k0 + TQ], -jnp.inf)
        m = jnp.max(s_d, axis=-1, keepdims=True)
        if qi > 0:
            s_o = jnp.dot(q, kt_ref[:, 0:k0], preferred_element_type=F32) + negc_ref[0, 0, :, 0:k0]
            m = jnp.maximum(m, jnp.max(s_o, axis=-1, keepdims=True))
        p_d = jnp.exp(s_d - m)
        l = jnp.sum(p_d, axis=-1, keepdims=True)
        acc = jnp.dot(p_d.astype(BF16), v_ref[k0:k0 + TQ, :], preferred_element_type=F32)
        if qi > 0:
            p_o = jnp.exp(s_o - m)
            l = l + jnp.sum(p_o, axis=-1, keepdims=True)
            acc = acc + jnp.dot(p_o.astype(BF16), v_ref[0:k0, :], preferred_element_type=F32)
        o_ref[k0:k0 + TQ, :] = (acc * (1.0 / l)).astype(BF16)


def _attention(q, kt, va, negc, batch, seq):
    t = q.shape[0]
    blk = pl.BlockSpec((seq, HEAD_DIM), lambda b, h: (b, h))
    return pl.pallas_call(
        _attn_kernel,
        out_shape=jax.ShapeDtypeStruct((t, FOX_WIDTH), BF16),
        grid=(batch, FOX_HEADS),
        in_specs=[blk,
                  pl.BlockSpec((HEAD_DIM, seq), lambda b, h: (h, b)),
                  blk,
                  pl.BlockSpec((1, 1, 1, seq), lambda b, h: (b, h, 0, 0))],
        out_specs=blk,
        compiler_params=pltpu.CompilerParams(
            dimension_semantics=("arbitrary", "arbitrary"), vmem_limit_bytes=VMEM_LIMIT),
        name="fox_attention",
    )(q, kt, va, negc)


def _mix_kernel(as_ref, af_ref, g_ref, x_ref, wps_ref, wpf_ref, wo_ref, g2_ref, wr_ref, br_ref,
                x1_ref, lg_ref):
    ys = jnp.dot(as_ref[...], wps_ref[...], preferred_element_type=F32)
    yf = jnp.dot(af_ref[...], wpf_ref[...], preferred_element_type=F32)
    m = (g_ref[:, :D_MODEL].astype(F32) * ys + g_ref[:, D_MODEL:].astype(F32) * yf).astype(BF16)
    x1 = x_ref[...] + jnp.dot(m, wo_ref[...], preferred_element_type=F32)
    x1_ref[...] = x1
    ms = jnp.mean(x1 * x1, axis=-1, keepdims=True)
    h2 = ((x1 * lax.rsqrt(ms + EPS)) * g2_ref[...]).astype(BF16)
    lg_ref[...] = jnp.dot(h2, wr_ref[...], preferred_element_type=F32) + br_ref[...]


def _mix(a_sgu, a_fox, gates, x2, wps, wpf, wo, g2, w_r, b_r):
    t = x2.shape[0]
    row = lambda i: (i, 0)
    const = lambda i: (0, 0)
    resident = functools.partial(pl.BlockSpec, index_map=const, pipeline_mode=pl.Buffered(1))
    return pl.pallas_call(
        _mix_kernel,
        out_shape=(jax.ShapeDtypeStruct((t, D_MODEL), F32),
                   jax.ShapeDtypeStruct((t, LANES), F32)),
        grid=(t // TM_MIX,),
        in_specs=[
            pl.BlockSpec((TM_MIX, SGU_WIDTH), row),
            pl.BlockSpec((TM_MIX, FOX_WIDTH), row),
            pl.BlockSpec((TM_MIX, 2 * D_MODEL), row),
            pl.BlockSpec((TM_MIX, D_MODEL), row),
            resident((SGU_WIDTH, D_MODEL)),
            resident((FOX_WIDTH, D_MODEL)),
            resident((D_MODEL, D_MODEL)),
            pl.BlockSpec((1, D_MODEL), const),
            resident((D_MODEL, LANES)),
            pl.BlockSpec((1, LANES), const),
        ],
        out_specs=(pl.BlockSpec((TM_MIX, D_MODEL), row), pl.BlockSpec((TM_MIX, LANES), row)),
        compiler_params=pltpu.CompilerParams(
            dimension_semantics=("arbitrary",), vmem_limit_bytes=VMEM_LIMIT),
        name="mix",
    )(a_sgu, a_fox, gates, x2, wps, wpf, wo, g2, w_r, b_r)


def _route_kernel(lg_ref, dt_ref, mf_ref, bm_ref, em_ref, mi_ref):
    t = lg_ref.shape[0]
    n_chunks = t // RCH
    lane_i = lax.broadcasted_iota(I32, (RCH, LANES), 1)
    lane = lane_i.astype(F32)
    lane_grp = ((lane_i - N_GROUPS) >> 3).astype(F32)
    is_grp = lane_i < N_GROUPS
    is_exp = (lane_i >= N_GROUPS) & (lane_i < N_GROUPS + N_EXPERTS)
    r_i = lax.broadcasted_iota(I32, (RCH, RCH), 0)
    c_i = lax.broadcasted_iota(I32, (RCH, RCH), 1)
    strict_lower = jnp.where(r_i > c_i, 1.0, 0.0).astype(BF16)
    neg_inf = -jnp.inf

    def first_max(vals):
        vmax = jnp.max(vals, axis=-1, keepdims=True)
        idx = jnp.min(jnp.where(vals == vmax, lane, float(LANES)), axis=-1, keepdims=True)
        return vmax, idx

    def pick(table, idx):
        return jnp.sum(jnp.where(lane == idx, table, 0.0), axis=-1, keepdims=True)

    def pack(cols):
        out = jnp.zeros((RCH, LANES), F32)
        for n, c in enumerate(cols):
            out = jnp.where(lane_i == n, c, out)
        return out

    def pass1(ci, counts):
        r0 = pl.multiple_of(ci * RCH, RCH)
        lg = lg_ref[pl.ds(r0, RCH), :]
        gmax, grp = first_max(jnp.where(is_grp, lg, neg_inf))
        p_grp = 1.0 / jnp.sum(jnp.where(is_grp, jnp.exp(lg - gmax), 0.0), axis=-1, keepdims=True)
        el = jnp.where(is_exp & (lane_grp == grp), lg, neg_inf)
        v1, i1 = first_max(el)
        v2, i2 = first_max(jnp.where(lane == i1, neg_inf, el))
        e21 = jnp.exp(v2 - v1)
        w1 = p_grp / (1.0 + e21)
        w2 = p_grp * e21 / (1.0 + e21)
        e1 = i1 - float(N_GROUPS)
        e2 = i2 - float(N_GROUPS)
        hot = jnp.where((lane == e1) | (lane == e2), 1.0, 0.0)
        before = jnp.dot(strict_lower, hot.astype(BF16), preferred_element_type=F32) + counts
        mi_ref[pl.ds(r0, RCH), :] = pack([e1, e2, pick(before, e1), pick(before, e2)]).astype(I32)
        mf_ref[pl.ds(r0, RCH), :] = pack([w1, w2])
        return counts + jnp.sum(hot, axis=0, keepdims=True)

    counts = lax.fori_loop(0, n_chunks, pass1, jnp.zeros((1, LANES), F32))

    nblk = jnp.floor((counts + float(EXPERT_BLOCK - 1)) * (1.0 / EXPERT_BLOCK))
    u_r = lax.broadcasted_iota(I32, (LANES, LANES), 0)
    u_c = lax.broadcasted_iota(I32, (LANES, LANES), 1)
    strict_upper = jnp.where(u_r < u_c, 1.0, 0.0).astype(BF16)
    bstart = jnp.dot(jnp.broadcast_to(nblk, (8, LANES)).astype(BF16), strict_upper,
                     preferred_element_type=F32)[0:1, :]
    bend = bstart + nblk

    def pass2(ci, _):
        r0 = pl.multiple_of(ci * RCH, RCH)
        mi = mi_ref[pl.ds(r0, RCH), :].astype(F32)
        cols = [jnp.sum(jnp.where(lane_i == n, mi, 0.0), axis=-1, keepdims=True) for n in range(4)]
        d1 = pick(bstart, cols[0]) * float(EXPERT_BLOCK) + cols[2]
        d2 = pick(bstart, cols[1]) * float(EXPERT_BLOCK) + cols[3]
        dt_ref[ci] = pack([d1, d2]).T[0:8, :].astype(I32)
        return 0

    lax.fori_loop(0, n_chunks, pass2, 0)

    nb_rows = bm_ref.shape[0]
    b_col = lax.broadcasted_iota(I32, (nb_rows, LANES), 0).astype(F32)
    b_lane = lax.broadcasted_iota(I32, (nb_rows, LANES), 1)
    done = jnp.where((bend <= b_col) & (b_lane < N_EXPERTS), 1.0, 0.0)
    blk_e = jnp.minimum(jnp.sum(done, axis=-1, keepdims=True), float(N_EXPERTS - 1))
    n_used = jnp.sum(jnp.where(b_lane[0:1, :] == N_EXPERTS - 1, bend, 0.0), axis=-1, keepdims=True)
    bm_ref[...] = jnp.where(b_lane == 0, blk_e, jnp.where(b_lane == 1, n_used, 0.0)).astype(I32)

    e_row = lax.broadcasted_iota(I32, (8, LANES), 0)
    em_ref[...] = jnp.where(e_row == 0, counts,
                            jnp.where(e_row == 1, bstart, jnp.where(e_row == 2, nblk, 0.0))).astype(I32)


def _route(logits, nb_rows):
    t = logits.shape[0]
    full = lambda shape: pl.BlockSpec(shape, lambda: (0,) * len(shape))
    return pl.pallas_call(
        _route_kernel,
        out_shape=(jax.ShapeDtypeStruct((t // RCH, 8, RCH), I32),
                   jax.ShapeDtypeStruct((t, LANES), F32),
                   jax.ShapeDtypeStruct((nb_rows, LANES), I32),
                   jax.ShapeDtypeStruct((8, LANES), I32)),
        in_specs=[full((t, LANES))],
        out_specs=(full((t // RCH, 8, RCH)), full((t, LANES)), full((nb_rows, LANES)),
                   full((8, LANES))),
        scratch_shapes=[pltpu.VMEM((t, LANES), I32)],
        compiler_params=pltpu.CompilerParams(vmem_limit_bytes=VMEM_LIMIT),
        name="route",
    )(logits)


DISPATCH_UNROLL = 8
TM_DSP = 1024


def _dispatch_kernel(d1_ref, d2_ref, cnt_ref, bstart_ref, nblk_ref, nused_ref, x_ref, xs_ref, sem):
    i = pl.program_id(0)
    tm = x_ref.shape[0]
    nb = xs_ref.shape[0] // EXPERT_BLOCK

    def row_copy(src_row, dst_row):
        return pltpu.make_async_copy(x_ref.at[pl.ds(src_row, 1), :],
                                     xs_ref.at[pl.ds(dst_row, 1), :], sem)

    def block_copy(dst_block):
        dst_row = pl.multiple_of(dst_block * EXPERT_BLOCK, EXPERT_BLOCK)
        return pltpu.make_async_copy(x_ref.at[pl.ds(0, EXPERT_BLOCK), :],
                                     xs_ref.at[pl.ds(dst_row, EXPERT_BLOCK), :], sem)

    def tok_body(r, _):
        tok = i * tm + r
        row_copy(r, d1_ref[tok]).start()
        row_copy(r, d2_ref[tok]).start()
        return 0

    lax.fori_loop(0, tm, tok_body, 0, unroll=DISPATCH_UNROLL)

    @pl.when(i == 0)
    def _():
        def pad_expert(e, n_pad):
            lo = bstart_ref[e] * EXPERT_BLOCK + cnt_ref[e]
            hi = (bstart_ref[e] + nblk_ref[e]) * EXPERT_BLOCK

            def pad_body(r, _):
                row_copy(0, r).start()
                return 0

            lax.fori_loop(lo, hi, pad_body, 0)
            return n_pad + (hi - lo)

        n_pad = lax.fori_loop(0, N_EXPERTS, pad_expert, 0)
        n_used = nused_ref[0]

        def fill_body(b, _):
            block_copy(b).start()
            return 0

        lax.fori_loop(n_used, nb, fill_body, 0)

        def block_wait(_, c):
            block_copy(0).wait()
            return c

        lax.fori_loop(n_used, nb, block_wait, 0)

        def row_wait(_, c):
            row_copy(0, 0).wait()
            return c

        lax.fori_loop(0, n_pad, row_wait, 0)

    for _ in range((2 * tm) // EXPERT_BLOCK):
        block_copy(0).wait()


def _dispatch(dest1, dest2, cnt, bstart, nblk, n_used, x1, n_rows):
    t = x1.shape[0]
    grid_spec = pltpu.PrefetchScalarGridSpec(
        num_scalar_prefetch=6,
        grid=(t // TM_DSP,),
        in_specs=[pl.BlockSpec((TM_DSP, D_MODEL), lambda i, *_: (i, 0))],
        out_specs=pl.BlockSpec(memory_space=pl.ANY),
        scratch_shapes=[pltpu.SemaphoreType.DMA(())],
    )
    return pl.pallas_call(
        _dispatch_kernel,
        out_shape=jax.ShapeDtypeStruct((n_rows, D_MODEL), F32),
        grid_spec=grid_spec,
        compiler_params=pltpu.CompilerParams(
            dimension_semantics=("arbitrary",), vmem_limit_bytes=VMEM_LIMIT),
        name="dispatch",
    )(dest1, dest2, cnt, bstart, nblk, n_used, x1)


def _experts_kernel(blk_e_ref, nused_ref, nblk_ref, x_ref, g2_ref, wg_hbm, wu_hbm, wd_hbm, y_ref,
                    wg_f, wu_f, wd_f, wg_s, wu_s, wd_s, slot_ref, sem):
    b = pl.program_id(0)
    n_used = nused_ref[0]
    used = b < n_used
    e = blk_e_ref[b]
    new_expert = (b == 0) | (e != blk_e_ref[jnp.maximum(b - 1, 0)])

    def weight_copies(expert, slot):
        return (pltpu.make_async_copy(wg_hbm.at[expert], wg_f.at[slot], sem.at[slot]),
                pltpu.make_async_copy(wu_hbm.at[expert], wu_f.at[slot], sem.at[slot]),
                pltpu.make_async_copy(wd_hbm.at[expert], wd_f.at[slot], sem.at[slot]))

    @pl.when(b == 0)
    def _():
        slot_ref[0] = 0
        for cp in weight_copies(e, 0):
            cp.start()

    @pl.when(used & new_expert)
    def _():
        slot = slot_ref[0]
        b_next = b + nblk_ref[e]

        @pl.when(b_next < n_used)
        def _():
            for cp in weight_copies(blk_e_ref[jnp.minimum(b_next, n_used - 1)], 1 - slot):
                cp.start(priority=PREFETCH_PRIORITY)

        for cp in weight_copies(e, slot):
            cp.wait()
        wg_s[...] = wg_f[slot].astype(BF16)
        wu_s[...] = wu_f[slot].astype(BF16)
        wd_s[...] = wd_f[slot].astype(BF16)
        slot_ref[0] = 1 - slot

    @pl.when(used)
    def _():
        x = x_ref[...]
        ms = jnp.mean(x * x, axis=-1, keepdims=True)
        h = ((x * lax.rsqrt(ms + EPS)) * g2_ref[...]).astype(BF16)
        a = jnp.dot(h, wg_s[...], preferred_element_type=F32)
        u = jnp.dot(h, wu_s[...], preferred_element_type=F32)
        mid = ((a * _sigmoid(a)) * u).astype(BF16)
        y_ref[...] = jnp.dot(mid, wd_s[...], preferred_element_type=F32)

    @pl.when(jnp.logical_not(used))
    def _():
        y_ref[...] = jnp.zeros_like(y_ref)


def _experts(blk_e, n_used, nblk, xs, g2, w_g, w_u, w_d):
    n_rows = xs.shape[0]
    nb = n_rows // EXPERT_BLOCK

    def row_blk(b, blk_e_ref, nused_ref, nblk_ref):
        return (jnp.minimum(b, nused_ref[0] - 1), 0)

    hbm = pl.BlockSpec(memory_space=pl.ANY)
    grid_spec = pltpu.PrefetchScalarGridSpec(
        num_scalar_prefetch=3,
        grid=(nb,),
        in_specs=[
            pl.BlockSpec((EXPERT_BLOCK, D_MODEL), row_blk),
            pl.BlockSpec((1, D_MODEL), lambda b, *_: (0, 0)),
            hbm, hbm, hbm,
        ],
        out_specs=pl.BlockSpec((EXPERT_BLOCK, D_MODEL), lambda b, *_: (b, 0)),
        scratch_shapes=[pltpu.VMEM((2, D_MODEL, D_EXPERT), F32),
                        pltpu.VMEM((2, D_MODEL, D_EXPERT), F32),
                        pltpu.VMEM((2, D_EXPERT, D_MODEL), F32),
                        pltpu.VMEM((D_MODEL, D_EXPERT), BF16),
                        pltpu.VMEM((D_MODEL, D_EXPERT), BF16),
                        pltpu.VMEM((D_EXPERT, D_MODEL), BF16),
                        pltpu.SMEM((1,), I32),
                        pltpu.SemaphoreType.DMA((2,))],
    )
    return pl.pallas_call(
        _experts_kernel,
        out_shape=jax.ShapeDtypeStruct((n_rows, D_MODEL), F32),
        grid_spec=grid_spec,
        compiler_params=pltpu.CompilerParams(
            dimension_semantics=("arbitrary",), vmem_limit_bytes=VMEM_LIMIT),
        name="experts",
    )(blk_e, n_used, nblk, xs, g2, w_g, w_u, w_d)


COMBINE_UNROLL = 8


def _combine_kernel(d1_ref, d2_ref, x1_ref, w_ref, y_ref, o_ref, gbuf, sem):
    i = pl.program_id(0)
    n = pl.num_programs(0)
    tm = x1_ref.shape[0]

    def issue(tile, slot):
        def body(r, _):
            tok = tile * tm + r
            pltpu.make_async_copy(y_ref.at[pl.ds(d1_ref[tok], 1), :],
                                  gbuf.at[slot, pl.ds(r, 1), :], sem.at[slot]).start()
            pltpu.make_async_copy(y_ref.at[pl.ds(d2_ref[tok], 1), :],
                                  gbuf.at[slot, pl.ds(tm + r, 1), :], sem.at[slot]).start()
            return 0

        lax.fori_loop(0, tm, body, 0, unroll=COMBINE_UNROLL)

    slot = i % 2

    @pl.when(i == 0)
    def _():
        issue(0, 0)

    @pl.when(i + 1 < n)
    def _():
        issue(i + 1, 1 - slot)

    pltpu.make_async_copy(y_ref.at[pl.ds(0, 2 * tm), :], gbuf.at[slot], sem.at[slot]).wait()
    w = w_ref[...]
    g = gbuf[slot]
    o_ref[...] = x1_ref[...] + (w[:, 0:1] * g[0:tm, :] + w[:, 1:2] * g[tm:2 * tm, :])


def _combine(dest1, dest2, x1, w, yb):
    t = x1.shape[0]
    grid_spec = pltpu.PrefetchScalarGridSpec(
        num_scalar_prefetch=2,
        grid=(t // TM_CMB,),
        in_specs=[
            pl.BlockSpec((TM_CMB, D_MODEL), lambda i, a, b: (i, 0)),
            pl.BlockSpec((TM_CMB, LANES), lambda i, a, b: (i, 0)),
            pl.BlockSpec(memory_space=pl.ANY),
        ],
        out_specs=pl.BlockSpec((TM_CMB, D_MODEL), lambda i, a, b: (i, 0)),
        scratch_shapes=[pltpu.VMEM((2, 2 * TM_CMB, D_MODEL), F32),
                        pltpu.SemaphoreType.DMA((2,))],
    )
    return pl.pallas_call(
        _combine_kernel,
        out_shape=jax.ShapeDtypeStruct((t, D_MODEL), F32),
        grid_spec=grid_spec,
        compiler_params=pltpu.CompilerParams(
            dimension_semantics=("arbitrary",), vmem_limit_bytes=VMEM_LIMIT),
        name="combine",
    )(dest1, dest2, x1, w, yb)


def kernel(x, norm1_g, w_in, b_gate, b_forget, sgu_ln_g, sgu_ln_b, w_spatial, b_spatial, q_norm_g, k_norm_g, w_proj_sgu, w_proj_fox, w_out, norm2_g, w_router_group, b_router_group, w_router_expert, b_router_expert, w_expert_gate, w_expert_up, w_expert_down):
    batch, seq, d = x.shape
    t = batch * seq
    l = 0
    x2 = x.reshape(t, d)

    wt = jnp.swapaxes(w_in[l], 0, 1)
    off_f = N_MAIN_SEC * SEC
    w_f = jnp.pad(wt[off_f:off_f + FOX_HEADS].T, ((0, 0), (0, LANES - FOX_HEADS))).astype(BF16)
    b_f = jnp.pad(b_forget[l], (0, LANES - FOX_HEADS)).reshape(1, LANES)
    n_r = N_GROUPS + N_EXPERTS
    w_r = jnp.pad(jnp.concatenate([w_router_group[l], w_router_expert[l]], axis=1),
                  ((0, 0), (0, LANES - n_r))).astype(BF16)
    b_r = jnp.pad(jnp.concatenate([b_router_group[l], b_router_expert[l]]),
                  (0, LANES - n_r)).reshape(1, LANES)

    h, lf = _norm1(x2, norm1_g[l].reshape(1, d), w_f, b_f)
    a_sgu, q, kt, va, gates = _inproj(
        h, wt, b_gate[l].reshape(1, 2 * d),
        sgu_ln_g[l].reshape(1, SGU_WIDTH), sgu_ln_b[l].reshape(1, SGU_WIDTH),
        w_spatial[l], b_spatial[l].T, q_norm_g[l].reshape(1, HEAD_DIM),
        k_norm_g[l].reshape(1, HEAD_DIM))
    negc = _forget_cumsum(lf, batch, seq).reshape(batch, FOX_HEADS, 1, seq)
    a_fox = _attention(q, kt, va, negc, batch, seq)
    x1, logits = _mix(a_sgu, a_fox, gates, x2, w_proj_sgu[l].astype(BF16),
                      w_proj_fox[l].astype(BF16), w_out[l].astype(BF16),
                      norm2_g[l].reshape(1, d), w_r, b_r)

    n_assign = 2 * t
    n_rows = n_assign + N_EXPERTS * EXPERT_BLOCK
    nb = n_rows // EXPERT_BLOCK
    nb_rows = -(-nb // 8) * 8
    dest_t, meta_f, bmeta, emeta = _route(logits, nb_rows)
    dest1, dest2 = dest_t[:, 0, :].reshape(t), dest_t[:, 1, :].reshape(t)
    n_used = bmeta[0:1, 1]
    xs = _dispatch(dest1, dest2, emeta[0, :N_EXPERTS], emeta[1, :N_EXPERTS], emeta[2, :N_EXPERTS],
                   n_used, x1, n_rows)
    yb = _experts(bmeta[:nb, 0], n_used, emeta[2, :N_EXPERTS], xs, norm2_g[l].reshape(1, d),
                  w_expert_gate[l], w_expert_up[l], w_expert_down[l])
    out = _combine(dest1, dest2, x1, meta_f, yb)
    return out.reshape(batch, seq, d)
```

```python
import functools

import jax
import jax.numpy as jnp
from jax import lax
from jax.experimental import pallas as pl
from jax.experimental.pallas import tpu as pltpu

F32 = jnp.float32
BF16 = jnp.bfloat16
I32 = jnp.int32

D_MODEL = 2048
CHUNK = 128
SGU_GROUPS = 8
SGU_WIDTH = 1024
FOX_HEADS = 8
HEAD_DIM = 128
FOX_WIDTH = 1024
N_GROUPS = 4
EXPERTS_PER_GROUP = 8
N_EXPERTS = 32
D_EXPERT = 512
EXPERT_BLOCK = 128
EPS = 1e-6

LANES = 128
PREFETCH_PRIORITY = 1
VMEM_LIMIT = 56 * 1024 * 1024

SEC = 1024
N_MAIN_SEC = 5
N_GATE_SEC = 4
TM_NORM = 1024
TM_IN = 512
TQ = 256
HEADS_PER_STEP = 4
TM_MIX = 256
RCH = 512
TM_CMB = 512
LOG2E = 1.4426950408889634
Q_SCALE = HEAD_DIM ** -0.5 * LOG2E


def _sigmoid(x):
    return 1.0 / (1.0 + jnp.exp(-x))


def _log_sigmoid(x):
    return jnp.minimum(x, 0.0) - jnp.log1p(jnp.exp(-jnp.abs(x)))


def _norm1_kernel(x_ref, g1_ref, wf_ref, bf_ref, h_ref, lf_ref):
    x = x_ref[...]
    ms = jnp.mean(x * x, axis=-1, keepdims=True)
    hb = ((x * lax.rsqrt(ms + EPS)) * g1_ref[...]).astype(BF16)
    h_ref[...] = hb
    f = jnp.dot(hb, wf_ref[...], preferred_element_type=F32) + bf_ref[...]
    lf_ref[...] = _log_sigmoid(f)


def _norm1(x2, g1, w_f, b_f):
    t = x2.shape[0]
    row = lambda i: (i, 0)
    const = lambda i: (0, 0)
    return pl.pallas_call(
        _norm1_kernel,
        out_shape=(jax.ShapeDtypeStruct((t, D_MODEL), BF16),
                   jax.ShapeDtypeStruct((t, LANES), F32)),
        grid=(t // TM_NORM,),
        in_specs=[pl.BlockSpec((TM_NORM, D_MODEL), row),
                  pl.BlockSpec((1, D_MODEL), const),
                  pl.BlockSpec((D_MODEL, LANES), const),
                  pl.BlockSpec((1, LANES), const)],
        out_specs=(pl.BlockSpec((TM_NORM, D_MODEL), row), pl.BlockSpec((TM_NORM, LANES), row)),
        compiler_params=pltpu.CompilerParams(
            dimension_semantics=("arbitrary",), vmem_limit_bytes=VMEM_LIMIT),
        name="norm1",
    )(x2, g1, w_f, b_f)


N_SECTIONS = 1 + 3 + N_GATE_SEC
N_CHUNKS = N_MAIN_SEC + N_GATE_SEC
GATE_SECTION0 = 4
WCONV_ROWS = 256


def _inproj_kernel(h_ref, wt_hbm, bg_ref, lng_ref, lnb_ref, wsp_ref, bsp_ref, qg_ref, kg_ref,
                   asgu_ref, q_ref, kt_ref, va_ref, gates_ref,
                   stage, wb, sem):
    s = pl.program_id(0)
    i = pl.program_id(1)
    tm = h_ref.shape[0]

    def chunk_copy(c, slot):
        r0 = pl.multiple_of(c * SEC + jnp.where(c >= N_MAIN_SEC, FOX_HEADS, 0), 8)
        return pltpu.make_async_copy(wt_hbm.at[pl.ds(r0, SEC), :], stage.at[slot], sem.at[slot])

    def convert(slot, k):
        for p in range(SEC // WCONV_ROWS):
            rs = slice(p * WCONV_ROWS, (p + 1) * WCONV_ROWS)
            wb[k, :, rs] = stage[slot, rs, :].T.astype(BF16)

    @pl.when((s == 0) & (i == 0))
    def _():
        chunk_copy(0, 0).start()
        chunk_copy(1, 1).start()
        chunk_copy(0, 0).wait()
        convert(0, 0)
        chunk_copy(2, 0).start(priority=PREFETCH_PRIORITY)
        chunk_copy(1, 1).wait()
        convert(1, 1)

    @pl.when((s > 0) & (i == 0))
    def _():
        c = s + 1
        slot = c % 2
        chunk_copy(c, slot).wait()

        @pl.when(c + 1 < N_CHUNKS)
        def _():
            chunk_copy(c + 1, 1 - slot).start(priority=PREFETCH_PRIORITY)

        convert(slot, 0)

    def section(k=0):
        return jnp.dot(h_ref[...], wb[k], preferred_element_type=F32)

    @pl.when(s == 0)
    def _():
        u = jax.nn.gelu(section(0))
        v = jax.nn.gelu(section(1))
        mu = jnp.mean(v, axis=-1, keepdims=True)
        vc = v - mu
        var = jnp.mean(vc * vc, axis=-1, keepdims=True)
        vn = ((vc * lax.rsqrt(var + EPS)) * lng_ref[...] + lnb_ref[...]).astype(BF16)
        row = lax.broadcasted_iota(I32, (CHUNK, CHUNK), 0)
        col = lax.broadcasted_iota(I32, (CHUNK, CHUNK), 1)
        causal = row >= col
        for g in range(SGU_GROUPS):
            wg = jnp.where(causal, wsp_ref[g], 0.0).astype(BF16)
            bcol = bsp_ref[:, g:g + 1]
            gs = slice(g * LANES, (g + 1) * LANES)
            for c in range(tm // CHUNK):
                rs = slice(c * CHUNK, (c + 1) * CHUNK)
                sg = jnp.dot(wg, vn[rs, gs], preferred_element_type=F32) + bcol
                asgu_ref[rs, gs] = (u[rs, gs] * sg).astype(BF16)

    def _head_norm(z, h, gain_ref):
        zh = z[:, h * HEAD_DIM:(h + 1) * HEAD_DIM]
        ms = jnp.mean(zh * zh, axis=-1, keepdims=True)
        return (zh * lax.rsqrt(ms + EPS)) * gain_ref[...]

    @pl.when(s == 1)
    def _():
        z = section()
        for h in range(FOX_HEADS):
            q_ref[:, h * HEAD_DIM:(h + 1) * HEAD_DIM] = (
                _head_norm(z, h, qg_ref) * Q_SCALE).astype(BF16)

    @pl.when(s == 2)
    def _():
        z = section()
        for h in range(FOX_HEADS):
            kt_ref[h * HEAD_DIM:(h + 1) * HEAD_DIM, :] = _head_norm(z, h, kg_ref).T.astype(BF16)

    @pl.when(s == 3)
    def _():
        va_ref[...] = section().astype(BF16)

    @pl.when(s >= GATE_SECTION0)
    def _():
        gates_ref[...] = (0.5 * jnp.tanh(0.5 * (section() + bg_ref[...])) + 0.5).astype(BF16)


def _inproj(h, wt, b_gate, ln_g, ln_b, w_sp, b_sp_t, q_g, k_g):
    t = h.shape[0]
    n_i = t // TM_IN

    def active(sec):
        return lambda s, i: jnp.where(s < sec, 0, jnp.where(s > sec, n_i - 1, i))

    def gate_blk(s, i):
        g = jnp.clip(s - GATE_SECTION0, 0, N_GATE_SEC - 1)
        return (jnp.where(s < GATE_SECTION0, 0, i), g)

    const2 = lambda s, i: (0, 0)
    out_shape = (
        jax.ShapeDtypeStruct((t, SGU_WIDTH), BF16),
        jax.ShapeDtypeStruct((t, FOX_WIDTH), BF16),
        jax.ShapeDtypeStruct((FOX_WIDTH, t), BF16),
        jax.ShapeDtypeStruct((t, FOX_WIDTH), BF16),
        jax.ShapeDtypeStruct((t, 2 * D_MODEL), BF16),
    )
    return pl.pallas_call(
        _inproj_kernel,
        out_shape=out_shape,
        grid=(N_SECTIONS, n_i),
        in_specs=[
            pl.BlockSpec((TM_IN, D_MODEL), lambda s, i: (i, 0)),
            pl.BlockSpec(memory_space=pl.ANY),
            pl.BlockSpec((1, SEC), lambda s, i: (0, jnp.clip(s - GATE_SECTION0, 0, N_GATE_SEC - 1))),
            pl.BlockSpec((1, SGU_WIDTH), const2),
            pl.BlockSpec((1, SGU_WIDTH), const2),
            pl.BlockSpec((SGU_GROUPS, CHUNK, CHUNK), lambda s, i: (0, 0, 0)),
            pl.BlockSpec((CHUNK, SGU_GROUPS), const2),
            pl.BlockSpec((1, HEAD_DIM), const2),
            pl.BlockSpec((1, HEAD_DIM), const2),
        ],
        out_specs=(
            pl.BlockSpec((TM_IN, SGU_WIDTH), lambda s, i: (active(0)(s, i), 0)),
            pl.BlockSpec((TM_IN, FOX_WIDTH), lambda s, i: (active(1)(s, i), 0)),
            pl.BlockSpec((FOX_WIDTH, TM_IN), lambda s, i: (0, active(2)(s, i))),
            pl.BlockSpec((TM_IN, FOX_WIDTH), lambda s, i: (active(3)(s, i), 0)),
            pl.BlockSpec((TM_IN, SEC), gate_blk),
        ),
        scratch_shapes=[pltpu.VMEM((2, SEC, D_MODEL), F32),
                        pltpu.VMEM((2, D_MODEL, SEC), BF16),
                        pltpu.SemaphoreType.DMA((2,))],
        compiler_params=pltpu.CompilerParams(
            dimension_semantics=("arbitrary", "arbitrary"), vmem_limit_bytes=VMEM_LIMIT),
        name="inproj",
    )(h, wt, b_gate, ln_g, ln_b, w_sp, b_sp_t, q_g, k_g)


def _split3(x):
    x0 = x.astype(BF16)
    r1 = x - x0.astype(F32)
    x1 = r1.astype(BF16)
    r2 = r1 - x1.astype(F32)
    return x0, x1, r2.astype(BF16)


def _cumsum_kernel(lf_ref, negc_ref, c_s):
    seq = lf_ref.shape[0]
    row = lax.broadcasted_iota(I32, (CHUNK, CHUNK), 0)
    col = lax.broadcasted_iota(I32, (CHUNK, CHUNK), 1)
    tri = jnp.where(row >= col, 1.0, 0.0).astype(BF16)
    carry = jnp.zeros((1, LANES), F32)
    for r in range(seq // CHUNK):
        rs = slice(r * CHUNK, (r + 1) * CHUNK)
        x0, x1, x2 = _split3(lf_ref[rs, :])
        cs = (jnp.dot(tri, x0, preferred_element_type=F32)
              + jnp.dot(tri, x1, preferred_element_type=F32)
              + jnp.dot(tri, x2, preferred_element_type=F32)) + carry
        carry = cs[CHUNK - 1:CHUNK, :]
        c_s[rs, :] = cs
    ct = c_s[...].T
    negc_ref[0] = -ct[0:FOX_HEADS, :]


def _forget_cumsum(lf, batch, seq):
    return pl.pallas_call(
        _cumsum_kernel,
        out_shape=jax.ShapeDtypeStruct((batch, FOX_HEADS, seq), F32),
        grid=(batch,),
        in_specs=[pl.BlockSpec((seq, LANES), lambda b: (b, 0))],
        out_specs=pl.BlockSpec((1, FOX_HEADS, seq), lambda b: (b, 0, 0)),
        scratch_shapes=[pltpu.VMEM((seq, LANES), F32)],
        compiler_params=pltpu.CompilerParams(
            dimension_semantics=("arbitrary",), vmem_limit_bytes=VMEM_LIMIT),
        name="forget_cumsum",
    )(lf)


def _attn_kernel(q_ref, kt_ref, v_ref, negc_ref, o_ref, kx, vx):
    seq = q_ref.shape[0]
    sub = lax.broadcasted_iota(I32, (HEAD_DIM, seq), 0)
    for hh in range(HEADS_PER_STEP):
        hs = slice(hh * HEAD_DIM, (hh + 1) * HEAD_DIM)
        c0, c1, c2 = (c.astype(F32) for c in _split3(negc_ref[0, hh] * LOG2E))
        kx[hh, 0:HEAD_DIM, :] = kt_ref[hs, :]
        kx[hh, HEAD_DIM:, :] = jnp.where(
            sub == 0, c0, jnp.where(sub == 1, c1, jnp.where(sub == 2, c2, 0.0))).astype(BF16)
        vx[hh, :, 0:HEAD_DIM] = v_ref[:, hs]
        vx[hh, :, HEAD_DIM:] = jnp.ones((seq, HEAD_DIM), BF16)
    lane = lax.broadcasted_iota(I32, (TQ, HEAD_DIM), 1)
    bias_cols = jnp.where(lane < 3, 1.0, 0.0).astype(BF16)
    row = lax.broadcasted_iota(I32, (TQ, TQ), 0)
    col = lax.broadcasted_iota(I32, (TQ, TQ), 1)
    causal = row >= col
    for qi in range(seq // TQ):
        for hh in range(HEADS_PER_STEP):
            hs = slice(hh * HEAD_DIM, (hh + 1) * HEAD_DIM)
            k0 = qi * TQ
            q = jnp.concatenate([q_ref[k0:k0 + TQ, hs], bias_cols], axis=1)
            s_d = jnp.dot(q, kx[hh, :, k0:k0 + TQ], preferred_element_type=F32)
            s_d = jnp.where(causal, s_d, -jnp.inf)
            m = jnp.max(s_d, axis=-1, keepdims=True)
            if qi > 0:
                s_o = jnp.dot(q, kx[hh, :, 0:k0], preferred_element_type=F32)
                m = jnp.maximum(m, jnp.max(s_o, axis=-1, keepdims=True))
            acc = jnp.dot(jnp.exp2(s_d - m).astype(BF16), vx[hh, k0:k0 + TQ, :],
                          preferred_element_type=F32)
            if qi > 0:
                acc = acc + jnp.dot(jnp.exp2(s_o - m).astype(BF16), vx[hh, 0:k0, :],
                                    preferred_element_type=F32)
            inv_l = 1.0 / acc[:, HEAD_DIM:HEAD_DIM + 1]
            o_ref[k0:k0 + TQ, hs] = (acc[:, 0:HEAD_DIM] * inv_l).astype(BF16)


def _attention(q, kt, va, negc, batch, seq):
    t = q.shape[0]
    width = HEADS_PER_STEP * HEAD_DIM
    blk = pl.BlockSpec((seq, width), lambda b, g: (b, g))
    return pl.pallas_call(
        _attn_kernel,
        out_shape=jax.ShapeDtypeStruct((t, FOX_WIDTH), BF16),
        grid=(batch, FOX_HEADS // HEADS_PER_STEP),
        in_specs=[blk,
                  pl.BlockSpec((width, seq), lambda b, g: (g, b)),
                  blk,
                  pl.BlockSpec((1, HEADS_PER_STEP, 1, seq), lambda b, g: (b, g, 0, 0))],
        out_specs=blk,
        scratch_shapes=[pltpu.VMEM((HEADS_PER_STEP, 2 * HEAD_DIM, seq), BF16),
                        pltpu.VMEM((HEADS_PER_STEP, seq, 2 * HEAD_DIM), BF16)],
        compiler_params=pltpu.CompilerParams(
            dimension_semantics=("arbitrary", "arbitrary"), vmem_limit_bytes=VMEM_LIMIT),
        name="fox_attention",
    )(q, kt, va, negc)


def _mix_kernel(as_ref, af_ref, g_ref, x_ref, wps_ref, wpf_ref, wo_ref, g2_ref, wr_ref, br_ref,
                x1_ref, lg_ref):
    ys = jnp.dot(as_ref[...], wps_ref[...], preferred_element_type=F32)
    yf = jnp.dot(af_ref[...], wpf_ref[...], preferred_element_type=F32)
    m = (g_ref[:, :D_MODEL].astype(F32) * ys + g_ref[:, D_MODEL:].astype(F32) * yf).astype(BF16)
    x1 = x_ref[...] + jnp.dot(m, wo_ref[...], preferred_element_type=F32)
    x1_ref[...] = x1
    ms = jnp.mean(x1 * x1, axis=-1, keepdims=True)
    h2 = ((x1 * lax.rsqrt(ms + EPS)) * g2_ref[...]).astype(BF16)
    lg_ref[...] = jnp.dot(h2, wr_ref[...], preferred_element_type=F32) + br_ref[...]


def _mix(a_sgu, a_fox, gates, x2, wps, wpf, wo, g2, w_r, b_r):
    t = x2.shape[0]
    row = lambda i: (i, 0)
    const = lambda i: (0, 0)
    resident = functools.partial(pl.BlockSpec, index_map=const, pipeline_mode=pl.Buffered(1))
    return pl.pallas_call(
        _mix_kernel,
        out_shape=(jax.ShapeDtypeStruct((t, D_MODEL), F32),
                   jax.ShapeDtypeStruct((t, LANES), F32)),
        grid=(t // TM_MIX,),
        in_specs=[
            pl.BlockSpec((TM_MIX, SGU_WIDTH), row),
            pl.BlockSpec((TM_MIX, FOX_WIDTH), row),
            pl.BlockSpec((TM_MIX, 2 * D_MODEL), row),
            pl.BlockSpec((TM_MIX, D_MODEL), row),
            resident((SGU_WIDTH, D_MODEL)),
            resident((FOX_WIDTH, D_MODEL)),
            resident((D_MODEL, D_MODEL)),
            pl.BlockSpec((1, D_MODEL), const),
            resident((D_MODEL, LANES)),
            pl.BlockSpec((1, LANES), const),
        ],
        out_specs=(pl.BlockSpec((TM_MIX, D_MODEL), row), pl.BlockSpec((TM_MIX, LANES), row)),
        compiler_params=pltpu.CompilerParams(
            dimension_semantics=("arbitrary",), vmem_limit_bytes=VMEM_LIMIT),
        name="mix",
    )(a_sgu, a_fox, gates, x2, wps, wpf, wo, g2, w_r, b_r)


def _route_kernel(lg_ref, dt_ref, mf_ref, bm_ref, em_ref, mi_ref):
    t = lg_ref.shape[0]
    n_chunks = t // RCH
    lane_i = lax.broadcasted_iota(I32, (RCH, LANES), 1)
    lane = lane_i.astype(F32)
    lane_grp = ((lane_i - N_GROUPS) >> 3).astype(F32)
    is_grp = lane_i < N_GROUPS
    is_exp = (lane_i >= N_GROUPS) & (lane_i < N_GROUPS + N_EXPERTS)
    r_i = lax.broadcasted_iota(I32, (RCH, RCH), 0)
    c_i = lax.broadcasted_iota(I32, (RCH, RCH), 1)
    strict_lower = jnp.where(r_i > c_i, 1.0, 0.0).astype(BF16)
    neg_inf = -jnp.inf

    def first_max(vals):
        vmax = jnp.max(vals, axis=-1, keepdims=True)
        idx = jnp.min(jnp.where(vals == vmax, lane, float(LANES)), axis=-1, keepdims=True)
        return vmax, idx

    def pick(table, idx):
        return jnp.sum(jnp.where(lane == idx, table, 0.0), axis=-1, keepdims=True)

    def pack(cols):
        out = jnp.zeros((RCH, LANES), F32)
        for n, c in enumerate(cols):
            out = jnp.where(lane_i == n, c, out)
        return out

    def pass1(ci, counts):
        r0 = pl.multiple_of(ci * RCH, RCH)
        lg = lg_ref[pl.ds(r0, RCH), :]
        gmax, grp = first_max(jnp.where(is_grp, lg, neg_inf))
        p_grp = 1.0 / jnp.sum(jnp.where(is_grp, jnp.exp(lg - gmax), 0.0), axis=-1, keepdims=True)
        el = jnp.where(is_exp & (lane_grp == grp), lg, neg_inf)
        v1, i1 = first_max(el)
        v2, i2 = first_max(jnp.where(lane == i1, neg_inf, el))
        e21 = jnp.exp(v2 - v1)
        w1 = p_grp / (1.0 + e21)
        w2 = p_grp * e21 / (1.0 + e21)
        e1 = i1 - float(N_GROUPS)
        e2 = i2 - float(N_GROUPS)
        hot = jnp.where((lane == e1) | (lane == e2), 1.0, 0.0)
        before = jnp.dot(strict_lower, hot.astype(BF16), preferred_element_type=F32) + counts
        mi_ref[pl.ds(r0, RCH), :] = pack([e1, e2, pick(before, e1), pick(before, e2)]).astype(I32)
        mf_ref[pl.ds(r0, RCH), :] = pack([w1, w2])
        return counts + jnp.sum(hot, axis=0, keepdims=True)

    counts = lax.fori_loop(0, n_chunks, pass1, jnp.zeros((1, LANES), F32))

    nblk = jnp.floor((counts + float(EXPERT_BLOCK - 1)) * (1.0 / EXPERT_BLOCK))
    u_r = lax.broadcasted_iota(I32, (LANES, LANES), 0)
    u_c = lax.broadcasted_iota(I32, (LANES, LANES), 1)
    strict_upper = jnp.where(u_r < u_c, 1.0, 0.0).astype(BF16)
    bstart = jnp.dot(jnp.broadcast_to(nblk, (8, LANES)).astype(BF16), strict_upper,
                     preferred_element_type=F32)[0:1, :]
    bend = bstart + nblk

    def pass2(ci, _):
        r0 = pl.multiple_of(ci * RCH, RCH)
        mi = mi_ref[pl.ds(r0, RCH), :].astype(F32)
        cols = [jnp.sum(jnp.where(lane_i == n, mi, 0.0), axis=-1, keepdims=True) for n in range(4)]
        d1 = pick(bstart, cols[0]) * float(EXPERT_BLOCK) + cols[2]
        d2 = pick(bstart, cols[1]) * float(EXPERT_BLOCK) + cols[3]
        dt_ref[ci] = pack([d1, d2]).T[0:8, :].astype(I32)
        return 0

    lax.fori_loop(0, n_chunks, pass2, 0)

    nb_rows = bm_ref.shape[0]
    b_col = lax.broadcasted_iota(I32, (nb_rows, LANES), 0).astype(F32)
    b_lane = lax.broadcasted_iota(I32, (nb_rows, LANES), 1)
    done = jnp.where((bend <= b_col) & (b_lane < N_EXPERTS), 1.0, 0.0)
    blk_e = jnp.minimum(jnp.sum(done, axis=-1, keepdims=True), float(N_EXPERTS - 1))
    n_used = jnp.sum(jnp.where(b_lane[0:1, :] == N_EXPERTS - 1, bend, 0.0), axis=-1, keepdims=True)
    bm_ref[...] = jnp.where(b_lane == 0, blk_e, jnp.where(b_lane == 1, n_used, 0.0)).astype(I32)

    e_row = lax.broadcasted_iota(I32, (8, LANES), 0)
    em_ref[...] = jnp.where(e_row == 0, counts,
                            jnp.where(e_row == 1, bstart, jnp.where(e_row == 2, nblk, 0.0))).astype(I32)


def _route(logits, nb_rows):
    t = logits.shape[0]
    full = lambda shape: pl.BlockSpec(shape, lambda: (0,) * len(shape))
    return pl.pallas_call(
        _route_kernel,
        out_shape=(jax.ShapeDtypeStruct((t // RCH, 8, RCH), I32),
                   jax.ShapeDtypeStruct((t, LANES), F32),
                   jax.ShapeDtypeStruct((nb_rows, LANES), I32),
                   jax.ShapeDtypeStruct((8, LANES), I32)),
        in_specs=[full((t, LANES))],
        out_specs=(full((t // RCH, 8, RCH)), full((t, LANES)), full((nb_rows, LANES)),
                   full((8, LANES))),
        scratch_shapes=[pltpu.VMEM((t, LANES), I32)],
        compiler_params=pltpu.CompilerParams(vmem_limit_bytes=VMEM_LIMIT),
        name="route",
    )(logits)


DISPATCH_UNROLL = 8
TM_DSP = 1024


def _dispatch_kernel(d1_ref, d2_ref, cnt_ref, bstart_ref, nblk_ref, nused_ref, x_ref, xs_ref, sem):
    i = pl.program_id(0)
    tm = x_ref.shape[0]
    nb = xs_ref.shape[0] // EXPERT_BLOCK

    def row_copy(src_row, dst_row):
        return pltpu.make_async_copy(x_ref.at[pl.ds(src_row, 1), :],
                                     xs_ref.at[pl.ds(dst_row, 1), :], sem)

    def block_copy(dst_block):
        dst_row = pl.multiple_of(dst_block * EXPERT_BLOCK, EXPERT_BLOCK)
        return pltpu.make_async_copy(x_ref.at[pl.ds(0, EXPERT_BLOCK), :],
                                     xs_ref.at[pl.ds(dst_row, EXPERT_BLOCK), :], sem)

    def tok_body(r, _):
        tok = i * tm + r
        row_copy(r, d1_ref[tok]).start()
        row_copy(r, d2_ref[tok]).start()
        return 0

    lax.fori_loop(0, tm, tok_body, 0, unroll=DISPATCH_UNROLL)

    @pl.when(i == 0)
    def _():
        def pad_expert(e, n_pad):
            lo = bstart_ref[e] * EXPERT_BLOCK + cnt_ref[e]
            hi = (bstart_ref[e] + nblk_ref[e]) * EXPERT_BLOCK

            def pad_body(r, _):
                row_copy(0, r).start()
                return 0

            lax.fori_loop(lo, hi, pad_body, 0)
            return n_pad + (hi - lo)

        n_pad = lax.fori_loop(0, N_EXPERTS, pad_expert, 0)
        n_used = nused_ref[0]

        def fill_body(b, _):
            block_copy(b).start()
            return 0

        lax.fori_loop(n_used, nb, fill_body, 0)

        def block_wait(_, c):
            block_copy(0).wait()
            return c

        lax.fori_loop(n_used, nb, block_wait, 0)

        def row_wait(_, c):
            row_copy(0, 0).wait()
            return c

        lax.fori_loop(0, n_pad, row_wait, 0)

    for _ in range((2 * tm) // EXPERT_BLOCK):
        block_copy(0).wait()


def _dispatch(dest1, dest2, cnt, bstart, nblk, n_used, x1, n_rows):
    t = x1.shape[0]
    grid_spec = pltpu.PrefetchScalarGridSpec(
        num_scalar_prefetch=6,
        grid=(t // TM_DSP,),
        in_specs=[pl.BlockSpec((TM_DSP, D_MODEL), lambda i, *_: (i, 0))],
        out_specs=pl.BlockSpec(memory_space=pl.ANY),
        scratch_shapes=[pltpu.SemaphoreType.DMA(())],
    )
    return pl.pallas_call(
        _dispatch_kernel,
        out_shape=jax.ShapeDtypeStruct((n_rows, D_MODEL), F32),
        grid_spec=grid_spec,
        compiler_params=pltpu.CompilerParams(
            dimension_semantics=("arbitrary",), vmem_limit_bytes=VMEM_LIMIT),
        name="dispatch",
    )(dest1, dest2, cnt, bstart, nblk, n_used, x1)


def _experts_kernel(blk_e_ref, nused_ref, nblk_ref, x_ref, g2_ref, wg_hbm, wu_hbm, wd_hbm, y_ref,
                    wg_f, wu_f, wd_f, wg_s, wu_s, wd_s, slot_ref, sem):
    b = pl.program_id(0)
    n_used = nused_ref[0]
    used = b < n_used
    e = blk_e_ref[b]
    new_expert = (b == 0) | (e != blk_e_ref[jnp.maximum(b - 1, 0)])

    def weight_copies(expert, slot):
        return (pltpu.make_async_copy(wg_hbm.at[expert], wg_f.at[slot], sem.at[slot]),
                pltpu.make_async_copy(wu_hbm.at[expert], wu_f.at[slot], sem.at[slot]),
                pltpu.make_async_copy(wd_hbm.at[expert], wd_f.at[slot], sem.at[slot]))

    @pl.when(b == 0)
    def _():
        slot_ref[0] = 0
        for cp in weight_copies(e, 0):
            cp.start()

    @pl.when(used & new_expert)
    def _():
        slot = slot_ref[0]
        b_next = b + nblk_ref[e]

        @pl.when(b_next < n_used)
        def _():
            for cp in weight_copies(blk_e_ref[jnp.minimum(b_next, n_used - 1)], 1 - slot):
                cp.start(priority=PREFETCH_PRIORITY)

        for cp in weight_copies(e, slot):
            cp.wait()
        wg_s[...] = wg_f[slot].astype(BF16)
        wu_s[...] = wu_f[slot].astype(BF16)
        wd_s[...] = wd_f[slot].astype(BF16)
        slot_ref[0] = 1 - slot

    @pl.when(used)
    def _():
        x = x_ref[...]
        ms = jnp.mean(x * x, axis=-1, keepdims=True)
        h = ((x * lax.rsqrt(ms + EPS)) * g2_ref[...]).astype(BF16)
        a = jnp.dot(h, wg_s[...], preferred_element_type=F32)
        u = jnp.dot(h, wu_s[...], preferred_element_type=F32)
        mid = ((a * _sigmoid(a)) * u).astype(BF16)
        y_ref[...] = jnp.dot(mid, wd_s[...], preferred_element_type=F32)

    @pl.when(jnp.logical_not(used))
    def _():
        y_ref[...] = jnp.zeros_like(y_ref)


def _experts(blk_e, n_used, nblk, xs, g2, w_g, w_u, w_d):
    n_rows = xs.shape[0]
    nb = n_rows // EXPERT_BLOCK

    def row_blk(b, blk_e_ref, nused_ref, nblk_ref):
        return (jnp.minimum(b, nused_ref[0] - 1), 0)

    hbm = pl.BlockSpec(memory_space=pl.ANY)
    grid_spec = pltpu.PrefetchScalarGridSpec(
        num_scalar_prefetch=3,
        grid=(nb,),
        in_specs=[
            pl.BlockSpec((EXPERT_BLOCK, D_MODEL), row_blk),
            pl.BlockSpec((1, D_MODEL), lambda b, *_: (0, 0)),
            hbm, hbm, hbm,
        ],
        out_specs=pl.BlockSpec((EXPERT_BLOCK, D_MODEL), lambda b, *_: (b, 0)),
        scratch_shapes=[pltpu.VMEM((2, D_MODEL, D_EXPERT), F32),
                        pltpu.VMEM((2, D_MODEL, D_EXPERT), F32),
                        pltpu.VMEM((2, D_EXPERT, D_MODEL), F32),
                        pltpu.VMEM((D_MODEL, D_EXPERT), BF16),
                        pltpu.VMEM((D_MODEL, D_EXPERT), BF16),
                        pltpu.VMEM((D_EXPERT, D_MODEL), BF16),
                        pltpu.SMEM((1,), I32),
                        pltpu.SemaphoreType.DMA((2,))],
    )
    return pl.pallas_call(
        _experts_kernel,
        out_shape=jax.ShapeDtypeStruct((n_rows, D_MODEL), F32),
        grid_spec=grid_spec,
        compiler_params=pltpu.CompilerParams(
            dimension_semantics=("arbitrary",), vmem_limit_bytes=VMEM_LIMIT),
        name="experts",
    )(blk_e, n_used, nblk, xs, g2, w_g, w_u, w_d)


COMBINE_UNROLL = 8


def _combine_kernel(d1_ref, d2_ref, x1_ref, w_ref, y_ref, o_ref, gbuf, sem):
    i = pl.program_id(0)
    n = pl.num_programs(0)
    tm = x1_ref.shape[0]

    def issue(tile, slot):
        def body(r, _):
            tok = tile * tm + r
            pltpu.make_async_copy(y_ref.at[pl.ds(d1_ref[tok], 1), :],
                                  gbuf.at[slot, pl.ds(r, 1), :], sem.at[slot]).start()
            pltpu.make_async_copy(y_ref.at[pl.ds(d2_ref[tok], 1), :],
                                  gbuf.at[slot, pl.ds(tm + r, 1), :], sem.at[slot]).start()
            return 0

        lax.fori_loop(0, tm, body, 0, unroll=COMBINE_UNROLL)

    slot = i % 2

    @pl.when(i == 0)
    def _():
        issue(0, 0)

    @pl.when(i + 1 < n)
    def _():
        issue(i + 1, 1 - slot)

    pltpu.make_async_copy(y_ref.at[pl.ds(0, 2 * tm), :], gbuf.at[slot], sem.at[slot]).wait()
    w = w_ref[...]
    g = gbuf[slot]
    o_ref[...] = x1_ref[...] + (w[:, 0:1] * g[0:tm, :] + w[:, 1:2] * g[tm:2 * tm, :])


def _combine(dest1, dest2, x1, w, yb):
    t = x1.shape[0]
    grid_spec = pltpu.PrefetchScalarGridSpec(
        num_scalar_prefetch=2,
        grid=(t // TM_CMB,),
        in_specs=[
            pl.BlockSpec((TM_CMB, D_MODEL), lambda i, a, b: (i, 0)),
            pl.BlockSpec((TM_CMB, LANES), lambda i, a, b: (i, 0)),
            pl.BlockSpec(memory_space=pl.ANY),
        ],
        out_specs=pl.BlockSpec((TM_CMB, D_MODEL), lambda i, a, b: (i, 0)),
        scratch_shapes=[pltpu.VMEM((2, 2 * TM_CMB, D_MODEL), F32),
                        pltpu.SemaphoreType.DMA((2,))],
    )
    return pl.pallas_call(
        _combine_kernel,
        out_shape=jax.ShapeDtypeStruct((t, D_MODEL), F32),
        grid_spec=grid_spec,
        compiler_params=pltpu.CompilerParams(
            dimension_semantics=("arbitrary",), vmem_limit_bytes=VMEM_LIMIT),
        name="combine",
    )(dest1, dest2, x1, w, yb)


def kernel(x, norm1_g, w_in, b_gate, b_forget, sgu_ln_g, sgu_ln_b, w_spatial, b_spatial, q_norm_g, k_norm_g, w_proj_sgu, w_proj_fox, w_out, norm2_g, w_router_group, b_router_group, w_router_expert, b_router_expert, w_expert_gate, w_expert_up, w_expert_down):
    batch, seq, d = x.shape
    t = batch * seq
    l = 0
    x2 = x.reshape(t, d)

    wt = jnp.swapaxes(w_in[l], 0, 1)
    off_f = N_MAIN_SEC * SEC
    w_f = jnp.pad(wt[off_f:off_f + FOX_HEADS].T, ((0, 0), (0, LANES - FOX_HEADS))).astype(BF16)
    b_f = jnp.pad(b_forget[l], (0, LANES - FOX_HEADS)).reshape(1, LANES)
    n_r = N_GROUPS + N_EXPERTS
    w_r = jnp.pad(jnp.concatenate([w_router_group[l], w_router_expert[l]], axis=1),
                  ((0, 0), (0, LANES - n_r))).astype(BF16)
    b_r = jnp.pad(jnp.concatenate([b_router_group[l], b_router_expert[l]]),
                  (0, LANES - n_r)).reshape(1, LANES)

    h, lf = _norm1(x2, norm1_g[l].reshape(1, d), w_f, b_f)
    a_sgu, q, kt, va, gates = _inproj(
        h, wt, b_gate[l].reshape(1, 2 * d),
        sgu_ln_g[l].reshape(1, SGU_WIDTH), sgu_ln_b[l].reshape(1, SGU_WIDTH),
        w_spatial[l], b_spatial[l].T, q_norm_g[l].reshape(1, HEAD_DIM),
        k_norm_g[l].reshape(1, HEAD_DIM))
    negc = _forget_cumsum(lf, batch, seq).reshape(batch, FOX_HEADS, 1, seq)
    a_fox = _attention(q, kt, va, negc, batch, seq)
    x1, logits = _mix(a_sgu, a_fox, gates, x2, w_proj_sgu[l].astype(BF16),
                      w_proj_fox[l].astype(BF16), w_out[l].astype(BF16),
                      norm2_g[l].reshape(1, d), w_r, b_r)

    n_assign = 2 * t
    n_rows = n_assign + N_EXPERTS * EXPERT_BLOCK
    nb = n_rows // EXPERT_BLOCK
    nb_rows = -(-nb // 8) * 8
    dest_t, meta_f, bmeta, emeta = _route(logits, nb_rows)
    dest1, dest2 = dest_t[:, 0, :].reshape(t), dest_t[:, 1, :].reshape(t)
    n_used = bmeta[0:1, 1]
    xs = _dispatch(dest1, dest2, emeta[0, :N_EXPERTS], emeta[1, :N_EXPERTS], emeta[2, :N_EXPERTS],
                   n_used, x1, n_rows)
    yb = _experts(bmeta[:nb, 0], n_used, emeta[2, :N_EXPERTS], xs, norm2_g[l].reshape(1, d),
                  w_expert_gate[l], w_expert_up[l], w_expert_down[l])
    out = _combine(dest1, dest2, x1, meta_f, yb)
    return out.reshape(batch, seq, d)
```

```python
import functools

import jax
import jax.numpy as jnp
from jax import lax
from jax.experimental import pallas as pl
from jax.experimental.pallas import tpu as pltpu

F32 = jnp.float32
BF16 = jnp.bfloat16
I32 = jnp.int32

D_MODEL = 2048
CHUNK = 128
SGU_GROUPS = 8
SGU_WIDTH = 1024
FOX_HEADS = 8
HEAD_DIM = 128
FOX_WIDTH = 1024
N_GROUPS = 4
EXPERTS_PER_GROUP = 8
N_EXPERTS = 32
D_EXPERT = 512
EXPERT_BLOCK = 128
EPS = 1e-6

LANES = 128
PREFETCH_PRIORITY = 1
VMEM_LIMIT = 56 * 1024 * 1024

SEC = 1024
N_MAIN_SEC = 5
N_GATE_SEC = 4
TM_NORM = 1024
TM_IN = 512
TQ = 256
HEADS_PER_STEP = 4
TM_MIX = 256
RCH = 512
TM_CMB = 512
LOG2E = 1.4426950408889634
Q_SCALE = HEAD_DIM ** -0.5 * LOG2E


def _sigmoid(x):
    return 1.0 / (1.0 + jnp.exp(-x))


def _log_sigmoid(x):
    return jnp.minimum(x, 0.0) - jnp.log1p(jnp.exp(-jnp.abs(x)))


def _norm1_kernel(x_ref, g1_ref, wf_ref, bf_ref, h_ref, lf_ref):
    x = x_ref[...]
    ms = jnp.mean(x * x, axis=-1, keepdims=True)
    hb = ((x * lax.rsqrt(ms + EPS)) * g1_ref[...]).astype(BF16)
    h_ref[...] = hb
    f = jnp.dot(hb, wf_ref[...], preferred_element_type=F32) + bf_ref[...]
    lf_ref[...] = _log_sigmoid(f)


def _norm1(x2, g1, w_f, b_f):
    t = x2.shape[0]
    row = lambda i: (i, 0)
    const = lambda i: (0, 0)
    return pl.pallas_call(
        _norm1_kernel,
        out_shape=(jax.ShapeDtypeStruct((t, D_MODEL), BF16),
                   jax.ShapeDtypeStruct((t, LANES), F32)),
        grid=(t // TM_NORM,),
        in_specs=[pl.BlockSpec((TM_NORM, D_MODEL), row),
                  pl.BlockSpec((1, D_MODEL), const),
                  pl.BlockSpec((D_MODEL, LANES), const),
                  pl.BlockSpec((1, LANES), const)],
        out_specs=(pl.BlockSpec((TM_NORM, D_MODEL), row), pl.BlockSpec((TM_NORM, LANES), row)),
        compiler_params=pltpu.CompilerParams(
            dimension_semantics=("arbitrary",), vmem_limit_bytes=VMEM_LIMIT),
        name="norm1",
    )(x2, g1, w_f, b_f)


N_SECTIONS = 1 + 3 + N_GATE_SEC
N_CHUNKS = N_MAIN_SEC + N_GATE_SEC
GATE_SECTION0 = 4
WCONV_ROWS = 256


def _inproj_kernel(h_ref, wt_hbm, bg_ref, lng_ref, lnb_ref, wsp_ref, bsp_ref, qg_ref, kg_ref,
                   asgu_ref, q_ref, kt_ref, va_ref, gates_ref,
                   stage, wb, sem):
    s = pl.program_id(0)
    i = pl.program_id(1)
    tm = h_ref.shape[0]

    def chunk_copy(c, slot):
        r0 = pl.multiple_of(c * SEC + jnp.where(c >= N_MAIN_SEC, FOX_HEADS, 0), 8)
        return pltpu.make_async_copy(wt_hbm.at[pl.ds(r0, SEC), :], stage.at[slot], sem.at[slot])

    def convert(slot, k):
        for p in range(SEC // WCONV_ROWS):
            rs = slice(p * WCONV_ROWS, (p + 1) * WCONV_ROWS)
            wb[k, :, rs] = stage[slot, rs, :].T.astype(BF16)

    @pl.when((s == 0) & (i == 0))
    def _():
        chunk_copy(0, 0).start()
        chunk_copy(1, 1).start()
        chunk_copy(0, 0).wait()
        convert(0, 0)
        chunk_copy(2, 0).start(priority=PREFETCH_PRIORITY)
        chunk_copy(1, 1).wait()
        convert(1, 1)

    @pl.when((s > 0) & (i == 0))
    def _():
        c = s + 1
        slot = c % 2
        chunk_copy(c, slot).wait()

        @pl.when(c + 1 < N_CHUNKS)
        def _():
            chunk_copy(c + 1, 1 - slot).start(priority=PREFETCH_PRIORITY)

        convert(slot, 0)

    def section(k=0):
        return jnp.dot(h_ref[...], wb[k], preferred_element_type=F32)

    @pl.when(s == 0)
    def _():
        u = jax.nn.gelu(section(0))
        v = jax.nn.gelu(section(1))
        mu = jnp.mean(v, axis=-1, keepdims=True)
        vc = v - mu
        var = jnp.mean(vc * vc, axis=-1, keepdims=True)
        vn = ((vc * lax.rsqrt(var + EPS)) * lng_ref[...] + lnb_ref[...]).astype(BF16)
        row = lax.broadcasted_iota(I32, (CHUNK, CHUNK), 0)
        col = lax.broadcasted_iota(I32, (CHUNK, CHUNK), 1)
        causal = row >= col
        for g in range(SGU_GROUPS):
            wg = jnp.where(causal, wsp_ref[g], 0.0).astype(BF16)
            bcol = bsp_ref[:, g:g + 1]
            gs = slice(g * LANES, (g + 1) * LANES)
            n_ch = tm // CHUNK
            rhs = jnp.concatenate([vn[c * CHUNK:(c + 1) * CHUNK, gs] for c in range(n_ch)], axis=1)
            sg = jnp.dot(wg, rhs, preferred_element_type=F32) + bcol
            for c in range(n_ch):
                rs = slice(c * CHUNK, (c + 1) * CHUNK)
                asgu_ref[rs, gs] = (u[rs, gs] * sg[:, c * LANES:(c + 1) * LANES]).astype(BF16)

    def _head_norm(z, h, gain_ref):
        zh = z[:, h * HEAD_DIM:(h + 1) * HEAD_DIM]
        ms = jnp.mean(zh * zh, axis=-1, keepdims=True)
        return (zh * lax.rsqrt(ms + EPS)) * gain_ref[...]

    @pl.when(s == 1)
    def _():
        z = section()
        for h in range(FOX_HEADS):
            q_ref[:, h * HEAD_DIM:(h + 1) * HEAD_DIM] = (
                _head_norm(z, h, qg_ref) * Q_SCALE).astype(BF16)

    @pl.when(s == 2)
    def _():
        z = section()
        for h in range(FOX_HEADS):
            kt_ref[h * HEAD_DIM:(h + 1) * HEAD_DIM, :] = _head_norm(z, h, kg_ref).T.astype(BF16)

    @pl.when(s == 3)
    def _():
        va_ref[...] = section().astype(BF16)

    @pl.when(s >= GATE_SECTION0)
    def _():
        gates_ref[...] = (0.5 * jnp.tanh(0.5 * (section() + bg_ref[...])) + 0.5).astype(BF16)


def _inproj(h, wt, b_gate, ln_g, ln_b, w_sp, b_sp_t, q_g, k_g):
    t = h.shape[0]
    n_i = t // TM_IN

    def active(sec):
        return lambda s, i: jnp.where(s < sec, 0, jnp.where(s > sec, n_i - 1, i))

    def gate_blk(s, i):
        g = jnp.clip(s - GATE_SECTION0, 0, N_GATE_SEC - 1)
        return (jnp.where(s < GATE_SECTION0, 0, i), g)

    const2 = lambda s, i: (0, 0)
    out_shape = (
        jax.ShapeDtypeStruct((t, SGU_WIDTH), BF16),
        jax.ShapeDtypeStruct((t, FOX_WIDTH), BF16),
        jax.ShapeDtypeStruct((FOX_WIDTH, t), BF16),
        jax.ShapeDtypeStruct((t, FOX_WIDTH), BF16),
        jax.ShapeDtypeStruct((t, 2 * D_MODEL), BF16),
    )
    return pl.pallas_call(
        _inproj_kernel,
        out_shape=out_shape,
        grid=(N_SECTIONS, n_i),
        in_specs=[
            pl.BlockSpec((TM_IN, D_MODEL), lambda s, i: (i, 0)),
            pl.BlockSpec(memory_space=pl.ANY),
            pl.BlockSpec((1, SEC), lambda s, i: (0, jnp.clip(s - GATE_SECTION0, 0, N_GATE_SEC - 1))),
            pl.BlockSpec((1, SGU_WIDTH), const2),
            pl.BlockSpec((1, SGU_WIDTH), const2),
            pl.BlockSpec((SGU_GROUPS, CHUNK, CHUNK), lambda s, i: (0, 0, 0)),
            pl.BlockSpec((CHUNK, SGU_GROUPS), const2),
            pl.BlockSpec((1, HEAD_DIM), const2),
            pl.BlockSpec((1, HEAD_DIM), const2),
        ],
        out_specs=(
            pl.BlockSpec((TM_IN, SGU_WIDTH), lambda s, i: (active(0)(s, i), 0)),
            pl.BlockSpec((TM_IN, FOX_WIDTH), lambda s, i: (active(1)(s, i), 0)),
            pl.BlockSpec((FOX_WIDTH, TM_IN), lambda s, i: (0, active(2)(s, i))),
            pl.BlockSpec((TM_IN, FOX_WIDTH), lambda s, i: (active(3)(s, i), 0)),
            pl.BlockSpec((TM_IN, SEC), gate_blk),
        ),
        scratch_shapes=[pltpu.VMEM((2, SEC, D_MODEL), F32),
                        pltpu.VMEM((2, D_MODEL, SEC), BF16),
                        pltpu.SemaphoreType.DMA((2,))],
        compiler_params=pltpu.CompilerParams(
            dimension_semantics=("arbitrary", "arbitrary"), vmem_limit_bytes=VMEM_LIMIT),
        name="inproj",
    )(h, wt, b_gate, ln_g, ln_b, w_sp, b_sp_t, q_g, k_g)


def _split3(x):
    x0 = x.astype(BF16)
    r1 = x - x0.astype(F32)
    x1 = r1.astype(BF16)
    r2 = r1 - x1.astype(F32)
    return x0, x1, r2.astype(BF16)


def _cumsum_kernel(lf_ref, negc_ref, c_s):
    seq = lf_ref.shape[0]
    row = lax.broadcasted_iota(I32, (CHUNK, CHUNK), 0)
    col = lax.broadcasted_iota(I32, (CHUNK, CHUNK), 1)
    tri = jnp.where(row >= col, 1.0, 0.0).astype(BF16)
    carry = jnp.zeros((1, LANES), F32)
    for r in range(seq // CHUNK):
        rs = slice(r * CHUNK, (r + 1) * CHUNK)
        x0, x1, x2 = _split3(lf_ref[rs, :])
        cs = (jnp.dot(tri, x0, preferred_element_type=F32)
              + jnp.dot(tri, x1, preferred_element_type=F32)
              + jnp.dot(tri, x2, preferred_element_type=F32)) + carry
        carry = cs[CHUNK - 1:CHUNK, :]
        c_s[rs, :] = cs
    ct = c_s[...].T
    negc_ref[0] = -ct[0:FOX_HEADS, :]


def _forget_cumsum(lf, batch, seq):
    return pl.pallas_call(
        _cumsum_kernel,
        out_shape=jax.ShapeDtypeStruct((batch, FOX_HEADS, seq), F32),
        grid=(batch,),
        in_specs=[pl.BlockSpec((seq, LANES), lambda b: (b, 0))],
        out_specs=pl.BlockSpec((1, FOX_HEADS, seq), lambda b: (b, 0, 0)),
        scratch_shapes=[pltpu.VMEM((seq, LANES), F32)],
        compiler_params=pltpu.CompilerParams(
            dimension_semantics=("arbitrary",), vmem_limit_bytes=VMEM_LIMIT),
        name="forget_cumsum",
    )(lf)


def _attn_kernel(q_ref, kt_ref, v_ref, negc_ref, o_ref, kx, vx):
    seq = q_ref.shape[0]
    sub = lax.broadcasted_iota(I32, (HEAD_DIM, seq), 0)
    for hh in range(HEADS_PER_STEP):
        hs = slice(hh * HEAD_DIM, (hh + 1) * HEAD_DIM)
        c0, c1, c2 = (c.astype(F32) for c in _split3(negc_ref[0, hh] * LOG2E))
        kx[hh, 0:HEAD_DIM, :] = kt_ref[hs, :]
        kx[hh, HEAD_DIM:, :] = jnp.where(
            sub == 0, c0, jnp.where(sub == 1, c1, jnp.where(sub == 2, c2, 0.0))).astype(BF16)
        vx[hh, :, 0:HEAD_DIM] = v_ref[:, hs]
        vx[hh, :, HEAD_DIM:] = jnp.ones((seq, HEAD_DIM), BF16)
    lane = lax.broadcasted_iota(I32, (TQ, HEAD_DIM), 1)
    bias_cols = jnp.where(lane < 3, 1.0, 0.0).astype(BF16)
    row = lax.broadcasted_iota(I32, (TQ, TQ), 0)
    col = lax.broadcasted_iota(I32, (TQ, TQ), 1)
    causal = row >= col
    for qi in range(seq // TQ):
        for hh in range(HEADS_PER_STEP):
            hs = slice(hh * HEAD_DIM, (hh + 1) * HEAD_DIM)
            k0 = qi * TQ
            q = jnp.concatenate([q_ref[k0:k0 + TQ, hs], bias_cols], axis=1)
            s_d = jnp.dot(q, kx[hh, :, k0:k0 + TQ], preferred_element_type=F32)
            s_d = jnp.where(causal, s_d, -jnp.inf)
            m = jnp.max(s_d, axis=-1, keepdims=True)
            if qi > 0:
                s_o = jnp.dot(q, kx[hh, :, 0:k0], preferred_element_type=F32)
                m = jnp.maximum(m, jnp.max(s_o, axis=-1, keepdims=True))
            acc = jnp.dot(jnp.exp2(s_d - m).astype(BF16), vx[hh, k0:k0 + TQ, :],
                          preferred_element_type=F32)
            if qi > 0:
                acc = acc + jnp.dot(jnp.exp2(s_o - m).astype(BF16), vx[hh, 0:k0, :],
                                    preferred_element_type=F32)
            inv_l = 1.0 / acc[:, HEAD_DIM:HEAD_DIM + 1]
            o_ref[k0:k0 + TQ, hs] = (acc[:, 0:HEAD_DIM] * inv_l).astype(BF16)


def _attention(q, kt, va, negc, batch, seq):
    t = q.shape[0]
    width = HEADS_PER_STEP * HEAD_DIM
    blk = pl.BlockSpec((seq, width), lambda b, g: (b, g))
    return pl.pallas_call(
        _attn_kernel,
        out_shape=jax.ShapeDtypeStruct((t, FOX_WIDTH), BF16),
        grid=(batch, FOX_HEADS // HEADS_PER_STEP),
        in_specs=[blk,
                  pl.BlockSpec((width, seq), lambda b, g: (g, b)),
                  blk,
                  pl.BlockSpec((1, HEADS_PER_STEP, 1, seq), lambda b, g: (b, g, 0, 0))],
        out_specs=blk,
        scratch_shapes=[pltpu.VMEM((HEADS_PER_STEP, 2 * HEAD_DIM, seq), BF16),
                        pltpu.VMEM((HEADS_PER_STEP, seq, 2 * HEAD_DIM), BF16)],
        compiler_params=pltpu.CompilerParams(
            dimension_semantics=("arbitrary", "arbitrary"), vmem_limit_bytes=VMEM_LIMIT),
        name="fox_attention",
    )(q, kt, va, negc)


def _mix_kernel(as_ref, af_ref, g_ref, x_ref, wps_ref, wpf_ref, wo_ref, g2_ref, wr_ref, br_ref,
                x1_ref, lg_ref):
    ys = jnp.dot(as_ref[...], wps_ref[...], preferred_element_type=F32)
    yf = jnp.dot(af_ref[...], wpf_ref[...], preferred_element_type=F32)
    m = (g_ref[:, :D_MODEL].astype(F32) * ys + g_ref[:, D_MODEL:].astype(F32) * yf).astype(BF16)
    x1 = x_ref[...] + jnp.dot(m, wo_ref[...], preferred_element_type=F32)
    x1_ref[...] = x1
    ms = jnp.mean(x1 * x1, axis=-1, keepdims=True)
    h2 = ((x1 * lax.rsqrt(ms + EPS)) * g2_ref[...]).astype(BF16)
    lg_ref[...] = jnp.dot(h2, wr_ref[...], preferred_element_type=F32) + br_ref[...]


def _mix(a_sgu, a_fox, gates, x2, wps, wpf, wo, g2, w_r, b_r):
    t = x2.shape[0]
    row = lambda i: (i, 0)
    const = lambda i: (0, 0)
    resident = functools.partial(pl.BlockSpec, index_map=const, pipeline_mode=pl.Buffered(1))
    return pl.pallas_call(
        _mix_kernel,
        out_shape=(jax.ShapeDtypeStruct((t, D_MODEL), F32),
                   jax.ShapeDtypeStruct((t, LANES), F32)),
        grid=(t // TM_MIX,),
        in_specs=[
            pl.BlockSpec((TM_MIX, SGU_WIDTH), row),
            pl.BlockSpec((TM_MIX, FOX_WIDTH), row),
            pl.BlockSpec((TM_MIX, 2 * D_MODEL), row),
            pl.BlockSpec((TM_MIX, D_MODEL), row),
            resident((SGU_WIDTH, D_MODEL)),
            resident((FOX_WIDTH, D_MODEL)),
            resident((D_MODEL, D_MODEL)),
            pl.BlockSpec((1, D_MODEL), const),
            resident((D_MODEL, LANES)),
            pl.BlockSpec((1, LANES), const),
        ],
        out_specs=(pl.BlockSpec((TM_MIX, D_MODEL), row), pl.BlockSpec((TM_MIX, LANES), row)),
        compiler_params=pltpu.CompilerParams(
            dimension_semantics=("arbitrary",), vmem_limit_bytes=VMEM_LIMIT),
        name="mix",
    )(a_sgu, a_fox, gates, x2, wps, wpf, wo, g2, w_r, b_r)


def _route_kernel(lg_ref, dt_ref, mf_ref, bm_ref, em_ref, mi_ref):
    t = lg_ref.shape[0]
    n_chunks = t // RCH
    lane_i = lax.broadcasted_iota(I32, (RCH, LANES), 1)
    lane = lane_i.astype(F32)
    lane_grp = ((lane_i - N_GROUPS) >> 3).astype(F32)
    is_grp = lane_i < N_GROUPS
    is_exp = (lane_i >= N_GROUPS) & (lane_i < N_GROUPS + N_EXPERTS)
    r_i = lax.broadcasted_iota(I32, (RCH, RCH), 0)
    c_i = lax.broadcasted_iota(I32, (RCH, RCH), 1)
    strict_lower = jnp.where(r_i > c_i, 1.0, 0.0).astype(BF16)
    neg_inf = -jnp.inf

    def first_max(vals):
        vmax = jnp.max(vals, axis=-1, keepdims=True)
        idx = jnp.min(jnp.where(vals == vmax, lane, float(LANES)), axis=-1, keepdims=True)
        return vmax, idx

    def pick(table, idx):
        return jnp.sum(jnp.where(lane == idx, table, 0.0), axis=-1, keepdims=True)

    def pack(cols):
        out = jnp.zeros((RCH, LANES), F32)
        for n, c in enumerate(cols):
            out = jnp.where(lane_i == n, c, out)
        return out

    def pass1(ci, counts):
        r0 = pl.multiple_of(ci * RCH, RCH)
        lg = lg_ref[pl.ds(r0, RCH), :]
        gmax, grp = first_max(jnp.where(is_grp, lg, neg_inf))
        p_grp = 1.0 / jnp.sum(jnp.where(is_grp, jnp.exp(lg - gmax), 0.0), axis=-1, keepdims=True)
        el = jnp.where(is_exp & (lane_grp == grp), lg, neg_inf)
        v1, i1 = first_max(el)
        v2, i2 = first_max(jnp.where(lane == i1, neg_inf, el))
        e21 = jnp.exp(v2 - v1)
        w1 = p_grp / (1.0 + e21)
        w2 = p_grp * e21 / (1.0 + e21)
        e1 = i1 - float(N_GROUPS)
        e2 = i2 - float(N_GROUPS)
        hot = jnp.where((lane == e1) | (lane == e2), 1.0, 0.0)
        before = jnp.dot(strict_lower, hot.astype(BF16), preferred_element_type=F32) + counts
        mi_ref[pl.ds(r0, RCH), :] = pack([e1, e2, pick(before, e1), pick(before, e2)]).astype(I32)
        mf_ref[pl.ds(r0, RCH), :] = pack([w1, w2])
        return counts + jnp.sum(hot, axis=0, keepdims=True)

    counts = lax.fori_loop(0, n_chunks, pass1, jnp.zeros((1, LANES), F32))

    nblk = jnp.floor((counts + float(EXPERT_BLOCK - 1)) * (1.0 / EXPERT_BLOCK))
    u_r = lax.broadcasted_iota(I32, (LANES, LANES), 0)
    u_c = lax.broadcasted_iota(I32, (LANES, LANES), 1)
    strict_upper = jnp.where(u_r < u_c, 1.0, 0.0).astype(BF16)
    bstart = jnp.dot(jnp.broadcast_to(nblk, (8, LANES)).astype(BF16), strict_upper,
                     preferred_element_type=F32)[0:1, :]
    bend = bstart + nblk

    def pass2(ci, _):
        r0 = pl.multiple_of(ci * RCH, RCH)
        mi = mi_ref[pl.ds(r0, RCH), :].astype(F32)
        cols = [jnp.sum(jnp.where(lane_i == n, mi, 0.0), axis=-1, keepdims=True) for n in range(4)]
        d1 = pick(bstart, cols[0]) * float(EXPERT_BLOCK) + cols[2]
        d2 = pick(bstart, cols[1]) * float(EXPERT_BLOCK) + cols[3]
        dt_ref[ci] = pack([d1, d2]).T[0:8, :].astype(I32)
        return 0

    lax.fori_loop(0, n_chunks, pass2, 0)

    nb_rows = bm_ref.shape[0]
    b_col = lax.broadcasted_iota(I32, (nb_rows, LANES), 0).astype(F32)
    b_lane = lax.broadcasted_iota(I32, (nb_rows, LANES), 1)
    done = jnp.where((bend <= b_col) & (b_lane < N_EXPERTS), 1.0, 0.0)
    blk_e = jnp.minimum(jnp.sum(done, axis=-1, keepdims=True), float(N_EXPERTS - 1))
    n_used = jnp.sum(jnp.where(b_lane[0:1, :] == N_EXPERTS - 1, bend, 0.0), axis=-1, keepdims=True)
    bm_ref[...] = jnp.where(b_lane == 0, blk_e, jnp.where(b_lane == 1, n_used, 0.0)).astype(I32)

    e_row = lax.broadcasted_iota(I32, (8, LANES), 0)
    em_ref[...] = jnp.where(e_row == 0, counts,
                            jnp.where(e_row == 1, bstart, jnp.where(e_row == 2, nblk, 0.0))).astype(I32)


def _route(logits, nb_rows):
    t = logits.shape[0]
    full = lambda shape: pl.BlockSpec(shape, lambda: (0,) * len(shape))
    return pl.pallas_call(
        _route_kernel,
        out_shape=(jax.ShapeDtypeStruct((t // RCH, 8, RCH), I32),
                   jax.ShapeDtypeStruct((t, LANES), F32),
                   jax.ShapeDtypeStruct((nb_rows, LANES), I32),
                   jax.ShapeDtypeStruct((8, LANES), I32)),
        in_specs=[full((t, LANES))],
        out_specs=(full((t // RCH, 8, RCH)), full((t, LANES)), full((nb_rows, LANES)),
                   full((8, LANES))),
        scratch_shapes=[pltpu.VMEM((t, LANES), I32)],
        compiler_params=pltpu.CompilerParams(vmem_limit_bytes=VMEM_LIMIT),
        name="route",
    )(logits)


DISPATCH_UNROLL = 8
TM_DSP = 1024


def _dispatch_kernel(d1_ref, d2_ref, cnt_ref, bstart_ref, nblk_ref, nused_ref, x_ref, xs_ref, sem):
    i = pl.program_id(0)
    tm = x_ref.shape[0]
    nb = xs_ref.shape[0] // EXPERT_BLOCK

    def row_copy(src_row, dst_row):
        return pltpu.make_async_copy(x_ref.at[pl.ds(src_row, 1), :],
                                     xs_ref.at[pl.ds(dst_row, 1), :], sem)

    def block_copy(dst_block):
        dst_row = pl.multiple_of(dst_block * EXPERT_BLOCK, EXPERT_BLOCK)
        return pltpu.make_async_copy(x_ref.at[pl.ds(0, EXPERT_BLOCK), :],
                                     xs_ref.at[pl.ds(dst_row, EXPERT_BLOCK), :], sem)

    def tok_body(r, _):
        tok = i * tm + r
        row_copy(r, d1_ref[tok]).start(priority=0)
        row_copy(r, d2_ref[tok]).start(priority=1)
        return 0

    lax.fori_loop(0, tm, tok_body, 0, unroll=DISPATCH_UNROLL)

    @pl.when(i == 0)
    def _():
        def pad_expert(e, n_pad):
            lo = bstart_ref[e] * EXPERT_BLOCK + cnt_ref[e]
            hi = (bstart_ref[e] + nblk_ref[e]) * EXPERT_BLOCK

            def pad_body(r, _):
                row_copy(0, r).start()
                return 0

            lax.fori_loop(lo, hi, pad_body, 0)
            return n_pad + (hi - lo)

        n_pad = lax.fori_loop(0, N_EXPERTS, pad_expert, 0)
        n_used = nused_ref[0]

        def fill_body(b, _):
            block_copy(b).start()
            return 0

        lax.fori_loop(n_used, nb, fill_body, 0)

        def block_wait(_, c):
            block_copy(0).wait()
            return c

        lax.fori_loop(n_used, nb, block_wait, 0)

        def row_wait(_, c):
            row_copy(0, 0).wait()
            return c

        lax.fori_loop(0, n_pad, row_wait, 0)

    for _ in range((2 * tm) // EXPERT_BLOCK):
        block_copy(0).wait()


def _dispatch(dest1, dest2, cnt, bstart, nblk, n_used, x1, n_rows):
    t = x1.shape[0]
    grid_spec = pltpu.PrefetchScalarGridSpec(
        num_scalar_prefetch=6,
        grid=(t // TM_DSP,),
        in_specs=[pl.BlockSpec((TM_DSP, D_MODEL), lambda i, *_: (i, 0))],
        out_specs=pl.BlockSpec(memory_space=pl.ANY),
        scratch_shapes=[pltpu.SemaphoreType.DMA(())],
    )
    return pl.pallas_call(
        _dispatch_kernel,
        out_shape=jax.ShapeDtypeStruct((n_rows, D_MODEL), F32),
        grid_spec=grid_spec,
        compiler_params=pltpu.CompilerParams(
            dimension_semantics=("arbitrary",), vmem_limit_bytes=VMEM_LIMIT),
        name="dispatch",
    )(dest1, dest2, cnt, bstart, nblk, n_used, x1)


def _experts_kernel(blk_e_ref, nused_ref, nblk_ref, x_ref, g2_ref, wg_hbm, wu_hbm, wd_hbm, y_ref,
                    wg_f, wu_f, wd_f, wg_s, wu_s, wd_s, slot_ref, sem):
    b = pl.program_id(0)
    n_used = nused_ref[0]
    used = b < n_used
    e = blk_e_ref[b]
    new_expert = (b == 0) | (e != blk_e_ref[jnp.maximum(b - 1, 0)])

    def weight_copies(expert, slot):
        return (pltpu.make_async_copy(wg_hbm.at[expert], wg_f.at[slot], sem.at[slot]),
                pltpu.make_async_copy(wu_hbm.at[expert], wu_f.at[slot], sem.at[slot]),
                pltpu.make_async_copy(wd_hbm.at[expert], wd_f.at[slot], sem.at[slot]))

    @pl.when(b == 0)
    def _():
        slot_ref[0] = 0
        for cp in weight_copies(e, 0):
            cp.start()

    @pl.when(used & new_expert)
    def _():
        slot = slot_ref[0]
        b_next = b + nblk_ref[e]

        @pl.when(b_next < n_used)
        def _():
            for cp in weight_copies(blk_e_ref[jnp.minimum(b_next, n_used - 1)], 1 - slot):
                cp.start(priority=PREFETCH_PRIORITY)

        for cp in weight_copies(e, slot):
            cp.wait()
        wg_s[...] = wg_f[slot].astype(BF16)
        wu_s[...] = wu_f[slot].astype(BF16)
        wd_s[...] = wd_f[slot].astype(BF16)
        slot_ref[0] = 1 - slot

    @pl.when(used)
    def _():
        x = x_ref[...]
        ms = jnp.mean(x * x, axis=-1, keepdims=True)
        h = ((x * lax.rsqrt(ms + EPS)) * g2_ref[...]).astype(BF16)
        a = jnp.dot(h, wg_s[...], preferred_element_type=F32)
        u = jnp.dot(h, wu_s[...], preferred_element_type=F32)
        mid = ((a * _sigmoid(a)) * u).astype(BF16)
        y_ref[...] = jnp.dot(mid, wd_s[...], preferred_element_type=F32)

    @pl.when(jnp.logical_not(used))
    def _():
        y_ref[...] = jnp.zeros_like(y_ref)


def _experts(blk_e, n_used, nblk, xs, g2, w_g, w_u, w_d):
    n_rows = xs.shape[0]
    nb = n_rows // EXPERT_BLOCK

    def row_blk(b, blk_e_ref, nused_ref, nblk_ref):
        return (jnp.minimum(b, nused_ref[0] - 1), 0)

    hbm = pl.BlockSpec(memory_space=pl.ANY)
    grid_spec = pltpu.PrefetchScalarGridSpec(
        num_scalar_prefetch=3,
        grid=(nb,),
        in_specs=[
            pl.BlockSpec((EXPERT_BLOCK, D_MODEL), row_blk),
            pl.BlockSpec((1, D_MODEL), lambda b, *_: (0, 0)),
            hbm, hbm, hbm,
        ],
        out_specs=pl.BlockSpec((EXPERT_BLOCK, D_MODEL), lambda b, *_: (b, 0)),
        scratch_shapes=[pltpu.VMEM((2, D_MODEL, D_EXPERT), F32),
                        pltpu.VMEM((2, D_MODEL, D_EXPERT), F32),
                        pltpu.VMEM((2, D_EXPERT, D_MODEL), F32),
                        pltpu.VMEM((D_MODEL, D_EXPERT), BF16),
                        pltpu.VMEM((D_MODEL, D_EXPERT), BF16),
                        pltpu.VMEM((D_EXPERT, D_MODEL), BF16),
                        pltpu.SMEM((1,), I32),
                        pltpu.SemaphoreType.DMA((2,))],
    )
    return pl.pallas_call(
        _experts_kernel,
        out_shape=jax.ShapeDtypeStruct((n_rows, D_MODEL), F32),
        grid_spec=grid_spec,
        compiler_params=pltpu.CompilerParams(
            dimension_semantics=("arbitrary",), vmem_limit_bytes=VMEM_LIMIT),
        name="experts",
    )(blk_e, n_used, nblk, xs, g2, w_g, w_u, w_d)


COMBINE_UNROLL = 8


def _combine_kernel(d1_ref, d2_ref, x1_ref, w_ref, y_ref, o_ref, gbuf, sem):
    i = pl.program_id(0)
    n = pl.num_programs(0)
    tm = x1_ref.shape[0]

    def issue(tile, slot):
        def body(r, _):
            tok = tile * tm + r
            pltpu.make_async_copy(y_ref.at[pl.ds(d1_ref[tok], 1), :],
                                  gbuf.at[slot, pl.ds(r, 1), :], sem.at[slot]).start(priority=0)
            pltpu.make_async_copy(y_ref.at[pl.ds(d2_ref[tok], 1), :],
                                  gbuf.at[slot, pl.ds(tm + r, 1), :], sem.at[slot]).start(priority=1)
            return 0

        lax.fori_loop(0, tm, body, 0, unroll=COMBINE_UNROLL)

    slot = i % 2

    @pl.when(i == 0)
    def _():
        issue(0, 0)

    @pl.when(i + 1 < n)
    def _():
        issue(i + 1, 1 - slot)

    pltpu.make_async_copy(y_ref.at[pl.ds(0, 2 * tm), :], gbuf.at[slot], sem.at[slot]).wait()
    w = w_ref[...]
    g = gbuf[slot]
    o_ref[...] = x1_ref[...] + (w[:, 0:1] * g[0:tm, :] + w[:, 1:2] * g[tm:2 * tm, :])


def _combine(dest1, dest2, x1, w, yb):
    t = x1.shape[0]
    grid_spec = pltpu.PrefetchScalarGridSpec(
        num_scalar_prefetch=2,
        grid=(t // TM_CMB,),
        in_specs=[
            pl.BlockSpec((TM_CMB, D_MODEL), lambda i, a, b: (i, 0)),
            pl.BlockSpec((TM_CMB, LANES), lambda i, a, b: (i, 0)),
            pl.BlockSpec(memory_space=pl.ANY),
        ],
        out_specs=pl.BlockSpec((TM_CMB, D_MODEL), lambda i, a, b: (i, 0)),
        scratch_shapes=[pltpu.VMEM((2, 2 * TM_CMB, D_MODEL), F32),
                        pltpu.SemaphoreType.DMA((2,))],
    )
    return pl.pallas_call(
        _combine_kernel,
        out_shape=jax.ShapeDtypeStruct((t, D_MODEL), F32),
        grid_spec=grid_spec,
        compiler_params=pltpu.CompilerParams(
            dimension_semantics=("arbitrary",), vmem_limit_bytes=VMEM_LIMIT),
        name="combine",
    )(dest1, dest2, x1, w, yb)


def kernel(x, norm1_g, w_in, b_gate, b_forget, sgu_ln_g, sgu_ln_b, w_spatial, b_spatial, q_norm_g, k_norm_g, w_proj_sgu, w_proj_fox, w_out, norm2_g, w_router_group, b_router_group, w_router_expert, b_router_expert, w_expert_gate, w_expert_up, w_expert_down):
    batch, seq, d = x.shape
    t = batch * seq
    l = 0
    x2 = x.reshape(t, d)

    wt = jnp.swapaxes(w_in[l], 0, 1)
    off_f = N_MAIN_SEC * SEC
    w_f = jnp.pad(wt[off_f:off_f + FOX_HEADS].T, ((0, 0), (0, LANES - FOX_HEADS))).astype(BF16)
    b_f = jnp.pad(b_forget[l], (0, LANES - FOX_HEADS)).reshape(1, LANES)
    n_r = N_GROUPS + N_EXPERTS
    w_r = jnp.pad(jnp.concatenate([w_router_group[l], w_router_expert[l]], axis=1),
                  ((0, 0), (0, LANES - n_r))).astype(BF16)
    b_r = jnp.pad(jnp.concatenate([b_router_group[l], b_router_expert[l]]),
                  (0, LANES - n_r)).reshape(1, LANES)

    h, lf = _norm1(x2, norm1_g[l].reshape(1, d), w_f, b_f)
    a_sgu, q, kt, va, gates = _inproj(
        h, wt, b_gate[l].reshape(1, 2 * d),
        sgu_ln_g[l].reshape(1, SGU_WIDTH), sgu_ln_b[l].reshape(1, SGU_WIDTH),
        w_spatial[l], b_spatial[l].T, q_norm_g[l].reshape(1, HEAD_DIM),
        k_norm_g[l].reshape(1, HEAD_DIM))
    negc = _forget_cumsum(lf, batch, seq).reshape(batch, FOX_HEADS, 1, seq)
    a_fox = _attention(q, kt, va, negc, batch, seq)
    x1, logits = _mix(a_sgu, a_fox, gates, x2, w_proj_sgu[l].astype(BF16),
                      w_proj_fox[l].astype(BF16), w_out[l].astype(BF16),
                      norm2_g[l].reshape(1, d), w_r, b_r)

    n_assign = 2 * t
    n_rows = n_assign + N_EXPERTS * EXPERT_BLOCK
    nb = n_rows // EXPERT_BLOCK
    nb_rows = -(-nb // 8) * 8
    dest_t, meta_f, bmeta, emeta = _route(logits, nb_rows)
    dest1, dest2 = dest_t[:, 0, :].reshape(t), dest_t[:, 1, :].reshape(t)
    n_used = bmeta[0:1, 1]
    xs = _dispatch(dest1, dest2, emeta[0, :N_EXPERTS], emeta[1, :N_EXPERTS], emeta[2, :N_EXPERTS],
                   n_used, x1, n_rows)
    yb = _experts(bmeta[:nb, 0], n_used, emeta[2, :N_EXPERTS], xs, norm2_g[l].reshape(1, d),
                  w_expert_gate[l], w_expert_up[l], w_expert_down[l])
    out = _combine(dest1, dest2, x1, meta_f, yb)
    return out.reshape(batch, seq, d)
```

```python
import functools

import jax
import jax.numpy as jnp
from jax import lax
from jax.experimental import pallas as pl
from jax.experimental.pallas import tpu as pltpu

F32 = jnp.float32
BF16 = jnp.bfloat16
I32 = jnp.int32

D_MODEL = 2048
CHUNK = 128
SGU_GROUPS = 8
SGU_WIDTH = 1024
FOX_HEADS = 8
HEAD_DIM = 128
FOX_WIDTH = 1024
N_GROUPS = 4
EXPERTS_PER_GROUP = 8
N_EXPERTS = 32
D_EXPERT = 512
EXPERT_BLOCK = 128
EPS = 1e-6

LANES = 128
SUBLANES = 8
PREFETCH_PRIORITY = 1
VMEM_LIMIT = 56 * 1024 * 1024

SEC = 1024
N_MAIN_SEC = 5
N_GATE_SEC = 4
TM_NORM = 1024
TM_IN = 512
TQ = 256
HEADS_PER_STEP = 4
TM_MIX = 256
RCH = 512
TM_CMB = 512
LOG2E = 1.4426950408889634
Q_SCALE = HEAD_DIM ** -0.5 * LOG2E


def _sigmoid(x):
    return 1.0 / (1.0 + jnp.exp(-x))


def _log_sigmoid(x):
    return jnp.minimum(x, 0.0) - jnp.log1p(jnp.exp(-jnp.abs(x)))


def _norm1_kernel(x_ref, g1_ref, wf_ref, bf_ref, h_ref, lf_ref):
    x = x_ref[...]
    ms = jnp.mean(x * x, axis=-1, keepdims=True)
    hb = ((x * lax.rsqrt(ms + EPS)) * g1_ref[...]).astype(BF16)
    h_ref[...] = hb
    f = jnp.dot(hb, wf_ref[...], preferred_element_type=F32) + bf_ref[...]
    lf_ref[...] = _log_sigmoid(f)


def _norm1(x2, g1, w_f, b_f):
    t = x2.shape[0]
    row = lambda i: (i, 0)
    const = lambda i: (0, 0)
    return pl.pallas_call(
        _norm1_kernel,
        out_shape=(jax.ShapeDtypeStruct((t, D_MODEL), BF16),
                   jax.ShapeDtypeStruct((t, LANES), F32)),
        grid=(t // TM_NORM,),
        in_specs=[pl.BlockSpec((TM_NORM, D_MODEL), row),
                  pl.BlockSpec((1, D_MODEL), const),
                  pl.BlockSpec((D_MODEL, LANES), const),
                  pl.BlockSpec((1, LANES), const)],
        out_specs=(pl.BlockSpec((TM_NORM, D_MODEL), row), pl.BlockSpec((TM_NORM, LANES), row)),
        compiler_params=pltpu.CompilerParams(
            dimension_semantics=("arbitrary",), vmem_limit_bytes=VMEM_LIMIT),
        name="norm1",
    )(x2, g1, w_f, b_f)


N_SECTIONS = 1 + 3 + N_GATE_SEC
N_CHUNKS = N_MAIN_SEC + N_GATE_SEC
GATE_SECTION0 = 4
WCONV_ROWS = 256


def _inproj_kernel(h_ref, wt_hbm, bg_ref, lng_ref, lnb_ref, wsp_ref, bsp_ref, qg_ref, kg_ref,
                   asgu_ref, q_ref, kt_ref, va_ref, gates_ref,
                   stage, wb, sem):
    s = pl.program_id(0)
    i = pl.program_id(1)
    tm = h_ref.shape[0]

    def chunk_copy(c, slot):
        r0 = pl.multiple_of(c * SEC + jnp.where(c >= N_MAIN_SEC, FOX_HEADS, 0), 8)
        return pltpu.make_async_copy(wt_hbm.at[pl.ds(r0, SEC), :], stage.at[slot], sem.at[slot])

    def convert(slot, k):
        for p in range(SEC // WCONV_ROWS):
            rs = slice(p * WCONV_ROWS, (p + 1) * WCONV_ROWS)
            wb[k, :, rs] = stage[slot, rs, :].T.astype(BF16)

    @pl.when((s == 0) & (i == 0))
    def _():
        chunk_copy(0, 0).start()
        chunk_copy(1, 1).start()
        chunk_copy(0, 0).wait()
        convert(0, 0)
        chunk_copy(2, 0).start(priority=PREFETCH_PRIORITY)
        chunk_copy(1, 1).wait()
        convert(1, 1)

    @pl.when((s > 0) & (i == 0))
    def _():
        c = s + 1
        slot = c % 2
        chunk_copy(c, slot).wait()

        @pl.when(c + 1 < N_CHUNKS)
        def _():
            chunk_copy(c + 1, 1 - slot).start(priority=PREFETCH_PRIORITY)

        convert(slot, 0)

    def section(k=0):
        return jnp.dot(h_ref[...], wb[k], preferred_element_type=F32)

    @pl.when(s == 0)
    def _():
        u = jax.nn.gelu(section(0))
        v = jax.nn.gelu(section(1))
        mu = jnp.mean(v, axis=-1, keepdims=True)
        vc = v - mu
        var = jnp.mean(vc * vc, axis=-1, keepdims=True)
        vn = ((vc * lax.rsqrt(var + EPS)) * lng_ref[...] + lnb_ref[...]).astype(BF16)
        row = lax.broadcasted_iota(I32, (CHUNK, CHUNK), 0)
        col = lax.broadcasted_iota(I32, (CHUNK, CHUNK), 1)
        causal = row >= col
        for g in range(SGU_GROUPS):
            wg = jnp.where(causal, wsp_ref[g], 0.0).astype(BF16)
            bcol = bsp_ref[:, g:g + 1]
            gs = slice(g * LANES, (g + 1) * LANES)
            n_ch = tm // CHUNK
            rhs = jnp.concatenate([vn[c * CHUNK:(c + 1) * CHUNK, gs] for c in range(n_ch)], axis=1)
            sg = jnp.dot(wg, rhs, preferred_element_type=F32) + bcol
            for c in range(n_ch):
                rs = slice(c * CHUNK, (c + 1) * CHUNK)
                asgu_ref[rs, gs] = (u[rs, gs] * sg[:, c * LANES:(c + 1) * LANES]).astype(BF16)

    def _head_norm(z, h, gain_ref):
        zh = z[:, h * HEAD_DIM:(h + 1) * HEAD_DIM]
        ms = jnp.mean(zh * zh, axis=-1, keepdims=True)
        return (zh * lax.rsqrt(ms + EPS)) * gain_ref[...]

    @pl.when(s == 1)
    def _():
        z = section()
        for h in range(FOX_HEADS):
            q_ref[:, h * HEAD_DIM:(h + 1) * HEAD_DIM] = (
                _head_norm(z, h, qg_ref) * Q_SCALE).astype(BF16)

    @pl.when(s == 2)
    def _():
        z = section()
        for h in range(FOX_HEADS):
            kt_ref[h * HEAD_DIM:(h + 1) * HEAD_DIM, :] = _head_norm(z, h, kg_ref).T.astype(BF16)

    @pl.when(s == 3)
    def _():
        va_ref[...] = section().astype(BF16)

    @pl.when(s >= GATE_SECTION0)
    def _():
        gates_ref[...] = (0.5 * jnp.tanh(0.5 * (section() + bg_ref[...])) + 0.5).astype(BF16)


def _inproj(h, wt, b_gate, ln_g, ln_b, w_sp, b_sp_t, q_g, k_g):
    t = h.shape[0]
    n_i = t // TM_IN

    def active(sec):
        return lambda s, i: jnp.where(s < sec, 0, jnp.where(s > sec, n_i - 1, i))

    def gate_blk(s, i):
        g = jnp.clip(s - GATE_SECTION0, 0, N_GATE_SEC - 1)
        return (jnp.where(s < GATE_SECTION0, 0, i), g)

    const2 = lambda s, i: (0, 0)
    out_shape = (
        jax.ShapeDtypeStruct((t, SGU_WIDTH), BF16),
        jax.ShapeDtypeStruct((t, FOX_WIDTH), BF16),
        jax.ShapeDtypeStruct((FOX_WIDTH, t), BF16),
        jax.ShapeDtypeStruct((t, FOX_WIDTH), BF16),
        jax.ShapeDtypeStruct((t, 2 * D_MODEL), BF16),
    )
    return pl.pallas_call(
        _inproj_kernel,
        out_shape=out_shape,
        grid=(N_SECTIONS, n_i),
        in_specs=[
            pl.BlockSpec((TM_IN, D_MODEL), lambda s, i: (i, 0)),
            pl.BlockSpec(memory_space=pl.ANY),
            pl.BlockSpec((1, SEC), lambda s, i: (0, jnp.clip(s - GATE_SECTION0, 0, N_GATE_SEC - 1))),
            pl.BlockSpec((1, SGU_WIDTH), const2),
            pl.BlockSpec((1, SGU_WIDTH), const2),
            pl.BlockSpec((SGU_GROUPS, CHUNK, CHUNK), lambda s, i: (0, 0, 0)),
            pl.BlockSpec((CHUNK, SGU_GROUPS), const2),
            pl.BlockSpec((1, HEAD_DIM), const2),
            pl.BlockSpec((1, HEAD_DIM), const2),
        ],
        out_specs=(
            pl.BlockSpec((TM_IN, SGU_WIDTH), lambda s, i: (active(0)(s, i), 0)),
            pl.BlockSpec((TM_IN, FOX_WIDTH), lambda s, i: (active(1)(s, i), 0)),
            pl.BlockSpec((FOX_WIDTH, TM_IN), lambda s, i: (0, active(2)(s, i))),
            pl.BlockSpec((TM_IN, FOX_WIDTH), lambda s, i: (active(3)(s, i), 0)),
            pl.BlockSpec((TM_IN, SEC), gate_blk),
        ),
        scratch_shapes=[pltpu.VMEM((2, SEC, D_MODEL), F32),
                        pltpu.VMEM((2, D_MODEL, SEC), BF16),
                        pltpu.SemaphoreType.DMA((2,))],
        compiler_params=pltpu.CompilerParams(
            dimension_semantics=("arbitrary", "arbitrary"), vmem_limit_bytes=VMEM_LIMIT),
        name="inproj",
    )(h, wt, b_gate, ln_g, ln_b, w_sp, b_sp_t, q_g, k_g)


def _split3(x):
    x0 = x.astype(BF16)
    r1 = x - x0.astype(F32)
    x1 = r1.astype(BF16)
    r2 = r1 - x1.astype(F32)
    return x0, x1, r2.astype(BF16)


def _cumsum_kernel(lf_ref, negc_ref, c_s):
    seq = lf_ref.shape[0]
    row = lax.broadcasted_iota(I32, (CHUNK, CHUNK), 0)
    col = lax.broadcasted_iota(I32, (CHUNK, CHUNK), 1)
    tri = jnp.where(row >= col, 1.0, 0.0).astype(BF16)
    carry = jnp.zeros((1, LANES), F32)
    for r in range(seq // CHUNK):
        rs = slice(r * CHUNK, (r + 1) * CHUNK)
        x0, x1, x2 = _split3(lf_ref[rs, :])
        cs = (jnp.dot(tri, x0, preferred_element_type=F32)
              + jnp.dot(tri, x1, preferred_element_type=F32)
              + jnp.dot(tri, x2, preferred_element_type=F32)) + carry
        carry = cs[CHUNK - 1:CHUNK, :]
        c_s[rs, :] = cs
    ct = c_s[...].T
    negc_ref[0] = -ct[0:FOX_HEADS, :]


def _forget_cumsum(lf, batch, seq):
    return pl.pallas_call(
        _cumsum_kernel,
        out_shape=jax.ShapeDtypeStruct((batch, FOX_HEADS, seq), F32),
        grid=(batch,),
        in_specs=[pl.BlockSpec((seq, LANES), lambda b: (b, 0))],
        out_specs=pl.BlockSpec((1, FOX_HEADS, seq), lambda b: (b, 0, 0)),
        scratch_shapes=[pltpu.VMEM((seq, LANES), F32)],
        compiler_params=pltpu.CompilerParams(
            dimension_semantics=("arbitrary",), vmem_limit_bytes=VMEM_LIMIT),
        name="forget_cumsum",
    )(lf)


def _attn_kernel(q_ref, kt_ref, v_ref, negc_ref, o_ref, kx, vx):
    seq = q_ref.shape[0]
    sub = lax.broadcasted_iota(I32, (HEAD_DIM, seq), 0)
    for hh in range(HEADS_PER_STEP):
        hs = slice(hh * HEAD_DIM, (hh + 1) * HEAD_DIM)
        c0, c1, c2 = (c.astype(F32) for c in _split3(negc_ref[0, hh] * LOG2E))
        kx[hh, 0:HEAD_DIM, :] = kt_ref[hs, :]
        kx[hh, HEAD_DIM:, :] = jnp.where(
            sub == 0, c0, jnp.where(sub == 1, c1, jnp.where(sub == 2, c2, 0.0))).astype(BF16)
        vx[hh, :, 0:HEAD_DIM] = v_ref[:, hs]
        vx[hh, :, HEAD_DIM:] = jnp.ones((seq, HEAD_DIM), BF16)
    lane = lax.broadcasted_iota(I32, (TQ, HEAD_DIM), 1)
    bias_cols = jnp.where(lane < 3, 1.0, 0.0).astype(BF16)
    row = lax.broadcasted_iota(I32, (TQ, TQ), 0)
    col = lax.broadcasted_iota(I32, (TQ, TQ), 1)
    causal = row >= col
    for qi in range(seq // TQ):
        for hh in range(HEADS_PER_STEP):
            hs = slice(hh * HEAD_DIM, (hh + 1) * HEAD_DIM)
            k0 = qi * TQ
            q = jnp.concatenate([q_ref[k0:k0 + TQ, hs], bias_cols], axis=1)
            s_d = jnp.dot(q, kx[hh, :, k0:k0 + TQ], preferred_element_type=F32)
            s_d = jnp.where(causal, s_d, -jnp.inf)
            m = jnp.max(s_d, axis=-1, keepdims=True)
            if qi > 0:
                s_o = jnp.dot(q, kx[hh, :, 0:k0], preferred_element_type=F32)
                m = jnp.maximum(m, jnp.max(s_o, axis=-1, keepdims=True))
            acc = jnp.dot(jnp.exp2(s_d - m).astype(BF16), vx[hh, k0:k0 + TQ, :],
                          preferred_element_type=F32)
            if qi > 0:
                acc = acc + jnp.dot(jnp.exp2(s_o - m).astype(BF16), vx[hh, 0:k0, :],
                                    preferred_element_type=F32)
            inv_l = 1.0 / acc[:, HEAD_DIM:HEAD_DIM + 1]
            o_ref[k0:k0 + TQ, hs] = (acc[:, 0:HEAD_DIM] * inv_l).astype(BF16)


def _attention(q, kt, va, negc, batch, seq):
    t = q.shape[0]
    width = HEADS_PER_STEP * HEAD_DIM
    blk = pl.BlockSpec((seq, width), lambda b, g: (b, g))
    return pl.pallas_call(
        _attn_kernel,
        out_shape=jax.ShapeDtypeStruct((t, FOX_WIDTH), BF16),
        grid=(batch, FOX_HEADS // HEADS_PER_STEP),
        in_specs=[blk,
                  pl.BlockSpec((width, seq), lambda b, g: (g, b)),
                  blk,
                  pl.BlockSpec((1, HEADS_PER_STEP, 1, seq), lambda b, g: (b, g, 0, 0))],
        out_specs=blk,
        scratch_shapes=[pltpu.VMEM((HEADS_PER_STEP, 2 * HEAD_DIM, seq), BF16),
                        pltpu.VMEM((HEADS_PER_STEP, seq, 2 * HEAD_DIM), BF16)],
        compiler_params=pltpu.CompilerParams(
            dimension_semantics=("arbitrary", "arbitrary"), vmem_limit_bytes=VMEM_LIMIT),
        name="fox_attention",
    )(q, kt, va, negc)


def _mix_kernel(as_ref, af_ref, g_ref, x_ref, wps_ref, wpf_ref, wo_ref, g2_ref, wr_ref, br_ref,
                x1_ref, lg_ref):
    ys = jnp.dot(as_ref[...], wps_ref[...], preferred_element_type=F32)
    yf = jnp.dot(af_ref[...], wpf_ref[...], preferred_element_type=F32)
    m = (g_ref[:, :D_MODEL].astype(F32) * ys + g_ref[:, D_MODEL:].astype(F32) * yf).astype(BF16)
    x1 = x_ref[...] + jnp.dot(m, wo_ref[...], preferred_element_type=F32)
    x1_ref[...] = x1
    ms = jnp.mean(x1 * x1, axis=-1, keepdims=True)
    h2 = ((x1 * lax.rsqrt(ms + EPS)) * g2_ref[...]).astype(BF16)
    lg_ref[...] = jnp.dot(h2, wr_ref[...], preferred_element_type=F32) + br_ref[...]


def _mix(a_sgu, a_fox, gates, x2, wps, wpf, wo, g2, w_r, b_r):
    t = x2.shape[0]
    row = lambda i: (i, 0)
    const = lambda i: (0, 0)
    resident = functools.partial(pl.BlockSpec, index_map=const, pipeline_mode=pl.Buffered(1))
    return pl.pallas_call(
        _mix_kernel,
        out_shape=(jax.ShapeDtypeStruct((t, D_MODEL), F32),
                   jax.ShapeDtypeStruct((t, LANES), F32)),
        grid=(t // TM_MIX,),
        in_specs=[
            pl.BlockSpec((TM_MIX, SGU_WIDTH), row),
            pl.BlockSpec((TM_MIX, FOX_WIDTH), row),
            pl.BlockSpec((TM_MIX, 2 * D_MODEL), row),
            pl.BlockSpec((TM_MIX, D_MODEL), row),
            resident((SGU_WIDTH, D_MODEL)),
            resident((FOX_WIDTH, D_MODEL)),
            resident((D_MODEL, D_MODEL)),
            pl.BlockSpec((1, D_MODEL), const),
            resident((D_MODEL, LANES)),
            pl.BlockSpec((1, LANES), const),
        ],
        out_specs=(pl.BlockSpec((TM_MIX, D_MODEL), row), pl.BlockSpec((TM_MIX, LANES), row)),
        compiler_params=pltpu.CompilerParams(
            dimension_semantics=("arbitrary",), vmem_limit_bytes=VMEM_LIMIT),
        name="mix",
    )(a_sgu, a_fox, gates, x2, wps, wpf, wo, g2, w_r, b_r)


def _route_kernel(lg_ref, dt_ref, mf_ref, bm_ref, em_ref, bt_ref, mi_ref, inv_ref):
    t = lg_ref.shape[0]
    n_chunks = t // RCH
    lane_i = lax.broadcasted_iota(I32, (RCH, LANES), 1)
    lane = lane_i.astype(F32)
    lane_grp = ((lane_i - N_GROUPS) >> 3).astype(F32)
    is_grp = lane_i < N_GROUPS
    is_exp = (lane_i >= N_GROUPS) & (lane_i < N_GROUPS + N_EXPERTS)
    r_i = lax.broadcasted_iota(I32, (RCH, RCH), 0)
    c_i = lax.broadcasted_iota(I32, (RCH, RCH), 1)
    strict_lower = jnp.where(r_i > c_i, 1.0, 0.0).astype(BF16)
    neg_inf = -jnp.inf

    def first_max(vals):
        vmax = jnp.max(vals, axis=-1, keepdims=True)
        idx = jnp.min(jnp.where(vals == vmax, lane, float(LANES)), axis=-1, keepdims=True)
        return vmax, idx

    def pick(table, idx):
        return jnp.sum(jnp.where(lane == idx, table, 0.0), axis=-1, keepdims=True)

    def pack(cols):
        out = jnp.zeros((RCH, LANES), F32)
        for n, c in enumerate(cols):
            out = jnp.where(lane_i == n, c, out)
        return out

    def pass1(ci, counts):
        r0 = pl.multiple_of(ci * RCH, RCH)
        lg = lg_ref[pl.ds(r0, RCH), :]
        gmax, grp = first_max(jnp.where(is_grp, lg, neg_inf))
        p_grp = 1.0 / jnp.sum(jnp.where(is_grp, jnp.exp(lg - gmax), 0.0), axis=-1, keepdims=True)
        el = jnp.where(is_exp & (lane_grp == grp), lg, neg_inf)
        v1, i1 = first_max(el)
        v2, i2 = first_max(jnp.where(lane == i1, neg_inf, el))
        e21 = jnp.exp(v2 - v1)
        w1 = p_grp / (1.0 + e21)
        w2 = p_grp * e21 / (1.0 + e21)
        e1 = i1 - float(N_GROUPS)
        e2 = i2 - float(N_GROUPS)
        hot = jnp.where((lane == e1) | (lane == e2), 1.0, 0.0)
        before = jnp.dot(strict_lower, hot.astype(BF16), preferred_element_type=F32) + counts
        mi_ref[pl.ds(r0, RCH), :] = pack([e1, e2, pick(before, e1), pick(before, e2)]).astype(I32)
        mf_ref[pl.ds(r0, RCH), :] = pack([w1, w2])
        return counts + jnp.sum(hot, axis=0, keepdims=True)

    counts = lax.fori_loop(0, n_chunks, pass1, jnp.zeros((1, LANES), F32))

    nblk = jnp.floor((counts + float(EXPERT_BLOCK - 1)) * (1.0 / EXPERT_BLOCK))
    u_r = lax.broadcasted_iota(I32, (LANES, LANES), 0)
    u_c = lax.broadcasted_iota(I32, (LANES, LANES), 1)
    strict_upper = jnp.where(u_r < u_c, 1.0, 0.0).astype(BF16)
    bstart = jnp.dot(jnp.broadcast_to(nblk, (8, LANES)).astype(BF16), strict_upper,
                     preferred_element_type=F32)[0:1, :]
    bend = bstart + nblk

    def pass2(ci, _):
        r0 = pl.multiple_of(ci * RCH, RCH)
        mi = mi_ref[pl.ds(r0, RCH), :].astype(F32)
        cols = [jnp.sum(jnp.where(lane_i == n, mi, 0.0), axis=-1, keepdims=True) for n in range(4)]
        d1 = pick(bstart, cols[0]) * float(EXPERT_BLOCK) + cols[2]
        d2 = pick(bstart, cols[1]) * float(EXPERT_BLOCK) + cols[3]
        d_t = pack([d1, d2]).T
        dt_ref[ci] = d_t[0:8, :].astype(I32)
        tok = (lax.broadcasted_iota(I32, (RCH, 1), 0) + r0).astype(F32)
        tok_hi = jnp.floor(tok * (1.0 / EXPERT_BLOCK))
        tok_lo = tok - tok_hi * float(EXPERT_BLOCK)
        for n, d in enumerate((d1, d2)):
            d_blk = jnp.floor(d * (1.0 / EXPERT_BLOCK))
            hit = lane == (d - d_blk * float(EXPERT_BLOCK))
            blk_row = jnp.floor(d_t[n:n + 1, :] * (1.0 / EXPERT_BLOCK))
            sel = jnp.where(inv_blk == blk_row, 1.0, 0.0).astype(BF16)
            inv_ref[0] += jnp.dot(sel, jnp.where(hit, tok_hi, 0.0).astype(BF16),
                                  preferred_element_type=F32)
            inv_ref[1] += jnp.dot(sel, jnp.where(hit, tok_lo, 0.0).astype(BF16),
                                  preferred_element_type=F32)
        return 0

    nb_rows = bm_ref.shape[0]
    inv_blk = lax.broadcasted_iota(I32, (nb_rows, RCH), 0).astype(F32)
    inv_ref[...] = jnp.zeros_like(inv_ref)
    lax.fori_loop(0, n_chunks, pass2, 0)
    bt_ref[...] = (inv_ref[0] * float(EXPERT_BLOCK) + inv_ref[1]).astype(I32)

    b_col = lax.broadcasted_iota(I32, (nb_rows, LANES), 0).astype(F32)
    b_lane = lax.broadcasted_iota(I32, (nb_rows, LANES), 1)
    done = jnp.where((bend <= b_col) & (b_lane < N_EXPERTS), 1.0, 0.0)
    blk_e = jnp.minimum(jnp.sum(done, axis=-1, keepdims=True), float(N_EXPERTS - 1))
    n_used = jnp.sum(jnp.where(b_lane[0:1, :] == N_EXPERTS - 1, bend, 0.0), axis=-1, keepdims=True)
    bm_ref[...] = jnp.where(b_lane == 0, blk_e, jnp.where(b_lane == 1, n_used, 0.0)).astype(I32)

    e_row = lax.broadcasted_iota(I32, (8, LANES), 0)
    em_ref[...] = jnp.where(e_row == 0, counts,
                            jnp.where(e_row == 1, bstart, jnp.where(e_row == 2, nblk, 0.0))).astype(I32)


def _route(logits, nb_rows):
    t = logits.shape[0]
    full = lambda shape: pl.BlockSpec(shape, lambda: (0,) * len(shape))
    return pl.pallas_call(
        _route_kernel,
        out_shape=(jax.ShapeDtypeStruct((t // RCH, 8, RCH), I32),
                   jax.ShapeDtypeStruct((t, LANES), F32),
                   jax.ShapeDtypeStruct((nb_rows, LANES), I32),
                   jax.ShapeDtypeStruct((8, LANES), I32),
                   jax.ShapeDtypeStruct((nb_rows, LANES), I32)),
        in_specs=[full((t, LANES))],
        out_specs=(full((t // RCH, 8, RCH)), full((t, LANES)), full((nb_rows, LANES)),
                   full((8, LANES)), full((nb_rows, LANES))),
        scratch_shapes=[pltpu.VMEM((t, LANES), I32), pltpu.VMEM((2, nb_rows, LANES), F32)],
        compiler_params=pltpu.CompilerParams(vmem_limit_bytes=VMEM_LIMIT),
        name="route",
    )(logits)


GATHER_GROUPS = 2


def _experts_kernel(blk_e_ref, nused_ref, nblk_ref, tok_ref, x_hbm, g2_ref, wg_hbm, wu_hbm, wd_hbm,
                    y_ref, xg, wg_f, wu_f, wd_f, wg_s, wu_s, wd_s, slot_ref, sem, gsem):
    b = pl.program_id(0)
    n_used = nused_ref[0]
    used = b < n_used
    e = blk_e_ref[b]
    new_expert = (b == 0) | (e != blk_e_ref[jnp.maximum(b - 1, 0)])
    xslot = b % 2

    def gather_rows(block, slot, lo, hi):
        for r in range(lo, hi):
            tok = tok_ref[block * EXPERT_BLOCK + r]
            pltpu.make_async_copy(x_hbm.at[pl.ds(tok, 1), :], xg.at[slot, pl.ds(r, 1), :],
                                  gsem.at[slot]).start()

    def gather_wait(slot):
        pltpu.make_async_copy(x_hbm.at[pl.ds(0, EXPERT_BLOCK), :], xg.at[slot], gsem.at[slot]).wait()

    @pl.when(b == 0)
    def _():
        gather_rows(0, 0, 0, EXPERT_BLOCK)

    def weight_copies(expert, slot):
        return (pltpu.make_async_copy(wg_hbm.at[expert], wg_f.at[slot], sem.at[slot]),
                pltpu.make_async_copy(wu_hbm.at[expert], wu_f.at[slot], sem.at[slot]),
                pltpu.make_async_copy(wd_hbm.at[expert], wd_f.at[slot], sem.at[slot]))

    @pl.when(b == 0)
    def _():
        slot_ref[0] = 0
        for cp in weight_copies(e, 0):
            cp.start()

    @pl.when(used & new_expert)
    def _():
        slot = slot_ref[0]
        b_next = b + nblk_ref[e]

        @pl.when(b_next < n_used)
        def _():
            for cp in weight_copies(blk_e_ref[jnp.minimum(b_next, n_used - 1)], 1 - slot):
                cp.start(priority=PREFETCH_PRIORITY)

        for cp in weight_copies(e, slot):
            cp.wait()
        wg_s[...] = wg_f[slot].astype(BF16)
        wu_s[...] = wu_f[slot].astype(BF16)
        wd_s[...] = wd_f[slot].astype(BF16)
        slot_ref[0] = 1 - slot

    @pl.when(used)
    def _():
        gather_wait(xslot)
        nxt = jnp.minimum(b + 1, n_used - 1)
        per_group = EXPERT_BLOCK // GATHER_GROUPS
        x = xg[xslot]
        ms = jnp.mean(x * x, axis=-1, keepdims=True)
        h = ((x * lax.rsqrt(ms + EPS)) * g2_ref[...]).astype(BF16)
        a = jnp.dot(h, wg_s[...], preferred_element_type=F32)
        gather_rows(nxt, 1 - xslot, 0, per_group)
        u = jnp.dot(h, wu_s[...], preferred_element_type=F32)
        gather_rows(nxt, 1 - xslot, per_group, EXPERT_BLOCK)
        mid = ((a * _sigmoid(a)) * u).astype(BF16)
        y_ref[...] = jnp.dot(mid, wd_s[...], preferred_element_type=F32)

    @pl.when(b == n_used - 1)
    def _():
        gather_wait(1 - xslot)

    @pl.when(jnp.logical_not(used))
    def _():
        y_ref[...] = jnp.zeros_like(y_ref)


def _experts(blk_e, n_used, nblk, row_tok, x1, g2, w_g, w_u, w_d):
    n_rows = row_tok.shape[0]
    nb = n_rows // EXPERT_BLOCK
    hbm = pl.BlockSpec(memory_space=pl.ANY)
    grid_spec = pltpu.PrefetchScalarGridSpec(
        num_scalar_prefetch=4,
        grid=(nb,),
        in_specs=[
            hbm,
            pl.BlockSpec((1, D_MODEL), lambda b, *_: (0, 0)),
            hbm, hbm, hbm,
        ],
        out_specs=pl.BlockSpec((EXPERT_BLOCK, D_MODEL), lambda b, *_: (b, 0)),
        scratch_shapes=[pltpu.VMEM((2, EXPERT_BLOCK, D_MODEL), F32),
                        pltpu.VMEM((2, D_MODEL, D_EXPERT), F32),
                        pltpu.VMEM((2, D_MODEL, D_EXPERT), F32),
                        pltpu.VMEM((2, D_EXPERT, D_MODEL), F32),
                        pltpu.VMEM((D_MODEL, D_EXPERT), BF16),
                        pltpu.VMEM((D_MODEL, D_EXPERT), BF16),
                        pltpu.VMEM((D_EXPERT, D_MODEL), BF16),
                        pltpu.SMEM((1,), I32),
                        pltpu.SemaphoreType.DMA((2,)),
                        pltpu.SemaphoreType.DMA((2,))],
    )
    return pl.pallas_call(
        _experts_kernel,
        out_shape=jax.ShapeDtypeStruct((n_rows, D_MODEL), F32),
        grid_spec=grid_spec,
        compiler_params=pltpu.CompilerParams(
            dimension_semantics=("arbitrary",), vmem_limit_bytes=VMEM_LIMIT),
        name="experts",
    )(blk_e, n_used, nblk, row_tok, x1, g2, w_g, w_u, w_d)


def _combine_kernel(d1_ref, d2_ref, x1_ref, w_ref, y_ref, o_ref, gbuf, sem):
    i = pl.program_id(0)
    n = pl.num_programs(0)
    tm = x1_ref.shape[0]

    def issue(tile, slot):
        def body(g, _):
            r0 = pl.multiple_of(g * SUBLANES, SUBLANES)
            for k in range(SUBLANES):
                tok = tile * tm + r0 + k
                pltpu.make_async_copy(
                    y_ref.at[pl.ds(d1_ref[tok], 1), :],
                    gbuf.at[slot, pl.ds(r0 + k, 1), :], sem.at[slot]).start(priority=0)
                pltpu.make_async_copy(
                    y_ref.at[pl.ds(d2_ref[tok], 1), :],
                    gbuf.at[slot, pl.ds(tm + r0 + k, 1), :], sem.at[slot]).start(priority=1)
            return 0

        lax.fori_loop(0, tm // SUBLANES, body, 0)

    slot = i % 2

    @pl.when(i == 0)
    def _():
        issue(0, 0)

    @pl.when(i + 1 < n)
    def _():
        issue(i + 1, 1 - slot)

    pltpu.make_async_copy(y_ref.at[pl.ds(0, 2 * tm), :], gbuf.at[slot], sem.at[slot]).wait()
    w = w_ref[...]
    g = gbuf[slot]
    o_ref[...] = x1_ref[...] + (w[:, 0:1] * g[0:tm, :] + w[:, 1:2] * g[tm:2 * tm, :])


def _combine(dest1, dest2, x1, w, yb):
    t = x1.shape[0]
    grid_spec = pltpu.PrefetchScalarGridSpec(
        num_scalar_prefetch=2,
        grid=(t // TM_CMB,),
        in_specs=[
            pl.BlockSpec((TM_CMB, D_MODEL), lambda i, a, b: (i, 0)),
            pl.BlockSpec((TM_CMB, LANES), lambda i, a, b: (i, 0)),
            pl.BlockSpec(memory_space=pl.ANY),
        ],
        out_specs=pl.BlockSpec((TM_CMB, D_MODEL), lambda i, a, b: (i, 0)),
        scratch_shapes=[pltpu.VMEM((2, 2 * TM_CMB, D_MODEL), F32),
                        pltpu.SemaphoreType.DMA((2,))],
    )
    return pl.pallas_call(
        _combine_kernel,
        out_shape=jax.ShapeDtypeStruct((t, D_MODEL), F32),
        grid_spec=grid_spec,
        compiler_params=pltpu.CompilerParams(
            dimension_semantics=("arbitrary",), vmem_limit_bytes=VMEM_LIMIT),
        name="combine",
    )(dest1, dest2, x1, w, yb)


def kernel(x, norm1_g, w_in, b_gate, b_forget, sgu_ln_g, sgu_ln_b, w_spatial, b_spatial, q_norm_g, k_norm_g, w_proj_sgu, w_proj_fox, w_out, norm2_g, w_router_group, b_router_group, w_router_expert, b_router_expert, w_expert_gate, w_expert_up, w_expert_down):
    batch, seq, d = x.shape
    t = batch * seq
    l = 0
    x2 = x.reshape(t, d)

    wt = jnp.swapaxes(w_in[l], 0, 1)
    off_f = N_MAIN_SEC * SEC
    w_f = jnp.pad(wt[off_f:off_f + FOX_HEADS].T, ((0, 0), (0, LANES - FOX_HEADS))).astype(BF16)
    b_f = jnp.pad(b_forget[l], (0, LANES - FOX_HEADS)).reshape(1, LANES)
    n_r = N_GROUPS + N_EXPERTS
    w_r = jnp.pad(jnp.concatenate([w_router_group[l], w_router_expert[l]], axis=1),
                  ((0, 0), (0, LANES - n_r))).astype(BF16)
    b_r = jnp.pad(jnp.concatenate([b_router_group[l], b_router_expert[l]]),
                  (0, LANES - n_r)).reshape(1, LANES)

    h, lf = _norm1(x2, norm1_g[l].reshape(1, d), w_f, b_f)
    a_sgu, q, kt, va, gates = _inproj(
        h, wt, b_gate[l].reshape(1, 2 * d),
        sgu_ln_g[l].reshape(1, SGU_WIDTH), sgu_ln_b[l].reshape(1, SGU_WIDTH),
        w_spatial[l], b_spatial[l].T, q_norm_g[l].reshape(1, HEAD_DIM),
        k_norm_g[l].reshape(1, HEAD_DIM))
    negc = _forget_cumsum(lf, batch, seq).reshape(batch, FOX_HEADS, 1, seq)
    a_fox = _attention(q, kt, va, negc, batch, seq)
    x1, logits = _mix(a_sgu, a_fox, gates, x2, w_proj_sgu[l].astype(BF16),
                      w_proj_fox[l].astype(BF16), w_out[l].astype(BF16),
                      norm2_g[l].reshape(1, d), w_r, b_r)

    n_assign = 2 * t
    n_rows = n_assign + N_EXPERTS * EXPERT_BLOCK
    nb = n_rows // EXPERT_BLOCK
    nb_rows = -(-nb // 8) * 8
    dest_t, meta_f, bmeta, emeta, row_tok = _route(logits, nb_rows)
    dest1, dest2 = dest_t[:, 0, :].reshape(t), dest_t[:, 1, :].reshape(t)
    n_used = bmeta[0:1, 1]
    yb = _experts(bmeta[:nb, 0], n_used, emeta[2, :N_EXPERTS], row_tok[:nb].reshape(n_rows), x1,
                  norm2_g[l].reshape(1, d), w_expert_gate[l], w_expert_up[l], w_expert_down[l])
    out = _combine(dest1, dest2, x1, meta_f, yb)
    return out.reshape(batch, seq, d)
```

```python
import functools

import jax
import jax.numpy as jnp
from jax import lax
from jax.experimental import pallas as pl
from jax.experimental.pallas import tpu as pltpu

F32 = jnp.float32
BF16 = jnp.bfloat16
I32 = jnp.int32

D_MODEL = 2048
CHUNK = 128
SGU_GROUPS = 8
SGU_WIDTH = 1024
FOX_HEADS = 8
HEAD_DIM = 128
FOX_WIDTH = 1024
N_GROUPS = 4
EXPERTS_PER_GROUP = 8
N_EXPERTS = 32
D_EXPERT = 512
EXPERT_BLOCK = 128
EPS = 1e-6

LANES = 128
SUBLANES = 8
PREFETCH_PRIORITY = 1
VMEM_LIMIT = 56 * 1024 * 1024

SEC = 1024
N_MAIN_SEC = 5
N_GATE_SEC = 4
TM_NORM = 1024
TM_IN = 512
TQ = 256
HEADS_PER_STEP = 4
TM_MIX = 256
RCH = 512
TM_CMB = 512
LOG2E = 1.4426950408889634
Q_SCALE = HEAD_DIM ** -0.5 * LOG2E


def _sigmoid(x):
    return 1.0 / (1.0 + jnp.exp(-x))


def _log_sigmoid(x):
    return jnp.minimum(x, 0.0) - jnp.log1p(jnp.exp(-jnp.abs(x)))


def _norm1_kernel(x_ref, g1_ref, wf_ref, bf_ref, h_ref, lf_ref):
    x = x_ref[...]
    ms = jnp.mean(x * x, axis=-1, keepdims=True)
    hb = ((x * lax.rsqrt(ms + EPS)) * g1_ref[...]).astype(BF16)
    h_ref[...] = hb
    f = jnp.dot(hb, wf_ref[...], preferred_element_type=F32) + bf_ref[...]
    lf_ref[...] = _log_sigmoid(f)


def _norm1(x2, g1, w_f, b_f):
    t = x2.shape[0]
    row = lambda i: (i, 0)
    const = lambda i: (0, 0)
    return pl.pallas_call(
        _norm1_kernel,
        out_shape=(jax.ShapeDtypeStruct((t, D_MODEL), BF16),
                   jax.ShapeDtypeStruct((t, LANES), F32)),
        grid=(t // TM_NORM,),
        in_specs=[pl.BlockSpec((TM_NORM, D_MODEL), row),
                  pl.BlockSpec((1, D_MODEL), const),
                  pl.BlockSpec((D_MODEL, LANES), const),
                  pl.BlockSpec((1, LANES), const)],
        out_specs=(pl.BlockSpec((TM_NORM, D_MODEL), row), pl.BlockSpec((TM_NORM, LANES), row)),
        compiler_params=pltpu.CompilerParams(
            dimension_semantics=("arbitrary",), vmem_limit_bytes=VMEM_LIMIT),
        name="norm1",
    )(x2, g1, w_f, b_f)


N_SECTIONS = 1 + 3 + N_GATE_SEC
N_CHUNKS = N_MAIN_SEC + N_GATE_SEC
GATE_SECTION0 = 4
WCONV_ROWS = 256


def _inproj_kernel(h_ref, wt_hbm, bg_ref, lng_ref, lnb_ref, wsp_ref, bsp_ref, qg_ref, kg_ref,
                   asgu_ref, q_ref, kt_ref, va_ref, gates_ref,
                   stage, wb, sem):
    s = pl.program_id(0)
    i = pl.program_id(1)
    tm = h_ref.shape[0]

    def chunk_copy(c, slot):
        r0 = pl.multiple_of(c * SEC + jnp.where(c >= N_MAIN_SEC, FOX_HEADS, 0), 8)
        return pltpu.make_async_copy(wt_hbm.at[pl.ds(r0, SEC), :], stage.at[slot], sem.at[slot])

    def convert(slot, k):
        for p in range(SEC // WCONV_ROWS):
            rs = slice(p * WCONV_ROWS, (p + 1) * WCONV_ROWS)
            wb[k, :, rs] = stage[slot, rs, :].T.astype(BF16)

    @pl.when((s == 0) & (i == 0))
    def _():
        chunk_copy(0, 0).start()
        chunk_copy(1, 1).start()
        chunk_copy(0, 0).wait()
        convert(0, 0)
        chunk_copy(2, 0).start(priority=PREFETCH_PRIORITY)
        chunk_copy(1, 1).wait()
        convert(1, 1)

    @pl.when((s > 0) & (i == 0))
    def _():
        c = s + 1
        slot = c % 2
        chunk_copy(c, slot).wait()

        @pl.when(c + 1 < N_CHUNKS)
        def _():
            chunk_copy(c + 1, 1 - slot).start(priority=PREFETCH_PRIORITY)

        convert(slot, 0)

    def section(k=0):
        return jnp.dot(h_ref[...], wb[k], preferred_element_type=F32)

    @pl.when(s == 0)
    def _():
        u = jax.nn.gelu(section(0))
        v = jax.nn.gelu(section(1))
        mu = jnp.mean(v, axis=-1, keepdims=True)
        vc = v - mu
        var = jnp.mean(vc * vc, axis=-1, keepdims=True)
        vn = ((vc * lax.rsqrt(var + EPS)) * lng_ref[...] + lnb_ref[...]).astype(BF16)
        row = lax.broadcasted_iota(I32, (CHUNK, CHUNK), 0)
        col = lax.broadcasted_iota(I32, (CHUNK, CHUNK), 1)
        causal = row >= col
        for g in range(SGU_GROUPS):
            wg = jnp.where(causal, wsp_ref[g], 0.0).astype(BF16)
            bcol = bsp_ref[:, g:g + 1]
            gs = slice(g * LANES, (g + 1) * LANES)
            n_ch = tm // CHUNK
            rhs = jnp.concatenate([vn[c * CHUNK:(c + 1) * CHUNK, gs] for c in range(n_ch)], axis=1)
            sg = jnp.dot(wg, rhs, preferred_element_type=F32) + bcol
            for c in range(n_ch):
                rs = slice(c * CHUNK, (c + 1) * CHUNK)
                asgu_ref[rs, gs] = (u[rs, gs] * sg[:, c * LANES:(c + 1) * LANES]).astype(BF16)

    def _head_norm(z, h, gain_ref):
        zh = z[:, h * HEAD_DIM:(h + 1) * HEAD_DIM]
        ms = jnp.mean(zh * zh, axis=-1, keepdims=True)
        return (zh * lax.rsqrt(ms + EPS)) * gain_ref[...]

    @pl.when(s == 1)
    def _():
        z = section()
        for h in range(FOX_HEADS):
            q_ref[:, h * HEAD_DIM:(h + 1) * HEAD_DIM] = (
                _head_norm(z, h, qg_ref) * Q_SCALE).astype(BF16)

    @pl.when(s == 2)
    def _():
        z = section()
        for h in range(FOX_HEADS):
            kt_ref[h * HEAD_DIM:(h + 1) * HEAD_DIM, :] = _head_norm(z, h, kg_ref).T.astype(BF16)

    @pl.when(s == 3)
    def _():
        va_ref[...] = section().astype(BF16)

    @pl.when(s >= GATE_SECTION0)
    def _():
        gates_ref[...] = (0.5 * jnp.tanh(0.5 * (section() + bg_ref[...])) + 0.5).astype(BF16)


def _inproj(h, wt, b_gate, ln_g, ln_b, w_sp, b_sp_t, q_g, k_g):
    t = h.shape[0]
    n_i = t // TM_IN

    def active(sec):
        return lambda s, i: jnp.where(s < sec, 0, jnp.where(s > sec, n_i - 1, i))

    def gate_blk(s, i):
        g = jnp.clip(s - GATE_SECTION0, 0, N_GATE_SEC - 1)
        return (jnp.where(s < GATE_SECTION0, 0, i), g)

    const2 = lambda s, i: (0, 0)
    out_shape = (
        jax.ShapeDtypeStruct((t, SGU_WIDTH), BF16),
        jax.ShapeDtypeStruct((t, FOX_WIDTH), BF16),
        jax.ShapeDtypeStruct((FOX_WIDTH, t), BF16),
        jax.ShapeDtypeStruct((t, FOX_WIDTH), BF16),
        jax.ShapeDtypeStruct((t, 2 * D_MODEL), BF16),
    )
    return pl.pallas_call(
        _inproj_kernel,
        out_shape=out_shape,
        grid=(N_SECTIONS, n_i),
        in_specs=[
            pl.BlockSpec((TM_IN, D_MODEL), lambda s, i: (i, 0)),
            pl.BlockSpec(memory_space=pl.ANY),
            pl.BlockSpec((1, SEC), lambda s, i: (0, jnp.clip(s - GATE_SECTION0, 0, N_GATE_SEC - 1))),
            pl.BlockSpec((1, SGU_WIDTH), const2),
            pl.BlockSpec((1, SGU_WIDTH), const2),
            pl.BlockSpec((SGU_GROUPS, CHUNK, CHUNK), lambda s, i: (0, 0, 0)),
            pl.BlockSpec((CHUNK, SGU_GROUPS), const2),
            pl.BlockSpec((1, HEAD_DIM), const2),
            pl.BlockSpec((1, HEAD_DIM), const2),
        ],
        out_specs=(
            pl.BlockSpec((TM_IN, SGU_WIDTH), lambda s, i: (active(0)(s, i), 0)),
            pl.BlockSpec((TM_IN, FOX_WIDTH), lambda s, i: (active(1)(s, i), 0)),
            pl.BlockSpec((FOX_WIDTH, TM_IN), lambda s, i: (0, active(2)(s, i))),
            pl.BlockSpec((TM_IN, FOX_WIDTH), lambda s, i: (active(3)(s, i), 0)),
            pl.BlockSpec((TM_IN, SEC), gate_blk),
        ),
        scratch_shapes=[pltpu.VMEM((2, SEC, D_MODEL), F32),
                        pltpu.VMEM((2, D_MODEL, SEC), BF16),
                        pltpu.SemaphoreType.DMA((2,))],
        compiler_params=pltpu.CompilerParams(
            dimension_semantics=("arbitrary", "arbitrary"), vmem_limit_bytes=VMEM_LIMIT),
        name="inproj",
    )(h, wt, b_gate, ln_g, ln_b, w_sp, b_sp_t, q_g, k_g)


def _split3(x):
    x0 = x.astype(BF16)
    r1 = x - x0.astype(F32)
    x1 = r1.astype(BF16)
    r2 = r1 - x1.astype(F32)
    return x0, x1, r2.astype(BF16)


def _cumsum_kernel(lf_ref, negc_ref, c_s):
    seq = lf_ref.shape[0]
    row = lax.broadcasted_iota(I32, (CHUNK, CHUNK), 0)
    col = lax.broadcasted_iota(I32, (CHUNK, CHUNK), 1)
    tri = jnp.where(row >= col, 1.0, 0.0).astype(BF16)
    carry = jnp.zeros((1, LANES), F32)
    for r in range(seq // CHUNK):
        rs = slice(r * CHUNK, (r + 1) * CHUNK)
        x0, x1, x2 = _split3(lf_ref[rs, :])
        cs = (jnp.dot(tri, x0, preferred_element_type=F32)
              + jnp.dot(tri, x1, preferred_element_type=F32)
              + jnp.dot(tri, x2, preferred_element_type=F32)) + carry
        carry = cs[CHUNK - 1:CHUNK, :]
        c_s[rs, :] = cs
    ct = c_s[...].T
    negc_ref[0] = -ct[0:FOX_HEADS, :]


def _forget_cumsum(lf, batch, seq):
    return pl.pallas_call(
        _cumsum_kernel,
        out_shape=jax.ShapeDtypeStruct((batch, FOX_HEADS, seq), F32),
        grid=(batch,),
        in_specs=[pl.BlockSpec((seq, LANES), lambda b: (b, 0))],
        out_specs=pl.BlockSpec((1, FOX_HEADS, seq), lambda b: (b, 0, 0)),
        scratch_shapes=[pltpu.VMEM((seq, LANES), F32)],
        compiler_params=pltpu.CompilerParams(
            dimension_semantics=("arbitrary",), vmem_limit_bytes=VMEM_LIMIT),
        name="forget_cumsum",
    )(lf)


def _attn_kernel(q_ref, kt_ref, v_ref, negc_ref, o_ref, kx, vx):
    seq = q_ref.shape[0]
    sub = lax.broadcasted_iota(I32, (HEAD_DIM, seq), 0)
    for hh in range(HEADS_PER_STEP):
        hs = slice(hh * HEAD_DIM, (hh + 1) * HEAD_DIM)
        c0, c1, c2 = (c.astype(F32) for c in _split3(negc_ref[0, hh] * LOG2E))
        kx[hh, 0:HEAD_DIM, :] = kt_ref[hs, :]
        kx[hh, HEAD_DIM:, :] = jnp.where(
            sub == 0, c0, jnp.where(sub == 1, c1, jnp.where(sub == 2, c2, 0.0))).astype(BF16)
        vx[hh, :, 0:HEAD_DIM] = v_ref[:, hs]
        vx[hh, :, HEAD_DIM:] = jnp.ones((seq, HEAD_DIM), BF16)
    lane = lax.broadcasted_iota(I32, (TQ, HEAD_DIM), 1)
    bias_cols = jnp.where(lane < 3, 1.0, 0.0).astype(BF16)
    row = lax.broadcasted_iota(I32, (TQ, TQ), 0)
    col = lax.broadcasted_iota(I32, (TQ, TQ), 1)
    causal = row >= col
    for qi in range(seq // TQ):
        for hh in range(HEADS_PER_STEP):
            hs = slice(hh * HEAD_DIM, (hh + 1) * HEAD_DIM)
            k0 = qi * TQ
            q = jnp.concatenate([q_ref[k0:k0 + TQ, hs], bias_cols], axis=1)
            s_d = jnp.dot(q, kx[hh, :, k0:k0 + TQ], preferred_element_type=F32)
            s_d = jnp.where(causal, s_d, -jnp.inf)
            m = jnp.max(s_d, axis=-1, keepdims=True)
            if qi > 0:
                s_o = jnp.dot(q, kx[hh, :, 0:k0], preferred_element_type=F32)
                m = jnp.maximum(m, jnp.max(s_o, axis=-1, keepdims=True))
            acc = jnp.dot(jnp.exp2(s_d - m).astype(BF16), vx[hh, k0:k0 + TQ, :],
                          preferred_element_type=F32)
            if qi > 0:
                acc = acc + jnp.dot(jnp.exp2(s_o - m).astype(BF16), vx[hh, 0:k0, :],
                                    preferred_element_type=F32)
            inv_l = 1.0 / acc[:, HEAD_DIM:HEAD_DIM + 1]
            o_ref[k0:k0 + TQ, hs] = (acc[:, 0:HEAD_DIM] * inv_l).astype(BF16)


def _attention(q, kt, va, negc, batch, seq):
    t = q.shape[0]
    width = HEADS_PER_STEP * HEAD_DIM
    blk = pl.BlockSpec((seq, width), lambda b, g: (b, g))
    return pl.pallas_call(
        _attn_kernel,
        out_shape=jax.ShapeDtypeStruct((t, FOX_WIDTH), BF16),
        grid=(batch, FOX_HEADS // HEADS_PER_STEP),
        in_specs=[blk,
                  pl.BlockSpec((width, seq), lambda b, g: (g, b)),
                  blk,
                  pl.BlockSpec((1, HEADS_PER_STEP, 1, seq), lambda b, g: (b, g, 0, 0))],
        out_specs=blk,
        scratch_shapes=[pltpu.VMEM((HEADS_PER_STEP, 2 * HEAD_DIM, seq), BF16),
                        pltpu.VMEM((HEADS_PER_STEP, seq, 2 * HEAD_DIM), BF16)],
        compiler_params=pltpu.CompilerParams(
            dimension_semantics=("arbitrary", "arbitrary"), vmem_limit_bytes=VMEM_LIMIT),
        name="fox_attention",
    )(q, kt, va, negc)


def _mix_kernel(as_ref, af_ref, g_ref, x_ref, wps_ref, wpf_ref, wo_ref, g2_ref, wr_ref, br_ref,
                x1_ref, lg_ref):
    ys = jnp.dot(as_ref[...], wps_ref[...], preferred_element_type=F32)
    yf = jnp.dot(af_ref[...], wpf_ref[...], preferred_element_type=F32)
    m = (g_ref[:, :D_MODEL].astype(F32) * ys + g_ref[:, D_MODEL:].astype(F32) * yf).astype(BF16)
    x1 = x_ref[...] + jnp.dot(m, wo_ref[...], preferred_element_type=F32)
    x1_ref[...] = x1
    ms = jnp.mean(x1 * x1, axis=-1, keepdims=True)
    h2 = ((x1 * lax.rsqrt(ms + EPS)) * g2_ref[...]).astype(BF16)
    lg_ref[...] = jnp.dot(h2, wr_ref[...], preferred_element_type=F32) + br_ref[...]


def _mix(a_sgu, a_fox, gates, x2, wps, wpf, wo, g2, w_r, b_r):
    t = x2.shape[0]
    row = lambda i: (i, 0)
    const = lambda i: (0, 0)
    resident = functools.partial(pl.BlockSpec, index_map=const, pipeline_mode=pl.Buffered(1))
    return pl.pallas_call(
        _mix_kernel,
        out_shape=(jax.ShapeDtypeStruct((t, D_MODEL), F32),
                   jax.ShapeDtypeStruct((t, LANES), F32)),
        grid=(t // TM_MIX,),
        in_specs=[
            pl.BlockSpec((TM_MIX, SGU_WIDTH), row),
            pl.BlockSpec((TM_MIX, FOX_WIDTH), row),
            pl.BlockSpec((TM_MIX, 2 * D_MODEL), row),
            pl.BlockSpec((TM_MIX, D_MODEL), row),
            resident((SGU_WIDTH, D_MODEL)),
            resident((FOX_WIDTH, D_MODEL)),
            resident((D_MODEL, D_MODEL)),
            pl.BlockSpec((1, D_MODEL), const),
            resident((D_MODEL, LANES)),
            pl.BlockSpec((1, LANES), const),
        ],
        out_specs=(pl.BlockSpec((TM_MIX, D_MODEL), row), pl.BlockSpec((TM_MIX, LANES), row)),
        compiler_params=pltpu.CompilerParams(
            dimension_semantics=("arbitrary",), vmem_limit_bytes=VMEM_LIMIT),
        name="mix",
    )(a_sgu, a_fox, gates, x2, wps, wpf, wo, g2, w_r, b_r)


def _route_kernel(lg_ref, dt_ref, mf_ref, bm_ref, em_ref, bt_ref, mi_ref, inv_ref):
    t = lg_ref.shape[0]
    n_chunks = t // RCH
    lane_i = lax.broadcasted_iota(I32, (RCH, LANES), 1)
    lane = lane_i.astype(F32)
    lane_grp = ((lane_i - N_GROUPS) >> 3).astype(F32)
    is_grp = lane_i < N_GROUPS
    is_exp = (lane_i >= N_GROUPS) & (lane_i < N_GROUPS + N_EXPERTS)
    r_i = lax.broadcasted_iota(I32, (RCH, RCH), 0)
    c_i = lax.broadcasted_iota(I32, (RCH, RCH), 1)
    strict_lower = jnp.where(r_i > c_i, 1.0, 0.0).astype(BF16)
    neg_inf = -jnp.inf

    def first_max(vals):
        vmax = jnp.max(vals, axis=-1, keepdims=True)
        idx = jnp.min(jnp.where(vals == vmax, lane, float(LANES)), axis=-1, keepdims=True)
        return vmax, idx

    def pick(table, idx):
        return jnp.sum(jnp.where(lane == idx, table, 0.0), axis=-1, keepdims=True)

    def pack(cols):
        out = jnp.zeros((RCH, LANES), F32)
        for n, c in enumerate(cols):
            out = jnp.where(lane_i == n, c, out)
        return out

    def pass1(ci, counts):
        r0 = pl.multiple_of(ci * RCH, RCH)
        lg = lg_ref[pl.ds(r0, RCH), :]
        gmax, grp = first_max(jnp.where(is_grp, lg, neg_inf))
        p_grp = 1.0 / jnp.sum(jnp.where(is_grp, jnp.exp(lg - gmax), 0.0), axis=-1, keepdims=True)
        el = jnp.where(is_exp & (lane_grp == grp), lg, neg_inf)
        v1, i1 = first_max(el)
        v2, i2 = first_max(jnp.where(lane == i1, neg_inf, el))
        e21 = jnp.exp(v2 - v1)
        w1 = p_grp / (1.0 + e21)
        w2 = p_grp * e21 / (1.0 + e21)
        e1 = i1 - float(N_GROUPS)
        e2 = i2 - float(N_GROUPS)
        hot = jnp.where((lane == e1) | (lane == e2), 1.0, 0.0)
        before = jnp.dot(strict_lower, hot.astype(BF16), preferred_element_type=F32) + counts
        mi_ref[pl.ds(r0, RCH), :] = pack([e1, e2, pick(before, e1), pick(before, e2)]).astype(I32)
        mf_ref[pl.ds(r0, RCH), :] = pack([w1, w2])
        return counts + jnp.sum(hot, axis=0, keepdims=True)

    counts = lax.fori_loop(0, n_chunks, pass1, jnp.zeros((1, LANES), F32))

    nblk = jnp.floor((counts + float(EXPERT_BLOCK - 1)) * (1.0 / EXPERT_BLOCK))
    u_r = lax.broadcasted_iota(I32, (LANES, LANES), 0)
    u_c = lax.broadcasted_iota(I32, (LANES, LANES), 1)
    strict_upper = jnp.where(u_r < u_c, 1.0, 0.0).astype(BF16)
    bstart = jnp.dot(jnp.broadcast_to(nblk, (8, LANES)).astype(BF16), strict_upper,
                     preferred_element_type=F32)[0:1, :]
    bend = bstart + nblk

    def pass2(ci, _):
        r0 = pl.multiple_of(ci * RCH, RCH)
        mi = mi_ref[pl.ds(r0, RCH), :].astype(F32)
        cols = [jnp.sum(jnp.where(lane_i == n, mi, 0.0), axis=-1, keepdims=True) for n in range(4)]
        d1 = pick(bstart, cols[0]) * float(EXPERT_BLOCK) + cols[2]
        d2 = pick(bstart, cols[1]) * float(EXPERT_BLOCK) + cols[3]
        d_t = pack([d1, d2]).T
        dt_ref[ci] = d_t[0:8, :].astype(I32)
        tok = (lax.broadcasted_iota(I32, (RCH, 1), 0) + r0).astype(F32)
        tok_hi = jnp.floor(tok * (1.0 / EXPERT_BLOCK))
        tok_lo = tok - tok_hi * float(EXPERT_BLOCK)
        for n, d in enumerate((d1, d2)):
            d_blk = jnp.floor(d * (1.0 / EXPERT_BLOCK))
            hit = lane == (d - d_blk * float(EXPERT_BLOCK))
            blk_row = jnp.floor(d_t[n:n + 1, :] * (1.0 / EXPERT_BLOCK))
            sel = jnp.where(inv_blk == blk_row, 1.0, 0.0).astype(BF16)
            inv_ref[0] += jnp.dot(sel, jnp.where(hit, tok_hi, 0.0).astype(BF16),
                                  preferred_element_type=F32)
            inv_ref[1] += jnp.dot(sel, jnp.where(hit, tok_lo, 0.0).astype(BF16),
                                  preferred_element_type=F32)
        return 0

    nb_rows = bm_ref.shape[0]
    inv_blk = lax.broadcasted_iota(I32, (nb_rows, RCH), 0).astype(F32)
    inv_ref[...] = jnp.zeros_like(inv_ref)
    lax.fori_loop(0, n_chunks, pass2, 0)
    bt_ref[...] = (inv_ref[0] * float(EXPERT_BLOCK) + inv_ref[1]).astype(I32)

    b_col = lax.broadcasted_iota(I32, (nb_rows, LANES), 0).astype(F32)
    b_lane = lax.broadcasted_iota(I32, (nb_rows, LANES), 1)
    done = jnp.where((bend <= b_col) & (b_lane < N_EXPERTS), 1.0, 0.0)
    blk_e = jnp.minimum(jnp.sum(done, axis=-1, keepdims=True), float(N_EXPERTS - 1))
    n_used = jnp.sum(jnp.where(b_lane[0:1, :] == N_EXPERTS - 1, bend, 0.0), axis=-1, keepdims=True)
    bm_ref[...] = jnp.where(b_lane == 0, blk_e, jnp.where(b_lane == 1, n_used, 0.0)).astype(I32)

    e_row = lax.broadcasted_iota(I32, (8, LANES), 0)
    em_ref[...] = jnp.where(e_row == 0, counts,
                            jnp.where(e_row == 1, bstart, jnp.where(e_row == 2, nblk, 0.0))).astype(I32)


def _route(logits, nb_rows):
    t = logits.shape[0]
    full = lambda shape: pl.BlockSpec(shape, lambda: (0,) * len(shape))
    return pl.pallas_call(
        _route_kernel,
        out_shape=(jax.ShapeDtypeStruct((t // RCH, 8, RCH), I32),
                   jax.ShapeDtypeStruct((t, LANES), F32),
                   jax.ShapeDtypeStruct((nb_rows, LANES), I32),
                   jax.ShapeDtypeStruct((8, LANES), I32),
                   jax.ShapeDtypeStruct((nb_rows, LANES), I32)),
        in_specs=[full((t, LANES))],
        out_specs=(full((t // RCH, 8, RCH)), full((t, LANES)), full((nb_rows, LANES)),
                   full((8, LANES)), full((nb_rows, LANES))),
        scratch_shapes=[pltpu.VMEM((t, LANES), I32), pltpu.VMEM((2, nb_rows, LANES), F32)],
        compiler_params=pltpu.CompilerParams(vmem_limit_bytes=VMEM_LIMIT),
        name="route",
    )(logits)


GATHER_GROUPS = 2
GATHER_AHEAD = 3


def _experts_kernel(blk_e_ref, nused_ref, nblk_ref, tok_ref, x_hbm, g2_ref, wg_hbm, wu_hbm, wd_hbm,
                    y_ref, xg, wg_f, wu_f, wd_f, wg_s, wu_s, wd_s, slot_ref, sem, gsem):
    b = pl.program_id(0)
    n_used = nused_ref[0]
    used = b < n_used
    e = blk_e_ref[b]
    new_expert = (b == 0) | (e != blk_e_ref[jnp.maximum(b - 1, 0)])
    n_slots = GATHER_AHEAD + 1
    xslot = b % n_slots

    def gather_rows(block, slot, lo, hi):
        row0 = jnp.minimum(block, n_used - 1) * EXPERT_BLOCK
        for r in range(lo, hi):
            pltpu.make_async_copy(x_hbm.at[pl.ds(tok_ref[row0 + r], 1), :],
                                  xg.at[slot, pl.ds(r, 1), :], gsem.at[slot]).start()

    def gather_wait(slot):
        pltpu.make_async_copy(x_hbm.at[pl.ds(0, EXPERT_BLOCK), :], xg.at[slot], gsem.at[slot]).wait()

    @pl.when(b == 0)
    def _():
        for k in range(GATHER_AHEAD):
            gather_rows(k, k, 0, EXPERT_BLOCK)

    def weight_copies(expert, slot):
        return (pltpu.make_async_copy(wg_hbm.at[expert], wg_f.at[slot], sem.at[slot]),
                pltpu.make_async_copy(wu_hbm.at[expert], wu_f.at[slot], sem.at[slot]),
                pltpu.make_async_copy(wd_hbm.at[expert], wd_f.at[slot], sem.at[slot]))

    @pl.when(b == 0)
    def _():
        slot_ref[0] = 0
        for cp in weight_copies(e, 0):
            cp.start()

    @pl.when(used & new_expert)
    def _():
        slot = slot_ref[0]
        b_next = b + nblk_ref[e]

        @pl.when(b_next < n_used)
        def _():
            for cp in weight_copies(blk_e_ref[jnp.minimum(b_next, n_used - 1)], 1 - slot):
                cp.start(priority=PREFETCH_PRIORITY)

        for cp in weight_copies(e, slot):
            cp.wait()
        wg_s[...] = wg_f[slot].astype(BF16)
        wu_s[...] = wu_f[slot].astype(BF16)
        wd_s[...] = wd_f[slot].astype(BF16)
        slot_ref[0] = 1 - slot

    @pl.when(used)
    def _():
        gather_wait(xslot)
        ahead = b + GATHER_AHEAD
        aslot = ahead % n_slots
        per_group = EXPERT_BLOCK // GATHER_GROUPS
        x = xg[xslot]
        ms = jnp.mean(x * x, axis=-1, keepdims=True)
        h = ((x * lax.rsqrt(ms + EPS)) * g2_ref[...]).astype(BF16)
        a = jnp.dot(h, wg_s[...], preferred_element_type=F32)
        gather_rows(ahead, aslot, 0, per_group)
        u = jnp.dot(h, wu_s[...], preferred_element_type=F32)
        gather_rows(ahead, aslot, per_group, EXPERT_BLOCK)
        mid = ((a * _sigmoid(a)) * u).astype(BF16)
        y_ref[...] = jnp.dot(mid, wd_s[...], preferred_element_type=F32)

    @pl.when(b == n_used - 1)
    def _():
        for k in range(1, GATHER_AHEAD + 1):
            gather_wait((b + k) % n_slots)

    @pl.when(jnp.logical_not(used))
    def _():
        y_ref[...] = jnp.zeros_like(y_ref)


def _experts(blk_e, n_used, nblk, row_tok, x1, g2, w_g, w_u, w_d):
    n_rows = row_tok.shape[0]
    nb = n_rows // EXPERT_BLOCK
    hbm = pl.BlockSpec(memory_space=pl.ANY)
    grid_spec = pltpu.PrefetchScalarGridSpec(
        num_scalar_prefetch=4,
        grid=(nb,),
        in_specs=[
            hbm,
            pl.BlockSpec((1, D_MODEL), lambda b, *_: (0, 0)),
            hbm, hbm, hbm,
        ],
        out_specs=pl.BlockSpec((EXPERT_BLOCK, D_MODEL), lambda b, *_: (b, 0)),
        scratch_shapes=[pltpu.VMEM((GATHER_AHEAD + 1, EXPERT_BLOCK, D_MODEL), F32),
                        pltpu.VMEM((2, D_MODEL, D_EXPERT), F32),
                        pltpu.VMEM((2, D_MODEL, D_EXPERT), F32),
                        pltpu.VMEM((2, D_EXPERT, D_MODEL), F32),
                        pltpu.VMEM((D_MODEL, D_EXPERT), BF16),
                        pltpu.VMEM((D_MODEL, D_EXPERT), BF16),
                        pltpu.VMEM((D_EXPERT, D_MODEL), BF16),
                        pltpu.SMEM((1,), I32),
                        pltpu.SemaphoreType.DMA((2,)),
                        pltpu.SemaphoreType.DMA((GATHER_AHEAD + 1,))],
    )
    return pl.pallas_call(
        _experts_kernel,
        out_shape=jax.ShapeDtypeStruct((n_rows, D_MODEL), F32),
        grid_spec=grid_spec,
        compiler_params=pltpu.CompilerParams(
            dimension_semantics=("arbitrary",), vmem_limit_bytes=VMEM_LIMIT),
        name="experts",
    )(blk_e, n_used, nblk, row_tok, x1, g2, w_g, w_u, w_d)


def _combine_kernel(d1_ref, d2_ref, x1_ref, w_ref, y_ref, o_ref, gbuf, sem):
    i = pl.program_id(0)
    n = pl.num_programs(0)
    tm = x1_ref.shape[0]

    def issue(tile, slot):
        def body(g, _):
            r0 = pl.multiple_of(g * SUBLANES, SUBLANES)
            for k in range(SUBLANES):
                tok = tile * tm + r0 + k
                pltpu.make_async_copy(
                    y_ref.at[pl.ds(d1_ref[tok], 1), :],
                    gbuf.at[slot, pl.ds(r0 + k, 1), :], sem.at[slot]).start(priority=0)
                pltpu.make_async_copy(
                    y_ref.at[pl.ds(d2_ref[tok], 1), :],
                    gbuf.at[slot, pl.ds(tm + r0 + k, 1), :], sem.at[slot]).start(priority=1)
            return 0

        lax.fori_loop(0, tm // SUBLANES, body, 0)

    slot = i % 2

    @pl.when(i == 0)
    def _():
        issue(0, 0)

    @pl.when(i + 1 < n)
    def _():
        issue(i + 1, 1 - slot)

    pltpu.make_async_copy(y_ref.at[pl.ds(0, 2 * tm), :], gbuf.at[slot], sem.at[slot]).wait()
    w = w_ref[...]
    g = gbuf[slot]
    o_ref[...] = x1_ref[...] + (w[:, 0:1] * g[0:tm, :] + w[:, 1:2] * g[tm:2 * tm, :])


def _combine(dest1, dest2, x1, w, yb):
    t = x1.shape[0]
    grid_spec = pltpu.PrefetchScalarGridSpec(
        num_scalar_prefetch=2,
        grid=(t // TM_CMB,),
        in_specs=[
            pl.BlockSpec((TM_CMB, D_MODEL), lambda i, a, b: (i, 0)),
            pl.BlockSpec((TM_CMB, LANES), lambda i, a, b: (i, 0)),
            pl.BlockSpec(memory_space=pl.ANY),
        ],
        out_specs=pl.BlockSpec((TM_CMB, D_MODEL), lambda i, a, b: (i, 0)),
        scratch_shapes=[pltpu.VMEM((2, 2 * TM_CMB, D_MODEL), F32),
                        pltpu.SemaphoreType.DMA((2,))],
    )
    return pl.pallas_call(
        _combine_kernel,
        out_shape=jax.ShapeDtypeStruct((t, D_MODEL), F32),
        grid_spec=grid_spec,
        compiler_params=pltpu.CompilerParams(
            dimension_semantics=("arbitrary",), vmem_limit_bytes=VMEM_LIMIT),
        name="combine",
    )(dest1, dest2, x1, w, yb)


def kernel(x, norm1_g, w_in, b_gate, b_forget, sgu_ln_g, sgu_ln_b, w_spatial, b_spatial, q_norm_g, k_norm_g, w_proj_sgu, w_proj_fox, w_out, norm2_g, w_router_group, b_router_group, w_router_expert, b_router_expert, w_expert_gate, w_expert_up, w_expert_down):
    batch, seq, d = x.shape
    t = batch * seq
    l = 0
    x2 = x.reshape(t, d)

    wt = jnp.swapaxes(w_in[l], 0, 1)
    off_f = N_MAIN_SEC * SEC
    w_f = jnp.pad(wt[off_f:off_f + FOX_HEADS].T, ((0, 0), (0, LANES - FOX_HEADS))).astype(BF16)
    b_f = jnp.pad(b_forget[l], (0, LANES - FOX_HEADS)).reshape(1, LANES)
    n_r = N_GROUPS + N_EXPERTS
    w_r = jnp.pad(jnp.concatenate([w_router_group[l], w_router_expert[l]], axis=1),
                  ((0, 0), (0, LANES - n_r))).astype(BF16)
    b_r = jnp.pad(jnp.concatenate([b_router_group[l], b_router_expert[l]]),
                  (0, LANES - n_r)).reshape(1, LANES)

    h, lf = _norm1(x2, norm1_g[l].reshape(1, d), w_f, b_f)
    a_sgu, q, kt, va, gates = _inproj(
        h, wt, b_gate[l].reshape(1, 2 * d),
        sgu_ln_g[l].reshape(1, SGU_WIDTH), sgu_ln_b[l].reshape(1, SGU_WIDTH),
        w_spatial[l], b_spatial[l].T, q_norm_g[l].reshape(1, HEAD_DIM),
        k_norm_g[l].reshape(1, HEAD_DIM))
    negc = _forget_cumsum(lf, batch, seq).reshape(batch, FOX_HEADS, 1, seq)
    a_fox = _attention(q, kt, va, negc, batch, seq)
    x1, logits = _mix(a_sgu, a_fox, gates, x2, w_proj_sgu[l].astype(BF16),
                      w_proj_fox[l].astype(BF16), w_out[l].astype(BF16),
                      norm2_g[l].reshape(1, d), w_r, b_r)

    n_assign = 2 * t
    n_rows = n_assign + N_EXPERTS * EXPERT_BLOCK
    nb = n_rows // EXPERT_BLOCK
    nb_rows = -(-nb // 8) * 8
    dest_t, meta_f, bmeta, emeta, row_tok = _route(logits, nb_rows)
    dest1, dest2 = dest_t[:, 0, :].reshape(t), dest_t[:, 1, :].reshape(t)
    n_used = bmeta[0:1, 1]
    yb = _experts(bmeta[:nb, 0], n_used, emeta[2, :N_EXPERTS], row_tok[:nb].reshape(n_rows), x1,
                  norm2_g[l].reshape(1, d), w_expert_gate[l], w_expert_up[l], w_expert_down[l])
    out = _combine(dest1, dest2, x1, meta_f, yb)
    return out.reshape(batch, seq, d)
```

```python
import functools

import jax
import jax.numpy as jnp
from jax import lax
from jax.experimental import pallas as pl
from jax.experimental.pallas import tpu as pltpu

F32 = jnp.float32
BF16 = jnp.bfloat16
I32 = jnp.int32

D_MODEL = 2048
CHUNK = 128
SGU_GROUPS = 8
SGU_WIDTH = 1024
FOX_HEADS = 8
HEAD_DIM = 128
FOX_WIDTH = 1024
N_GROUPS = 4
EXPERTS_PER_GROUP = 8
N_EXPERTS = 32
D_EXPERT = 512
EXPERT_BLOCK = 128
EPS = 1e-6

LANES = 128
SUBLANES = 8
PREFETCH_PRIORITY = 1
VMEM_LIMIT = 56 * 1024 * 1024

SEC = 1024
N_MAIN_SEC = 5
N_GATE_SEC = 4
TM_NORM = 1024
TM_IN = 512
TQ = 256
HEADS_PER_STEP = 4
TM_MIX = 256
RCH = 512
TM_CMB = 512
LOG2E = 1.4426950408889634
Q_SCALE = HEAD_DIM ** -0.5 * LOG2E


def _sigmoid(x):
    return 1.0 / (1.0 + jnp.exp(-x))


def _log_sigmoid(x):
    return jnp.minimum(x, 0.0) - jnp.log1p(jnp.exp(-jnp.abs(x)))


def _norm1_kernel(x_ref, g1_ref, wf_ref, bf_ref, h_ref, lf_ref):
    x = x_ref[...]
    ms = jnp.mean(x * x, axis=-1, keepdims=True)
    hb = ((x * lax.rsqrt(ms + EPS)) * g1_ref[...]).astype(BF16)
    h_ref[...] = hb
    f = jnp.dot(hb, wf_ref[...], preferred_element_type=F32) + bf_ref[...]
    lf_ref[...] = _log_sigmoid(f)


def _norm1(x2, g1, w_f, b_f):
    t = x2.shape[0]
    row = lambda i: (i, 0)
    const = lambda i: (0, 0)
    return pl.pallas_call(
        _norm1_kernel,
        out_shape=(jax.ShapeDtypeStruct((t, D_MODEL), BF16),
                   jax.ShapeDtypeStruct((t, LANES), F32)),
        grid=(t // TM_NORM,),
        in_specs=[pl.BlockSpec((TM_NORM, D_MODEL), row),
                  pl.BlockSpec((1, D_MODEL), const),
                  pl.BlockSpec((D_MODEL, LANES), const),
                  pl.BlockSpec((1, LANES), const)],
        out_specs=(pl.BlockSpec((TM_NORM, D_MODEL), row), pl.BlockSpec((TM_NORM, LANES), row)),
        compiler_params=pltpu.CompilerParams(
            dimension_semantics=("arbitrary",), vmem_limit_bytes=VMEM_LIMIT),
        name="norm1",
    )(x2, g1, w_f, b_f)


N_SECTIONS = 1 + 3 + N_GATE_SEC
N_CHUNKS = N_MAIN_SEC + N_GATE_SEC
GATE_SECTION0 = 4
WCONV_ROWS = 256


def _inproj_kernel(h_ref, wt_hbm, bg_ref, lng_ref, lnb_ref, wsp_ref, bsp_ref, qg_ref, kg_ref,
                   asgu_ref, q_ref, kt_ref, va_ref, gates_ref,
                   stage, wb, sem):
    s = pl.program_id(0)
    i = pl.program_id(1)
    tm = h_ref.shape[0]

    def chunk_copy(c, slot):
        r0 = pl.multiple_of(c * SEC + jnp.where(c >= N_MAIN_SEC, FOX_HEADS, 0), 8)
        return pltpu.make_async_copy(wt_hbm.at[pl.ds(r0, SEC), :], stage.at[slot], sem.at[slot])

    def convert(slot, k):
        for p in range(SEC // WCONV_ROWS):
            rs = slice(p * WCONV_ROWS, (p + 1) * WCONV_ROWS)
            wb[k, :, rs] = stage[slot, rs, :].T.astype(BF16)

    @pl.when((s == 0) & (i == 0))
    def _():
        chunk_copy(0, 0).start()
        chunk_copy(1, 1).start()
        chunk_copy(0, 0).wait()
        convert(0, 0)
        chunk_copy(2, 0).start(priority=PREFETCH_PRIORITY)
        chunk_copy(1, 1).wait()
        convert(1, 1)

    @pl.when((s > 0) & (i == 0))
    def _():
        c = s + 1
        slot = c % 2
        chunk_copy(c, slot).wait()

        @pl.when(c + 1 < N_CHUNKS)
        def _():
            chunk_copy(c + 1, 1 - slot).start(priority=PREFETCH_PRIORITY)

        convert(slot, 0)

    def section(k=0):
        return jnp.dot(h_ref[...], wb[k], preferred_element_type=F32)

    @pl.when(s == 0)
    def _():
        u = jax.nn.gelu(section(0))
        v = jax.nn.gelu(section(1))
        mu = jnp.mean(v, axis=-1, keepdims=True)
        vc = v - mu
        var = jnp.mean(vc * vc, axis=-1, keepdims=True)
        vn = ((vc * lax.rsqrt(var + EPS)) * lng_ref[...] + lnb_ref[...]).astype(BF16)
        row = lax.broadcasted_iota(I32, (CHUNK, CHUNK), 0)
        col = lax.broadcasted_iota(I32, (CHUNK, CHUNK), 1)
        causal = row >= col
        for g in range(SGU_GROUPS):
            wg = jnp.where(causal, wsp_ref[g], 0.0).astype(BF16)
            bcol = bsp_ref[:, g:g + 1]
            gs = slice(g * LANES, (g + 1) * LANES)
            n_ch = tm // CHUNK
            rhs = jnp.concatenate([vn[c * CHUNK:(c + 1) * CHUNK, gs] for c in range(n_ch)], axis=1)
            sg = jnp.dot(wg, rhs, preferred_element_type=F32) + bcol
            for c in range(n_ch):
                rs = slice(c * CHUNK, (c + 1) * CHUNK)
                asgu_ref[rs, gs] = (u[rs, gs] * sg[:, c * LANES:(c + 1) * LANES]).astype(BF16)

    def _head_norm(z, h, gain_ref):
        zh = z[:, h * HEAD_DIM:(h + 1) * HEAD_DIM]
        ms = jnp.mean(zh * zh, axis=-1, keepdims=True)
        return (zh * lax.rsqrt(ms + EPS)) * gain_ref[...]

    @pl.when(s == 1)
    def _():
        z = section()
        for h in range(FOX_HEADS):
            q_ref[:, h * HEAD_DIM:(h + 1) * HEAD_DIM] = (
                _head_norm(z, h, qg_ref) * Q_SCALE).astype(BF16)

    @pl.when(s == 2)
    def _():
        z = section()
        for h in range(FOX_HEADS):
            kt_ref[h * HEAD_DIM:(h + 1) * HEAD_DIM, :] = _head_norm(z, h, kg_ref).T.astype(BF16)

    @pl.when(s == 3)
    def _():
        va_ref[...] = section().astype(BF16)

    @pl.when(s >= GATE_SECTION0)
    def _():
        gates_ref[...] = (0.5 * jnp.tanh(0.5 * (section() + bg_ref[...])) + 0.5).astype(BF16)


def _inproj(h, wt, b_gate, ln_g, ln_b, w_sp, b_sp_t, q_g, k_g):
    t = h.shape[0]
    n_i = t // TM_IN

    def active(sec):
        return lambda s, i: jnp.where(s < sec, 0, jnp.where(s > sec, n_i - 1, i))

    def gate_blk(s, i):
        g = jnp.clip(s - GATE_SECTION0, 0, N_GATE_SEC - 1)
        return (jnp.where(s < GATE_SECTION0, 0, i), g)

    const2 = lambda s, i: (0, 0)
    out_shape = (
        jax.ShapeDtypeStruct((t, SGU_WIDTH), BF16),
        jax.ShapeDtypeStruct((t, FOX_WIDTH), BF16),
        jax.ShapeDtypeStruct((FOX_WIDTH, t), BF16),
        jax.ShapeDtypeStruct((t, FOX_WIDTH), BF16),
        jax.ShapeDtypeStruct((t, 2 * D_MODEL), BF16),
    )
    return pl.pallas_call(
        _inproj_kernel,
        out_shape=out_shape,
        grid=(N_SECTIONS, n_i),
        in_specs=[
            pl.BlockSpec((TM_IN, D_MODEL), lambda s, i: (i, 0)),
            pl.BlockSpec(memory_space=pl.ANY),
            pl.BlockSpec((1, SEC), lambda s, i: (0, jnp.clip(s - GATE_SECTION0, 0, N_GATE_SEC - 1))),
            pl.BlockSpec((1, SGU_WIDTH), const2),
            pl.BlockSpec((1, SGU_WIDTH), const2),
            pl.BlockSpec((SGU_GROUPS, CHUNK, CHUNK), lambda s, i: (0, 0, 0)),
            pl.BlockSpec((CHUNK, SGU_GROUPS), const2),
            pl.BlockSpec((1, HEAD_DIM), const2),
            pl.BlockSpec((1, HEAD_DIM), const2),
        ],
        out_specs=(
            pl.BlockSpec((TM_IN, SGU_WIDTH), lambda s, i: (active(0)(s, i), 0)),
            pl.BlockSpec((TM_IN, FOX_WIDTH), lambda s, i: (active(1)(s, i), 0)),
            pl.BlockSpec((FOX_WIDTH, TM_IN), lambda s, i: (0, active(2)(s, i))),
            pl.BlockSpec((TM_IN, FOX_WIDTH), lambda s, i: (active(3)(s, i), 0)),
            pl.BlockSpec((TM_IN, SEC), gate_blk),
        ),
        scratch_shapes=[pltpu.VMEM((2, SEC, D_MODEL), F32),
                        pltpu.VMEM((2, D_MODEL, SEC), BF16),
                        pltpu.SemaphoreType.DMA((2,))],
        compiler_params=pltpu.CompilerParams(
            dimension_semantics=("arbitrary", "arbitrary"), vmem_limit_bytes=VMEM_LIMIT),
        name="inproj",
    )(h, wt, b_gate, ln_g, ln_b, w_sp, b_sp_t, q_g, k_g)


def _split3(x):
    x0 = x.astype(BF16)
    r1 = x - x0.astype(F32)
    x1 = r1.astype(BF16)
    r2 = r1 - x1.astype(F32)
    return x0, x1, r2.astype(BF16)


def _cumsum_kernel(lf_ref, negc_ref, c_s):
    seq = lf_ref.shape[0]
    row = lax.broadcasted_iota(I32, (CHUNK, CHUNK), 0)
    col = lax.broadcasted_iota(I32, (CHUNK, CHUNK), 1)
    tri = jnp.where(row >= col, 1.0, 0.0).astype(BF16)
    carry = jnp.zeros((1, LANES), F32)
    for r in range(seq // CHUNK):
        rs = slice(r * CHUNK, (r + 1) * CHUNK)
        x0, x1, x2 = _split3(lf_ref[rs, :])
        cs = (jnp.dot(tri, x0, preferred_element_type=F32)
              + jnp.dot(tri, x1, preferred_element_type=F32)
              + jnp.dot(tri, x2, preferred_element_type=F32)) + carry
        carry = cs[CHUNK - 1:CHUNK, :]
        c_s[rs, :] = cs
    ct = c_s[...].T
    negc_ref[0] = -ct[0:FOX_HEADS, :]


def _forget_cumsum(lf, batch, seq):
    return pl.pallas_call(
        _cumsum_kernel,
        out_shape=jax.ShapeDtypeStruct((batch, FOX_HEADS, seq), F32),
        grid=(batch,),
        in_specs=[pl.BlockSpec((seq, LANES), lambda b: (b, 0))],
        out_specs=pl.BlockSpec((1, FOX_HEADS, seq), lambda b: (b, 0, 0)),
        scratch_shapes=[pltpu.VMEM((seq, LANES), F32)],
        compiler_params=pltpu.CompilerParams(
            dimension_semantics=("arbitrary",), vmem_limit_bytes=VMEM_LIMIT),
        name="forget_cumsum",
    )(lf)


def _attn_kernel(q_ref, kt_ref, v_ref, negc_ref, o_ref, kx, vx):
    seq = q_ref.shape[0]
    sub = lax.broadcasted_iota(I32, (HEAD_DIM, seq), 0)
    for hh in range(HEADS_PER_STEP):
        hs = slice(hh * HEAD_DIM, (hh + 1) * HEAD_DIM)
        c0, c1, c2 = (c.astype(F32) for c in _split3(negc_ref[0, hh] * LOG2E))
        kx[hh, 0:HEAD_DIM, :] = kt_ref[hs, :]
        kx[hh, HEAD_DIM:, :] = jnp.where(
            sub == 0, c0, jnp.where(sub == 1, c1, jnp.where(sub == 2, c2, 0.0))).astype(BF16)
        vx[hh, :, 0:HEAD_DIM] = v_ref[:, hs]
        vx[hh, :, HEAD_DIM:] = jnp.ones((seq, HEAD_DIM), BF16)
    lane = lax.broadcasted_iota(I32, (TQ, HEAD_DIM), 1)
    bias_cols = jnp.where(lane < 3, 1.0, 0.0).astype(BF16)
    row = lax.broadcasted_iota(I32, (TQ, TQ), 0)
    col = lax.broadcasted_iota(I32, (TQ, TQ), 1)
    causal = row >= col
    for qi in range(seq // TQ):
        for hh in range(HEADS_PER_STEP):
            hs = slice(hh * HEAD_DIM, (hh + 1) * HEAD_DIM)
            k0 = qi * TQ
            q = jnp.concatenate([q_ref[k0:k0 + TQ, hs], bias_cols], axis=1)
            s_d = jnp.dot(q, kx[hh, :, k0:k0 + TQ], preferred_element_type=F32)
            s_d = jnp.where(causal, s_d, -jnp.inf)
            m = jnp.max(s_d, axis=-1, keepdims=True)
            if qi > 0:
                s_o = jnp.dot(q, kx[hh, :, 0:k0], preferred_element_type=F32)
                m = jnp.maximum(m, jnp.max(s_o, axis=-1, keepdims=True))
            acc = jnp.dot(jnp.exp2(s_d - m).astype(BF16), vx[hh, k0:k0 + TQ, :],
                          preferred_element_type=F32)
            if qi > 0:
                acc = acc + jnp.dot(jnp.exp2(s_o - m).astype(BF16), vx[hh, 0:k0, :],
                                    preferred_element_type=F32)
            inv_l = 1.0 / acc[:, HEAD_DIM:HEAD_DIM + 1]
            o_ref[k0:k0 + TQ, hs] = (acc[:, 0:HEAD_DIM] * inv_l).astype(BF16)


def _attention(q, kt, va, negc, batch, seq):
    t = q.shape[0]
    width = HEADS_PER_STEP * HEAD_DIM
    blk = pl.BlockSpec((seq, width), lambda b, g: (b, g))
    return pl.pallas_call(
        _attn_kernel,
        out_shape=jax.ShapeDtypeStruct((t, FOX_WIDTH), BF16),
        grid=(batch, FOX_HEADS // HEADS_PER_STEP),
        in_specs=[blk,
                  pl.BlockSpec((width, seq), lambda b, g: (g, b)),
                  blk,
                  pl.BlockSpec((1, HEADS_PER_STEP, 1, seq), lambda b, g: (b, g, 0, 0))],
        out_specs=blk,
        scratch_shapes=[pltpu.VMEM((HEADS_PER_STEP, 2 * HEAD_DIM, seq), BF16),
                        pltpu.VMEM((HEADS_PER_STEP, seq, 2 * HEAD_DIM), BF16)],
        compiler_params=pltpu.CompilerParams(
            dimension_semantics=("arbitrary", "arbitrary"), vmem_limit_bytes=VMEM_LIMIT),
        name="fox_attention",
    )(q, kt, va, negc)


def _mix_kernel(as_ref, af_ref, g_ref, x_ref, wps_ref, wpf_ref, wo_ref, g2_ref, wr_ref, br_ref,
                x1_ref, lg_ref):
    ys = jnp.dot(as_ref[...], wps_ref[...], preferred_element_type=F32)
    yf = jnp.dot(af_ref[...], wpf_ref[...], preferred_element_type=F32)
    m = (g_ref[:, :D_MODEL].astype(F32) * ys + g_ref[:, D_MODEL:].astype(F32) * yf).astype(BF16)
    x1 = x_ref[...] + jnp.dot(m, wo_ref[...], preferred_element_type=F32)
    x1_ref[...] = x1
    ms = jnp.mean(x1 * x1, axis=-1, keepdims=True)
    h2 = ((x1 * lax.rsqrt(ms + EPS)) * g2_ref[...]).astype(BF16)
    lg_ref[...] = jnp.dot(h2, wr_ref[...], preferred_element_type=F32) + br_ref[...]


def _mix(a_sgu, a_fox, gates, x2, wps, wpf, wo, g2, w_r, b_r):
    t = x2.shape[0]
    row = lambda i: (i, 0)
    const = lambda i: (0, 0)
    resident = functools.partial(pl.BlockSpec, index_map=const, pipeline_mode=pl.Buffered(1))
    return pl.pallas_call(
        _mix_kernel,
        out_shape=(jax.ShapeDtypeStruct((t, D_MODEL), F32),
                   jax.ShapeDtypeStruct((t, LANES), F32)),
        grid=(t // TM_MIX,),
        in_specs=[
            pl.BlockSpec((TM_MIX, SGU_WIDTH), row),
            pl.BlockSpec((TM_MIX, FOX_WIDTH), row),
            pl.BlockSpec((TM_MIX, 2 * D_MODEL), row),
            pl.BlockSpec((TM_MIX, D_MODEL), row),
            resident((SGU_WIDTH, D_MODEL)),
            resident((FOX_WIDTH, D_MODEL)),
            resident((D_MODEL, D_MODEL)),
            pl.BlockSpec((1, D_MODEL), const),
            resident((D_MODEL, LANES)),
            pl.BlockSpec((1, LANES), const),
        ],
        out_specs=(pl.BlockSpec((TM_MIX, D_MODEL), row), pl.BlockSpec((TM_MIX, LANES), row)),
        compiler_params=pltpu.CompilerParams(
            dimension_semantics=("arbitrary",), vmem_limit_bytes=VMEM_LIMIT),
        name="mix",
    )(a_sgu, a_fox, gates, x2, wps, wpf, wo, g2, w_r, b_r)


def _route_kernel(lg_ref, dt_ref, mf_ref, bm_ref, em_ref, bt_ref, mi_ref, inv_ref):
    t = lg_ref.shape[0]
    n_chunks = t // RCH
    lane_i = lax.broadcasted_iota(I32, (RCH, LANES), 1)
    lane = lane_i.astype(F32)
    lane_grp = ((lane_i - N_GROUPS) >> 3).astype(F32)
    is_grp = lane_i < N_GROUPS
    is_exp = (lane_i >= N_GROUPS) & (lane_i < N_GROUPS + N_EXPERTS)
    r_i = lax.broadcasted_iota(I32, (RCH, RCH), 0)
    c_i = lax.broadcasted_iota(I32, (RCH, RCH), 1)
    strict_lower = jnp.where(r_i > c_i, 1.0, 0.0).astype(BF16)
    neg_inf = -jnp.inf

    def first_max(vals):
        vmax = jnp.max(vals, axis=-1, keepdims=True)
        idx = jnp.min(jnp.where(vals == vmax, lane, float(LANES)), axis=-1, keepdims=True)
        return vmax, idx

    def pick(table, idx):
        return jnp.sum(jnp.where(lane == idx, table, 0.0), axis=-1, keepdims=True)

    def pack(cols):
        out = jnp.zeros((RCH, LANES), F32)
        for n, c in enumerate(cols):
            out = jnp.where(lane_i == n, c, out)
        return out

    def pass1(ci, counts):
        r0 = pl.multiple_of(ci * RCH, RCH)
        lg = lg_ref[pl.ds(r0, RCH), :]
        gmax, grp = first_max(jnp.where(is_grp, lg, neg_inf))
        p_grp = 1.0 / jnp.sum(jnp.where(is_grp, jnp.exp(lg - gmax), 0.0), axis=-1, keepdims=True)
        el = jnp.where(is_exp & (lane_grp == grp), lg, neg_inf)
        v1, i1 = first_max(el)
        v2, i2 = first_max(jnp.where(lane == i1, neg_inf, el))
        e21 = jnp.exp(v2 - v1)
        w1 = p_grp / (1.0 + e21)
        w2 = p_grp * e21 / (1.0 + e21)
        e1 = i1 - float(N_GROUPS)
        e2 = i2 - float(N_GROUPS)
        hot = jnp.where((lane == e1) | (lane == e2), 1.0, 0.0)
        before = jnp.dot(strict_lower, hot.astype(BF16), preferred_element_type=F32) + counts
        mi_ref[pl.ds(r0, RCH), :] = pack([e1, e2, pick(before, e1), pick(before, e2)]).astype(I32)
        mf_ref[pl.ds(r0, RCH), :] = pack([w1, w2])
        return counts + jnp.sum(hot, axis=0, keepdims=True)

    counts = lax.fori_loop(0, n_chunks, pass1, jnp.zeros((1, LANES), F32))

    nblk = jnp.floor((counts + float(EXPERT_BLOCK - 1)) * (1.0 / EXPERT_BLOCK))
    u_r = lax.broadcasted_iota(I32, (LANES, LANES), 0)
    u_c = lax.broadcasted_iota(I32, (LANES, LANES), 1)
    strict_upper = jnp.where(u_r < u_c, 1.0, 0.0).astype(BF16)
    bstart = jnp.dot(jnp.broadcast_to(nblk, (8, LANES)).astype(BF16), strict_upper,
                     preferred_element_type=F32)[0:1, :]
    bend = bstart + nblk

    def pass2(ci, _):
        r0 = pl.multiple_of(ci * RCH, RCH)
        mi = mi_ref[pl.ds(r0, RCH), :].astype(F32)
        cols = [jnp.sum(jnp.where(lane_i == n, mi, 0.0), axis=-1, keepdims=True) for n in range(4)]
        d1 = pick(bstart, cols[0]) * float(EXPERT_BLOCK) + cols[2]
        d2 = pick(bstart, cols[1]) * float(EXPERT_BLOCK) + cols[3]
        d_t = pack([d1, d2]).T
        dt_ref[ci] = d_t[0:8, :].astype(I32)
        tok = (lax.broadcasted_iota(I32, (RCH, 1), 0) + r0).astype(F32)
        tok_hi = jnp.floor(tok * (1.0 / EXPERT_BLOCK))
        tok_lo = tok - tok_hi * float(EXPERT_BLOCK)
        for n, d in enumerate((d1, d2)):
            d_blk = jnp.floor(d * (1.0 / EXPERT_BLOCK))
            hit = lane == (d - d_blk * float(EXPERT_BLOCK))
            blk_row = jnp.floor(d_t[n:n + 1, :] * (1.0 / EXPERT_BLOCK))
            sel = jnp.where(inv_blk == blk_row, 1.0, 0.0).astype(BF16)
            inv_ref[0] += jnp.dot(sel, jnp.where(hit, tok_hi, 0.0).astype(BF16),
                                  preferred_element_type=F32)
            inv_ref[1] += jnp.dot(sel, jnp.where(hit, tok_lo, 0.0).astype(BF16),
                                  preferred_element_type=F32)
        return 0

    nb_rows = bm_ref.shape[0]
    inv_blk = lax.broadcasted_iota(I32, (nb_rows, RCH), 0).astype(F32)
    inv_ref[...] = jnp.zeros_like(inv_ref)
    lax.fori_loop(0, n_chunks, pass2, 0)
    bt_ref[...] = (inv_ref[0] * float(EXPERT_BLOCK) + inv_ref[1]).astype(I32)

    b_col = lax.broadcasted_iota(I32, (nb_rows, LANES), 0).astype(F32)
    b_lane = lax.broadcasted_iota(I32, (nb_rows, LANES), 1)
    done = jnp.where((bend <= b_col) & (b_lane < N_EXPERTS), 1.0, 0.0)
    blk_e = jnp.minimum(jnp.sum(done, axis=-1, keepdims=True), float(N_EXPERTS - 1))
    n_used = jnp.sum(jnp.where(b_lane[0:1, :] == N_EXPERTS - 1, bend, 0.0), axis=-1, keepdims=True)
    bm_ref[...] = jnp.where(b_lane == 0, blk_e, jnp.where(b_lane == 1, n_used, 0.0)).astype(I32)

    e_row = lax.broadcasted_iota(I32, (8, LANES), 0)
    em_ref[...] = jnp.where(e_row == 0, counts,
                            jnp.where(e_row == 1, bstart, jnp.where(e_row == 2, nblk, 0.0))).astype(I32)


def _route(logits, nb_rows):
    t = logits.shape[0]
    full = lambda shape: pl.BlockSpec(shape, lambda: (0,) * len(shape))
    return pl.pallas_call(
        _route_kernel,
        out_shape=(jax.ShapeDtypeStruct((t // RCH, 8, RCH), I32),
                   jax.ShapeDtypeStruct((t, LANES), F32),
                   jax.ShapeDtypeStruct((nb_rows, LANES), I32),
                   jax.ShapeDtypeStruct((8, LANES), I32),
                   jax.ShapeDtypeStruct((nb_rows, LANES), I32)),
        in_specs=[full((t, LANES))],
        out_specs=(full((t // RCH, 8, RCH)), full((t, LANES)), full((nb_rows, LANES)),
                   full((8, LANES)), full((nb_rows, LANES))),
        scratch_shapes=[pltpu.VMEM((t, LANES), I32), pltpu.VMEM((2, nb_rows, LANES), F32)],
        compiler_params=pltpu.CompilerParams(vmem_limit_bytes=VMEM_LIMIT),
        name="route",
    )(logits)


GATHER_GROUPS = 4
GATHER_AHEAD = 6


def _experts_kernel(blk_e_ref, nused_ref, nblk_ref, tok_ref, x_hbm, g2_ref, wg_hbm, wu_hbm, wd_hbm,
                    y_ref, xg, wg_f, wu_f, wd_f, wg_s, wu_s, wd_s, slot_ref, sem, gsem):
    b = pl.program_id(0)
    n_used = nused_ref[0]
    used = b < n_used
    e = blk_e_ref[b]
    new_expert = (b == 0) | (e != blk_e_ref[jnp.maximum(b - 1, 0)])
    n_slots = GATHER_AHEAD + 1
    xslot = b % n_slots

    def gather_rows(block, slot, lo, hi):
        row0 = jnp.minimum(block, n_used - 1) * EXPERT_BLOCK
        for r in range(lo, hi):
            pltpu.make_async_copy(x_hbm.at[pl.ds(tok_ref[row0 + r], 1), :],
                                  xg.at[slot, pl.ds(r, 1), :], gsem.at[slot]).start()

    def gather_wait(slot):
        pltpu.make_async_copy(x_hbm.at[pl.ds(0, EXPERT_BLOCK), :], xg.at[slot], gsem.at[slot]).wait()

    @pl.when(b == 0)
    def _():
        for k in range(GATHER_AHEAD):
            gather_rows(k, k, 0, EXPERT_BLOCK)

    def weight_copies(expert, slot):
        return (pltpu.make_async_copy(wg_hbm.at[expert], wg_f.at[slot], sem.at[slot]),
                pltpu.make_async_copy(wu_hbm.at[expert], wu_f.at[slot], sem.at[slot]),
                pltpu.make_async_copy(wd_hbm.at[expert], wd_f.at[slot], sem.at[slot]))

    @pl.when(b == 0)
    def _():
        slot_ref[0] = 0
        for cp in weight_copies(e, 0):
            cp.start()

    @pl.when(used & new_expert)
    def _():
        slot = slot_ref[0]
        b_next = b + nblk_ref[e]

        @pl.when(b_next < n_used)
        def _():
            for cp in weight_copies(blk_e_ref[jnp.minimum(b_next, n_used - 1)], 1 - slot):
                cp.start(priority=PREFETCH_PRIORITY)

        for cp in weight_copies(e, slot):
            cp.wait()
        wg_s[...] = wg_f[slot].astype(BF16)
        wu_s[...] = wu_f[slot].astype(BF16)
        wd_s[...] = wd_f[slot].astype(BF16)
        slot_ref[0] = 1 - slot

    @pl.when(used)
    def _():
        gather_wait(xslot)
        ahead = b + GATHER_AHEAD
        aslot = ahead % n_slots
        per_group = EXPERT_BLOCK // GATHER_GROUPS
        x = xg[xslot]
        ms = jnp.mean(x * x, axis=-1, keepdims=True)
        h = ((x * lax.rsqrt(ms + EPS)) * g2_ref[...]).astype(BF16)
        gather_rows(ahead, aslot, 0, per_group)
        a = jnp.dot(h, wg_s[...], preferred_element_type=F32)
        gather_rows(ahead, aslot, per_group, 2 * per_group)
        u = jnp.dot(h, wu_s[...], preferred_element_type=F32)
        gather_rows(ahead, aslot, 2 * per_group, 3 * per_group)
        mid = ((a * _sigmoid(a)) * u).astype(BF16)
        gather_rows(ahead, aslot, 3 * per_group, EXPERT_BLOCK)
        y_ref[...] = jnp.dot(mid, wd_s[...], preferred_element_type=F32)

    @pl.when(b == n_used - 1)
    def _():
        for k in range(1, GATHER_AHEAD + 1):
            gather_wait((b + k) % n_slots)

    @pl.when(jnp.logical_not(used))
    def _():
        y_ref[...] = jnp.zeros_like(y_ref)


def _experts(blk_e, n_used, nblk, row_tok, x1, g2, w_g, w_u, w_d):
    n_rows = row_tok.shape[0]
    nb = n_rows // EXPERT_BLOCK
    hbm = pl.BlockSpec(memory_space=pl.ANY)
    grid_spec = pltpu.PrefetchScalarGridSpec(
        num_scalar_prefetch=4,
        grid=(nb,),
        in_specs=[
            hbm,
            pl.BlockSpec((1, D_MODEL), lambda b, *_: (0, 0)),
            hbm, hbm, hbm,
        ],
        out_specs=pl.BlockSpec((EXPERT_BLOCK, D_MODEL), lambda b, *_: (b, 0)),
        scratch_shapes=[pltpu.VMEM((GATHER_AHEAD + 1, EXPERT_BLOCK, D_MODEL), F32),
                        pltpu.VMEM((2, D_MODEL, D_EXPERT), F32),
                        pltpu.VMEM((2, D_MODEL, D_EXPERT), F32),
                        pltpu.VMEM((2, D_EXPERT, D_MODEL), F32),
                        pltpu.VMEM((D_MODEL, D_EXPERT), BF16),
                        pltpu.VMEM((D_MODEL, D_EXPERT), BF16),
                        pltpu.VMEM((D_EXPERT, D_MODEL), BF16),
                        pltpu.SMEM((1,), I32),
                        pltpu.SemaphoreType.DMA((2,)),
                        pltpu.SemaphoreType.DMA((GATHER_AHEAD + 1,))],
    )
    return pl.pallas_call(
        _experts_kernel,
        out_shape=jax.ShapeDtypeStruct((n_rows, D_MODEL), F32),
        grid_spec=grid_spec,
        compiler_params=pltpu.CompilerParams(
            dimension_semantics=("arbitrary",), vmem_limit_bytes=VMEM_LIMIT),
        name="experts",
    )(blk_e, n_used, nblk, row_tok, x1, g2, w_g, w_u, w_d)


def _combine_kernel(d1_ref, d2_ref, x1_ref, w_ref, y_ref, o_ref, gbuf, sem):
    i = pl.program_id(0)
    n = pl.num_programs(0)
    tm = x1_ref.shape[0]

    def issue(tile, slot):
        def body(g, _):
            r0 = pl.multiple_of(g * SUBLANES, SUBLANES)
            for k in range(SUBLANES):
                tok = tile * tm + r0 + k
                pltpu.make_async_copy(
                    y_ref.at[pl.ds(d1_ref[tok], 1), :],
                    gbuf.at[slot, pl.ds(r0 + k, 1), :], sem.at[slot]).start(priority=0)
                pltpu.make_async_copy(
                    y_ref.at[pl.ds(d2_ref[tok], 1), :],
                    gbuf.at[slot, pl.ds(tm + r0 + k, 1), :], sem.at[slot]).start(priority=1)
            return 0

        lax.fori_loop(0, tm // SUBLANES, body, 0)

    slot = i % 2

    @pl.when(i == 0)
    def _():
        issue(0, 0)

    @pl.when(i + 1 < n)
    def _():
        issue(i + 1, 1 - slot)

    pltpu.make_async_copy(y_ref.at[pl.ds(0, 2 * tm), :], gbuf.at[slot], sem.at[slot]).wait()
    w = w_ref[...]
    g = gbuf[slot]
    o_ref[...] = x1_ref[...] + (w[:, 0:1] * g[0:tm, :] + w[:, 1:2] * g[tm:2 * tm, :])


def _combine(dest1, dest2, x1, w, yb):
    t = x1.shape[0]
    grid_spec = pltpu.PrefetchScalarGridSpec(
        num_scalar_prefetch=2,
        grid=(t // TM_CMB,),
        in_specs=[
            pl.BlockSpec((TM_CMB, D_MODEL), lambda i, a, b: (i, 0)),
            pl.BlockSpec((TM_CMB, LANES), lambda i, a, b: (i, 0)),
            pl.BlockSpec(memory_space=pl.ANY),
        ],
        out_specs=pl.BlockSpec((TM_CMB, D_MODEL), lambda i, a, b: (i, 0)),
        scratch_shapes=[pltpu.VMEM((2, 2 * TM_CMB, D_MODEL), F32),
                        pltpu.SemaphoreType.DMA((2,))],
    )
    return pl.pallas_call(
        _combine_kernel,
        out_shape=jax.ShapeDtypeStruct((t, D_MODEL), F32),
        grid_spec=grid_spec,
        compiler_params=pltpu.CompilerParams(
            dimension_semantics=("arbitrary",), vmem_limit_bytes=VMEM_LIMIT),
        name="combine",
    )(dest1, dest2, x1, w, yb)


def kernel(x, norm1_g, w_in, b_gate, b_forget, sgu_ln_g, sgu_ln_b, w_spatial, b_spatial, q_norm_g, k_norm_g, w_proj_sgu, w_proj_fox, w_out, norm2_g, w_router_group, b_router_group, w_router_expert, b_router_expert, w_expert_gate, w_expert_up, w_expert_down):
    batch, seq, d = x.shape
    t = batch * seq
    l = 0
    x2 = x.reshape(t, d)

    wt = jnp.swapaxes(w_in[l], 0, 1)
    off_f = N_MAIN_SEC * SEC
    w_f = jnp.pad(wt[off_f:off_f + FOX_HEADS].T, ((0, 0), (0, LANES - FOX_HEADS))).astype(BF16)
    b_f = jnp.pad(b_forget[l], (0, LANES - FOX_HEADS)).reshape(1, LANES)
    n_r = N_GROUPS + N_EXPERTS
    w_r = jnp.pad(jnp.concatenate([w_router_group[l], w_router_expert[l]], axis=1),
                  ((0, 0), (0, LANES - n_r))).astype(BF16)
    b_r = jnp.pad(jnp.concatenate([b_router_group[l], b_router_expert[l]]),
                  (0, LANES - n_r)).reshape(1, LANES)

    h, lf = _norm1(x2, norm1_g[l].reshape(1, d), w_f, b_f)
    a_sgu, q, kt, va, gates = _inproj(
        h, wt, b_gate[l].reshape(1, 2 * d),
        sgu_ln_g[l].reshape(1, SGU_WIDTH), sgu_ln_b[l].reshape(1, SGU_WIDTH),
        w_spatial[l], b_spatial[l].T, q_norm_g[l].reshape(1, HEAD_DIM),
        k_norm_g[l].reshape(1, HEAD_DIM))
    negc = _forget_cumsum(lf, batch, seq).reshape(batch, FOX_HEADS, 1, seq)
    a_fox = _attention(q, kt, va, negc, batch, seq)
    x1, logits = _mix(a_sgu, a_fox, gates, x2, w_proj_sgu[l].astype(BF16),
                      w_proj_fox[l].astype(BF16), w_out[l].astype(BF16),
                      norm2_g[l].reshape(1, d), w_r, b_r)

    n_assign = 2 * t
    n_rows = n_assign + N_EXPERTS * EXPERT_BLOCK
    nb = n_rows // EXPERT_BLOCK
    nb_rows = -(-nb // 8) * 8
    dest_t, meta_f, bmeta, emeta, row_tok = _route(logits, nb_rows)
    dest1, dest2 = dest_t[:, 0, :].reshape(t), dest_t[:, 1, :].reshape(t)
    n_used = bmeta[0:1, 1]
    yb = _experts(bmeta[:nb, 0], n_used, emeta[2, :N_EXPERTS], row_tok[:nb].reshape(n_rows), x1,
                  norm2_g[l].reshape(1, d), w_expert_gate[l], w_expert_up[l], w_expert_down[l])
    out = _combine(dest1, dest2, x1, meta_f, yb)
    return out.reshape(batch, seq, d)
```

```python
import functools

import jax
import jax.numpy as jnp
from jax import lax
from jax.experimental import pallas as pl
from jax.experimental.pallas import tpu as pltpu

F32 = jnp.float32
BF16 = jnp.bfloat16
I32 = jnp.int32

D_MODEL = 2048
CHUNK = 128
SGU_GROUPS = 8
SGU_WIDTH = 1024
FOX_HEADS = 8
HEAD_DIM = 128
FOX_WIDTH = 1024
N_GROUPS = 4
EXPERTS_PER_GROUP = 8
N_EXPERTS = 32
D_EXPERT = 512
EXPERT_BLOCK = 128
EPS = 1e-6

LANES = 128
SUBLANES = 8
PREFETCH_PRIORITY = 1
VMEM_LIMIT = 56 * 1024 * 1024

SEC = 1024
N_MAIN_SEC = 5
N_GATE_SEC = 4
TM_NORM = 1024
TM_IN = 512
TQ = 256
HEADS_PER_STEP = 4
TM_MIX = 256
RCH = 512
TM_CMB = 512
LOG2E = 1.4426950408889634
Q_SCALE = HEAD_DIM ** -0.5 * LOG2E


def _sigmoid(x):
    return 1.0 / (1.0 + jnp.exp(-x))


def _log_sigmoid(x):
    return jnp.minimum(x, 0.0) - jnp.log1p(jnp.exp(-jnp.abs(x)))


def _norm1_kernel(x_ref, g1_ref, wf_ref, bf_ref, h_ref, lf_ref):
    x = x_ref[...]
    ms = jnp.mean(x * x, axis=-1, keepdims=True)
    hb = ((x * lax.rsqrt(ms + EPS)) * g1_ref[...]).astype(BF16)
    h_ref[...] = hb
    f = jnp.dot(hb, wf_ref[...], preferred_element_type=F32) + bf_ref[...]
    lf_ref[...] = _log_sigmoid(f)


def _norm1(x2, g1, w_f, b_f):
    t = x2.shape[0]
    row = lambda i: (i, 0)
    const = lambda i: (0, 0)
    return pl.pallas_call(
        _norm1_kernel,
        out_shape=(jax.ShapeDtypeStruct((t, D_MODEL), BF16),
                   jax.ShapeDtypeStruct((t, LANES), F32)),
        grid=(t // TM_NORM,),
        in_specs=[pl.BlockSpec((TM_NORM, D_MODEL), row),
                  pl.BlockSpec((1, D_MODEL), const),
                  pl.BlockSpec((D_MODEL, LANES), const),
                  pl.BlockSpec((1, LANES), const)],
        out_specs=(pl.BlockSpec((TM_NORM, D_MODEL), row), pl.BlockSpec((TM_NORM, LANES), row)),
        compiler_params=pltpu.CompilerParams(
            dimension_semantics=("arbitrary",), vmem_limit_bytes=VMEM_LIMIT),
        name="norm1",
    )(x2, g1, w_f, b_f)


N_SECTIONS = 1 + 3 + N_GATE_SEC
N_CHUNKS = N_MAIN_SEC + N_GATE_SEC
GATE_SECTION0 = 4
WCONV_ROWS = 256


def _inproj_kernel(h_ref, wt_hbm, bg_ref, lng_ref, lnb_ref, wsp_ref, bsp_ref, qg_ref, kg_ref,
                   asgu_ref, q_ref, kt_ref, va_ref, gates_ref,
                   stage, wb, sem):
    s = pl.program_id(0)
    i = pl.program_id(1)
    tm = h_ref.shape[0]

    def chunk_copy(c, slot):
        r0 = pl.multiple_of(c * SEC + jnp.where(c >= N_MAIN_SEC, FOX_HEADS, 0), 8)
        return pltpu.make_async_copy(wt_hbm.at[pl.ds(r0, SEC), :], stage.at[slot], sem.at[slot])

    def convert(slot, k):
        for p in range(SEC // WCONV_ROWS):
            rs = slice(p * WCONV_ROWS, (p + 1) * WCONV_ROWS)
            wb[k, :, rs] = stage[slot, rs, :].T.astype(BF16)

    @pl.when((s == 0) & (i == 0))
    def _():
        chunk_copy(0, 0).start()
        chunk_copy(1, 1).start()
        chunk_copy(0, 0).wait()
        convert(0, 0)
        chunk_copy(2, 0).start(priority=PREFETCH_PRIORITY)
        chunk_copy(1, 1).wait()
        convert(1, 1)

    @pl.when((s > 0) & (i == 0))
    def _():
        c = s + 1
        slot = c % 2
        chunk_copy(c, slot).wait()

        @pl.when(c + 1 < N_CHUNKS)
        def _():
            chunk_copy(c + 1, 1 - slot).start(priority=PREFETCH_PRIORITY)

        convert(slot, 0)

    def section(k=0):
        return jnp.dot(h_ref[...], wb[k], preferred_element_type=F32)

    @pl.when(s == 0)
    def _():
        u = jax.nn.gelu(section(0))
        v = jax.nn.gelu(section(1))
        mu = jnp.mean(v, axis=-1, keepdims=True)
        vc = v - mu
        var = jnp.mean(vc * vc, axis=-1, keepdims=True)
        vn = ((vc * lax.rsqrt(var + EPS)) * lng_ref[...] + lnb_ref[...]).astype(BF16)
        row = lax.broadcasted_iota(I32, (CHUNK, CHUNK), 0)
        col = lax.broadcasted_iota(I32, (CHUNK, CHUNK), 1)
        causal = row >= col
        for g in range(SGU_GROUPS):
            wg = jnp.where(causal, wsp_ref[g], 0.0).astype(BF16)
            bcol = bsp_ref[:, g:g + 1]
            gs = slice(g * LANES, (g + 1) * LANES)
            n_ch = tm // CHUNK
            rhs = jnp.concatenate([vn[c * CHUNK:(c + 1) * CHUNK, gs] for c in range(n_ch)], axis=1)
            sg = jnp.dot(wg, rhs, preferred_element_type=F32) + bcol
            for c in range(n_ch):
                rs = slice(c * CHUNK, (c + 1) * CHUNK)
                asgu_ref[rs, gs] = (u[rs, gs] * sg[:, c * LANES:(c + 1) * LANES]).astype(BF16)

    def _head_norm(z, h, gain_ref):
        zh = z[:, h * HEAD_DIM:(h + 1) * HEAD_DIM]
        ms = jnp.mean(zh * zh, axis=-1, keepdims=True)
        return (zh * lax.rsqrt(ms + EPS)) * gain_ref[...]

    @pl.when(s == 1)
    def _():
        z = section()
        for h in range(FOX_HEADS):
            q_ref[:, h * HEAD_DIM:(h + 1) * HEAD_DIM] = (
                _head_norm(z, h, qg_ref) * Q_SCALE).astype(BF16)

    @pl.when(s == 2)
    def _():
        z = section()
        for h in range(FOX_HEADS):
            kt_ref[h * HEAD_DIM:(h + 1) * HEAD_DIM, :] = _head_norm(z, h, kg_ref).T.astype(BF16)

    @pl.when(s == 3)
    def _():
        va_ref[...] = section().astype(BF16)

    @pl.when(s >= GATE_SECTION0)
    def _():
        gates_ref[...] = (0.5 * jnp.tanh(0.5 * (section() + bg_ref[...])) + 0.5).astype(BF16)


def _inproj(h, wt, b_gate, ln_g, ln_b, w_sp, b_sp_t, q_g, k_g):
    t = h.shape[0]
    n_i = t // TM_IN

    def active(sec):
        return lambda s, i: jnp.where(s < sec, 0, jnp.where(s > sec, n_i - 1, i))

    def gate_blk(s, i):
        g = jnp.clip(s - GATE_SECTION0, 0, N_GATE_SEC - 1)
        return (jnp.where(s < GATE_SECTION0, 0, i), g)

    const2 = lambda s, i: (0, 0)
    out_shape = (
        jax.ShapeDtypeStruct((t, SGU_WIDTH), BF16),
        jax.ShapeDtypeStruct((t, FOX_WIDTH), BF16),
        jax.ShapeDtypeStruct((FOX_WIDTH, t), BF16),
        jax.ShapeDtypeStruct((t, FOX_WIDTH), BF16),
        jax.ShapeDtypeStruct((t, 2 * D_MODEL), BF16),
    )
    return pl.pallas_call(
        _inproj_kernel,
        out_shape=out_shape,
        grid=(N_SECTIONS, n_i),
        in_specs=[
            pl.BlockSpec((TM_IN, D_MODEL), lambda s, i: (i, 0)),
            pl.BlockSpec(memory_space=pl.ANY),
            pl.BlockSpec((1, SEC), lambda s, i: (0, jnp.clip(s - GATE_SECTION0, 0, N_GATE_SEC - 1))),
            pl.BlockSpec((1, SGU_WIDTH), const2),
            pl.BlockSpec((1, SGU_WIDTH), const2),
            pl.BlockSpec((SGU_GROUPS, CHUNK, CHUNK), lambda s, i: (0, 0, 0)),
            pl.BlockSpec((CHUNK, SGU_GROUPS), const2),
            pl.BlockSpec((1, HEAD_DIM), const2),
            pl.BlockSpec((1, HEAD_DIM), const2),
        ],
        out_specs=(
            pl.BlockSpec((TM_IN, SGU_WIDTH), lambda s, i: (active(0)(s, i), 0)),
            pl.BlockSpec((TM_IN, FOX_WIDTH), lambda s, i: (active(1)(s, i), 0)),
            pl.BlockSpec((FOX_WIDTH, TM_IN), lambda s, i: (0, active(2)(s, i))),
            pl.BlockSpec((TM_IN, FOX_WIDTH), lambda s, i: (active(3)(s, i), 0)),
            pl.BlockSpec((TM_IN, SEC), gate_blk),
        ),
        scratch_shapes=[pltpu.VMEM((2, SEC, D_MODEL), F32),
                        pltpu.VMEM((2, D_MODEL, SEC), BF16),
                        pltpu.SemaphoreType.DMA((2,))],
        compiler_params=pltpu.CompilerParams(
            dimension_semantics=("arbitrary", "arbitrary"), vmem_limit_bytes=VMEM_LIMIT),
        name="inproj",
    )(h, wt, b_gate, ln_g, ln_b, w_sp, b_sp_t, q_g, k_g)


def _split3(x):
    x0 = x.astype(BF16)
    r1 = x - x0.astype(F32)
    x1 = r1.astype(BF16)
    r2 = r1 - x1.astype(F32)
    return x0, x1, r2.astype(BF16)


def _cumsum_kernel(lf_ref, negc_ref, c_s):
    seq = lf_ref.shape[0]
    row = lax.broadcasted_iota(I32, (CHUNK, CHUNK), 0)
    col = lax.broadcasted_iota(I32, (CHUNK, CHUNK), 1)
    tri = jnp.where(row >= col, 1.0, 0.0).astype(BF16)
    carry = jnp.zeros((1, LANES), F32)
    for r in range(seq // CHUNK):
        rs = slice(r * CHUNK, (r + 1) * CHUNK)
        x0, x1, x2 = _split3(lf_ref[rs, :])
        cs = (jnp.dot(tri, x0, preferred_element_type=F32)
              + jnp.dot(tri, x1, preferred_element_type=F32)
              + jnp.dot(tri, x2, preferred_element_type=F32)) + carry
        carry = cs[CHUNK - 1:CHUNK, :]
        c_s[rs, :] = cs
    ct = c_s[...].T
    negc_ref[0] = -ct[0:FOX_HEADS, :]


def _forget_cumsum(lf, batch, seq):
    return pl.pallas_call(
        _cumsum_kernel,
        out_shape=jax.ShapeDtypeStruct((batch, FOX_HEADS, seq), F32),
        grid=(batch,),
        in_specs=[pl.BlockSpec((seq, LANES), lambda b: (b, 0))],
        out_specs=pl.BlockSpec((1, FOX_HEADS, seq), lambda b: (b, 0, 0)),
        scratch_shapes=[pltpu.VMEM((seq, LANES), F32)],
        compiler_params=pltpu.CompilerParams(
            dimension_semantics=("arbitrary",), vmem_limit_bytes=VMEM_LIMIT),
        name="forget_cumsum",
    )(lf)


def _attn_kernel(q_ref, kt_ref, v_ref, negc_ref, o_ref, kx, vx):
    seq = q_ref.shape[0]
    sub = lax.broadcasted_iota(I32, (HEAD_DIM, seq), 0)
    for hh in range(HEADS_PER_STEP):
        hs = slice(hh * HEAD_DIM, (hh + 1) * HEAD_DIM)
        c0, c1, c2 = (c.astype(F32) for c in _split3(negc_ref[0, hh] * LOG2E))
        kx[hh, 0:HEAD_DIM, :] = kt_ref[hs, :]
        kx[hh, HEAD_DIM:, :] = jnp.where(
            sub == 0, c0, jnp.where(sub == 1, c1, jnp.where(sub == 2, c2, 0.0))).astype(BF16)
        vx[hh, :, 0:HEAD_DIM] = v_ref[:, hs]
        vx[hh, :, HEAD_DIM:] = jnp.ones((seq, HEAD_DIM), BF16)
    lane = lax.broadcasted_iota(I32, (TQ, HEAD_DIM), 1)
    bias_cols = jnp.where(lane < 3, 1.0, 0.0).astype(BF16)
    row = lax.broadcasted_iota(I32, (TQ, TQ), 0)
    col = lax.broadcasted_iota(I32, (TQ, TQ), 1)
    causal = row >= col
    for qi in range(seq // TQ):
        for hh in range(HEADS_PER_STEP):
            hs = slice(hh * HEAD_DIM, (hh + 1) * HEAD_DIM)
            k0 = qi * TQ
            q = jnp.concatenate([q_ref[k0:k0 + TQ, hs], bias_cols], axis=1)
            s_d = jnp.dot(q, kx[hh, :, k0:k0 + TQ], preferred_element_type=F32)
            s_d = jnp.where(causal, s_d, -jnp.inf)
            m = jnp.max(s_d, axis=-1, keepdims=True)
            if qi > 0:
                s_o = jnp.dot(q, kx[hh, :, 0:k0], preferred_element_type=F32)
                m = jnp.maximum(m, jnp.max(s_o, axis=-1, keepdims=True))
            acc = jnp.dot(jnp.exp2(s_d - m).astype(BF16), vx[hh, k0:k0 + TQ, :],
                          preferred_element_type=F32)
            if qi > 0:
                acc = acc + jnp.dot(jnp.exp2(s_o - m).astype(BF16), vx[hh, 0:k0, :],
                                    preferred_element_type=F32)
            inv_l = 1.0 / acc[:, HEAD_DIM:HEAD_DIM + 1]
            o_ref[k0:k0 + TQ, hs] = (acc[:, 0:HEAD_DIM] * inv_l).astype(BF16)


def _attention(q, kt, va, negc, batch, seq):
    t = q.shape[0]
    width = HEADS_PER_STEP * HEAD_DIM
    blk = pl.BlockSpec((seq, width), lambda b, g: (b, g))
    return pl.pallas_call(
        _attn_kernel,
        out_shape=jax.ShapeDtypeStruct((t, FOX_WIDTH), BF16),
        grid=(batch, FOX_HEADS // HEADS_PER_STEP),
        in_specs=[blk,
                  pl.BlockSpec((width, seq), lambda b, g: (g, b)),
                  blk,
                  pl.BlockSpec((1, HEADS_PER_STEP, 1, seq), lambda b, g: (b, g, 0, 0))],
        out_specs=blk,
        scratch_shapes=[pltpu.VMEM((HEADS_PER_STEP, 2 * HEAD_DIM, seq), BF16),
                        pltpu.VMEM((HEADS_PER_STEP, seq, 2 * HEAD_DIM), BF16)],
        compiler_params=pltpu.CompilerParams(
            dimension_semantics=("arbitrary", "arbitrary"), vmem_limit_bytes=VMEM_LIMIT),
        name="fox_attention",
    )(q, kt, va, negc)


def _mix_kernel(as_ref, af_ref, g_ref, x_ref, wps_ref, wpf_ref, wo_ref, g2_ref, wr_ref, br_ref,
                x1_ref, lg_ref, x1_rows_ref):
    ys = jnp.dot(as_ref[...], wps_ref[...], preferred_element_type=F32)
    yf = jnp.dot(af_ref[...], wpf_ref[...], preferred_element_type=F32)
    m = (g_ref[:, :D_MODEL].astype(F32) * ys + g_ref[:, D_MODEL:].astype(F32) * yf).astype(BF16)
    x1 = x_ref[...] + jnp.dot(m, wo_ref[...], preferred_element_type=F32)
    x1_ref[...] = x1
    x1_rows_ref[:, 0, :] = x1
    ms = jnp.mean(x1 * x1, axis=-1, keepdims=True)
    h2 = ((x1 * lax.rsqrt(ms + EPS)) * g2_ref[...]).astype(BF16)
    lg_ref[...] = jnp.dot(h2, wr_ref[...], preferred_element_type=F32) + br_ref[...]


def _mix(a_sgu, a_fox, gates, x2, wps, wpf, wo, g2, w_r, b_r):
    t = x2.shape[0]
    row = lambda i: (i, 0)
    const = lambda i: (0, 0)
    resident = functools.partial(pl.BlockSpec, index_map=const, pipeline_mode=pl.Buffered(1))
    return pl.pallas_call(
        _mix_kernel,
        out_shape=(jax.ShapeDtypeStruct((t, D_MODEL), F32),
                   jax.ShapeDtypeStruct((t, LANES), F32),
                   jax.ShapeDtypeStruct((t, 1, D_MODEL), F32)),
        grid=(t // TM_MIX,),
        in_specs=[
            pl.BlockSpec((TM_MIX, SGU_WIDTH), row),
            pl.BlockSpec((TM_MIX, FOX_WIDTH), row),
            pl.BlockSpec((TM_MIX, 2 * D_MODEL), row),
            pl.BlockSpec((TM_MIX, D_MODEL), row),
            resident((SGU_WIDTH, D_MODEL)),
            resident((FOX_WIDTH, D_MODEL)),
            resident((D_MODEL, D_MODEL)),
            pl.BlockSpec((1, D_MODEL), const),
            resident((D_MODEL, LANES)),
            pl.BlockSpec((1, LANES), const),
        ],
        out_specs=(pl.BlockSpec((TM_MIX, D_MODEL), row), pl.BlockSpec((TM_MIX, LANES), row),
                   pl.BlockSpec((TM_MIX, 1, D_MODEL), lambda i: (i, 0, 0))),
        compiler_params=pltpu.CompilerParams(
            dimension_semantics=("arbitrary",), vmem_limit_bytes=VMEM_LIMIT),
        name="mix",
    )(a_sgu, a_fox, gates, x2, wps, wpf, wo, g2, w_r, b_r)


def _route_kernel(lg_ref, dt_ref, mf_ref, bm_ref, em_ref, bt_ref, mi_ref, inv_ref):
    t = lg_ref.shape[0]
    n_chunks = t // RCH
    lane_i = lax.broadcasted_iota(I32, (RCH, LANES), 1)
    lane = lane_i.astype(F32)
    lane_grp = ((lane_i - N_GROUPS) >> 3).astype(F32)
    is_grp = lane_i < N_GROUPS
    is_exp = (lane_i >= N_GROUPS) & (lane_i < N_GROUPS + N_EXPERTS)
    r_i = lax.broadcasted_iota(I32, (RCH, RCH), 0)
    c_i = lax.broadcasted_iota(I32, (RCH, RCH), 1)
    strict_lower = jnp.where(r_i > c_i, 1.0, 0.0).astype(BF16)
    neg_inf = -jnp.inf

    def first_max(vals):
        vmax = jnp.max(vals, axis=-1, keepdims=True)
        idx = jnp.min(jnp.where(vals == vmax, lane, float(LANES)), axis=-1, keepdims=True)
        return vmax, idx

    def pick(table, idx):
        return jnp.sum(jnp.where(lane == idx, table, 0.0), axis=-1, keepdims=True)

    def pack(cols):
        out = jnp.zeros((RCH, LANES), F32)
        for n, c in enumerate(cols):
            out = jnp.where(lane_i == n, c, out)
        return out

    def pass1(ci, counts):
        r0 = pl.multiple_of(ci * RCH, RCH)
        lg = lg_ref[pl.ds(r0, RCH), :]
        gmax, grp = first_max(jnp.where(is_grp, lg, neg_inf))
        p_grp = 1.0 / jnp.sum(jnp.where(is_grp, jnp.exp(lg - gmax), 0.0), axis=-1, keepdims=True)
        el = jnp.where(is_exp & (lane_grp == grp), lg, neg_inf)
        v1, i1 = first_max(el)
        v2, i2 = first_max(jnp.where(lane == i1, neg_inf, el))
        e21 = jnp.exp(v2 - v1)
        w1 = p_grp / (1.0 + e21)
        w2 = p_grp * e21 / (1.0 + e21)
        e1 = i1 - float(N_GROUPS)
        e2 = i2 - float(N_GROUPS)
        hot = jnp.where((lane == e1) | (lane == e2), 1.0, 0.0)
        before = jnp.dot(strict_lower, hot.astype(BF16), preferred_element_type=F32) + counts
        mi_ref[pl.ds(r0, RCH), :] = pack([e1, e2, pick(before, e1), pick(before, e2)]).astype(I32)
        mf_ref[pl.ds(r0, RCH), :] = pack([w1, w2])
        return counts + jnp.sum(hot, axis=0, keepdims=True)

    counts = lax.fori_loop(0, n_chunks, pass1, jnp.zeros((1, LANES), F32))

    nblk = jnp.floor((counts + float(EXPERT_BLOCK - 1)) * (1.0 / EXPERT_BLOCK))
    u_r = lax.broadcasted_iota(I32, (LANES, LANES), 0)
    u_c = lax.broadcasted_iota(I32, (LANES, LANES), 1)
    strict_upper = jnp.where(u_r < u_c, 1.0, 0.0).astype(BF16)
    bstart = jnp.dot(jnp.broadcast_to(nblk, (8, LANES)).astype(BF16), strict_upper,
                     preferred_element_type=F32)[0:1, :]
    bend = bstart + nblk

    def pass2(ci, _):
        r0 = pl.multiple_of(ci * RCH, RCH)
        mi = mi_ref[pl.ds(r0, RCH), :].astype(F32)
        cols = [jnp.sum(jnp.where(lane_i == n, mi, 0.0), axis=-1, keepdims=True) for n in range(4)]
        d1 = pick(bstart, cols[0]) * float(EXPERT_BLOCK) + cols[2]
        d2 = pick(bstart, cols[1]) * float(EXPERT_BLOCK) + cols[3]
        d_t = pack([d1, d2]).T
        dt_ref[ci] = d_t[0:8, :].astype(I32)
        tok = (lax.broadcasted_iota(I32, (RCH, 1), 0) + r0).astype(F32)
        tok_hi = jnp.floor(tok * (1.0 / EXPERT_BLOCK))
        tok_lo = tok - tok_hi * float(EXPERT_BLOCK)
        for n, d in enumerate((d1, d2)):
            d_blk = jnp.floor(d * (1.0 / EXPERT_BLOCK))
            hit = lane == (d - d_blk * float(EXPERT_BLOCK))
            blk_row = jnp.floor(d_t[n:n + 1, :] * (1.0 / EXPERT_BLOCK))
            sel = jnp.where(inv_blk == blk_row, 1.0, 0.0).astype(BF16)
            inv_ref[0] += jnp.dot(sel, jnp.where(hit, tok_hi, 0.0).astype(BF16),
                                  preferred_element_type=F32)
            inv_ref[1] += jnp.dot(sel, jnp.where(hit, tok_lo, 0.0).astype(BF16),
                                  preferred_element_type=F32)
        return 0

    nb_rows = bm_ref.shape[0]
    inv_blk = lax.broadcasted_iota(I32, (nb_rows, RCH), 0).astype(F32)
    inv_ref[...] = jnp.zeros_like(inv_ref)
    lax.fori_loop(0, n_chunks, pass2, 0)
    bt_ref[...] = (inv_ref[0] * float(EXPERT_BLOCK) + inv_ref[1]).astype(I32)

    b_col = lax.broadcasted_iota(I32, (nb_rows, LANES), 0).astype(F32)
    b_lane = lax.broadcasted_iota(I32, (nb_rows, LANES), 1)
    done = jnp.where((bend <= b_col) & (b_lane < N_EXPERTS), 1.0, 0.0)
    blk_e = jnp.minimum(jnp.sum(done, axis=-1, keepdims=True), float(N_EXPERTS - 1))
    n_used = jnp.sum(jnp.where(b_lane[0:1, :] == N_EXPERTS - 1, bend, 0.0), axis=-1, keepdims=True)
    bm_ref[...] = jnp.where(b_lane == 0, blk_e, jnp.where(b_lane == 1, n_used, 0.0)).astype(I32)

    e_row = lax.broadcasted_iota(I32, (8, LANES), 0)
    em_ref[...] = jnp.where(e_row == 0, counts,
                            jnp.where(e_row == 1, bstart, jnp.where(e_row == 2, nblk, 0.0))).astype(I32)


def _route(logits, nb_rows):
    t = logits.shape[0]
    full = lambda shape: pl.BlockSpec(shape, lambda: (0,) * len(shape))
    return pl.pallas_call(
        _route_kernel,
        out_shape=(jax.ShapeDtypeStruct((t // RCH, 8, RCH), I32),
                   jax.ShapeDtypeStruct((t, LANES), F32),
                   jax.ShapeDtypeStruct((nb_rows, LANES), I32),
                   jax.ShapeDtypeStruct((8, LANES), I32),
                   jax.ShapeDtypeStruct((nb_rows, LANES), I32)),
        in_specs=[full((t, LANES))],
        out_specs=(full((t // RCH, 8, RCH)), full((t, LANES)), full((nb_rows, LANES)),
                   full((8, LANES)), full((nb_rows, LANES))),
        scratch_shapes=[pltpu.VMEM((t, LANES), I32), pltpu.VMEM((2, nb_rows, LANES), F32)],
        compiler_params=pltpu.CompilerParams(vmem_limit_bytes=VMEM_LIMIT),
        name="route",
    )(logits)


GATHER_GROUPS = 4
GATHER_AHEAD = 6


def _experts_kernel(blk_e_ref, nused_ref, nblk_ref, tok_ref, x_hbm, g2_ref, wg_hbm, wu_hbm, wd_hbm,
                    y_ref, xg, wg_f, wu_f, wd_f, wg_s, wu_s, wd_s, slot_ref, sem, gsem):
    b = pl.program_id(0)
    n_used = nused_ref[0]
    used = b < n_used
    e = blk_e_ref[b]
    new_expert = (b == 0) | (e != blk_e_ref[jnp.maximum(b - 1, 0)])
    n_slots = GATHER_AHEAD + 1
    xslot = b % n_slots

    def gather_rows(block, slot, lo, hi):
        row0 = jnp.minimum(block, n_used - 1) * EXPERT_BLOCK
        for r in range(lo, hi):
            pltpu.make_async_copy(x_hbm.at[tok_ref[row0 + r]],
                                  xg.at[slot, pl.ds(r, 1), :], gsem.at[slot]).start()

    def gather_wait(slot):
        pltpu.make_async_copy(x_hbm.at[pl.ds(0, EXPERT_BLOCK), 0, :], xg.at[slot],
                              gsem.at[slot]).wait()

    @pl.when(b == 0)
    def _():
        for k in range(GATHER_AHEAD):
            gather_rows(k, k, 0, EXPERT_BLOCK)

    def weight_copies(expert, slot):
        return (pltpu.make_async_copy(wg_hbm.at[expert], wg_f.at[slot], sem.at[slot]),
                pltpu.make_async_copy(wu_hbm.at[expert], wu_f.at[slot], sem.at[slot]),
                pltpu.make_async_copy(wd_hbm.at[expert], wd_f.at[slot], sem.at[slot]))

    @pl.when(b == 0)
    def _():
        slot_ref[0] = 0
        for cp in weight_copies(e, 0):
            cp.start()

    @pl.when(used & new_expert)
    def _():
        slot = slot_ref[0]
        b_next = b + nblk_ref[e]

        @pl.when(b_next < n_used)
        def _():
            for cp in weight_copies(blk_e_ref[jnp.minimum(b_next, n_used - 1)], 1 - slot):
                cp.start(priority=PREFETCH_PRIORITY)

        for cp in weight_copies(e, slot):
            cp.wait()
        wg_s[...] = wg_f[slot].astype(BF16)
        wu_s[...] = wu_f[slot].astype(BF16)
        wd_s[...] = wd_f[slot].astype(BF16)
        slot_ref[0] = 1 - slot

    @pl.when(used)
    def _():
        gather_wait(xslot)
        ahead = b + GATHER_AHEAD
        aslot = ahead % n_slots
        per_group = EXPERT_BLOCK // GATHER_GROUPS
        x = xg[xslot]
        ms = jnp.mean(x * x, axis=-1, keepdims=True)
        h = ((x * lax.rsqrt(ms + EPS)) * g2_ref[...]).astype(BF16)
        gather_rows(ahead, aslot, 0, per_group)
        a = jnp.dot(h, wg_s[...], preferred_element_type=F32)
        gather_rows(ahead, aslot, per_group, 2 * per_group)
        u = jnp.dot(h, wu_s[...], preferred_element_type=F32)
        gather_rows(ahead, aslot, 2 * per_group, 3 * per_group)
        mid = ((a * _sigmoid(a)) * u).astype(BF16)
        gather_rows(ahead, aslot, 3 * per_group, EXPERT_BLOCK)
        y_ref[...] = jnp.dot(mid, wd_s[...], preferred_element_type=F32)

    @pl.when(b == n_used - 1)
    def _():
        for k in range(1, GATHER_AHEAD + 1):
            gather_wait((b + k) % n_slots)

    @pl.when(jnp.logical_not(used))
    def _():
        y_ref[...] = jnp.zeros_like(y_ref)


def _experts(blk_e, n_used, nblk, row_tok, x1, g2, w_g, w_u, w_d):
    n_rows = row_tok.shape[0]
    nb = n_rows // EXPERT_BLOCK
    hbm = pl.BlockSpec(memory_space=pl.ANY)
    grid_spec = pltpu.PrefetchScalarGridSpec(
        num_scalar_prefetch=4,
        grid=(nb,),
        in_specs=[
            hbm,
            pl.BlockSpec((1, D_MODEL), lambda b, *_: (0, 0)),
            hbm, hbm, hbm,
        ],
        out_specs=pl.BlockSpec((EXPERT_BLOCK, D_MODEL), lambda b, *_: (b, 0)),
        scratch_shapes=[pltpu.VMEM((GATHER_AHEAD + 1, EXPERT_BLOCK, D_MODEL), F32),
                        pltpu.VMEM((2, D_MODEL, D_EXPERT), F32),
                        pltpu.VMEM((2, D_MODEL, D_EXPERT), F32),
                        pltpu.VMEM((2, D_EXPERT, D_MODEL), F32),
                        pltpu.VMEM((D_MODEL, D_EXPERT), BF16),
                        pltpu.VMEM((D_MODEL, D_EXPERT), BF16),
                        pltpu.VMEM((D_EXPERT, D_MODEL), BF16),
                        pltpu.SMEM((1,), I32),
                        pltpu.SemaphoreType.DMA((2,)),
                        pltpu.SemaphoreType.DMA((GATHER_AHEAD + 1,))],
    )
    return pl.pallas_call(
        _experts_kernel,
        out_shape=jax.ShapeDtypeStruct((n_rows, D_MODEL), F32),
        grid_spec=grid_spec,
        compiler_params=pltpu.CompilerParams(
            dimension_semantics=("arbitrary",), vmem_limit_bytes=VMEM_LIMIT),
        name="experts",
    )(blk_e, n_used, nblk, row_tok, x1, g2, w_g, w_u, w_d)


def _combine_kernel(d1_ref, d2_ref, x1_ref, w_ref, y_ref, o_ref, gbuf, sem):
    i = pl.program_id(0)
    n = pl.num_programs(0)
    tm = x1_ref.shape[0]

    def issue(tile, slot):
        def body(g, _):
            r0 = pl.multiple_of(g * SUBLANES, SUBLANES)
            for k in range(SUBLANES):
                tok = tile * tm + r0 + k
                pltpu.make_async_copy(
                    y_ref.at[pl.ds(d1_ref[tok], 1), :],
                    gbuf.at[slot, pl.ds(r0 + k, 1), :], sem.at[slot]).start(priority=0)
                pltpu.make_async_copy(
                    y_ref.at[pl.ds(d2_ref[tok], 1), :],
                    gbuf.at[slot, pl.ds(tm + r0 + k, 1), :], sem.at[slot]).start(priority=1)
            return 0

        lax.fori_loop(0, tm // SUBLANES, body, 0)

    slot = i % 2

    @pl.when(i == 0)
    def _():
        issue(0, 0)

    @pl.when(i + 1 < n)
    def _():
        issue(i + 1, 1 - slot)

    pltpu.make_async_copy(y_ref.at[pl.ds(0, 2 * tm), :], gbuf.at[slot], sem.at[slot]).wait()
    w = w_ref[...]
    g = gbuf[slot]
    o_ref[...] = x1_ref[...] + (w[:, 0:1] * g[0:tm, :] + w[:, 1:2] * g[tm:2 * tm, :])


def _combine(dest1, dest2, x1, w, yb):
    t = x1.shape[0]
    grid_spec = pltpu.PrefetchScalarGridSpec(
        num_scalar_prefetch=2,
        grid=(t // TM_CMB,),
        in_specs=[
            pl.BlockSpec((TM_CMB, D_MODEL), lambda i, a, b: (i, 0)),
            pl.BlockSpec((TM_CMB, LANES), lambda i, a, b: (i, 0)),
            pl.BlockSpec(memory_space=pl.ANY),
        ],
        out_specs=pl.BlockSpec((TM_CMB, D_MODEL), lambda i, a, b: (i, 0)),
        scratch_shapes=[pltpu.VMEM((2, 2 * TM_CMB, D_MODEL), F32),
                        pltpu.SemaphoreType.DMA((2,))],
    )
    return pl.pallas_call(
        _combine_kernel,
        out_shape=jax.ShapeDtypeStruct((t, D_MODEL), F32),
        grid_spec=grid_spec,
        compiler_params=pltpu.CompilerParams(
            dimension_semantics=("arbitrary",), vmem_limit_bytes=VMEM_LIMIT),
        name="combine",
    )(dest1, dest2, x1, w, yb)


def kernel(x, norm1_g, w_in, b_gate, b_forget, sgu_ln_g, sgu_ln_b, w_spatial, b_spatial, q_norm_g, k_norm_g, w_proj_sgu, w_proj_fox, w_out, norm2_g, w_router_group, b_router_group, w_router_expert, b_router_expert, w_expert_gate, w_expert_up, w_expert_down):
    batch, seq, d = x.shape
    t = batch * seq
    l = 0
    x2 = x.reshape(t, d)

    wt = jnp.swapaxes(w_in[l], 0, 1)
    off_f = N_MAIN_SEC * SEC
    w_f = jnp.pad(wt[off_f:off_f + FOX_HEADS].T, ((0, 0), (0, LANES - FOX_HEADS))).astype(BF16)
    b_f = jnp.pad(b_forget[l], (0, LANES - FOX_HEADS)).reshape(1, LANES)
    n_r = N_GROUPS + N_EXPERTS
    w_r = jnp.pad(jnp.concatenate([w_router_group[l], w_router_expert[l]], axis=1),
                  ((0, 0), (0, LANES - n_r))).astype(BF16)
    b_r = jnp.pad(jnp.concatenate([b_router_group[l], b_router_expert[l]]),
                  (0, LANES - n_r)).reshape(1, LANES)

    h, lf = _norm1(x2, norm1_g[l].reshape(1, d), w_f, b_f)
    a_sgu, q, kt, va, gates = _inproj(
        h, wt, b_gate[l].reshape(1, 2 * d),
        sgu_ln_g[l].reshape(1, SGU_WIDTH), sgu_ln_b[l].reshape(1, SGU_WIDTH),
        w_spatial[l], b_spatial[l].T, q_norm_g[l].reshape(1, HEAD_DIM),
        k_norm_g[l].reshape(1, HEAD_DIM))
    negc = _forget_cumsum(lf, batch, seq).reshape(batch, FOX_HEADS, 1, seq)
    a_fox = _attention(q, kt, va, negc, batch, seq)
    x1, logits, x1_rows = _mix(a_sgu, a_fox, gates, x2, w_proj_sgu[l].astype(BF16),
                      w_proj_fox[l].astype(BF16), w_out[l].astype(BF16),
                      norm2_g[l].reshape(1, d), w_r, b_r)

    n_assign = 2 * t
    n_rows = n_assign + N_EXPERTS * EXPERT_BLOCK
    nb = n_rows // EXPERT_BLOCK
    nb_rows = -(-nb // 8) * 8
    dest_t, meta_f, bmeta, emeta, row_tok = _route(logits, nb_rows)
    dest1, dest2 = dest_t[:, 0, :].reshape(t), dest_t[:, 1, :].reshape(t)
    n_used = bmeta[0:1, 1]
    yb = _experts(bmeta[:nb, 0], n_used, emeta[2, :N_EXPERTS], row_tok[:nb].reshape(n_rows), x1_rows,
                  norm2_g[l].reshape(1, d), w_expert_gate[l], w_expert_up[l], w_expert_down[l])
    out = _combine(dest1, dest2, x1, meta_f, yb)
    return out.reshape(batch, seq, d)
```

```python
import functools

import jax
import jax.numpy as jnp
from jax import lax
from jax.experimental import pallas as pl
from jax.experimental.pallas import tpu as pltpu

F32 = jnp.float32
BF16 = jnp.bfloat16
I32 = jnp.int32

D_MODEL = 2048
CHUNK = 128
SGU_GROUPS = 8
SGU_WIDTH = 1024
FOX_HEADS = 8
HEAD_DIM = 128
FOX_WIDTH = 1024
N_GROUPS = 4
EXPERTS_PER_GROUP = 8
N_EXPERTS = 32
D_EXPERT = 512
EXPERT_BLOCK = 128
EPS = 1e-6

LANES = 128
SUBLANES = 8
PREFETCH_PRIORITY = 1
VMEM_LIMIT = 56 * 1024 * 1024

SEC = 1024
N_MAIN_SEC = 5
N_GATE_SEC = 4
TM_NORM = 1024
TM_IN = 512
TQ = 256
HEADS_PER_STEP = 4
TM_MIX = 256
RCH = 512
TM_CMB = 512
LOG2E = 1.4426950408889634
Q_SCALE = HEAD_DIM ** -0.5 * LOG2E


def _sigmoid(x):
    return 1.0 / (1.0 + jnp.exp(-x))


def _log_sigmoid(x):
    return jnp.minimum(x, 0.0) - jnp.log1p(jnp.exp(-jnp.abs(x)))


def _norm1_kernel(x_ref, g1_ref, wf_ref, bf_ref, h_ref, lf_ref):
    x = x_ref[...]
    ms = jnp.mean(x * x, axis=-1, keepdims=True)
    hb = ((x * lax.rsqrt(ms + EPS)) * g1_ref[...]).astype(BF16)
    h_ref[...] = hb
    f = jnp.dot(hb, wf_ref[...], preferred_element_type=F32) + bf_ref[...]
    lf_ref[...] = _log_sigmoid(f)


def _norm1(x2, g1, w_f, b_f):
    t = x2.shape[0]
    row = lambda i: (i, 0)
    const = lambda i: (0, 0)
    return pl.pallas_call(
        _norm1_kernel,
        out_shape=(jax.ShapeDtypeStruct((t, D_MODEL), BF16),
                   jax.ShapeDtypeStruct((t, LANES), F32)),
        grid=(t // TM_NORM,),
        in_specs=[pl.BlockSpec((TM_NORM, D_MODEL), row),
                  pl.BlockSpec((1, D_MODEL), const),
                  pl.BlockSpec((D_MODEL, LANES), const),
                  pl.BlockSpec((1, LANES), const)],
        out_specs=(pl.BlockSpec((TM_NORM, D_MODEL), row), pl.BlockSpec((TM_NORM, LANES), row)),
        compiler_params=pltpu.CompilerParams(
            dimension_semantics=("arbitrary",), vmem_limit_bytes=VMEM_LIMIT),
        name="norm1",
    )(x2, g1, w_f, b_f)


N_SECTIONS = 1 + 3 + N_GATE_SEC
N_CHUNKS = N_MAIN_SEC + N_GATE_SEC
GATE_SECTION0 = 4
WCONV_ROWS = 256


def _inproj_kernel(h_ref, wt_hbm, bg_ref, lng_ref, lnb_ref, wsp_ref, bsp_ref, qg_ref, kg_ref,
                   asgu_ref, q_ref, kt_ref, va_ref, gates_ref,
                   stage, wb, sem):
    s = pl.program_id(0)
    i = pl.program_id(1)
    tm = h_ref.shape[0]

    def chunk_copy(c, slot):
        r0 = pl.multiple_of(c * SEC + jnp.where(c >= N_MAIN_SEC, FOX_HEADS, 0), 8)
        return pltpu.make_async_copy(wt_hbm.at[pl.ds(r0, SEC), :], stage.at[slot], sem.at[slot])

    def convert(slot, k):
        for p in range(SEC // WCONV_ROWS):
            rs = slice(p * WCONV_ROWS, (p + 1) * WCONV_ROWS)
            wb[k, :, rs] = stage[slot, rs, :].T.astype(BF16)

    @pl.when((s == 0) & (i == 0))
    def _():
        chunk_copy(0, 0).start()
        chunk_copy(1, 1).start()
        chunk_copy(0, 0).wait()
        convert(0, 0)
        chunk_copy(2, 0).start(priority=PREFETCH_PRIORITY)
        chunk_copy(1, 1).wait()
        convert(1, 1)

    @pl.when((s > 0) & (i == 0))
    def _():
        c = s + 1
        slot = c % 2
        chunk_copy(c, slot).wait()

        @pl.when(c + 1 < N_CHUNKS)
        def _():
            chunk_copy(c + 1, 1 - slot).start(priority=PREFETCH_PRIORITY)

        convert(slot, 0)

    def section(k=0):
        return jnp.dot(h_ref[...], wb[k], preferred_element_type=F32)

    @pl.when(s == 0)
    def _():
        u = jax.nn.gelu(section(0))
        v = jax.nn.gelu(section(1))
        mu = jnp.mean(v, axis=-1, keepdims=True)
        vc = v - mu
        var = jnp.mean(vc * vc, axis=-1, keepdims=True)
        vn = ((vc * lax.rsqrt(var + EPS)) * lng_ref[...] + lnb_ref[...]).astype(BF16)
        row = lax.broadcasted_iota(I32, (CHUNK, CHUNK), 0)
        col = lax.broadcasted_iota(I32, (CHUNK, CHUNK), 1)
        causal = row >= col
        for g in range(SGU_GROUPS):
            wg = jnp.where(causal, wsp_ref[g], 0.0).astype(BF16)
            bcol = bsp_ref[:, g:g + 1]
            gs = slice(g * LANES, (g + 1) * LANES)
            n_ch = tm // CHUNK
            rhs = jnp.concatenate([vn[c * CHUNK:(c + 1) * CHUNK, gs] for c in range(n_ch)], axis=1)
            sg = jnp.dot(wg, rhs, preferred_element_type=F32) + bcol
            for c in range(n_ch):
                rs = slice(c * CHUNK, (c + 1) * CHUNK)
                asgu_ref[rs, gs] = (u[rs, gs] * sg[:, c * LANES:(c + 1) * LANES]).astype(BF16)

    def _head_norm(z, h, gain_ref):
        zh = z[:, h * HEAD_DIM:(h + 1) * HEAD_DIM]
        ms = jnp.mean(zh * zh, axis=-1, keepdims=True)
        return (zh * lax.rsqrt(ms + EPS)) * gain_ref[...]

    @pl.when(s == 1)
    def _():
        z = section()
        for h in range(FOX_HEADS):
            q_ref[:, h * HEAD_DIM:(h + 1) * HEAD_DIM] = (
                _head_norm(z, h, qg_ref) * Q_SCALE).astype(BF16)

    @pl.when(s == 2)
    def _():
        z = section()
        for h in range(FOX_HEADS):
            kt_ref[h * HEAD_DIM:(h + 1) * HEAD_DIM, :] = _head_norm(z, h, kg_ref).T.astype(BF16)

    @pl.when(s == 3)
    def _():
        va_ref[...] = section().astype(BF16)

    @pl.when(s >= GATE_SECTION0)
    def _():
        gates_ref[...] = (0.5 * jnp.tanh(0.5 * (section() + bg_ref[...])) + 0.5).astype(BF16)


def _inproj(h, wt, b_gate, ln_g, ln_b, w_sp, b_sp_t, q_g, k_g):
    t = h.shape[0]
    n_i = t // TM_IN

    def active(sec):
        return lambda s, i: jnp.where(s < sec, 0, jnp.where(s > sec, n_i - 1, i))

    def gate_blk(s, i):
        g = jnp.clip(s - GATE_SECTION0, 0, N_GATE_SEC - 1)
        return (jnp.where(s < GATE_SECTION0, 0, i), g)

    const2 = lambda s, i: (0, 0)
    out_shape = (
        jax.ShapeDtypeStruct((t, SGU_WIDTH), BF16),
        jax.ShapeDtypeStruct((t, FOX_WIDTH), BF16),
        jax.ShapeDtypeStruct((FOX_WIDTH, t), BF16),
        jax.ShapeDtypeStruct((t, FOX_WIDTH), BF16),
        jax.ShapeDtypeStruct((t, 2 * D_MODEL), BF16),
    )
    return pl.pallas_call(
        _inproj_kernel,
        out_shape=out_shape,
        grid=(N_SECTIONS, n_i),
        in_specs=[
            pl.BlockSpec((TM_IN, D_MODEL), lambda s, i: (i, 0)),
            pl.BlockSpec(memory_space=pl.ANY),
            pl.BlockSpec((1, SEC), lambda s, i: (0, jnp.clip(s - GATE_SECTION0, 0, N_GATE_SEC - 1))),
            pl.BlockSpec((1, SGU_WIDTH), const2),
            pl.BlockSpec((1, SGU_WIDTH), const2),
            pl.BlockSpec((SGU_GROUPS, CHUNK, CHUNK), lambda s, i: (0, 0, 0)),
            pl.BlockSpec((CHUNK, SGU_GROUPS), const2),
            pl.BlockSpec((1, HEAD_DIM), const2),
            pl.BlockSpec((1, HEAD_DIM), const2),
        ],
        out_specs=(
            pl.BlockSpec((TM_IN, SGU_WIDTH), lambda s, i: (active(0)(s, i), 0)),
            pl.BlockSpec((TM_IN, FOX_WIDTH), lambda s, i: (active(1)(s, i), 0)),
            pl.BlockSpec((FOX_WIDTH, TM_IN), lambda s, i: (0, active(2)(s, i))),
            pl.BlockSpec((TM_IN, FOX_WIDTH), lambda s, i: (active(3)(s, i), 0)),
            pl.BlockSpec((TM_IN, SEC), gate_blk),
        ),
        scratch_shapes=[pltpu.VMEM((2, SEC, D_MODEL), F32),
                        pltpu.VMEM((2, D_MODEL, SEC), BF16),
                        pltpu.SemaphoreType.DMA((2,))],
        compiler_params=pltpu.CompilerParams(
            dimension_semantics=("arbitrary", "arbitrary"), vmem_limit_bytes=VMEM_LIMIT),
        name="inproj",
    )(h, wt, b_gate, ln_g, ln_b, w_sp, b_sp_t, q_g, k_g)


def _split3(x):
    x0 = x.astype(BF16)
    r1 = x - x0.astype(F32)
    x1 = r1.astype(BF16)
    r2 = r1 - x1.astype(F32)
    return x0, x1, r2.astype(BF16)


def _cumsum_kernel(lf_ref, negc_ref, c_s):
    seq = lf_ref.shape[0]
    row = lax.broadcasted_iota(I32, (CHUNK, CHUNK), 0)
    col = lax.broadcasted_iota(I32, (CHUNK, CHUNK), 1)
    tri = jnp.where(row >= col, 1.0, 0.0).astype(BF16)
    carry = jnp.zeros((1, LANES), F32)
    for r in range(seq // CHUNK):
        rs = slice(r * CHUNK, (r + 1) * CHUNK)
        x0, x1, x2 = _split3(lf_ref[rs, :])
        cs = (jnp.dot(tri, x0, preferred_element_type=F32)
              + jnp.dot(tri, x1, preferred_element_type=F32)
              + jnp.dot(tri, x2, preferred_element_type=F32)) + carry
        carry = cs[CHUNK - 1:CHUNK, :]
        c_s[rs, :] = cs
    ct = c_s[...].T
    negc_ref[0] = -ct[0:FOX_HEADS, :]


def _forget_cumsum(lf, batch, seq):
    return pl.pallas_call(
        _cumsum_kernel,
        out_shape=jax.ShapeDtypeStruct((batch, FOX_HEADS, seq), F32),
        grid=(batch,),
        in_specs=[pl.BlockSpec((seq, LANES), lambda b: (b, 0))],
        out_specs=pl.BlockSpec((1, FOX_HEADS, seq), lambda b: (b, 0, 0)),
        scratch_shapes=[pltpu.VMEM((seq, LANES), F32)],
        compiler_params=pltpu.CompilerParams(
            dimension_semantics=("arbitrary",), vmem_limit_bytes=VMEM_LIMIT),
        name="forget_cumsum",
    )(lf)


def _attn_kernel(q_ref, kt_ref, v_ref, negc_ref, o_ref, kx, vx):
    seq = q_ref.shape[0]
    sub = lax.broadcasted_iota(I32, (HEAD_DIM, seq), 0)
    for hh in range(HEADS_PER_STEP):
        hs = slice(hh * HEAD_DIM, (hh + 1) * HEAD_DIM)
        c0, c1, c2 = (c.astype(F32) for c in _split3(negc_ref[0, hh] * LOG2E))
        kx[hh, 0:HEAD_DIM, :] = kt_ref[hs, :]
        kx[hh, HEAD_DIM:, :] = jnp.where(
            sub == 0, c0, jnp.where(sub == 1, c1, jnp.where(sub == 2, c2, 0.0))).astype(BF16)
        vx[hh, :, 0:HEAD_DIM] = v_ref[:, hs]
        vx[hh, :, HEAD_DIM:] = jnp.ones((seq, HEAD_DIM), BF16)
    lane = lax.broadcasted_iota(I32, (TQ, HEAD_DIM), 1)
    bias_cols = jnp.where(lane < 3, 1.0, 0.0).astype(BF16)
    row = lax.broadcasted_iota(I32, (TQ, TQ), 0)
    col = lax.broadcasted_iota(I32, (TQ, TQ), 1)
    causal = row >= col
    for qi in range(seq // TQ):
        for hh in range(HEADS_PER_STEP):
            hs = slice(hh * HEAD_DIM, (hh + 1) * HEAD_DIM)
            k0 = qi * TQ
            q = jnp.concatenate([q_ref[k0:k0 + TQ, hs], bias_cols], axis=1)
            s_d = jnp.dot(q, kx[hh, :, k0:k0 + TQ], preferred_element_type=F32)
            s_d = jnp.where(causal, s_d, -jnp.inf)
            m = jnp.max(s_d, axis=-1, keepdims=True)
            if qi > 0:
                s_o = jnp.dot(q, kx[hh, :, 0:k0], preferred_element_type=F32)
                m = jnp.maximum(m, jnp.max(s_o, axis=-1, keepdims=True))
            acc = jnp.dot(jnp.exp2(s_d - m).astype(BF16), vx[hh, k0:k0 + TQ, :],
                          preferred_element_type=F32)
            if qi > 0:
                acc = acc + jnp.dot(jnp.exp2(s_o - m).astype(BF16), vx[hh, 0:k0, :],
                                    preferred_element_type=F32)
            inv_l = 1.0 / acc[:, HEAD_DIM:HEAD_DIM + 1]
            o_ref[k0:k0 + TQ, hs] = (acc[:, 0:HEAD_DIM] * inv_l).astype(BF16)


def _attention(q, kt, va, negc, batch, seq):
    t = q.shape[0]
    width = HEADS_PER_STEP * HEAD_DIM
    blk = pl.BlockSpec((seq, width), lambda b, g: (b, g))
    return pl.pallas_call(
        _attn_kernel,
        out_shape=jax.ShapeDtypeStruct((t, FOX_WIDTH), BF16),
        grid=(batch, FOX_HEADS // HEADS_PER_STEP),
        in_specs=[blk,
                  pl.BlockSpec((width, seq), lambda b, g: (g, b)),
                  blk,
                  pl.BlockSpec((1, HEADS_PER_STEP, 1, seq), lambda b, g: (b, g, 0, 0))],
        out_specs=blk,
        scratch_shapes=[pltpu.VMEM((HEADS_PER_STEP, 2 * HEAD_DIM, seq), BF16),
                        pltpu.VMEM((HEADS_PER_STEP, seq, 2 * HEAD_DIM), BF16)],
        compiler_params=pltpu.CompilerParams(
            dimension_semantics=("arbitrary", "arbitrary"), vmem_limit_bytes=VMEM_LIMIT),
        name="fox_attention",
    )(q, kt, va, negc)


def _mix_kernel(as_ref, af_ref, g_ref, x_ref, wps_ref, wpf_ref, wo_ref, g2_ref, wr_ref, br_ref,
                x1_ref, lg_ref):
    ys = jnp.dot(as_ref[...], wps_ref[...], preferred_element_type=F32)
    yf = jnp.dot(af_ref[...], wpf_ref[...], preferred_element_type=F32)
    m = (g_ref[:, :D_MODEL].astype(F32) * ys + g_ref[:, D_MODEL:].astype(F32) * yf).astype(BF16)
    x1 = x_ref[...] + jnp.dot(m, wo_ref[...], preferred_element_type=F32)
    x1_ref[...] = x1
    ms = jnp.mean(x1 * x1, axis=-1, keepdims=True)
    h2 = ((x1 * lax.rsqrt(ms + EPS)) * g2_ref[...]).astype(BF16)
    lg_ref[...] = jnp.dot(h2, wr_ref[...], preferred_element_type=F32) + br_ref[...]


def _mix(a_sgu, a_fox, gates, x2, wps, wpf, wo, g2, w_r, b_r):
    t = x2.shape[0]
    row = lambda i: (i, 0)
    const = lambda i: (0, 0)
    resident = functools.partial(pl.BlockSpec, index_map=const, pipeline_mode=pl.Buffered(1))
    return pl.pallas_call(
        _mix_kernel,
        out_shape=(jax.ShapeDtypeStruct((t, D_MODEL), F32),
                   jax.ShapeDtypeStruct((t, LANES), F32)),
        grid=(t // TM_MIX,),
        in_specs=[
            pl.BlockSpec((TM_MIX, SGU_WIDTH), row),
            pl.BlockSpec((TM_MIX, FOX_WIDTH), row),
            pl.BlockSpec((TM_MIX, 2 * D_MODEL), row),
            pl.BlockSpec((TM_MIX, D_MODEL), row),
            resident((SGU_WIDTH, D_MODEL)),
            resident((FOX_WIDTH, D_MODEL)),
            resident((D_MODEL, D_MODEL)),
            pl.BlockSpec((1, D_MODEL), const),
            resident((D_MODEL, LANES)),
            pl.BlockSpec((1, LANES), const),
        ],
        out_specs=(pl.BlockSpec((TM_MIX, D_MODEL), row), pl.BlockSpec((TM_MIX, LANES), row)),
        compiler_params=pltpu.CompilerParams(
            dimension_semantics=("arbitrary",), vmem_limit_bytes=VMEM_LIMIT),
        name="mix",
    )(a_sgu, a_fox, gates, x2, wps, wpf, wo, g2, w_r, b_r)


def _route_kernel(lg_ref, dt_ref, mf_ref, bm_ref, em_ref, bt_ref, mi_ref, inv_ref):
    t = lg_ref.shape[0]
    n_chunks = t // RCH
    lane_i = lax.broadcasted_iota(I32, (RCH, LANES), 1)
    lane = lane_i.astype(F32)
    lane_grp = ((lane_i - N_GROUPS) >> 3).astype(F32)
    is_grp = lane_i < N_GROUPS
    is_exp = (lane_i >= N_GROUPS) & (lane_i < N_GROUPS + N_EXPERTS)
    r_i = lax.broadcasted_iota(I32, (RCH, RCH), 0)
    c_i = lax.broadcasted_iota(I32, (RCH, RCH), 1)
    strict_lower = jnp.where(r_i > c_i, 1.0, 0.0).astype(BF16)
    neg_inf = -jnp.inf

    def first_max(vals):
        vmax = jnp.max(vals, axis=-1, keepdims=True)
        idx = jnp.min(jnp.where(vals == vmax, lane, float(LANES)), axis=-1, keepdims=True)
        return vmax, idx

    def pick(table, idx):
        return jnp.sum(jnp.where(lane == idx, table, 0.0), axis=-1, keepdims=True)

    def pack(cols):
        out = jnp.zeros((RCH, LANES), F32)
        for n, c in enumerate(cols):
            out = jnp.where(lane_i == n, c, out)
        return out

    def pass1(ci, counts):
        r0 = pl.multiple_of(ci * RCH, RCH)
        lg = lg_ref[pl.ds(r0, RCH), :]
        gmax, grp = first_max(jnp.where(is_grp, lg, neg_inf))
        p_grp = 1.0 / jnp.sum(jnp.where(is_grp, jnp.exp(lg - gmax), 0.0), axis=-1, keepdims=True)
        el = jnp.where(is_exp & (lane_grp == grp), lg, neg_inf)
        v1, i1 = first_max(el)
        v2, i2 = first_max(jnp.where(lane == i1, neg_inf, el))
        e21 = jnp.exp(v2 - v1)
        w1 = p_grp / (1.0 + e21)
        w2 = p_grp * e21 / (1.0 + e21)
        e1 = i1 - float(N_GROUPS)
        e2 = i2 - float(N_GROUPS)
        hot = jnp.where((lane == e1) | (lane == e2), 1.0, 0.0)
        before = jnp.dot(strict_lower, hot.astype(BF16), preferred_element_type=F32) + counts
        mi_ref[pl.ds(r0, RCH), :] = pack([e1, e2, pick(before, e1), pick(before, e2)]).astype(I32)
        mf_ref[pl.ds(r0, RCH), :] = pack([w1, w2])
        return counts + jnp.sum(hot, axis=0, keepdims=True)

    counts = lax.fori_loop(0, n_chunks, pass1, jnp.zeros((1, LANES), F32))

    nblk = jnp.floor((counts + float(EXPERT_BLOCK - 1)) * (1.0 / EXPERT_BLOCK))
    u_r = lax.broadcasted_iota(I32, (LANES, LANES), 0)
    u_c = lax.broadcasted_iota(I32, (LANES, LANES), 1)
    strict_upper = jnp.where(u_r < u_c, 1.0, 0.0).astype(BF16)
    bstart = jnp.dot(jnp.broadcast_to(nblk, (8, LANES)).astype(BF16), strict_upper,
                     preferred_element_type=F32)[0:1, :]
    bend = bstart + nblk

    def pass2(ci, _):
        r0 = pl.multiple_of(ci * RCH, RCH)
        mi = mi_ref[pl.ds(r0, RCH), :].astype(F32)
        cols = [jnp.sum(jnp.where(lane_i == n, mi, 0.0), axis=-1, keepdims=True) for n in range(4)]
        d1 = pick(bstart, cols[0]) * float(EXPERT_BLOCK) + cols[2]
        d2 = pick(bstart, cols[1]) * float(EXPERT_BLOCK) + cols[3]
        d_t = pack([d1, d2]).T
        dt_ref[ci] = d_t[0:8, :].astype(I32)
        tok = (lax.broadcasted_iota(I32, (RCH, 1), 0) + r0).astype(F32)
        tok_hi = jnp.floor(tok * (1.0 / EXPERT_BLOCK))
        tok_lo = tok - tok_hi * float(EXPERT_BLOCK)
        for n, d in enumerate((d1, d2)):
            d_blk = jnp.floor(d * (1.0 / EXPERT_BLOCK))
            hit = lane == (d - d_blk * float(EXPERT_BLOCK))
            blk_row = jnp.floor(d_t[n:n + 1, :] * (1.0 / EXPERT_BLOCK))
            sel = jnp.where(inv_blk == blk_row, 1.0, 0.0).astype(BF16)
            inv_ref[0] += jnp.dot(sel, jnp.where(hit, tok_hi, 0.0).astype(BF16),
                                  preferred_element_type=F32)
            inv_ref[1] += jnp.dot(sel, jnp.where(hit, tok_lo, 0.0).astype(BF16),
                                  preferred_element_type=F32)
        return 0

    nb_rows = bm_ref.shape[0]
    inv_blk = lax.broadcasted_iota(I32, (nb_rows, RCH), 0).astype(F32)
    inv_ref[...] = jnp.zeros_like(inv_ref)
    lax.fori_loop(0, n_chunks, pass2, 0)
    bt_ref[...] = (inv_ref[0] * float(EXPERT_BLOCK) + inv_ref[1]).astype(I32)

    b_col = lax.broadcasted_iota(I32, (nb_rows, LANES), 0).astype(F32)
    b_lane = lax.broadcasted_iota(I32, (nb_rows, LANES), 1)
    done = jnp.where((bend <= b_col) & (b_lane < N_EXPERTS), 1.0, 0.0)
    blk_e = jnp.minimum(jnp.sum(done, axis=-1, keepdims=True), float(N_EXPERTS - 1))
    n_used = jnp.sum(jnp.where(b_lane[0:1, :] == N_EXPERTS - 1, bend, 0.0), axis=-1, keepdims=True)
    bm_ref[...] = jnp.where(b_lane == 0, blk_e, jnp.where(b_lane == 1, n_used, 0.0)).astype(I32)

    e_row = lax.broadcasted_iota(I32, (8, LANES), 0)
    em_ref[...] = jnp.where(e_row == 0, counts,
                            jnp.where(e_row == 1, bstart, jnp.where(e_row == 2, nblk, 0.0))).astype(I32)


def _route(logits, nb_rows):
    t = logits.shape[0]
    full = lambda shape: pl.BlockSpec(shape, lambda: (0,) * len(shape))
    return pl.pallas_call(
        _route_kernel,
        out_shape=(jax.ShapeDtypeStruct((t // RCH, 8, RCH), I32),
                   jax.ShapeDtypeStruct((t, LANES), F32),
                   jax.ShapeDtypeStruct((nb_rows, LANES), I32),
                   jax.ShapeDtypeStruct((8, LANES), I32),
                   jax.ShapeDtypeStruct((nb_rows, LANES), I32)),
        in_specs=[full((t, LANES))],
        out_specs=(full((t // RCH, 8, RCH)), full((t, LANES)), full((nb_rows, LANES)),
                   full((8, LANES)), full((nb_rows, LANES))),
        scratch_shapes=[pltpu.VMEM((t, LANES), I32), pltpu.VMEM((2, nb_rows, LANES), F32)],
        compiler_params=pltpu.CompilerParams(vmem_limit_bytes=VMEM_LIMIT),
        name="route",
    )(logits)


GATHER_GROUPS = 4
GATHER_AHEAD = 4
WEIGHT_SLOTS = 3


def _experts_kernel(blk_e_ref, nused_ref, nblk_ref, tok_ref, x_hbm, g2_ref, wg_hbm, wu_hbm, wd_hbm,
                    y_ref, xg, wg_f, wu_f, wd_f, wg_s, wu_s, wd_s, slot_ref, sem, gsem):
    b = pl.program_id(0)
    n_used = nused_ref[0]
    used = b < n_used
    e = blk_e_ref[b]
    new_expert = (b == 0) | (e != blk_e_ref[jnp.maximum(b - 1, 0)])
    n_slots = GATHER_AHEAD + 1
    xslot = b % n_slots

    def gather_rows(block, slot, lo, hi):
        row0 = jnp.minimum(block, n_used - 1) * EXPERT_BLOCK
        for r in range(lo, hi):
            pltpu.make_async_copy(x_hbm.at[pl.ds(tok_ref[row0 + r], 1), :],
                                  xg.at[slot, pl.ds(r, 1), :], gsem.at[slot]).start()

    def gather_wait(slot):
        pltpu.make_async_copy(x_hbm.at[pl.ds(0, EXPERT_BLOCK), :], xg.at[slot], gsem.at[slot]).wait()

    @pl.when(b == 0)
    def _():
        for k in range(GATHER_AHEAD):
            gather_rows(k, k, 0, EXPERT_BLOCK)

    def weight_copies(expert, slot):
        return (pltpu.make_async_copy(wg_hbm.at[expert], wg_f.at[slot], sem.at[slot]),
                pltpu.make_async_copy(wu_hbm.at[expert], wu_f.at[slot], sem.at[slot]),
                pltpu.make_async_copy(wd_hbm.at[expert], wd_f.at[slot], sem.at[slot]))

    def block_expert(blk):
        return blk_e_ref[jnp.minimum(blk, n_used - 1)]

    b_next = b + nblk_ref[e]
    b_next2 = b_next + nblk_ref[block_expert(b_next)]

    @pl.when(b == 0)
    def _():
        slot_ref[0] = 0
        for cp in weight_copies(e, 0):
            cp.start()

        @pl.when(b_next < n_used)
        def _():
            for cp in weight_copies(block_expert(b_next), 1):
                cp.start(priority=PREFETCH_PRIORITY)

    @pl.when(used & new_expert)
    def _():
        slot = slot_ref[0]

        @pl.when(b_next2 < n_used)
        def _():
            for cp in weight_copies(block_expert(b_next2), (slot + 2) % WEIGHT_SLOTS):
                cp.start(priority=PREFETCH_PRIORITY)

        for cp in weight_copies(e, slot):
            cp.wait()
        wg_s[...] = wg_f[slot].astype(BF16)
        wu_s[...] = wu_f[slot].astype(BF16)
        wd_s[...] = wd_f[slot].astype(BF16)
        slot_ref[0] = (slot + 1) % WEIGHT_SLOTS

    @pl.when(used)
    def _():
        gather_wait(xslot)
        ahead = b + GATHER_AHEAD
        aslot = ahead % n_slots
        per_group = EXPERT_BLOCK // GATHER_GROUPS
        x = xg[xslot]
        ms = jnp.mean(x * x, axis=-1, keepdims=True)
        h = ((x * lax.rsqrt(ms + EPS)) * g2_ref[...]).astype(BF16)
        gather_rows(ahead, aslot, 0, per_group)
        a = jnp.dot(h, wg_s[...], preferred_element_type=F32)
        gather_rows(ahead, aslot, per_group, 2 * per_group)
        u = jnp.dot(h, wu_s[...], preferred_element_type=F32)
        gather_rows(ahead, aslot, 2 * per_group, 3 * per_group)
        mid = ((a * _sigmoid(a)) * u).astype(BF16)
        gather_rows(ahead, aslot, 3 * per_group, EXPERT_BLOCK)
        y_ref[...] = jnp.dot(mid, wd_s[...], preferred_element_type=F32)

    @pl.when(b == n_used - 1)
    def _():
        for k in range(1, GATHER_AHEAD + 1):
            gather_wait((b + k) % n_slots)

    @pl.when(jnp.logical_not(used))
    def _():
        y_ref[...] = jnp.zeros_like(y_ref)


def _experts(blk_e, n_used, nblk, row_tok, x1, g2, w_g, w_u, w_d):
    n_rows = row_tok.shape[0]
    nb = n_rows // EXPERT_BLOCK
    hbm = pl.BlockSpec(memory_space=pl.ANY)
    grid_spec = pltpu.PrefetchScalarGridSpec(
        num_scalar_prefetch=4,
        grid=(nb,),
        in_specs=[
            hbm,
            pl.BlockSpec((1, D_MODEL), lambda b, *_: (0, 0)),
            hbm, hbm, hbm,
        ],
        out_specs=pl.BlockSpec((EXPERT_BLOCK, D_MODEL), lambda b, *_: (b, 0)),
        scratch_shapes=[pltpu.VMEM((GATHER_AHEAD + 1, EXPERT_BLOCK, D_MODEL), F32),
                        pltpu.VMEM((WEIGHT_SLOTS, D_MODEL, D_EXPERT), F32),
                        pltpu.VMEM((WEIGHT_SLOTS, D_MODEL, D_EXPERT), F32),
                        pltpu.VMEM((WEIGHT_SLOTS, D_EXPERT, D_MODEL), F32),
                        pltpu.VMEM((D_MODEL, D_EXPERT), BF16),
                        pltpu.VMEM((D_MODEL, D_EXPERT), BF16),
                        pltpu.VMEM((D_EXPERT, D_MODEL), BF16),
                        pltpu.SMEM((1,), I32),
                        pltpu.SemaphoreType.DMA((WEIGHT_SLOTS,)),
                        pltpu.SemaphoreType.DMA((GATHER_AHEAD + 1,))],
    )
    return pl.pallas_call(
        _experts_kernel,
        out_shape=jax.ShapeDtypeStruct((n_rows, D_MODEL), F32),
        grid_spec=grid_spec,
        compiler_params=pltpu.CompilerParams(
            dimension_semantics=("arbitrary",), vmem_limit_bytes=VMEM_LIMIT),
        name="experts",
    )(blk_e, n_used, nblk, row_tok, x1, g2, w_g, w_u, w_d)


def _combine_kernel(d1_ref, d2_ref, x1_ref, w_ref, y_ref, o_ref, gbuf, sem):
    i = pl.program_id(0)
    n = pl.num_programs(0)
    tm = x1_ref.shape[0]

    def issue(tile, slot):
        def body(g, _):
            r0 = pl.multiple_of(g * SUBLANES, SUBLANES)
            for k in range(SUBLANES):
                tok = tile * tm + r0 + k
                pltpu.make_async_copy(
                    y_ref.at[pl.ds(d1_ref[tok], 1), :],
                    gbuf.at[slot, pl.ds(r0 + k, 1), :], sem.at[slot]).start(priority=0)
                pltpu.make_async_copy(
                    y_ref.at[pl.ds(d2_ref[tok], 1), :],
                    gbuf.at[slot, pl.ds(tm + r0 + k, 1), :], sem.at[slot]).start(priority=1)
            return 0

        lax.fori_loop(0, tm // SUBLANES, body, 0)

    slot = i % 2

    @pl.when(i == 0)
    def _():
        issue(0, 0)

    @pl.when(i + 1 < n)
    def _():
        issue(i + 1, 1 - slot)

    pltpu.make_async_copy(y_ref.at[pl.ds(0, 2 * tm), :], gbuf.at[slot], sem.at[slot]).wait()
    w = w_ref[...]
    g = gbuf[slot]
    o_ref[...] = x1_ref[...] + (w[:, 0:1] * g[0:tm, :] + w[:, 1:2] * g[tm:2 * tm, :])


def _combine(dest1, dest2, x1, w, yb):
    t = x1.shape[0]
    grid_spec = pltpu.PrefetchScalarGridSpec(
        num_scalar_prefetch=2,
        grid=(t // TM_CMB,),
        in_specs=[
            pl.BlockSpec((TM_CMB, D_MODEL), lambda i, a, b: (i, 0)),
            pl.BlockSpec((TM_CMB, LANES), lambda i, a, b: (i, 0)),
            pl.BlockSpec(memory_space=pl.ANY),
        ],
        out_specs=pl.BlockSpec((TM_CMB, D_MODEL), lambda i, a, b: (i, 0)),
        scratch_shapes=[pltpu.VMEM((2, 2 * TM_CMB, D_MODEL), F32),
                        pltpu.SemaphoreType.DMA((2,))],
    )
    return pl.pallas_call(
        _combine_kernel,
        out_shape=jax.ShapeDtypeStruct((t, D_MODEL), F32),
        grid_spec=grid_spec,
        compiler_params=pltpu.CompilerParams(
            dimension_semantics=("arbitrary",), vmem_limit_bytes=VMEM_LIMIT),
        name="combine",
    )(dest1, dest2, x1, w, yb)


def kernel(x, norm1_g, w_in, b_gate, b_forget, sgu_ln_g, sgu_ln_b, w_spatial, b_spatial, q_norm_g, k_norm_g, w_proj_sgu, w_proj_fox, w_out, norm2_g, w_router_group, b_router_group, w_router_expert, b_router_expert, w_expert_gate, w_expert_up, w_expert_down):
    batch, seq, d = x.shape
    t = batch * seq
    l = 0
    x2 = x.reshape(t, d)

    wt = jnp.swapaxes(w_in[l], 0, 1)
    off_f = N_MAIN_SEC * SEC
    w_f = jnp.pad(wt[off_f:off_f + FOX_HEADS].T, ((0, 0), (0, LANES - FOX_HEADS))).astype(BF16)
    b_f = jnp.pad(b_forget[l], (0, LANES - FOX_HEADS)).reshape(1, LANES)
    n_r = N_GROUPS + N_EXPERTS
    w_r = jnp.pad(jnp.concatenate([w_router_group[l], w_router_expert[l]], axis=1),
                  ((0, 0), (0, LANES - n_r))).astype(BF16)
    b_r = jnp.pad(jnp.concatenate([b_router_group[l], b_router_expert[l]]),
                  (0, LANES - n_r)).reshape(1, LANES)

    h, lf = _norm1(x2, norm1_g[l].reshape(1, d), w_f, b_f)
    a_sgu, q, kt, va, gates = _inproj(
        h, wt, b_gate[l].reshape(1, 2 * d),
        sgu_ln_g[l].reshape(1, SGU_WIDTH), sgu_ln_b[l].reshape(1, SGU_WIDTH),
        w_spatial[l], b_spatial[l].T, q_norm_g[l].reshape(1, HEAD_DIM),
        k_norm_g[l].reshape(1, HEAD_DIM))
    negc = _forget_cumsum(lf, batch, seq).reshape(batch, FOX_HEADS, 1, seq)
    a_fox = _attention(q, kt, va, negc, batch, seq)
    x1, logits = _mix(a_sgu, a_fox, gates, x2, w_proj_sgu[l].astype(BF16),
                      w_proj_fox[l].astype(BF16), w_out[l].astype(BF16),
                      norm2_g[l].reshape(1, d), w_r, b_r)

    n_assign = 2 * t
    n_rows = n_assign + N_EXPERTS * EXPERT_BLOCK
    nb = n_rows // EXPERT_BLOCK
    nb_rows = -(-nb // 8) * 8
    dest_t, meta_f, bmeta, emeta, row_tok = _route(logits, nb_rows)
    dest1, dest2 = dest_t[:, 0, :].reshape(t), dest_t[:, 1, :].reshape(t)
    n_used = bmeta[0:1, 1]
    yb = _experts(bmeta[:nb, 0], n_used, emeta[2, :N_EXPERTS], row_tok[:nb].reshape(n_rows), x1,
                  norm2_g[l].reshape(1, d), w_expert_gate[l], w_expert_up[l], w_expert_down[l])
    out = _combine(dest1, dest2, x1, meta_f, yb)
    return out.reshape(batch, seq, d)
```

```python
import functools

import jax
import jax.numpy as jnp
from jax import lax
from jax.experimental import pallas as pl
from jax.experimental.pallas import tpu as pltpu

F32 = jnp.float32
BF16 = jnp.bfloat16
I32 = jnp.int32

D_MODEL = 2048
CHUNK = 128
SGU_GROUPS = 8
SGU_WIDTH = 1024
FOX_HEADS = 8
HEAD_DIM = 128
FOX_WIDTH = 1024
N_GROUPS = 4
EXPERTS_PER_GROUP = 8
N_EXPERTS = 32
D_EXPERT = 512
EXPERT_BLOCK = 128
EPS = 1e-6

LANES = 128
SUBLANES = 8
PREFETCH_PRIORITY = 1
VMEM_LIMIT = 56 * 1024 * 1024

SEC = 1024
N_MAIN_SEC = 5
N_GATE_SEC = 4
TM_NORM = 1024
TM_IN = 512
TQ = 256
HEADS_PER_STEP = 4
TM_MIX = 256
RCH = 512
TM_CMB = 512
COMBINE_ROWS = 64
N_SPLIT = 3
EXPERT_GROUP_SHIFT = EXPERTS_PER_GROUP.bit_length() - 1
LOG2E = 1.4426950408889634
Q_SCALE = HEAD_DIM ** -0.5 * LOG2E


def _sigmoid(x):
    return 1.0 / (1.0 + jnp.exp(-x))


def _log_sigmoid(x):
    return jnp.minimum(x, 0.0) - jnp.log1p(jnp.exp(-jnp.abs(x)))


def _norm1_kernel(x_ref, g1_ref, wf_ref, bf_ref, h_ref, lf_ref):
    x = x_ref[...]
    ms = jnp.mean(x * x, axis=-1, keepdims=True)
    hb = ((x * lax.rsqrt(ms + EPS)) * g1_ref[...]).astype(BF16)
    h_ref[...] = hb
    f = jnp.dot(hb, wf_ref[...], preferred_element_type=F32) + bf_ref[...]
    lf_ref[...] = _log_sigmoid(f)


def _norm1(x2, g1, w_f, b_f):
    t = x2.shape[0]
    row = lambda i: (i, 0)
    const = lambda i: (0, 0)
    return pl.pallas_call(
        _norm1_kernel,
        out_shape=(jax.ShapeDtypeStruct((t, D_MODEL), BF16),
                   jax.ShapeDtypeStruct((t, LANES), F32)),
        grid=(t // TM_NORM,),
        in_specs=[pl.BlockSpec((TM_NORM, D_MODEL), row),
                  pl.BlockSpec((1, D_MODEL), const),
                  pl.BlockSpec((D_MODEL, LANES), const),
                  pl.BlockSpec((1, LANES), const)],
        out_specs=(pl.BlockSpec((TM_NORM, D_MODEL), row), pl.BlockSpec((TM_NORM, LANES), row)),
        compiler_params=pltpu.CompilerParams(
            dimension_semantics=("arbitrary",), vmem_limit_bytes=VMEM_LIMIT),
        name="norm1",
    )(x2, g1, w_f, b_f)


N_SECTIONS = 1 + 3 + N_GATE_SEC
N_CHUNKS = N_MAIN_SEC + N_GATE_SEC
GATE_SECTION0 = 4
WCONV_ROWS = 256


def _inproj_kernel(h_ref, wt_hbm, bg_ref, lng_ref, lnb_ref, wsp_ref, bsp_ref, qg_ref, kg_ref,
                   asgu_ref, q_ref, kt_ref, va_ref, gates_ref,
                   stage, wb, sem):
    s = pl.program_id(0)
    i = pl.program_id(1)
    tm = h_ref.shape[0]

    def chunk_copy(c, slot):
        r0 = pl.multiple_of(c * SEC + jnp.where(c >= N_MAIN_SEC, FOX_HEADS, 0), 8)
        return pltpu.make_async_copy(wt_hbm.at[pl.ds(r0, SEC), :], stage.at[slot], sem.at[slot])

    def convert(slot, k):
        for p in range(SEC // WCONV_ROWS):
            rs = slice(p * WCONV_ROWS, (p + 1) * WCONV_ROWS)
            wb[k, :, rs] = stage[slot, rs, :].T.astype(BF16)

    @pl.when((s == 0) & (i == 0))
    def _():
        chunk_copy(0, 0).start()
        chunk_copy(1, 1).start()
        chunk_copy(0, 0).wait()
        convert(0, 0)
        chunk_copy(2, 0).start(priority=PREFETCH_PRIORITY)
        chunk_copy(1, 1).wait()
        convert(1, 1)

    @pl.when((s > 0) & (i == 0))
    def _():
        c = s + 1
        slot = c % 2
        chunk_copy(c, slot).wait()

        @pl.when(c + 1 < N_CHUNKS)
        def _():
            chunk_copy(c + 1, 1 - slot).start(priority=PREFETCH_PRIORITY)

        convert(slot, 0)

    def section(k=0):
        return jnp.dot(h_ref[...], wb[k], preferred_element_type=F32)

    @pl.when(s == 0)
    def _():
        u = jax.nn.gelu(section(0))
        v = jax.nn.gelu(section(1))
        mu = jnp.mean(v, axis=-1, keepdims=True)
        vc = v - mu
        var = jnp.mean(vc * vc, axis=-1, keepdims=True)
        vn = ((vc * lax.rsqrt(var + EPS)) * lng_ref[...] + lnb_ref[...]).astype(BF16)
        row = lax.broadcasted_iota(I32, (CHUNK, CHUNK), 0)
        col = lax.broadcasted_iota(I32, (CHUNK, CHUNK), 1)
        causal = row >= col
        for g in range(SGU_GROUPS):
            wg = jnp.where(causal, wsp_ref[g], 0.0).astype(BF16)
            bcol = bsp_ref[:, g:g + 1]
            gs = slice(g * LANES, (g + 1) * LANES)
            n_ch = tm // CHUNK
            rhs = jnp.concatenate([vn[c * CHUNK:(c + 1) * CHUNK, gs] for c in range(n_ch)], axis=1)
            sg = jnp.dot(wg, rhs, preferred_element_type=F32) + bcol
            for c in range(n_ch):
                rs = slice(c * CHUNK, (c + 1) * CHUNK)
                asgu_ref[rs, gs] = (u[rs, gs] * sg[:, c * LANES:(c + 1) * LANES]).astype(BF16)

    def _head_norm(z, h, gain_ref):
        zh = z[:, h * HEAD_DIM:(h + 1) * HEAD_DIM]
        ms = jnp.mean(zh * zh, axis=-1, keepdims=True)
        return (zh * lax.rsqrt(ms + EPS)) * gain_ref[...]

    @pl.when(s == 1)
    def _():
        z = section()
        for h in range(FOX_HEADS):
            q_ref[:, h * HEAD_DIM:(h + 1) * HEAD_DIM] = (
                _head_norm(z, h, qg_ref) * Q_SCALE).astype(BF16)

    @pl.when(s == 2)
    def _():
        z = section()
        for h in range(FOX_HEADS):
            kt_ref[h * HEAD_DIM:(h + 1) * HEAD_DIM, :] = _head_norm(z, h, kg_ref).T.astype(BF16)

    @pl.when(s == 3)
    def _():
        va_ref[...] = section().astype(BF16)

    @pl.when(s >= GATE_SECTION0)
    def _():
        gates_ref[...] = (0.5 * jnp.tanh(0.5 * (section() + bg_ref[...])) + 0.5).astype(BF16)


def _inproj(h, wt, b_gate, ln_g, ln_b, w_sp, b_sp_t, q_g, k_g):
    t = h.shape[0]
    n_i = t // TM_IN

    def active(sec):
        return lambda s, i: jnp.where(s < sec, 0, jnp.where(s > sec, n_i - 1, i))

    def gate_blk(s, i):
        g = jnp.clip(s - GATE_SECTION0, 0, N_GATE_SEC - 1)
        return (jnp.where(s < GATE_SECTION0, 0, i), g)

    const2 = lambda s, i: (0, 0)
    out_shape = (
        jax.ShapeDtypeStruct((t, SGU_WIDTH), BF16),
        jax.ShapeDtypeStruct((t, FOX_WIDTH), BF16),
        jax.ShapeDtypeStruct((FOX_WIDTH, t), BF16),
        jax.ShapeDtypeStruct((t, FOX_WIDTH), BF16),
        jax.ShapeDtypeStruct((t, 2 * D_MODEL), BF16),
    )
    return pl.pallas_call(
        _inproj_kernel,
        out_shape=out_shape,
        grid=(N_SECTIONS, n_i),
        in_specs=[
            pl.BlockSpec((TM_IN, D_MODEL), lambda s, i: (i, 0)),
            pl.BlockSpec(memory_space=pl.ANY),
            pl.BlockSpec((1, SEC), lambda s, i: (0, jnp.clip(s - GATE_SECTION0, 0, N_GATE_SEC - 1))),
            pl.BlockSpec((1, SGU_WIDTH), const2),
            pl.BlockSpec((1, SGU_WIDTH), const2),
            pl.BlockSpec((SGU_GROUPS, CHUNK, CHUNK), lambda s, i: (0, 0, 0)),
            pl.BlockSpec((CHUNK, SGU_GROUPS), const2),
            pl.BlockSpec((1, HEAD_DIM), const2),
            pl.BlockSpec((1, HEAD_DIM), const2),
        ],
        out_specs=(
            pl.BlockSpec((TM_IN, SGU_WIDTH), lambda s, i: (active(0)(s, i), 0)),
            pl.BlockSpec((TM_IN, FOX_WIDTH), lambda s, i: (active(1)(s, i), 0)),
            pl.BlockSpec((FOX_WIDTH, TM_IN), lambda s, i: (0, active(2)(s, i))),
            pl.BlockSpec((TM_IN, FOX_WIDTH), lambda s, i: (active(3)(s, i), 0)),
            pl.BlockSpec((TM_IN, SEC), gate_blk),
        ),
        scratch_shapes=[pltpu.VMEM((2, SEC, D_MODEL), F32),
                        pltpu.VMEM((2, D_MODEL, SEC), BF16),
                        pltpu.SemaphoreType.DMA((2,))],
        compiler_params=pltpu.CompilerParams(
            dimension_semantics=("arbitrary", "arbitrary"), vmem_limit_bytes=VMEM_LIMIT),
        name="inproj",
    )(h, wt, b_gate, ln_g, ln_b, w_sp, b_sp_t, q_g, k_g)


def _split3(x):
    x0 = x.astype(BF16)
    r1 = x - x0.astype(F32)
    x1 = r1.astype(BF16)
    r2 = r1 - x1.astype(F32)
    return x0, x1, r2.astype(BF16)


def _cumsum_kernel(lf_ref, negc_ref, c_s):
    seq = lf_ref.shape[0]
    row = lax.broadcasted_iota(I32, (CHUNK, CHUNK), 0)
    col = lax.broadcasted_iota(I32, (CHUNK, CHUNK), 1)
    tri = jnp.where(row >= col, 1.0, 0.0).astype(BF16)
    carry = jnp.zeros((1, LANES), F32)
    for r in range(seq // CHUNK):
        rs = slice(r * CHUNK, (r + 1) * CHUNK)
        x0, x1, x2 = _split3(lf_ref[rs, :])
        cs = (jnp.dot(tri, x0, preferred_element_type=F32)
              + jnp.dot(tri, x1, preferred_element_type=F32)
              + jnp.dot(tri, x2, preferred_element_type=F32)) + carry
        carry = cs[CHUNK - 1:CHUNK, :]
        c_s[rs, :] = cs
    ct = c_s[...].T
    negc_ref[0] = -ct[0:FOX_HEADS, :]


def _forget_cumsum(lf, batch, seq):
    return pl.pallas_call(
        _cumsum_kernel,
        out_shape=jax.ShapeDtypeStruct((batch, FOX_HEADS, seq), F32),
        grid=(batch,),
        in_specs=[pl.BlockSpec((seq, LANES), lambda b: (b, 0))],
        out_specs=pl.BlockSpec((1, FOX_HEADS, seq), lambda b: (b, 0, 0)),
        scratch_shapes=[pltpu.VMEM((seq, LANES), F32)],
        compiler_params=pltpu.CompilerParams(
            dimension_semantics=("arbitrary",), vmem_limit_bytes=VMEM_LIMIT),
        name="forget_cumsum",
    )(lf)


def _attn_kernel(q_ref, kt_ref, v_ref, negc_ref, o_ref, kx, vx):
    seq = q_ref.shape[0]
    sub = lax.broadcasted_iota(I32, (HEAD_DIM, seq), 0)
    for hh in range(HEADS_PER_STEP):
        hs = slice(hh * HEAD_DIM, (hh + 1) * HEAD_DIM)
        c0, c1, c2 = (c.astype(F32) for c in _split3(negc_ref[0, hh] * LOG2E))
        kx[hh, 0:HEAD_DIM, :] = kt_ref[hs, :]
        kx[hh, HEAD_DIM:, :] = jnp.where(
            sub == 0, c0, jnp.where(sub == 1, c1, jnp.where(sub == 2, c2, 0.0))).astype(BF16)
        vx[hh, :, 0:HEAD_DIM] = v_ref[:, hs]
        vx[hh, :, HEAD_DIM:] = jnp.ones((seq, HEAD_DIM), BF16)
    lane = lax.broadcasted_iota(I32, (TQ, HEAD_DIM), 1)
    bias_cols = jnp.where(lane < N_SPLIT, 1.0, 0.0).astype(BF16)
    row = lax.broadcasted_iota(I32, (TQ, TQ), 0)
    col = lax.broadcasted_iota(I32, (TQ, TQ), 1)
    causal = row >= col
    for qi in range(seq // TQ):
        for hh in range(HEADS_PER_STEP):
            hs = slice(hh * HEAD_DIM, (hh + 1) * HEAD_DIM)
            k0 = qi * TQ
            q = jnp.concatenate([q_ref[k0:k0 + TQ, hs], bias_cols], axis=1)
            s_d = jnp.dot(q, kx[hh, :, k0:k0 + TQ], preferred_element_type=F32)
            s_d = jnp.where(causal, s_d, -jnp.inf)
            m = jnp.max(s_d, axis=-1, keepdims=True)
            if qi > 0:
                s_o = jnp.dot(q, kx[hh, :, 0:k0], preferred_element_type=F32)
                m = jnp.maximum(m, jnp.max(s_o, axis=-1, keepdims=True))
            acc = jnp.dot(jnp.exp2(s_d - m).astype(BF16), vx[hh, k0:k0 + TQ, :],
                          preferred_element_type=F32)
            if qi > 0:
                acc = acc + jnp.dot(jnp.exp2(s_o - m).astype(BF16), vx[hh, 0:k0, :],
                                    preferred_element_type=F32)
            inv_l = 1.0 / acc[:, HEAD_DIM:HEAD_DIM + 1]
            o_ref[k0:k0 + TQ, hs] = (acc[:, 0:HEAD_DIM] * inv_l).astype(BF16)


def _attention(q, kt, va, negc, batch, seq):
    t = q.shape[0]
    width = HEADS_PER_STEP * HEAD_DIM
    blk = pl.BlockSpec((seq, width), lambda b, g: (b, g))
    return pl.pallas_call(
        _attn_kernel,
        out_shape=jax.ShapeDtypeStruct((t, FOX_WIDTH), BF16),
        grid=(batch, FOX_HEADS // HEADS_PER_STEP),
        in_specs=[blk,
                  pl.BlockSpec((width, seq), lambda b, g: (g, b)),
                  blk,
                  pl.BlockSpec((1, HEADS_PER_STEP, 1, seq), lambda b, g: (b, g, 0, 0))],
        out_specs=blk,
        scratch_shapes=[pltpu.VMEM((HEADS_PER_STEP, 2 * HEAD_DIM, seq), BF16),
                        pltpu.VMEM((HEADS_PER_STEP, seq, 2 * HEAD_DIM), BF16)],
        compiler_params=pltpu.CompilerParams(
            dimension_semantics=("arbitrary", "arbitrary"), vmem_limit_bytes=VMEM_LIMIT),
        name="fox_attention",
    )(q, kt, va, negc)


def _mix_kernel(as_ref, af_ref, g_ref, x_ref, wps_ref, wpf_ref, wo_ref, g2_ref, wr_ref, br_ref,
                x1_ref, lg_ref):
    ys = jnp.dot(as_ref[...], wps_ref[...], preferred_element_type=F32)
    yf = jnp.dot(af_ref[...], wpf_ref[...], preferred_element_type=F32)
    m = (g_ref[:, :D_MODEL].astype(F32) * ys + g_ref[:, D_MODEL:].astype(F32) * yf).astype(BF16)
    x1 = x_ref[...] + jnp.dot(m, wo_ref[...], preferred_element_type=F32)
    x1_ref[...] = x1
    ms = jnp.mean(x1 * x1, axis=-1, keepdims=True)
    h2 = ((x1 * lax.rsqrt(ms + EPS)) * g2_ref[...]).astype(BF16)
    lg_ref[...] = jnp.dot(h2, wr_ref[...], preferred_element_type=F32) + br_ref[...]


def _mix(a_sgu, a_fox, gates, x2, wps, wpf, wo, g2, w_r, b_r):
    t = x2.shape[0]
    row = lambda i: (i, 0)
    const = lambda i: (0, 0)
    resident = functools.partial(pl.BlockSpec, index_map=const, pipeline_mode=pl.Buffered(1))
    return pl.pallas_call(
        _mix_kernel,
        out_shape=(jax.ShapeDtypeStruct((t, D_MODEL), F32),
                   jax.ShapeDtypeStruct((t, LANES), F32)),
        grid=(t // TM_MIX,),
        in_specs=[
            pl.BlockSpec((TM_MIX, SGU_WIDTH), row),
            pl.BlockSpec((TM_MIX, FOX_WIDTH), row),
            pl.BlockSpec((TM_MIX, 2 * D_MODEL), row),
            pl.BlockSpec((TM_MIX, D_MODEL), row),
            resident((SGU_WIDTH, D_MODEL)),
            resident((FOX_WIDTH, D_MODEL)),
            resident((D_MODEL, D_MODEL)),
            pl.BlockSpec((1, D_MODEL), const),
            resident((D_MODEL, LANES)),
            pl.BlockSpec((1, LANES), const),
        ],
        out_specs=(pl.BlockSpec((TM_MIX, D_MODEL), row), pl.BlockSpec((TM_MIX, LANES), row)),
        compiler_params=pltpu.CompilerParams(
            dimension_semantics=("arbitrary",), vmem_limit_bytes=VMEM_LIMIT),
        name="mix",
    )(a_sgu, a_fox, gates, x2, wps, wpf, wo, g2, w_r, b_r)


def _route_kernel(lg_ref, dt_ref, mf_ref, bm_ref, em_ref, bt_ref, mi_ref, inv_ref):
    t = lg_ref.shape[0]
    n_chunks = t // RCH
    lane_i = lax.broadcasted_iota(I32, (RCH, LANES), 1)
    lane = lane_i.astype(F32)
    lane_grp = ((lane_i - N_GROUPS) >> EXPERT_GROUP_SHIFT).astype(F32)
    is_grp = lane_i < N_GROUPS
    is_exp = (lane_i >= N_GROUPS) & (lane_i < N_GROUPS + N_EXPERTS)
    r_i = lax.broadcasted_iota(I32, (RCH, RCH), 0)
    c_i = lax.broadcasted_iota(I32, (RCH, RCH), 1)
    strict_lower = jnp.where(r_i > c_i, 1.0, 0.0).astype(BF16)
    neg_inf = -jnp.inf

    def first_max(vals):
        vmax = jnp.max(vals, axis=-1, keepdims=True)
        idx = jnp.min(jnp.where(vals == vmax, lane, float(LANES)), axis=-1, keepdims=True)
        return vmax, idx

    def pick(table, idx):
        return jnp.sum(jnp.where(lane == idx, table, 0.0), axis=-1, keepdims=True)

    def pack(cols):
        out = jnp.zeros((RCH, LANES), F32)
        for n, c in enumerate(cols):
            out = jnp.where(lane_i == n, c, out)
        return out

    def pass1(ci, counts):
        r0 = pl.multiple_of(ci * RCH, RCH)
        lg = lg_ref[pl.ds(r0, RCH), :]
        gmax, grp = first_max(jnp.where(is_grp, lg, neg_inf))
        p_grp = 1.0 / jnp.sum(jnp.where(is_grp, jnp.exp(lg - gmax), 0.0), axis=-1, keepdims=True)
        el = jnp.where(is_exp & (lane_grp == grp), lg, neg_inf)
        v1, i1 = first_max(el)
        v2, i2 = first_max(jnp.where(lane == i1, neg_inf, el))
        e21 = jnp.exp(v2 - v1)
        w1 = p_grp / (1.0 + e21)
        w2 = p_grp * e21 / (1.0 + e21)
        e1 = i1 - float(N_GROUPS)
        e2 = i2 - float(N_GROUPS)
        hot = jnp.where((lane == e1) | (lane == e2), 1.0, 0.0)
        before = jnp.dot(strict_lower, hot.astype(BF16), preferred_element_type=F32) + counts
        mi_ref[pl.ds(r0, RCH), :] = pack([e1, e2, pick(before, e1), pick(before, e2)]).astype(I32)
        mf_ref[pl.ds(r0, RCH), :] = pack([w1, w2])
        return counts + jnp.sum(hot, axis=0, keepdims=True)

    counts = lax.fori_loop(0, n_chunks, pass1, jnp.zeros((1, LANES), F32))

    nblk = jnp.floor((counts + float(EXPERT_BLOCK - 1)) * (1.0 / EXPERT_BLOCK))
    u_r = lax.broadcasted_iota(I32, (LANES, LANES), 0)
    u_c = lax.broadcasted_iota(I32, (LANES, LANES), 1)
    strict_upper = jnp.where(u_r < u_c, 1.0, 0.0).astype(BF16)
    bstart = jnp.dot(jnp.broadcast_to(nblk, (SUBLANES, LANES)).astype(BF16), strict_upper,
                     preferred_element_type=F32)[0:1, :]
    bend = bstart + nblk

    def pass2(ci, _):
        r0 = pl.multiple_of(ci * RCH, RCH)
        mi = mi_ref[pl.ds(r0, RCH), :].astype(F32)
        cols = [jnp.sum(jnp.where(lane_i == n, mi, 0.0), axis=-1, keepdims=True) for n in range(4)]
        d1 = pick(bstart, cols[0]) * float(EXPERT_BLOCK) + cols[2]
        d2 = pick(bstart, cols[1]) * float(EXPERT_BLOCK) + cols[3]
        d_t = pack([d1, d2]).T
        dt_ref[ci] = d_t[0:SUBLANES, :].astype(I32)
        tok = (lax.broadcasted_iota(I32, (RCH, 1), 0) + r0).astype(F32)
        tok_hi = jnp.floor(tok * (1.0 / EXPERT_BLOCK))
        tok_lo = tok - tok_hi * float(EXPERT_BLOCK)
        for n, d in enumerate((d1, d2)):
            d_blk = jnp.floor(d * (1.0 / EXPERT_BLOCK))
            hit = lane == (d - d_blk * float(EXPERT_BLOCK))
            blk_row = jnp.floor(d_t[n:n + 1, :] * (1.0 / EXPERT_BLOCK))
            sel = jnp.where(inv_blk == blk_row, 1.0, 0.0).astype(BF16)
            inv_ref[0] += jnp.dot(sel, jnp.where(hit, tok_hi, 0.0).astype(BF16),
                                  preferred_element_type=F32)
            inv_ref[1] += jnp.dot(sel, jnp.where(hit, tok_lo, 0.0).astype(BF16),
                                  preferred_element_type=F32)
        return 0

    nb_rows = bm_ref.shape[0]
    inv_blk = lax.broadcasted_iota(I32, (nb_rows, RCH), 0).astype(F32)
    inv_ref[...] = jnp.zeros_like(inv_ref)
    lax.fori_loop(0, n_chunks, pass2, 0)
    bt_ref[...] = (inv_ref[0] * float(EXPERT_BLOCK) + inv_ref[1]).astype(I32)

    b_col = lax.broadcasted_iota(I32, (nb_rows, LANES), 0).astype(F32)
    b_lane = lax.broadcasted_iota(I32, (nb_rows, LANES), 1)
    done = jnp.where((bend <= b_col) & (b_lane < N_EXPERTS), 1.0, 0.0)
    blk_e = jnp.minimum(jnp.sum(done, axis=-1, keepdims=True), float(N_EXPERTS - 1))
    n_used = jnp.sum(jnp.where(b_lane[0:1, :] == N_EXPERTS - 1, bend, 0.0), axis=-1, keepdims=True)
    bm_ref[...] = jnp.where(b_lane == 0, blk_e, jnp.where(b_lane == 1, n_used, 0.0)).astype(I32)

    em_ref[...] = jnp.broadcast_to(nblk, (SUBLANES, LANES)).astype(I32)


def _route(logits, nb_rows):
    t = logits.shape[0]
    full = lambda shape: pl.BlockSpec(shape, lambda: (0,) * len(shape))
    return pl.pallas_call(
        _route_kernel,
        out_shape=(jax.ShapeDtypeStruct((t // RCH, SUBLANES, RCH), I32),
                   jax.ShapeDtypeStruct((t, LANES), F32),
                   jax.ShapeDtypeStruct((nb_rows, LANES), I32),
                   jax.ShapeDtypeStruct((SUBLANES, LANES), I32),
                   jax.ShapeDtypeStruct((nb_rows, LANES), I32)),
        in_specs=[full((t, LANES))],
        out_specs=(full((t // RCH, SUBLANES, RCH)), full((t, LANES)), full((nb_rows, LANES)),
                   full((SUBLANES, LANES)), full((nb_rows, LANES))),
        scratch_shapes=[pltpu.VMEM((t, LANES), I32), pltpu.VMEM((2, nb_rows, LANES), F32)],
        compiler_params=pltpu.CompilerParams(vmem_limit_bytes=VMEM_LIMIT),
        name="route",
    )(logits)


GATHER_GROUPS = 4
GATHER_AHEAD = 4
WEIGHT_SLOTS = 3


def _experts_kernel(blk_e_ref, nused_ref, nblk_ref, tok_ref, x_hbm, g2_ref, wg_hbm, wu_hbm, wd_hbm,
                    y_ref, xg, wg_f, wu_f, wd_f, wg_s, wu_s, wd_s, slot_ref, sem, gsem):
    b = pl.program_id(0)
    n_used = nused_ref[0]
    used = b < n_used
    e = blk_e_ref[b]
    new_expert = (b == 0) | (e != blk_e_ref[jnp.maximum(b - 1, 0)])
    n_slots = GATHER_AHEAD + 1
    xslot = b % n_slots

    def gather_rows(block, slot, lo, hi):
        row0 = jnp.minimum(block, n_used - 1) * EXPERT_BLOCK
        for r in range(lo, hi):
            pltpu.make_async_copy(x_hbm.at[pl.ds(tok_ref[row0 + r], 1), :],
                                  xg.at[slot, pl.ds(r, 1), :], gsem.at[slot]).start()

    def gather_wait(slot):
        pltpu.make_async_copy(x_hbm.at[pl.ds(0, EXPERT_BLOCK), :], xg.at[slot], gsem.at[slot]).wait()

    @pl.when(b == 0)
    def _():
        for k in range(GATHER_AHEAD):
            gather_rows(k, k, 0, EXPERT_BLOCK)

    def weight_copies(expert, slot):
        return (pltpu.make_async_copy(wg_hbm.at[expert], wg_f.at[slot], sem.at[slot]),
                pltpu.make_async_copy(wu_hbm.at[expert], wu_f.at[slot], sem.at[slot]),
                pltpu.make_async_copy(wd_hbm.at[expert], wd_f.at[slot], sem.at[slot]))

    def block_expert(blk):
        return blk_e_ref[jnp.minimum(blk, n_used - 1)]

    b_next = b + nblk_ref[e]
    b_next2 = b_next + nblk_ref[block_expert(b_next)]

    @pl.when(b == 0)
    def _():
        slot_ref[0] = 0
        for cp in weight_copies(e, 0):
            cp.start()

        @pl.when(b_next < n_used)
        def _():
            for cp in weight_copies(block_expert(b_next), 1):
                cp.start(priority=PREFETCH_PRIORITY)

    @pl.when(used & new_expert)
    def _():
        slot = slot_ref[0]

        @pl.when(b_next2 < n_used)
        def _():
            for cp in weight_copies(block_expert(b_next2), (slot + 2) % WEIGHT_SLOTS):
                cp.start(priority=PREFETCH_PRIORITY)

        for cp in weight_copies(e, slot):
            cp.wait()
        wg_s[...] = wg_f[slot].astype(BF16)
        wu_s[...] = wu_f[slot].astype(BF16)
        wd_s[...] = wd_f[slot].astype(BF16)
        slot_ref[0] = (slot + 1) % WEIGHT_SLOTS

    @pl.when(used)
    def _():
        gather_wait(xslot)
        ahead = b + GATHER_AHEAD
        aslot = ahead % n_slots
        per_group = EXPERT_BLOCK // GATHER_GROUPS
        x = xg[xslot]
        ms = jnp.mean(x * x, axis=-1, keepdims=True)
        h = ((x * lax.rsqrt(ms + EPS)) * g2_ref[...]).astype(BF16)
        gather_rows(ahead, aslot, 0, per_group)
        a = jnp.dot(h, wg_s[...], preferred_element_type=F32)
        gather_rows(ahead, aslot, per_group, 2 * per_group)
        u = jnp.dot(h, wu_s[...], preferred_element_type=F32)
        gather_rows(ahead, aslot, 2 * per_group, 3 * per_group)
        mid = ((a * _sigmoid(a)) * u).astype(BF16)
        gather_rows(ahead, aslot, 3 * per_group, EXPERT_BLOCK)
        y_ref[...] = jnp.dot(mid, wd_s[...], preferred_element_type=F32)

    @pl.when(b == n_used - 1)
    def _():
        for k in range(1, GATHER_AHEAD + 1):
            gather_wait((b + k) % n_slots)

    @pl.when(jnp.logical_not(used))
    def _():
        y_ref[...] = jnp.zeros_like(y_ref)


def _experts(blk_e, n_used, nblk, row_tok, x1, g2, w_g, w_u, w_d):
    n_rows = row_tok.shape[0]
    nb = n_rows // EXPERT_BLOCK
    hbm = pl.BlockSpec(memory_space=pl.ANY)
    grid_spec = pltpu.PrefetchScalarGridSpec(
        num_scalar_prefetch=4,
        grid=(nb,),
        in_specs=[
            hbm,
            pl.BlockSpec((1, D_MODEL), lambda b, *_: (0, 0)),
            hbm, hbm, hbm,
        ],
        out_specs=pl.BlockSpec((EXPERT_BLOCK, D_MODEL), lambda b, *_: (b, 0)),
        scratch_shapes=[pltpu.VMEM((GATHER_AHEAD + 1, EXPERT_BLOCK, D_MODEL), F32),
                        pltpu.VMEM((WEIGHT_SLOTS, D_MODEL, D_EXPERT), F32),
                        pltpu.VMEM((WEIGHT_SLOTS, D_MODEL, D_EXPERT), F32),
                        pltpu.VMEM((WEIGHT_SLOTS, D_EXPERT, D_MODEL), F32),
                        pltpu.VMEM((D_MODEL, D_EXPERT), BF16),
                        pltpu.VMEM((D_MODEL, D_EXPERT), BF16),
                        pltpu.VMEM((D_EXPERT, D_MODEL), BF16),
                        pltpu.SMEM((1,), I32),
                        pltpu.SemaphoreType.DMA((WEIGHT_SLOTS,)),
                        pltpu.SemaphoreType.DMA((GATHER_AHEAD + 1,))],
    )
    return pl.pallas_call(
        _experts_kernel,
        out_shape=jax.ShapeDtypeStruct((n_rows, D_MODEL), F32),
        grid_spec=grid_spec,
        compiler_params=pltpu.CompilerParams(
            dimension_semantics=("arbitrary",), vmem_limit_bytes=VMEM_LIMIT),
        name="experts",
    )(blk_e, n_used, nblk, row_tok, x1, g2, w_g, w_u, w_d)


def _combine_kernel(d1_ref, d2_ref, x1_ref, w_ref, y_ref, o_ref, gbuf, sem):
    i = pl.program_id(0)
    n = pl.num_programs(0)
    tm = x1_ref.shape[0]

    def request_row(base, slot, r):
        pltpu.make_async_copy(y_ref.at[pl.ds(d1_ref[base + r], 1), :],
                              gbuf.at[slot, pl.ds(r, 1), :], sem.at[slot]).start()
        pltpu.make_async_copy(y_ref.at[pl.ds(d2_ref[base + r], 1), :],
                              gbuf.at[slot, pl.ds(tm + r, 1), :], sem.at[slot]).start()

    def issue(tile, slot, lo, hi):
        base = jnp.minimum(tile, n - 1) * tm
        for r in range(lo, hi):
            request_row(base, slot, r)

    def tile_wait(slot):
        pltpu.make_async_copy(y_ref.at[pl.ds(0, 2 * tm), :], gbuf.at[slot], sem.at[slot]).wait()

    slot = i % 2

    @pl.when(i == 0)
    def _():
        def first_tile(r, _):
            request_row(0, 0, r)
            return 0

        lax.fori_loop(0, tm, first_tile, 0, unroll=SUBLANES)

    tile_wait(slot)
    for c in range(tm // COMBINE_ROWS):
        rs = slice(c * COMBINE_ROWS, (c + 1) * COMBINE_ROWS)
        rs2 = slice(tm + c * COMBINE_ROWS, tm + (c + 1) * COMBINE_ROWS)
        w = w_ref[rs, :]
        o_ref[rs, :] = x1_ref[rs, :] + (w[:, 0:1] * gbuf[slot, rs, :] + w[:, 1:2] * gbuf[slot, rs2, :])
        issue(i + 1, 1 - slot, c * COMBINE_ROWS, (c + 1) * COMBINE_ROWS)

    @pl.when(i == n - 1)
    def _():
        tile_wait(1 - slot)


def _combine(dest1, dest2, x1, w, yb):
    t = x1.shape[0]
    grid_spec = pltpu.PrefetchScalarGridSpec(
        num_scalar_prefetch=2,
        grid=(t // TM_CMB,),
        in_specs=[
            pl.BlockSpec((TM_CMB, D_MODEL), lambda i, a, b: (i, 0)),
            pl.BlockSpec((TM_CMB, LANES), lambda i, a, b: (i, 0)),
            pl.BlockSpec(memory_space=pl.ANY),
        ],
        out_specs=pl.BlockSpec((TM_CMB, D_MODEL), lambda i, a, b: (i, 0)),
        scratch_shapes=[pltpu.VMEM((2, 2 * TM_CMB, D_MODEL), F32),
                        pltpu.SemaphoreType.DMA((2,))],
    )
    return pl.pallas_call(
        _combine_kernel,
        out_shape=jax.ShapeDtypeStruct((t, D_MODEL), F32),
        grid_spec=grid_spec,
        compiler_params=pltpu.CompilerParams(
            dimension_semantics=("arbitrary",), vmem_limit_bytes=VMEM_LIMIT),
        name="combine",
    )(dest1, dest2, x1, w, yb)


def kernel(x, norm1_g, w_in, b_gate, b_forget, sgu_ln_g, sgu_ln_b, w_spatial, b_spatial, q_norm_g, k_norm_g, w_proj_sgu, w_proj_fox, w_out, norm2_g, w_router_group, b_router_group, w_router_expert, b_router_expert, w_expert_gate, w_expert_up, w_expert_down):
    batch, seq, d = x.shape
    t = batch * seq
    l = 0
    x2 = x.reshape(t, d)

    wt = jnp.swapaxes(w_in[l], 0, 1)
    off_f = N_MAIN_SEC * SEC
    w_f = jnp.pad(wt[off_f:off_f + FOX_HEADS].T, ((0, 0), (0, LANES - FOX_HEADS))).astype(BF16)
    b_f = jnp.pad(b_forget[l], (0, LANES - FOX_HEADS)).reshape(1, LANES)
    n_r = N_GROUPS + N_EXPERTS
    w_r = jnp.pad(jnp.concatenate([w_router_group[l], w_router_expert[l]], axis=1),
                  ((0, 0), (0, LANES - n_r))).astype(BF16)
    b_r = jnp.pad(jnp.concatenate([b_router_group[l], b_router_expert[l]]),
                  (0, LANES - n_r)).reshape(1, LANES)

    h, lf = _norm1(x2, norm1_g[l].reshape(1, d), w_f, b_f)
    a_sgu, q, kt, va, gates = _inproj(
        h, wt, b_gate[l].reshape(1, 2 * d),
        sgu_ln_g[l].reshape(1, SGU_WIDTH), sgu_ln_b[l].reshape(1, SGU_WIDTH),
        w_spatial[l], b_spatial[l].T, q_norm_g[l].reshape(1, HEAD_DIM),
        k_norm_g[l].reshape(1, HEAD_DIM))
    negc = _forget_cumsum(lf, batch, seq).reshape(batch, FOX_HEADS, 1, seq)
    a_fox = _attention(q, kt, va, negc, batch, seq)
    x1, logits = _mix(a_sgu, a_fox, gates, x2, w_proj_sgu[l].astype(BF16),
                      w_proj_fox[l].astype(BF16), w_out[l].astype(BF16),
                      norm2_g[l].reshape(1, d), w_r, b_r)

    n_assign = 2 * t
    n_rows = n_assign + N_EXPERTS * EXPERT_BLOCK
    nb = n_rows // EXPERT_BLOCK
    nb_rows = -(-nb // SUBLANES) * SUBLANES
    dest_t, meta_f, bmeta, nblk, row_tok = _route(logits, nb_rows)
    dest1, dest2 = dest_t[:, 0, :].reshape(t), dest_t[:, 1, :].reshape(t)
    n_used = bmeta[0:1, 1]
    yb = _experts(bmeta[:nb, 0], n_used, nblk[0, :N_EXPERTS], row_tok[:nb].reshape(n_rows), x1,
                  norm2_g[l].reshape(1, d), w_expert_gate[l], w_expert_up[l], w_expert_down[l])
    out = _combine(dest1, dest2, x1, meta_f, yb)
    return out.reshape(batch, seq, d)
```

```python
import functools

import jax
import jax.numpy as jnp
from jax import lax
from jax.experimental import pallas as pl
from jax.experimental.pallas import tpu as pltpu

F32 = jnp.float32
BF16 = jnp.bfloat16
I32 = jnp.int32

D_MODEL = 2048
CHUNK = 128
SGU_GROUPS = 8
SGU_WIDTH = 1024
FOX_HEADS = 8
HEAD_DIM = 128
FOX_WIDTH = 1024
N_GROUPS = 4
EXPERTS_PER_GROUP = 8
N_EXPERTS = 32
D_EXPERT = 512
EXPERT_BLOCK = 256
EPS = 1e-6

LANES = 128
SUBLANES = 8
PREFETCH_PRIORITY = 1
VMEM_LIMIT = 56 * 1024 * 1024

SEC = 1024
N_MAIN_SEC = 5
N_GATE_SEC = 4
TM_NORM = 1024
TM_IN = 512
TQ = 256
HEADS_PER_STEP = 4
TM_MIX = 256
RCH = 512
TM_CMB = 512
COMBINE_ROWS = 64
N_SPLIT = 3
EXPERT_GROUP_SHIFT = EXPERTS_PER_GROUP.bit_length() - 1
LOG2E = 1.4426950408889634
Q_SCALE = HEAD_DIM ** -0.5 * LOG2E


def _sigmoid(x):
    return 1.0 / (1.0 + jnp.exp(-x))


def _log_sigmoid(x):
    return jnp.minimum(x, 0.0) - jnp.log1p(jnp.exp(-jnp.abs(x)))


def _norm1_kernel(x_ref, g1_ref, wf_ref, bf_ref, h_ref, lf_ref):
    x = x_ref[...]
    ms = jnp.mean(x * x, axis=-1, keepdims=True)
    hb = ((x * lax.rsqrt(ms + EPS)) * g1_ref[...]).astype(BF16)
    h_ref[...] = hb
    f = jnp.dot(hb, wf_ref[...], preferred_element_type=F32) + bf_ref[...]
    lf_ref[...] = _log_sigmoid(f)


def _norm1(x2, g1, w_f, b_f):
    t = x2.shape[0]
    row = lambda i: (i, 0)
    const = lambda i: (0, 0)
    return pl.pallas_call(
        _norm1_kernel,
        out_shape=(jax.ShapeDtypeStruct((t, D_MODEL), BF16),
                   jax.ShapeDtypeStruct((t, LANES), F32)),
        grid=(t // TM_NORM,),
        in_specs=[pl.BlockSpec((TM_NORM, D_MODEL), row),
                  pl.BlockSpec((1, D_MODEL), const),
                  pl.BlockSpec((D_MODEL, LANES), const),
                  pl.BlockSpec((1, LANES), const)],
        out_specs=(pl.BlockSpec((TM_NORM, D_MODEL), row), pl.BlockSpec((TM_NORM, LANES), row)),
        compiler_params=pltpu.CompilerParams(
            dimension_semantics=("arbitrary",), vmem_limit_bytes=VMEM_LIMIT),
        name="norm1",
    )(x2, g1, w_f, b_f)


N_SECTIONS = 1 + 3 + N_GATE_SEC
N_CHUNKS = N_MAIN_SEC + N_GATE_SEC
GATE_SECTION0 = 4
WCONV_ROWS = 256


def _inproj_kernel(h_ref, wt_hbm, bg_ref, lng_ref, lnb_ref, wsp_ref, bsp_ref, qg_ref, kg_ref,
                   asgu_ref, q_ref, kt_ref, va_ref, gates_ref,
                   stage, wb, sem):
    s = pl.program_id(0)
    i = pl.program_id(1)
    tm = h_ref.shape[0]

    def chunk_copy(c, slot):
        r0 = pl.multiple_of(c * SEC + jnp.where(c >= N_MAIN_SEC, FOX_HEADS, 0), 8)
        return pltpu.make_async_copy(wt_hbm.at[pl.ds(r0, SEC), :], stage.at[slot], sem.at[slot])

    def convert(slot, k):
        for p in range(SEC // WCONV_ROWS):
            rs = slice(p * WCONV_ROWS, (p + 1) * WCONV_ROWS)
            wb[k, :, rs] = stage[slot, rs, :].T.astype(BF16)

    @pl.when((s == 0) & (i == 0))
    def _():
        chunk_copy(0, 0).start()
        chunk_copy(1, 1).start()
        chunk_copy(0, 0).wait()
        convert(0, 0)
        chunk_copy(2, 0).start(priority=PREFETCH_PRIORITY)
        chunk_copy(1, 1).wait()
        convert(1, 1)

    @pl.when((s > 0) & (i == 0))
    def _():
        c = s + 1
        slot = c % 2
        chunk_copy(c, slot).wait()

        @pl.when(c + 1 < N_CHUNKS)
        def _():
            chunk_copy(c + 1, 1 - slot).start(priority=PREFETCH_PRIORITY)

        convert(slot, 0)

    def section(k=0):
        return jnp.dot(h_ref[...], wb[k], preferred_element_type=F32)

    @pl.when(s == 0)
    def _():
        u = jax.nn.gelu(section(0))
        v = jax.nn.gelu(section(1))
        mu = jnp.mean(v, axis=-1, keepdims=True)
        vc = v - mu
        var = jnp.mean(vc * vc, axis=-1, keepdims=True)
        vn = ((vc * lax.rsqrt(var + EPS)) * lng_ref[...] + lnb_ref[...]).astype(BF16)
        row = lax.broadcasted_iota(I32, (CHUNK, CHUNK), 0)
        col = lax.broadcasted_iota(I32, (CHUNK, CHUNK), 1)
        causal = row >= col
        for g in range(SGU_GROUPS):
            wg = jnp.where(causal, wsp_ref[g], 0.0).astype(BF16)
            bcol = bsp_ref[:, g:g + 1]
            gs = slice(g * LANES, (g + 1) * LANES)
            n_ch = tm // CHUNK
            rhs = jnp.concatenate([vn[c * CHUNK:(c + 1) * CHUNK, gs] for c in range(n_ch)], axis=1)
            sg = jnp.dot(wg, rhs, preferred_element_type=F32) + bcol
            for c in range(n_ch):
                rs = slice(c * CHUNK, (c + 1) * CHUNK)
                asgu_ref[rs, gs] = (u[rs, gs] * sg[:, c * LANES:(c + 1) * LANES]).astype(BF16)

    def _head_norm(z, h, gain_ref):
        zh = z[:, h * HEAD_DIM:(h + 1) * HEAD_DIM]
        ms = jnp.mean(zh * zh, axis=-1, keepdims=True)
        return (zh * lax.rsqrt(ms + EPS)) * gain_ref[...]

    @pl.when(s == 1)
    def _():
        z = section()
        for h in range(FOX_HEADS):
            q_ref[:, h * HEAD_DIM:(h + 1) * HEAD_DIM] = (
                _head_norm(z, h, qg_ref) * Q_SCALE).astype(BF16)

    @pl.when(s == 2)
    def _():
        z = section()
        for h in range(FOX_HEADS):
            kt_ref[h * HEAD_DIM:(h + 1) * HEAD_DIM, :] = _head_norm(z, h, kg_ref).T.astype(BF16)

    @pl.when(s == 3)
    def _():
        va_ref[...] = section().astype(BF16)

    @pl.when(s >= GATE_SECTION0)
    def _():
        gates_ref[...] = (0.5 * jnp.tanh(0.5 * (section() + bg_ref[...])) + 0.5).astype(BF16)


def _inproj(h, wt, b_gate, ln_g, ln_b, w_sp, b_sp_t, q_g, k_g):
    t = h.shape[0]
    n_i = t // TM_IN

    def active(sec):
        return lambda s, i: jnp.where(s < sec, 0, jnp.where(s > sec, n_i - 1, i))

    def gate_blk(s, i):
        g = jnp.clip(s - GATE_SECTION0, 0, N_GATE_SEC - 1)
        return (jnp.where(s < GATE_SECTION0, 0, i), g)

    const2 = lambda s, i: (0, 0)
    out_shape = (
        jax.ShapeDtypeStruct((t, SGU_WIDTH), BF16),
        jax.ShapeDtypeStruct((t, FOX_WIDTH), BF16),
        jax.ShapeDtypeStruct((FOX_WIDTH, t), BF16),
        jax.ShapeDtypeStruct((t, FOX_WIDTH), BF16),
        jax.ShapeDtypeStruct((t, 2 * D_MODEL), BF16),
    )
    return pl.pallas_call(
        _inproj_kernel,
        out_shape=out_shape,
        grid=(N_SECTIONS, n_i),
        in_specs=[
            pl.BlockSpec((TM_IN, D_MODEL), lambda s, i: (i, 0)),
            pl.BlockSpec(memory_space=pl.ANY),
            pl.BlockSpec((1, SEC), lambda s, i: (0, jnp.clip(s - GATE_SECTION0, 0, N_GATE_SEC - 1))),
            pl.BlockSpec((1, SGU_WIDTH), const2),
            pl.BlockSpec((1, SGU_WIDTH), const2),
            pl.BlockSpec((SGU_GROUPS, CHUNK, CHUNK), lambda s, i: (0, 0, 0)),
            pl.BlockSpec((CHUNK, SGU_GROUPS), const2),
            pl.BlockSpec((1, HEAD_DIM), const2),
            pl.BlockSpec((1, HEAD_DIM), const2),
        ],
        out_specs=(
            pl.BlockSpec((TM_IN, SGU_WIDTH), lambda s, i: (active(0)(s, i), 0)),
            pl.BlockSpec((TM_IN, FOX_WIDTH), lambda s, i: (active(1)(s, i), 0)),
            pl.BlockSpec((FOX_WIDTH, TM_IN), lambda s, i: (0, active(2)(s, i))),
            pl.BlockSpec((TM_IN, FOX_WIDTH), lambda s, i: (active(3)(s, i), 0)),
            pl.BlockSpec((TM_IN, SEC), gate_blk),
        ),
        scratch_shapes=[pltpu.VMEM((2, SEC, D_MODEL), F32),
                        pltpu.VMEM((2, D_MODEL, SEC), BF16),
                        pltpu.SemaphoreType.DMA((2,))],
        compiler_params=pltpu.CompilerParams(
            dimension_semantics=("arbitrary", "arbitrary"), vmem_limit_bytes=VMEM_LIMIT),
        name="inproj",
    )(h, wt, b_gate, ln_g, ln_b, w_sp, b_sp_t, q_g, k_g)


def _split3(x):
    x0 = x.astype(BF16)
    r1 = x - x0.astype(F32)
    x1 = r1.astype(BF16)
    r2 = r1 - x1.astype(F32)
    return x0, x1, r2.astype(BF16)


def _cumsum_kernel(lf_ref, negc_ref, c_s):
    seq = lf_ref.shape[0]
    row = lax.broadcasted_iota(I32, (CHUNK, CHUNK), 0)
    col = lax.broadcasted_iota(I32, (CHUNK, CHUNK), 1)
    tri = jnp.where(row >= col, 1.0, 0.0).astype(BF16)
    carry = jnp.zeros((1, LANES), F32)
    for r in range(seq // CHUNK):
        rs = slice(r * CHUNK, (r + 1) * CHUNK)
        x0, x1, x2 = _split3(lf_ref[rs, :])
        cs = (jnp.dot(tri, x0, preferred_element_type=F32)
              + jnp.dot(tri, x1, preferred_element_type=F32)
              + jnp.dot(tri, x2, preferred_element_type=F32)) + carry
        carry = cs[CHUNK - 1:CHUNK, :]
        c_s[rs, :] = cs
    ct = c_s[...].T
    negc_ref[0] = -ct[0:FOX_HEADS, :]


def _forget_cumsum(lf, batch, seq):
    return pl.pallas_call(
        _cumsum_kernel,
        out_shape=jax.ShapeDtypeStruct((batch, FOX_HEADS, seq), F32),
        grid=(batch,),
        in_specs=[pl.BlockSpec((seq, LANES), lambda b: (b, 0))],
        out_specs=pl.BlockSpec((1, FOX_HEADS, seq), lambda b: (b, 0, 0)),
        scratch_shapes=[pltpu.VMEM((seq, LANES), F32)],
        compiler_params=pltpu.CompilerParams(
            dimension_semantics=("arbitrary",), vmem_limit_bytes=VMEM_LIMIT),
        name="forget_cumsum",
    )(lf)


def _attn_kernel(q_ref, kt_ref, v_ref, negc_ref, o_ref, kx, vx):
    seq = q_ref.shape[0]
    sub = lax.broadcasted_iota(I32, (HEAD_DIM, seq), 0)
    for hh in range(HEADS_PER_STEP):
        hs = slice(hh * HEAD_DIM, (hh + 1) * HEAD_DIM)
        c0, c1, c2 = (c.astype(F32) for c in _split3(negc_ref[0, hh] * LOG2E))
        kx[hh, 0:HEAD_DIM, :] = kt_ref[hs, :]
        kx[hh, HEAD_DIM:, :] = jnp.where(
            sub == 0, c0, jnp.where(sub == 1, c1, jnp.where(sub == 2, c2, 0.0))).astype(BF16)
        vx[hh, :, 0:HEAD_DIM] = v_ref[:, hs]
        vx[hh, :, HEAD_DIM:] = jnp.ones((seq, HEAD_DIM), BF16)
    lane = lax.broadcasted_iota(I32, (TQ, HEAD_DIM), 1)
    bias_cols = jnp.where(lane < N_SPLIT, 1.0, 0.0).astype(BF16)
    row = lax.broadcasted_iota(I32, (TQ, TQ), 0)
    col = lax.broadcasted_iota(I32, (TQ, TQ), 1)
    causal = row >= col
    for qi in range(seq // TQ):
        for hh in range(HEADS_PER_STEP):
            hs = slice(hh * HEAD_DIM, (hh + 1) * HEAD_DIM)
            k0 = qi * TQ
            q = jnp.concatenate([q_ref[k0:k0 + TQ, hs], bias_cols], axis=1)
            s_d = jnp.dot(q, kx[hh, :, k0:k0 + TQ], preferred_element_type=F32)
            s_d = jnp.where(causal, s_d, -jnp.inf)
            m = jnp.max(s_d, axis=-1, keepdims=True)
            if qi > 0:
                s_o = jnp.dot(q, kx[hh, :, 0:k0], preferred_element_type=F32)
                m = jnp.maximum(m, jnp.max(s_o, axis=-1, keepdims=True))
            acc = jnp.dot(jnp.exp2(s_d - m).astype(BF16), vx[hh, k0:k0 + TQ, :],
                          preferred_element_type=F32)
            if qi > 0:
                acc = acc + jnp.dot(jnp.exp2(s_o - m).astype(BF16), vx[hh, 0:k0, :],
                                    preferred_element_type=F32)
            inv_l = 1.0 / acc[:, HEAD_DIM:HEAD_DIM + 1]
            o_ref[k0:k0 + TQ, hs] = (acc[:, 0:HEAD_DIM] * inv_l).astype(BF16)


def _attention(q, kt, va, negc, batch, seq):
    t = q.shape[0]
    width = HEADS_PER_STEP * HEAD_DIM
    blk = pl.BlockSpec((seq, width), lambda b, g: (b, g))
    return pl.pallas_call(
        _attn_kernel,
        out_shape=jax.ShapeDtypeStruct((t, FOX_WIDTH), BF16),
        grid=(batch, FOX_HEADS // HEADS_PER_STEP),
        in_specs=[blk,
                  pl.BlockSpec((width, seq), lambda b, g: (g, b)),
                  blk,
                  pl.BlockSpec((1, HEADS_PER_STEP, 1, seq), lambda b, g: (b, g, 0, 0))],
        out_specs=blk,
        scratch_shapes=[pltpu.VMEM((HEADS_PER_STEP, 2 * HEAD_DIM, seq), BF16),
                        pltpu.VMEM((HEADS_PER_STEP, seq, 2 * HEAD_DIM), BF16)],
        compiler_params=pltpu.CompilerParams(
            dimension_semantics=("arbitrary", "arbitrary"), vmem_limit_bytes=VMEM_LIMIT),
        name="fox_attention",
    )(q, kt, va, negc)


MIX_WROWS = 512


def _mix_kernel(as_ref, af_ref, g_ref, x_ref, wps_hbm, wpf_hbm, wo_hbm, g2_ref, wr_ref, br_ref,
                x1_ref, lg_ref, wps_ref, wpf_ref, wo_ref, stage, sem):
    @pl.when(pl.program_id(0) == 0)
    def _():
        pieces = [(src, dst, r0) for src, dst in ((wps_hbm, wps_ref), (wpf_hbm, wpf_ref),
                                                  (wo_hbm, wo_ref))
                  for r0 in range(0, dst.shape[0], MIX_WROWS)]

        def piece_copy(k):
            src, _, r0 = pieces[k]
            return pltpu.make_async_copy(src.at[pl.ds(r0, MIX_WROWS), :], stage.at[k % 2],
                                         sem.at[k % 2])

        piece_copy(0).start()
        for k, (_, dst, r0) in enumerate(pieces):
            if k + 1 < len(pieces):
                piece_copy(k + 1).start()
            piece_copy(k).wait()
            dst[r0:r0 + MIX_WROWS, :] = stage[k % 2].astype(BF16)

    ys = jnp.dot(as_ref[...], wps_ref[...], preferred_element_type=F32)
    yf = jnp.dot(af_ref[...], wpf_ref[...], preferred_element_type=F32)
    m = (g_ref[:, :D_MODEL].astype(F32) * ys + g_ref[:, D_MODEL:].astype(F32) * yf).astype(BF16)
    x1 = x_ref[...] + jnp.dot(m, wo_ref[...], preferred_element_type=F32)
    x1_ref[...] = x1
    ms = jnp.mean(x1 * x1, axis=-1, keepdims=True)
    h2 = ((x1 * lax.rsqrt(ms + EPS)) * g2_ref[...]).astype(BF16)
    lg_ref[...] = jnp.dot(h2, wr_ref[...], preferred_element_type=F32) + br_ref[...]


def _mix(a_sgu, a_fox, gates, x2, wps, wpf, wo, g2, w_r, b_r):
    t = x2.shape[0]
    row = lambda i: (i, 0)
    const = lambda i: (0, 0)
    resident = functools.partial(pl.BlockSpec, index_map=const, pipeline_mode=pl.Buffered(1))
    return pl.pallas_call(
        _mix_kernel,
        out_shape=(jax.ShapeDtypeStruct((t, D_MODEL), F32),
                   jax.ShapeDtypeStruct((t, LANES), F32)),
        grid=(t // TM_MIX,),
        in_specs=[
            pl.BlockSpec((TM_MIX, SGU_WIDTH), row),
            pl.BlockSpec((TM_MIX, FOX_WIDTH), row),
            pl.BlockSpec((TM_MIX, 2 * D_MODEL), row),
            pl.BlockSpec((TM_MIX, D_MODEL), row),
            pl.BlockSpec(memory_space=pl.ANY),
            pl.BlockSpec(memory_space=pl.ANY),
            pl.BlockSpec(memory_space=pl.ANY),
            pl.BlockSpec((1, D_MODEL), const),
            resident((D_MODEL, LANES)),
            pl.BlockSpec((1, LANES), const),
        ],
        out_specs=(pl.BlockSpec((TM_MIX, D_MODEL), row), pl.BlockSpec((TM_MIX, LANES), row)),
        scratch_shapes=[pltpu.VMEM((SGU_WIDTH, D_MODEL), BF16),
                        pltpu.VMEM((FOX_WIDTH, D_MODEL), BF16),
                        pltpu.VMEM((D_MODEL, D_MODEL), BF16),
                        pltpu.VMEM((2, MIX_WROWS, D_MODEL), F32),
                        pltpu.SemaphoreType.DMA((2,))],
        compiler_params=pltpu.CompilerParams(
            dimension_semantics=("arbitrary",), vmem_limit_bytes=VMEM_LIMIT),
        name="mix",
    )(a_sgu, a_fox, gates, x2, wps, wpf, wo, g2, w_r, b_r)


def _route_kernel(lg_ref, dt_ref, mf_ref, bm_ref, em_ref, bt_ref, mi_ref, inv_ref):
    t = lg_ref.shape[0]
    n_chunks = t // RCH
    lane_i = lax.broadcasted_iota(I32, (RCH, LANES), 1)
    lane = lane_i.astype(F32)
    lane_grp = ((lane_i - N_GROUPS) >> EXPERT_GROUP_SHIFT).astype(F32)
    is_grp = lane_i < N_GROUPS
    is_exp = (lane_i >= N_GROUPS) & (lane_i < N_GROUPS + N_EXPERTS)
    r_i = lax.broadcasted_iota(I32, (RCH, RCH), 0)
    c_i = lax.broadcasted_iota(I32, (RCH, RCH), 1)
    strict_lower = jnp.where(r_i > c_i, 1.0, 0.0).astype(BF16)
    neg_inf = -jnp.inf

    def first_max(vals):
        vmax = jnp.max(vals, axis=-1, keepdims=True)
        idx = jnp.min(jnp.where(vals == vmax, lane, float(LANES)), axis=-1, keepdims=True)
        return vmax, idx

    def pick(table, idx):
        return jnp.sum(jnp.where(lane == idx, table, 0.0), axis=-1, keepdims=True)

    def pack(cols):
        out = jnp.zeros((RCH, LANES), F32)
        for n, c in enumerate(cols):
            out = jnp.where(lane_i == n, c, out)
        return out

    def pass1(ci, counts):
        r0 = pl.multiple_of(ci * RCH, RCH)
        lg = lg_ref[pl.ds(r0, RCH), :]
        gmax, grp = first_max(jnp.where(is_grp, lg, neg_inf))
        p_grp = 1.0 / jnp.sum(jnp.where(is_grp, jnp.exp(lg - gmax), 0.0), axis=-1, keepdims=True)
        el = jnp.where(is_exp & (lane_grp == grp), lg, neg_inf)
        v1, i1 = first_max(el)
        v2, i2 = first_max(jnp.where(lane == i1, neg_inf, el))
        e21 = jnp.exp(v2 - v1)
        w1 = p_grp / (1.0 + e21)
        w2 = p_grp * e21 / (1.0 + e21)
        e1 = i1 - float(N_GROUPS)
        e2 = i2 - float(N_GROUPS)
        hot = jnp.where((lane == e1) | (lane == e2), 1.0, 0.0)
        before = jnp.dot(strict_lower, hot.astype(BF16), preferred_element_type=F32) + counts
        mi_ref[pl.ds(r0, RCH), :] = pack([e1, e2, pick(before, e1), pick(before, e2)]).astype(I32)
        mf_ref[pl.ds(r0, RCH), :] = pack([w1, w2])
        return counts + jnp.sum(hot, axis=0, keepdims=True)

    counts = lax.fori_loop(0, n_chunks, pass1, jnp.zeros((1, LANES), F32))

    nblk = jnp.floor((counts + float(EXPERT_BLOCK - 1)) * (1.0 / EXPERT_BLOCK))
    u_r = lax.broadcasted_iota(I32, (LANES, LANES), 0)
    u_c = lax.broadcasted_iota(I32, (LANES, LANES), 1)
    strict_upper = jnp.where(u_r < u_c, 1.0, 0.0).astype(BF16)
    bstart = jnp.dot(jnp.broadcast_to(nblk, (SUBLANES, LANES)).astype(BF16), strict_upper,
                     preferred_element_type=F32)[0:1, :]
    bend = bstart + nblk

    def pass2(ci, _):
        r0 = pl.multiple_of(ci * RCH, RCH)
        mi = mi_ref[pl.ds(r0, RCH), :].astype(F32)
        cols = [jnp.sum(jnp.where(lane_i == n, mi, 0.0), axis=-1, keepdims=True) for n in range(4)]
        d1 = pick(bstart, cols[0]) * float(EXPERT_BLOCK) + cols[2]
        d2 = pick(bstart, cols[1]) * float(EXPERT_BLOCK) + cols[3]
        d_t = pack([d1, d2]).T
        dt_ref[ci] = d_t[0:SUBLANES, :].astype(I32)
        tok = (lax.broadcasted_iota(I32, (RCH, 1), 0) + r0).astype(F32)
        tok_hi = jnp.floor(tok * (1.0 / LANES))
        tok_lo = tok - tok_hi * float(LANES)
        for n, d in enumerate((d1, d2)):
            d_hi = jnp.floor(d * (1.0 / LANES))
            hit = lane == (d - d_hi * float(LANES))
            blk_row = jnp.floor(d_t[n:n + 1, :] * (1.0 / LANES))
            sel = jnp.where(inv_blk == blk_row, 1.0, 0.0).astype(BF16)
            inv_ref[0] += jnp.dot(sel, jnp.where(hit, tok_hi, 0.0).astype(BF16),
                                  preferred_element_type=F32)
            inv_ref[1] += jnp.dot(sel, jnp.where(hit, tok_lo, 0.0).astype(BF16),
                                  preferred_element_type=F32)
        return 0

    nb_rows = bm_ref.shape[0]
    inv_blk = lax.broadcasted_iota(I32, (nb_rows, RCH), 0).astype(F32)
    inv_ref[...] = jnp.zeros_like(inv_ref)
    lax.fori_loop(0, n_chunks, pass2, 0)
    bt_ref[...] = (inv_ref[0] * float(LANES) + inv_ref[1]).astype(I32)

    b_col = lax.broadcasted_iota(I32, (nb_rows, LANES), 0).astype(F32)
    b_lane = lax.broadcasted_iota(I32, (nb_rows, LANES), 1)
    done = jnp.where((bend <= b_col) & (b_lane < N_EXPERTS), 1.0, 0.0)
    blk_e = jnp.minimum(jnp.sum(done, axis=-1, keepdims=True), float(N_EXPERTS - 1))
    n_used = jnp.sum(jnp.where(b_lane[0:1, :] == N_EXPERTS - 1, bend, 0.0), axis=-1, keepdims=True)
    bm_ref[...] = jnp.where(b_lane == 0, blk_e, jnp.where(b_lane == 1, n_used, 0.0)).astype(I32)

    em_ref[...] = jnp.broadcast_to(nblk, (SUBLANES, LANES)).astype(I32)


def _route(logits, nb_rows):
    t = logits.shape[0]
    full = lambda shape: pl.BlockSpec(shape, lambda: (0,) * len(shape))
    return pl.pallas_call(
        _route_kernel,
        out_shape=(jax.ShapeDtypeStruct((t // RCH, SUBLANES, RCH), I32),
                   jax.ShapeDtypeStruct((t, LANES), F32),
                   jax.ShapeDtypeStruct((nb_rows, LANES), I32),
                   jax.ShapeDtypeStruct((SUBLANES, LANES), I32),
                   jax.ShapeDtypeStruct((nb_rows, LANES), I32)),
        in_specs=[full((t, LANES))],
        out_specs=(full((t // RCH, SUBLANES, RCH)), full((t, LANES)), full((nb_rows, LANES)),
                   full((SUBLANES, LANES)), full((nb_rows, LANES))),
        scratch_shapes=[pltpu.VMEM((t, LANES), I32), pltpu.VMEM((2, nb_rows, LANES), F32)],
        compiler_params=pltpu.CompilerParams(vmem_limit_bytes=VMEM_LIMIT),
        name="route",
    )(logits)


GATHER_GROUPS = 4
GATHER_AHEAD = 2
WEIGHT_SLOTS = 3


def _experts_kernel(blk_e_ref, nused_ref, nblk_ref, tok_ref, x_hbm, g2_ref, wg_hbm, wu_hbm, wd_hbm,
                    y_ref, xg, wg_f, wu_f, wd_f, wg_s, wu_s, wd_s, slot_ref, sem, gsem):
    b = pl.program_id(0)
    n_used = nused_ref[0]
    used = b < n_used
    e = blk_e_ref[b]
    new_expert = (b == 0) | (e != blk_e_ref[jnp.maximum(b - 1, 0)])
    n_slots = GATHER_AHEAD + 1
    xslot = b % n_slots

    def gather_rows(block, slot, lo, hi):
        row0 = jnp.minimum(block, n_used - 1) * EXPERT_BLOCK
        for r in range(lo, hi):
            pltpu.make_async_copy(x_hbm.at[pl.ds(tok_ref[row0 + r], 1), :],
                                  xg.at[slot, pl.ds(r, 1), :], gsem.at[slot]).start()

    def gather_wait(slot):
        pltpu.make_async_copy(x_hbm.at[pl.ds(0, EXPERT_BLOCK), :], xg.at[slot], gsem.at[slot]).wait()

    @pl.when(b == 0)
    def _():
        for k in range(GATHER_AHEAD):
            gather_rows(k, k, 0, EXPERT_BLOCK)

    def weight_copies(expert, slot):
        return (pltpu.make_async_copy(wg_hbm.at[expert], wg_f.at[slot], sem.at[slot]),
                pltpu.make_async_copy(wu_hbm.at[expert], wu_f.at[slot], sem.at[slot]),
                pltpu.make_async_copy(wd_hbm.at[expert], wd_f.at[slot], sem.at[slot]))

    def block_expert(blk):
        return blk_e_ref[jnp.minimum(blk, n_used - 1)]

    b_next = b + nblk_ref[e]
    b_next2 = b_next + nblk_ref[block_expert(b_next)]

    @pl.when(b == 0)
    def _():
        slot_ref[0] = 0
        for cp in weight_copies(e, 0):
            cp.start()

        @pl.when(b_next < n_used)
        def _():
            for cp in weight_copies(block_expert(b_next), 1):
                cp.start(priority=PREFETCH_PRIORITY)

    @pl.when(used & new_expert)
    def _():
        slot = slot_ref[0]

        @pl.when(b_next2 < n_used)
        def _():
            for cp in weight_copies(block_expert(b_next2), (slot + 2) % WEIGHT_SLOTS):
                cp.start(priority=PREFETCH_PRIORITY)

        for cp in weight_copies(e, slot):
            cp.wait()
        wg_s[...] = wg_f[slot].astype(BF16)
        wu_s[...] = wu_f[slot].astype(BF16)
        wd_s[...] = wd_f[slot].astype(BF16)
        slot_ref[0] = (slot + 1) % WEIGHT_SLOTS

    @pl.when(used)
    def _():
        gather_wait(xslot)
        ahead = b + GATHER_AHEAD
        aslot = ahead % n_slots
        per_group = EXPERT_BLOCK // GATHER_GROUPS
        x = xg[xslot]
        ms = jnp.mean(x * x, axis=-1, keepdims=True)
        h = ((x * lax.rsqrt(ms + EPS)) * g2_ref[...]).astype(BF16)
        gather_rows(ahead, aslot, 0, per_group)
        a = jnp.dot(h, wg_s[...], preferred_element_type=F32)
        gather_rows(ahead, aslot, per_group, 2 * per_group)
        u = jnp.dot(h, wu_s[...], preferred_element_type=F32)
        gather_rows(ahead, aslot, 2 * per_group, 3 * per_group)
        mid = ((a * _sigmoid(a)) * u).astype(BF16)
        gather_rows(ahead, aslot, 3 * per_group, EXPERT_BLOCK)
        y_ref[...] = jnp.dot(mid, wd_s[...], preferred_element_type=F32)

    @pl.when(b == n_used - 1)
    def _():
        for k in range(1, GATHER_AHEAD + 1):
            gather_wait((b + k) % n_slots)

    @pl.when(jnp.logical_not(used))
    def _():
        y_ref[...] = jnp.zeros_like(y_ref)


def _experts(blk_e, n_used, nblk, row_tok, x1, g2, w_g, w_u, w_d):
    n_rows = row_tok.shape[0]
    nb = n_rows // EXPERT_BLOCK
    hbm = pl.BlockSpec(memory_space=pl.ANY)
    grid_spec = pltpu.PrefetchScalarGridSpec(
        num_scalar_prefetch=4,
        grid=(nb,),
        in_specs=[
            hbm,
            pl.BlockSpec((1, D_MODEL), lambda b, *_: (0, 0)),
            hbm, hbm, hbm,
        ],
        out_specs=pl.BlockSpec((EXPERT_BLOCK, D_MODEL), lambda b, *_: (b, 0)),
        scratch_shapes=[pltpu.VMEM((GATHER_AHEAD + 1, EXPERT_BLOCK, D_MODEL), F32),
                        pltpu.VMEM((WEIGHT_SLOTS, D_MODEL, D_EXPERT), F32),
                        pltpu.VMEM((WEIGHT_SLOTS, D_MODEL, D_EXPERT), F32),
                        pltpu.VMEM((WEIGHT_SLOTS, D_EXPERT, D_MODEL), F32),
                        pltpu.VMEM((D_MODEL, D_EXPERT), BF16),
                        pltpu.VMEM((D_MODEL, D_EXPERT), BF16),
                        pltpu.VMEM((D_EXPERT, D_MODEL), BF16),
                        pltpu.SMEM((1,), I32),
                        pltpu.SemaphoreType.DMA((WEIGHT_SLOTS,)),
                        pltpu.SemaphoreType.DMA((GATHER_AHEAD + 1,))],
    )
    return pl.pallas_call(
        _experts_kernel,
        out_shape=jax.ShapeDtypeStruct((n_rows, D_MODEL), F32),
        grid_spec=grid_spec,
        compiler_params=pltpu.CompilerParams(
            dimension_semantics=("arbitrary",), vmem_limit_bytes=VMEM_LIMIT),
        name="experts",
    )(blk_e, n_used, nblk, row_tok, x1, g2, w_g, w_u, w_d)


def _combine_kernel(d1_ref, d2_ref, x1_ref, w_ref, y_ref, o_ref, gbuf, sem):
    i = pl.program_id(0)
    n = pl.num_programs(0)
    tm = x1_ref.shape[0]

    def request_row(base, slot, r):
        pltpu.make_async_copy(y_ref.at[pl.ds(d1_ref[base + r], 1), :],
                              gbuf.at[slot, pl.ds(r, 1), :], sem.at[slot]).start()
        pltpu.make_async_copy(y_ref.at[pl.ds(d2_ref[base + r], 1), :],
                              gbuf.at[slot, pl.ds(tm + r, 1), :], sem.at[slot]).start()

    def issue(tile, slot, lo, hi):
        base = jnp.minimum(tile, n - 1) * tm
        for r in range(lo, hi):
            request_row(base, slot, r)

    def tile_wait(slot):
        pltpu.make_async_copy(y_ref.at[pl.ds(0, 2 * tm), :], gbuf.at[slot], sem.at[slot]).wait()

    slot = i % 2

    @pl.when(i == 0)
    def _():
        def first_tile(r, _):
            request_row(0, 0, r)
            return 0

        lax.fori_loop(0, tm, first_tile, 0, unroll=SUBLANES)

    tile_wait(slot)
    for c in range(tm // COMBINE_ROWS):
        rs = slice(c * COMBINE_ROWS, (c + 1) * COMBINE_ROWS)
        rs2 = slice(tm + c * COMBINE_ROWS, tm + (c + 1) * COMBINE_ROWS)
        w = w_ref[rs, :]
        o_ref[rs, :] = x1_ref[rs, :] + (w[:, 0:1] * gbuf[slot, rs, :] + w[:, 1:2] * gbuf[slot, rs2, :])
        issue(i + 1, 1 - slot, c * COMBINE_ROWS, (c + 1) * COMBINE_ROWS)

    @pl.when(i == n - 1)
    def _():
        tile_wait(1 - slot)


def _combine(dest1, dest2, x1, w, yb):
    t = x1.shape[0]
    grid_spec = pltpu.PrefetchScalarGridSpec(
        num_scalar_prefetch=2,
        grid=(t // TM_CMB,),
        in_specs=[
            pl.BlockSpec((TM_CMB, D_MODEL), lambda i, a, b: (i, 0)),
            pl.BlockSpec((TM_CMB, LANES), lambda i, a, b: (i, 0)),
            pl.BlockSpec(memory_space=pl.ANY),
        ],
        out_specs=pl.BlockSpec((TM_CMB, D_MODEL), lambda i, a, b: (i, 0)),
        scratch_shapes=[pltpu.VMEM((2, 2 * TM_CMB, D_MODEL), F32),
                        pltpu.SemaphoreType.DMA((2,))],
    )
    return pl.pallas_call(
        _combine_kernel,
        out_shape=jax.ShapeDtypeStruct((t, D_MODEL), F32),
        grid_spec=grid_spec,
        compiler_params=pltpu.CompilerParams(
            dimension_semantics=("arbitrary",), vmem_limit_bytes=VMEM_LIMIT),
        name="combine",
    )(dest1, dest2, x1, w, yb)


def kernel(x, norm1_g, w_in, b_gate, b_forget, sgu_ln_g, sgu_ln_b, w_spatial, b_spatial, q_norm_g, k_norm_g, w_proj_sgu, w_proj_fox, w_out, norm2_g, w_router_group, b_router_group, w_router_expert, b_router_expert, w_expert_gate, w_expert_up, w_expert_down):
    batch, seq, d = x.shape
    t = batch * seq
    l = 0
    x2 = x.reshape(t, d)

    wt = jnp.swapaxes(w_in[l], 0, 1)
    off_f = N_MAIN_SEC * SEC
    w_f = jnp.pad(wt[off_f:off_f + FOX_HEADS].T, ((0, 0), (0, LANES - FOX_HEADS))).astype(BF16)
    b_f = jnp.pad(b_forget[l], (0, LANES - FOX_HEADS)).reshape(1, LANES)
    n_r = N_GROUPS + N_EXPERTS
    w_r = jnp.pad(jnp.concatenate([w_router_group[l], w_router_expert[l]], axis=1),
                  ((0, 0), (0, LANES - n_r))).astype(BF16)
    b_r = jnp.pad(jnp.concatenate([b_router_group[l], b_router_expert[l]]),
                  (0, LANES - n_r)).reshape(1, LANES)

    h, lf = _norm1(x2, norm1_g[l].reshape(1, d), w_f, b_f)
    a_sgu, q, kt, va, gates = _inproj(
        h, wt, b_gate[l].reshape(1, 2 * d),
        sgu_ln_g[l].reshape(1, SGU_WIDTH), sgu_ln_b[l].reshape(1, SGU_WIDTH),
        w_spatial[l], b_spatial[l].T, q_norm_g[l].reshape(1, HEAD_DIM),
        k_norm_g[l].reshape(1, HEAD_DIM))
    negc = _forget_cumsum(lf, batch, seq).reshape(batch, FOX_HEADS, 1, seq)
    a_fox = _attention(q, kt, va, negc, batch, seq)
    x1, logits = _mix(a_sgu, a_fox, gates, x2, w_proj_sgu[l], w_proj_fox[l], w_out[l],
                      norm2_g[l].reshape(1, d), w_r, b_r)

    n_assign = 2 * t
    n_rows = n_assign + N_EXPERTS * EXPERT_BLOCK
    nb = n_rows // EXPERT_BLOCK
    dest_t, meta_f, bmeta, nblk, row_tok = _route(logits, n_rows // LANES)
    dest1, dest2 = dest_t[:, 0, :].reshape(t), dest_t[:, 1, :].reshape(t)
    n_used = bmeta[0:1, 1]
    yb = _experts(bmeta[:nb, 0], n_used, nblk[0, :N_EXPERTS], row_tok.reshape(n_rows), x1,
                  norm2_g[l].reshape(1, d), w_expert_gate[l], w_expert_up[l], w_expert_down[l])
    out = _combine(dest1, dest2, x1, meta_f, yb)
    return out.reshape(batch, seq, d)
```

```python
import functools

import jax
import jax.numpy as jnp
from jax import lax
from jax.experimental import pallas as pl
from jax.experimental.pallas import tpu as pltpu

F32 = jnp.float32
BF16 = jnp.bfloat16
I32 = jnp.int32

D_MODEL = 2048
CHUNK = 128
SGU_GROUPS = 8
SGU_WIDTH = 1024
FOX_HEADS = 8
HEAD_DIM = 128
FOX_WIDTH = 1024
N_GROUPS = 4
EXPERTS_PER_GROUP = 8
N_EXPERTS = 32
D_EXPERT = 512
EXPERT_BLOCK = 128
EPS = 1e-6

LANES = 128
SUBLANES = 8
PREFETCH_PRIORITY = 1
VMEM_LIMIT = 56 * 1024 * 1024

SEC = 1024
N_MAIN_SEC = 5
N_GATE_SEC = 4
TM_NORM = 1024
TM_IN = 512
TQ = 256
HEADS_PER_STEP = 4
TM_MIX = 256
RCH = 512
TM_CMB = 512
COMBINE_ROWS = 64
N_SPLIT = 3
EXPERT_GROUP_SHIFT = EXPERTS_PER_GROUP.bit_length() - 1
LOG2E = 1.4426950408889634
Q_SCALE = HEAD_DIM ** -0.5 * LOG2E


def _sigmoid(x):
    return 1.0 / (1.0 + jnp.exp(-x))


def _log_sigmoid(x):
    return jnp.minimum(x, 0.0) - jnp.log1p(jnp.exp(-jnp.abs(x)))


def _norm1_kernel(x_ref, g1_ref, wf_ref, bf_ref, h_ref, lf_ref):
    x = x_ref[...]
    ms = jnp.mean(x * x, axis=-1, keepdims=True)
    hb = ((x * lax.rsqrt(ms + EPS)) * g1_ref[...]).astype(BF16)
    h_ref[...] = hb
    f = jnp.dot(hb, wf_ref[...], preferred_element_type=F32) + bf_ref[...]
    lf_ref[...] = _log_sigmoid(f)


def _norm1(x2, g1, w_f, b_f):
    t = x2.shape[0]
    row = lambda i: (i, 0)
    const = lambda i: (0, 0)
    return pl.pallas_call(
        _norm1_kernel,
        out_shape=(jax.ShapeDtypeStruct((t, D_MODEL), BF16),
                   jax.ShapeDtypeStruct((t, LANES), F32)),
        grid=(t // TM_NORM,),
        in_specs=[pl.BlockSpec((TM_NORM, D_MODEL), row),
                  pl.BlockSpec((1, D_MODEL), const),
                  pl.BlockSpec((D_MODEL, LANES), const),
                  pl.BlockSpec((1, LANES), const)],
        out_specs=(pl.BlockSpec((TM_NORM, D_MODEL), row), pl.BlockSpec((TM_NORM, LANES), row)),
        compiler_params=pltpu.CompilerParams(
            dimension_semantics=("arbitrary",), vmem_limit_bytes=VMEM_LIMIT),
        name="norm1",
    )(x2, g1, w_f, b_f)


N_SECTIONS = 1 + 3 + N_GATE_SEC
N_CHUNKS = N_MAIN_SEC + N_GATE_SEC
GATE_SECTION0 = 4
WCONV_ROWS = 256


def _inproj_kernel(h_ref, wt_hbm, bg_ref, lng_ref, lnb_ref, wsp_ref, bsp_ref, qg_ref, kg_ref,
                   asgu_ref, q_ref, kt_ref, va_ref, gates_ref,
                   stage, wb, sem):
    s = pl.program_id(0)
    i = pl.program_id(1)
    tm = h_ref.shape[0]

    def chunk_copy(c, slot):
        r0 = pl.multiple_of(c * SEC + jnp.where(c >= N_MAIN_SEC, FOX_HEADS, 0), 8)
        return pltpu.make_async_copy(wt_hbm.at[pl.ds(r0, SEC), :], stage.at[slot], sem.at[slot])

    def convert(slot, k):
        for p in range(SEC // WCONV_ROWS):
            rs = slice(p * WCONV_ROWS, (p + 1) * WCONV_ROWS)
            wb[k, :, rs] = stage[slot, rs, :].T.astype(BF16)

    @pl.when((s == 0) & (i == 0))
    def _():
        chunk_copy(0, 0).start()
        chunk_copy(1, 1).start()
        chunk_copy(0, 0).wait()
        convert(0, 0)
        chunk_copy(2, 0).start(priority=PREFETCH_PRIORITY)
        chunk_copy(1, 1).wait()
        convert(1, 1)

    @pl.when((s > 0) & (i == 0))
    def _():
        c = s + 1
        slot = c % 2
        chunk_copy(c, slot).wait()

        @pl.when(c + 1 < N_CHUNKS)
        def _():
            chunk_copy(c + 1, 1 - slot).start(priority=PREFETCH_PRIORITY)

        convert(slot, 0)

    def section(k=0):
        return jnp.dot(h_ref[...], wb[k], preferred_element_type=F32)

    @pl.when(s == 0)
    def _():
        u = jax.nn.gelu(section(0))
        v = jax.nn.gelu(section(1))
        mu = jnp.mean(v, axis=-1, keepdims=True)
        vc = v - mu
        var = jnp.mean(vc * vc, axis=-1, keepdims=True)
        vn = ((vc * lax.rsqrt(var + EPS)) * lng_ref[...] + lnb_ref[...]).astype(BF16)
        row = lax.broadcasted_iota(I32, (CHUNK, CHUNK), 0)
        col = lax.broadcasted_iota(I32, (CHUNK, CHUNK), 1)
        causal = row >= col
        for g in range(SGU_GROUPS):
            wg = jnp.where(causal, wsp_ref[g], 0.0).astype(BF16)
            bcol = bsp_ref[:, g:g + 1]
            gs = slice(g * LANES, (g + 1) * LANES)
            n_ch = tm // CHUNK
            rhs = jnp.concatenate([vn[c * CHUNK:(c + 1) * CHUNK, gs] for c in range(n_ch)], axis=1)
            sg = jnp.dot(wg, rhs, preferred_element_type=F32) + bcol
            for c in range(n_ch):
                rs = slice(c * CHUNK, (c + 1) * CHUNK)
                asgu_ref[rs, gs] = (u[rs, gs] * sg[:, c * LANES:(c + 1) * LANES]).astype(BF16)

    def _head_norm(z, h, gain_ref):
        zh = z[:, h * HEAD_DIM:(h + 1) * HEAD_DIM]
        ms = jnp.mean(zh * zh, axis=-1, keepdims=True)
        return (zh * lax.rsqrt(ms + EPS)) * gain_ref[...]

    @pl.when(s == 1)
    def _():
        z = section()
        for h in range(FOX_HEADS):
            q_ref[:, h * HEAD_DIM:(h + 1) * HEAD_DIM] = (
                _head_norm(z, h, qg_ref) * Q_SCALE).astype(BF16)

    @pl.when(s == 2)
    def _():
        z = section()
        for h in range(FOX_HEADS):
            kt_ref[h * HEAD_DIM:(h + 1) * HEAD_DIM, :] = _head_norm(z, h, kg_ref).T.astype(BF16)

    @pl.when(s == 3)
    def _():
        va_ref[...] = section().astype(BF16)

    @pl.when(s >= GATE_SECTION0)
    def _():
        gates_ref[...] = (0.5 * jnp.tanh(0.5 * (section() + bg_ref[...])) + 0.5).astype(BF16)


def _inproj(h, wt, b_gate, ln_g, ln_b, w_sp, b_sp_t, q_g, k_g):
    t = h.shape[0]
    n_i = t // TM_IN

    def active(sec):
        return lambda s, i: jnp.where(s < sec, 0, jnp.where(s > sec, n_i - 1, i))

    def gate_blk(s, i):
        g = jnp.clip(s - GATE_SECTION0, 0, N_GATE_SEC - 1)
        return (jnp.where(s < GATE_SECTION0, 0, i), g)

    const2 = lambda s, i: (0, 0)
    out_shape = (
        jax.ShapeDtypeStruct((t, SGU_WIDTH), BF16),
        jax.ShapeDtypeStruct((t, FOX_WIDTH), BF16),
        jax.ShapeDtypeStruct((FOX_WIDTH, t), BF16),
        jax.ShapeDtypeStruct((t, FOX_WIDTH), BF16),
        jax.ShapeDtypeStruct((t, 2 * D_MODEL), BF16),
    )
    return pl.pallas_call(
        _inproj_kernel,
        out_shape=out_shape,
        grid=(N_SECTIONS, n_i),
        in_specs=[
            pl.BlockSpec((TM_IN, D_MODEL), lambda s, i: (i, 0)),
            pl.BlockSpec(memory_space=pl.ANY),
            pl.BlockSpec((1, SEC), lambda s, i: (0, jnp.clip(s - GATE_SECTION0, 0, N_GATE_SEC - 1))),
            pl.BlockSpec((1, SGU_WIDTH), const2),
            pl.BlockSpec((1, SGU_WIDTH), const2),
            pl.BlockSpec((SGU_GROUPS, CHUNK, CHUNK), lambda s, i: (0, 0, 0)),
            pl.BlockSpec((CHUNK, SGU_GROUPS), const2),
            pl.BlockSpec((1, HEAD_DIM), const2),
            pl.BlockSpec((1, HEAD_DIM), const2),
        ],
        out_specs=(
            pl.BlockSpec((TM_IN, SGU_WIDTH), lambda s, i: (active(0)(s, i), 0)),
            pl.BlockSpec((TM_IN, FOX_WIDTH), lambda s, i: (active(1)(s, i), 0)),
            pl.BlockSpec((FOX_WIDTH, TM_IN), lambda s, i: (0, active(2)(s, i))),
            pl.BlockSpec((TM_IN, FOX_WIDTH), lambda s, i: (active(3)(s, i), 0)),
            pl.BlockSpec((TM_IN, SEC), gate_blk),
        ),
        scratch_shapes=[pltpu.VMEM((2, SEC, D_MODEL), F32),
                        pltpu.VMEM((2, D_MODEL, SEC), BF16),
                        pltpu.SemaphoreType.DMA((2,))],
        compiler_params=pltpu.CompilerParams(
            dimension_semantics=("arbitrary", "arbitrary"), vmem_limit_bytes=VMEM_LIMIT),
        name="inproj",
    )(h, wt, b_gate, ln_g, ln_b, w_sp, b_sp_t, q_g, k_g)


def _split3(x):
    x0 = x.astype(BF16)
    r1 = x - x0.astype(F32)
    x1 = r1.astype(BF16)
    r2 = r1 - x1.astype(F32)
    return x0, x1, r2.astype(BF16)


def _cumsum_kernel(lf_ref, negc_ref, c_s):
    seq = lf_ref.shape[0]
    row = lax.broadcasted_iota(I32, (CHUNK, CHUNK), 0)
    col = lax.broadcasted_iota(I32, (CHUNK, CHUNK), 1)
    tri = jnp.where(row >= col, 1.0, 0.0).astype(BF16)
    carry = jnp.zeros((1, LANES), F32)
    for r in range(seq // CHUNK):
        rs = slice(r * CHUNK, (r + 1) * CHUNK)
        x0, x1, x2 = _split3(lf_ref[rs, :])
        cs = (jnp.dot(tri, x0, preferred_element_type=F32)
              + jnp.dot(tri, x1, preferred_element_type=F32)
              + jnp.dot(tri, x2, preferred_element_type=F32)) + carry
        carry = cs[CHUNK - 1:CHUNK, :]
        c_s[rs, :] = cs
    ct = c_s[...].T
    negc_ref[0] = -ct[0:FOX_HEADS, :]


def _forget_cumsum(lf, batch, seq):
    return pl.pallas_call(
        _cumsum_kernel,
        out_shape=jax.ShapeDtypeStruct((batch, FOX_HEADS, seq), F32),
        grid=(batch,),
        in_specs=[pl.BlockSpec((seq, LANES), lambda b: (b, 0))],
        out_specs=pl.BlockSpec((1, FOX_HEADS, seq), lambda b: (b, 0, 0)),
        scratch_shapes=[pltpu.VMEM((seq, LANES), F32)],
        compiler_params=pltpu.CompilerParams(
            dimension_semantics=("arbitrary",), vmem_limit_bytes=VMEM_LIMIT),
        name="forget_cumsum",
    )(lf)


def _attn_kernel(q_ref, kt_ref, v_ref, negc_ref, o_ref, kx, vx):
    seq = q_ref.shape[0]
    sub = lax.broadcasted_iota(I32, (HEAD_DIM, seq), 0)
    for hh in range(HEADS_PER_STEP):
        hs = slice(hh * HEAD_DIM, (hh + 1) * HEAD_DIM)
        c0, c1, c2 = (c.astype(F32) for c in _split3(negc_ref[0, hh] * LOG2E))
        kx[hh, 0:HEAD_DIM, :] = kt_ref[hs, :]
        kx[hh, HEAD_DIM:, :] = jnp.where(
            sub == 0, c0, jnp.where(sub == 1, c1, jnp.where(sub == 2, c2, 0.0))).astype(BF16)
        vx[hh, :, 0:HEAD_DIM] = v_ref[:, hs]
        vx[hh, :, HEAD_DIM:] = jnp.ones((seq, HEAD_DIM), BF16)
    lane = lax.broadcasted_iota(I32, (TQ, HEAD_DIM), 1)
    bias_cols = jnp.where(lane < N_SPLIT, 1.0, 0.0).astype(BF16)
    row = lax.broadcasted_iota(I32, (TQ, TQ), 0)
    col = lax.broadcasted_iota(I32, (TQ, TQ), 1)
    causal = row >= col
    for qi in range(seq // TQ):
        for hh in range(HEADS_PER_STEP):
            hs = slice(hh * HEAD_DIM, (hh + 1) * HEAD_DIM)
            k0 = qi * TQ
            q = jnp.concatenate([q_ref[k0:k0 + TQ, hs], bias_cols], axis=1)
            s_d = jnp.dot(q, kx[hh, :, k0:k0 + TQ], preferred_element_type=F32)
            s_d = jnp.where(causal, s_d, -jnp.inf)
            m = jnp.max(s_d, axis=-1, keepdims=True)
            if qi > 0:
                s_o = jnp.dot(q, kx[hh, :, 0:k0], preferred_element_type=F32)
                m = jnp.maximum(m, jnp.max(s_o, axis=-1, keepdims=True))
            acc = jnp.dot(jnp.exp2(s_d - m).astype(BF16), vx[hh, k0:k0 + TQ, :],
                          preferred_element_type=F32)
            if qi > 0:
                acc = acc + jnp.dot(jnp.exp2(s_o - m).astype(BF16), vx[hh, 0:k0, :],
                                    preferred_element_type=F32)
            inv_l = 1.0 / acc[:, HEAD_DIM:HEAD_DIM + 1]
            o_ref[k0:k0 + TQ, hs] = (acc[:, 0:HEAD_DIM] * inv_l).astype(BF16)


def _attention(q, kt, va, negc, batch, seq):
    t = q.shape[0]
    width = HEADS_PER_STEP * HEAD_DIM
    blk = pl.BlockSpec((seq, width), lambda b, g: (b, g))
    return pl.pallas_call(
        _attn_kernel,
        out_shape=jax.ShapeDtypeStruct((t, FOX_WIDTH), BF16),
        grid=(batch, FOX_HEADS // HEADS_PER_STEP),
        in_specs=[blk,
                  pl.BlockSpec((width, seq), lambda b, g: (g, b)),
                  blk,
                  pl.BlockSpec((1, HEADS_PER_STEP, 1, seq), lambda b, g: (b, g, 0, 0))],
        out_specs=blk,
        scratch_shapes=[pltpu.VMEM((HEADS_PER_STEP, 2 * HEAD_DIM, seq), BF16),
                        pltpu.VMEM((HEADS_PER_STEP, seq, 2 * HEAD_DIM), BF16)],
        compiler_params=pltpu.CompilerParams(
            dimension_semantics=("arbitrary", "arbitrary"), vmem_limit_bytes=VMEM_LIMIT),
        name="fox_attention",
    )(q, kt, va, negc)


MIX_WROWS = 512


def _mix_kernel(as_ref, af_ref, g_ref, x_ref, wps_hbm, wpf_hbm, wo_hbm, g2_ref, wr_ref, br_ref,
                x1_ref, lg_ref, wps_ref, wpf_ref, wo_ref, stage, sem):
    @pl.when(pl.program_id(0) == 0)
    def _():
        pieces = [(src, dst, r0) for src, dst in ((wps_hbm, wps_ref), (wpf_hbm, wpf_ref),
                                                  (wo_hbm, wo_ref))
                  for r0 in range(0, dst.shape[0], MIX_WROWS)]

        def piece_copy(k):
            src, _, r0 = pieces[k]
            return pltpu.make_async_copy(src.at[pl.ds(r0, MIX_WROWS), :], stage.at[k % 2],
                                         sem.at[k % 2])

        piece_copy(0).start()
        for k, (_, dst, r0) in enumerate(pieces):
            if k + 1 < len(pieces):
                piece_copy(k + 1).start()
            piece_copy(k).wait()
            dst[r0:r0 + MIX_WROWS, :] = stage[k % 2].astype(BF16)

    ys = jnp.dot(as_ref[...], wps_ref[...], preferred_element_type=F32)
    yf = jnp.dot(af_ref[...], wpf_ref[...], preferred_element_type=F32)
    m = (g_ref[:, :D_MODEL].astype(F32) * ys + g_ref[:, D_MODEL:].astype(F32) * yf).astype(BF16)
    x1 = x_ref[...] + jnp.dot(m, wo_ref[...], preferred_element_type=F32)
    x1_ref[...] = x1
    ms = jnp.mean(x1 * x1, axis=-1, keepdims=True)
    h2 = ((x1 * lax.rsqrt(ms + EPS)) * g2_ref[...]).astype(BF16)
    lg_ref[...] = jnp.dot(h2, wr_ref[...], preferred_element_type=F32) + br_ref[...]


def _mix(a_sgu, a_fox, gates, x2, wps, wpf, wo, g2, w_r, b_r):
    t = x2.shape[0]
    row = lambda i: (i, 0)
    const = lambda i: (0, 0)
    resident = functools.partial(pl.BlockSpec, index_map=const, pipeline_mode=pl.Buffered(1))
    return pl.pallas_call(
        _mix_kernel,
        out_shape=(jax.ShapeDtypeStruct((t, D_MODEL), F32),
                   jax.ShapeDtypeStruct((t, LANES), F32)),
        grid=(t // TM_MIX,),
        in_specs=[
            pl.BlockSpec((TM_MIX, SGU_WIDTH), row),
            pl.BlockSpec((TM_MIX, FOX_WIDTH), row),
            pl.BlockSpec((TM_MIX, 2 * D_MODEL), row),
            pl.BlockSpec((TM_MIX, D_MODEL), row),
            pl.BlockSpec(memory_space=pl.ANY),
            pl.BlockSpec(memory_space=pl.ANY),
            pl.BlockSpec(memory_space=pl.ANY),
            pl.BlockSpec((1, D_MODEL), const),
            resident((D_MODEL, LANES)),
            pl.BlockSpec((1, LANES), const),
        ],
        out_specs=(pl.BlockSpec((TM_MIX, D_MODEL), row), pl.BlockSpec((TM_MIX, LANES), row)),
        scratch_shapes=[pltpu.VMEM((SGU_WIDTH, D_MODEL), BF16),
                        pltpu.VMEM((FOX_WIDTH, D_MODEL), BF16),
                        pltpu.VMEM((D_MODEL, D_MODEL), BF16),
                        pltpu.VMEM((2, MIX_WROWS, D_MODEL), F32),
                        pltpu.SemaphoreType.DMA((2,))],
        compiler_params=pltpu.CompilerParams(
            dimension_semantics=("arbitrary",), vmem_limit_bytes=VMEM_LIMIT),
        name="mix",
    )(a_sgu, a_fox, gates, x2, wps, wpf, wo, g2, w_r, b_r)


def _route_kernel(lg_ref, dt_ref, mf_ref, bm_ref, em_ref, bt_ref, mi_ref, inv_ref):
    t = lg_ref.shape[0]
    n_chunks = t // RCH
    lane_i = lax.broadcasted_iota(I32, (RCH, LANES), 1)
    lane = lane_i.astype(F32)
    lane_grp = ((lane_i - N_GROUPS) >> EXPERT_GROUP_SHIFT).astype(F32)
    is_grp = lane_i < N_GROUPS
    is_exp = (lane_i >= N_GROUPS) & (lane_i < N_GROUPS + N_EXPERTS)
    r_i = lax.broadcasted_iota(I32, (RCH, RCH), 0)
    c_i = lax.broadcasted_iota(I32, (RCH, RCH), 1)
    strict_lower = jnp.where(r_i > c_i, 1.0, 0.0).astype(BF16)
    neg_inf = -jnp.inf

    def first_max(vals):
        vmax = jnp.max(vals, axis=-1, keepdims=True)
        idx = jnp.min(jnp.where(vals == vmax, lane, float(LANES)), axis=-1, keepdims=True)
        return vmax, idx

    def pick(table, idx):
        return jnp.sum(jnp.where(lane == idx, table, 0.0), axis=-1, keepdims=True)

    def pack(cols):
        out = jnp.zeros((RCH, LANES), F32)
        for n, c in enumerate(cols):
            out = jnp.where(lane_i == n, c, out)
        return out

    def pass1(ci, counts):
        r0 = pl.multiple_of(ci * RCH, RCH)
        lg = lg_ref[pl.ds(r0, RCH), :]
        gmax, grp = first_max(jnp.where(is_grp, lg, neg_inf))
        p_grp = 1.0 / jnp.sum(jnp.where(is_grp, jnp.exp(lg - gmax), 0.0), axis=-1, keepdims=True)
        el = jnp.where(is_exp & (lane_grp == grp), lg, neg_inf)
        v1, i1 = first_max(el)
        v2, i2 = first_max(jnp.where(lane == i1, neg_inf, el))
        e21 = jnp.exp(v2 - v1)
        w1 = p_grp / (1.0 + e21)
        w2 = p_grp * e21 / (1.0 + e21)
        e1 = i1 - float(N_GROUPS)
        e2 = i2 - float(N_GROUPS)
        hot = jnp.where((lane == e1) | (lane == e2), 1.0, 0.0)
        before = jnp.dot(strict_lower, hot.astype(BF16), preferred_element_type=F32) + counts
        mi_ref[pl.ds(r0, RCH), :] = pack([e1, e2, pick(before, e1), pick(before, e2)]).astype(I32)
        mf_ref[pl.ds(r0, RCH), :] = pack([w1, w2])
        return counts + jnp.sum(hot, axis=0, keepdims=True)

    counts = lax.fori_loop(0, n_chunks, pass1, jnp.zeros((1, LANES), F32))

    nblk = jnp.floor((counts + float(EXPERT_BLOCK - 1)) * (1.0 / EXPERT_BLOCK))
    u_r = lax.broadcasted_iota(I32, (LANES, LANES), 0)
    u_c = lax.broadcasted_iota(I32, (LANES, LANES), 1)
    strict_upper = jnp.where(u_r < u_c, 1.0, 0.0).astype(BF16)
    bstart = jnp.dot(jnp.broadcast_to(nblk, (SUBLANES, LANES)).astype(BF16), strict_upper,
                     preferred_element_type=F32)[0:1, :]
    bend = bstart + nblk

    def pass2(ci, _):
        r0 = pl.multiple_of(ci * RCH, RCH)
        mi = mi_ref[pl.ds(r0, RCH), :].astype(F32)
        cols = [jnp.sum(jnp.where(lane_i == n, mi, 0.0), axis=-1, keepdims=True) for n in range(4)]
        d1 = pick(bstart, cols[0]) * float(EXPERT_BLOCK) + cols[2]
        d2 = pick(bstart, cols[1]) * float(EXPERT_BLOCK) + cols[3]
        d_t = pack([d1, d2]).T
        dt_ref[ci] = d_t[0:SUBLANES, :].astype(I32)
        tok = (lax.broadcasted_iota(I32, (RCH, 1), 0) + r0).astype(F32)
        tok_hi = jnp.floor(tok * (1.0 / LANES))
        tok_lo = tok - tok_hi * float(LANES)
        for n, d in enumerate((d1, d2)):
            d_hi = jnp.floor(d * (1.0 / LANES))
            hit = lane == (d - d_hi * float(LANES))
            blk_row = jnp.floor(d_t[n:n + 1, :] * (1.0 / LANES))
            sel = jnp.where(inv_blk == blk_row, 1.0, 0.0).astype(BF16)
            inv_ref[0] += jnp.dot(sel, jnp.where(hit, tok_hi, 0.0).astype(BF16),
                                  preferred_element_type=F32)
            inv_ref[1] += jnp.dot(sel, jnp.where(hit, tok_lo, 0.0).astype(BF16),
                                  preferred_element_type=F32)
        return 0

    nb_rows = bm_ref.shape[0]
    inv_blk = lax.broadcasted_iota(I32, (nb_rows, RCH), 0).astype(F32)
    inv_ref[...] = jnp.zeros_like(inv_ref)
    lax.fori_loop(0, n_chunks, pass2, 0)
    bt_ref[...] = (inv_ref[0] * float(LANES) + inv_ref[1]).astype(I32)

    b_col = lax.broadcasted_iota(I32, (nb_rows, LANES), 0).astype(F32)
    b_lane = lax.broadcasted_iota(I32, (nb_rows, LANES), 1)
    done = jnp.where((bend <= b_col) & (b_lane < N_EXPERTS), 1.0, 0.0)
    blk_e = jnp.minimum(jnp.sum(done, axis=-1, keepdims=True), float(N_EXPERTS - 1))
    n_used = jnp.sum(jnp.where(b_lane[0:1, :] == N_EXPERTS - 1, bend, 0.0), axis=-1, keepdims=True)
    bm_ref[...] = jnp.where(b_lane == 0, blk_e, jnp.where(b_lane == 1, n_used, 0.0)).astype(I32)

    em_ref[...] = jnp.broadcast_to(nblk, (SUBLANES, LANES)).astype(I32)


def _route(logits, nb_rows):
    t = logits.shape[0]
    full = lambda shape: pl.BlockSpec(shape, lambda: (0,) * len(shape))
    return pl.pallas_call(
        _route_kernel,
        out_shape=(jax.ShapeDtypeStruct((t // RCH, SUBLANES, RCH), I32),
                   jax.ShapeDtypeStruct((t, LANES), F32),
                   jax.ShapeDtypeStruct((nb_rows, LANES), I32),
                   jax.ShapeDtypeStruct((SUBLANES, LANES), I32),
                   jax.ShapeDtypeStruct((nb_rows, LANES), I32)),
        in_specs=[full((t, LANES))],
        out_specs=(full((t // RCH, SUBLANES, RCH)), full((t, LANES)), full((nb_rows, LANES)),
                   full((SUBLANES, LANES)), full((nb_rows, LANES))),
        scratch_shapes=[pltpu.VMEM((t, LANES), I32), pltpu.VMEM((2, nb_rows, LANES), F32)],
        compiler_params=pltpu.CompilerParams(vmem_limit_bytes=VMEM_LIMIT),
        name="route",
    )(logits)


GATHER_GROUPS = 4
GATHER_AHEAD = 6
WEIGHT_SLOTS = 3


def _experts_kernel(blk_e_ref, nused_ref, nblk_ref, tok_ref, x_hbm, g2_ref, wg_hbm, wu_hbm, wd_hbm,
                    y_ref, xg, wg_f, wu_f, wd_f, wg_s, wu_s, wd_s, slot_ref, sem, gsem):
    b = pl.program_id(0)
    n_used = nused_ref[0]
    used = b < n_used
    e = blk_e_ref[b]
    new_expert = (b == 0) | (e != blk_e_ref[jnp.maximum(b - 1, 0)])
    n_slots = GATHER_AHEAD + 1
    xslot = b % n_slots

    def gather_rows(block, slot, lo, hi):
        row0 = jnp.minimum(block, n_used - 1) * EXPERT_BLOCK
        for r in range(lo, hi):
            pltpu.make_async_copy(x_hbm.at[pl.ds(tok_ref[row0 + r], 1), :],
                                  xg.at[slot, pl.ds(r, 1), :], gsem.at[slot]).start()

    def gather_wait(slot):
        pltpu.make_async_copy(x_hbm.at[pl.ds(0, EXPERT_BLOCK), :], xg.at[slot], gsem.at[slot]).wait()

    @pl.when(b == 0)
    def _():
        for k in range(GATHER_AHEAD):
            gather_rows(k, k, 0, EXPERT_BLOCK)

    def weight_copies(expert, slot):
        return (pltpu.make_async_copy(wg_hbm.at[expert], wg_f.at[slot], sem.at[slot]),
                pltpu.make_async_copy(wu_hbm.at[expert], wu_f.at[slot], sem.at[slot]),
                pltpu.make_async_copy(wd_hbm.at[expert], wd_f.at[slot], sem.at[slot]))

    def block_expert(blk):
        return blk_e_ref[jnp.minimum(blk, n_used - 1)]

    b_next = b + nblk_ref[e]
    b_next2 = b_next + nblk_ref[block_expert(b_next)]

    @pl.when(b == 0)
    def _():
        slot_ref[0] = 0
        for cp in weight_copies(e, 0):
            cp.start()

        @pl.when(b_next < n_used)
        def _():
            for cp in weight_copies(block_expert(b_next), 1):
                cp.start(priority=PREFETCH_PRIORITY)

    @pl.when(used & new_expert)
    def _():
        slot = slot_ref[0]

        @pl.when(b_next2 < n_used)
        def _():
            for cp in weight_copies(block_expert(b_next2), (slot + 2) % WEIGHT_SLOTS):
                cp.start(priority=PREFETCH_PRIORITY)

        for cp in weight_copies(e, slot):
            cp.wait()
        wg_s[...] = wg_f[slot].astype(BF16)
        wu_s[...] = wu_f[slot].astype(BF16)
        wd_s[...] = wd_f[slot].astype(BF16)
        slot_ref[0] = (slot + 1) % WEIGHT_SLOTS

    @pl.when(used)
    def _():
        gather_wait(xslot)
        ahead = b + GATHER_AHEAD
        aslot = ahead % n_slots
        per_group = EXPERT_BLOCK // GATHER_GROUPS
        x = xg[xslot]
        ms = jnp.mean(x * x, axis=-1, keepdims=True)
        h = ((x * lax.rsqrt(ms + EPS)) * g2_ref[...]).astype(BF16)
        gather_rows(ahead, aslot, 0, per_group)
        a = jnp.dot(h, wg_s[...], preferred_element_type=F32)
        gather_rows(ahead, aslot, per_group, 2 * per_group)
        u = jnp.dot(h, wu_s[...], preferred_element_type=F32)
        gather_rows(ahead, aslot, 2 * per_group, 3 * per_group)
        mid = ((a * _sigmoid(a)) * u).astype(BF16)
        gather_rows(ahead, aslot, 3 * per_group, EXPERT_BLOCK)
        y_ref[...] = jnp.dot(mid, wd_s[...], preferred_element_type=F32)

    @pl.when(b == n_used - 1)
    def _():
        for k in range(1, GATHER_AHEAD + 1):
            gather_wait((b + k) % n_slots)

    @pl.when(jnp.logical_not(used))
    def _():
        y_ref[...] = jnp.zeros_like(y_ref)


def _experts(blk_e, n_used, nblk, row_tok, x1, g2, w_g, w_u, w_d):
    n_rows = row_tok.shape[0]
    nb = n_rows // EXPERT_BLOCK
    hbm = pl.BlockSpec(memory_space=pl.ANY)
    grid_spec = pltpu.PrefetchScalarGridSpec(
        num_scalar_prefetch=4,
        grid=(nb,),
        in_specs=[
            hbm,
            pl.BlockSpec((1, D_MODEL), lambda b, *_: (0, 0)),
            hbm, hbm, hbm,
        ],
        out_specs=pl.BlockSpec((EXPERT_BLOCK, D_MODEL), lambda b, *_: (b, 0)),
        scratch_shapes=[pltpu.VMEM((GATHER_AHEAD + 1, EXPERT_BLOCK, D_MODEL), F32),
                        pltpu.VMEM((WEIGHT_SLOTS, D_MODEL, D_EXPERT), F32),
                        pltpu.VMEM((WEIGHT_SLOTS, D_MODEL, D_EXPERT), F32),
                        pltpu.VMEM((WEIGHT_SLOTS, D_EXPERT, D_MODEL), F32),
                        pltpu.VMEM((D_MODEL, D_EXPERT), BF16),
                        pltpu.VMEM((D_MODEL, D_EXPERT), BF16),
                        pltpu.VMEM((D_EXPERT, D_MODEL), BF16),
                        pltpu.SMEM((1,), I32),
                        pltpu.SemaphoreType.DMA((WEIGHT_SLOTS,)),
                        pltpu.SemaphoreType.DMA((GATHER_AHEAD + 1,))],
    )
    return pl.pallas_call(
        _experts_kernel,
        out_shape=jax.ShapeDtypeStruct((n_rows, D_MODEL), F32),
        grid_spec=grid_spec,
        compiler_params=pltpu.CompilerParams(
            dimension_semantics=("arbitrary",), vmem_limit_bytes=VMEM_LIMIT),
        name="experts",
    )(blk_e, n_used, nblk, row_tok, x1, g2, w_g, w_u, w_d)


def _combine_kernel(d1_ref, d2_ref, x1_ref, w_ref, y_ref, o_ref, gbuf, sem):
    i = pl.program_id(0)
    n = pl.num_programs(0)
    tm = x1_ref.shape[0]

    def request_row(base, slot, r):
        pltpu.make_async_copy(y_ref.at[pl.ds(d1_ref[base + r], 1), :],
                              gbuf.at[slot, pl.ds(r, 1), :], sem.at[slot]).start()
        pltpu.make_async_copy(y_ref.at[pl.ds(d2_ref[base + r], 1), :],
                              gbuf.at[slot, pl.ds(tm + r, 1), :], sem.at[slot]).start()

    def issue(tile, slot, lo, hi):
        base = jnp.minimum(tile, n - 1) * tm
        for r in range(lo, hi):
            request_row(base, slot, r)

    def tile_wait(slot):
        pltpu.make_async_copy(y_ref.at[pl.ds(0, 2 * tm), :], gbuf.at[slot], sem.at[slot]).wait()

    slot = i % 2

    @pl.when(i == 0)
    def _():
        def first_tile(r, _):
            request_row(0, 0, r)
            return 0

        lax.fori_loop(0, tm, first_tile, 0, unroll=SUBLANES)

    tile_wait(slot)
    for c in range(tm // COMBINE_ROWS):
        rs = slice(c * COMBINE_ROWS, (c + 1) * COMBINE_ROWS)
        rs2 = slice(tm + c * COMBINE_ROWS, tm + (c + 1) * COMBINE_ROWS)
        w = w_ref[rs, :]
        o_ref[rs, :] = x1_ref[rs, :] + (w[:, 0:1] * gbuf[slot, rs, :] + w[:, 1:2] * gbuf[slot, rs2, :])
        issue(i + 1, 1 - slot, c * COMBINE_ROWS, (c + 1) * COMBINE_ROWS)

    @pl.when(i == n - 1)
    def _():
        tile_wait(1 - slot)


def _combine(dest1, dest2, x1, w, yb):
    t = x1.shape[0]
    grid_spec = pltpu.PrefetchScalarGridSpec(
        num_scalar_prefetch=2,
        grid=(t // TM_CMB,),
        in_specs=[
            pl.BlockSpec((TM_CMB, D_MODEL), lambda i, a, b: (i, 0)),
            pl.BlockSpec((TM_CMB, LANES), lambda i, a, b: (i, 0)),
            pl.BlockSpec(memory_space=pl.ANY),
        ],
        out_specs=pl.BlockSpec((TM_CMB, D_MODEL), lambda i, a, b: (i, 0)),
        scratch_shapes=[pltpu.VMEM((2, 2 * TM_CMB, D_MODEL), F32),
                        pltpu.SemaphoreType.DMA((2,))],
    )
    return pl.pallas_call(
        _combine_kernel,
        out_shape=jax.ShapeDtypeStruct((t, D_MODEL), F32),
        grid_spec=grid_spec,
        compiler_params=pltpu.CompilerParams(
            dimension_semantics=("arbitrary",), vmem_limit_bytes=VMEM_LIMIT),
        name="combine",
    )(dest1, dest2, x1, w, yb)


def kernel(x, norm1_g, w_in, b_gate, b_forget, sgu_ln_g, sgu_ln_b, w_spatial, b_spatial, q_norm_g, k_norm_g, w_proj_sgu, w_proj_fox, w_out, norm2_g, w_router_group, b_router_group, w_router_expert, b_router_expert, w_expert_gate, w_expert_up, w_expert_down):
    batch, seq, d = x.shape
    t = batch * seq
    l = 0
    x2 = x.reshape(t, d)

    wt = jnp.swapaxes(w_in[l], 0, 1)
    off_f = N_MAIN_SEC * SEC
    w_f = jnp.pad(wt[off_f:off_f + FOX_HEADS].T, ((0, 0), (0, LANES - FOX_HEADS))).astype(BF16)
    b_f = jnp.pad(b_forget[l], (0, LANES - FOX_HEADS)).reshape(1, LANES)
    n_r = N_GROUPS + N_EXPERTS
    w_r = jnp.pad(jnp.concatenate([w_router_group[l], w_router_expert[l]], axis=1),
                  ((0, 0), (0, LANES - n_r))).astype(BF16)
    b_r = jnp.pad(jnp.concatenate([b_router_group[l], b_router_expert[l]]),
                  (0, LANES - n_r)).reshape(1, LANES)

    h, lf = _norm1(x2, norm1_g[l].reshape(1, d), w_f, b_f)
    a_sgu, q, kt, va, gates = _inproj(
        h, wt, b_gate[l].reshape(1, 2 * d),
        sgu_ln_g[l].reshape(1, SGU_WIDTH), sgu_ln_b[l].reshape(1, SGU_WIDTH),
        w_spatial[l], b_spatial[l].T, q_norm_g[l].reshape(1, HEAD_DIM),
        k_norm_g[l].reshape(1, HEAD_DIM))
    negc = _forget_cumsum(lf, batch, seq).reshape(batch, FOX_HEADS, 1, seq)
    a_fox = _attention(q, kt, va, negc, batch, seq)
    x1, logits = _mix(a_sgu, a_fox, gates, x2, w_proj_sgu[l], w_proj_fox[l], w_out[l],
                      norm2_g[l].reshape(1, d), w_r, b_r)

    n_assign = 2 * t
    n_rows = n_assign + N_EXPERTS * EXPERT_BLOCK
    nb = n_rows // EXPERT_BLOCK
    dest_t, meta_f, bmeta, nblk, row_tok = _route(logits, n_rows // LANES)
    dest1, dest2 = dest_t[:, 0, :].reshape(t), dest_t[:, 1, :].reshape(t)
    n_used = bmeta[0:1, 1]
    yb = _experts(bmeta[:nb, 0], n_used, nblk[0, :N_EXPERTS], row_tok.reshape(n_rows), x1,
                  norm2_g[l].reshape(1, d), w_expert_gate[l], w_expert_up[l], w_expert_down[l])
    out = _combine(dest1, dest2, x1, meta_f, yb)
    return out.reshape(batch, seq, d)
```

```python
import functools

import jax
import jax.numpy as jnp
from jax import lax
from jax.experimental import pallas as pl
from jax.experimental.pallas import tpu as pltpu

F32 = jnp.float32
BF16 = jnp.bfloat16
I32 = jnp.int32

D_MODEL = 2048
CHUNK = 128
SGU_GROUPS = 8
SGU_WIDTH = 1024
FOX_HEADS = 8
HEAD_DIM = 128
FOX_WIDTH = 1024
N_GROUPS = 4
EXPERTS_PER_GROUP = 8
N_EXPERTS = 32
D_EXPERT = 512
EXPERT_BLOCK = 128
EPS = 1e-6

LANES = 128
SUBLANES = 8
PREFETCH_PRIORITY = 1
VMEM_LIMIT = 56 * 1024 * 1024

SEC = 1024
N_MAIN_SEC = 5
N_GATE_SEC = 4
TM_NORM = 1024
TM_IN = 512
TQ = 256
HEADS_PER_STEP = 4
TM_MIX = 256
RCH = 512
TM_CMB = 512
COMBINE_ROWS = 64
N_SPLIT = 3
EXPERT_GROUP_SHIFT = EXPERTS_PER_GROUP.bit_length() - 1
LOG2E = 1.4426950408889634
Q_SCALE = HEAD_DIM ** -0.5 * LOG2E


def _sigmoid(x):
    return 1.0 / (1.0 + jnp.exp(-x))


def _log_sigmoid(x):
    return jnp.minimum(x, 0.0) - jnp.log1p(jnp.exp(-jnp.abs(x)))


def _norm1_kernel(x_ref, g1_ref, wf_ref, bf_ref, h_ref, lf_ref):
    x = x_ref[...]
    ms = jnp.mean(x * x, axis=-1, keepdims=True)
    hb = ((x * lax.rsqrt(ms + EPS)) * g1_ref[...]).astype(BF16)
    h_ref[...] = hb
    f = jnp.dot(hb, wf_ref[...], preferred_element_type=F32) + bf_ref[...]
    lf_ref[...] = _log_sigmoid(f)


def _norm1(x2, g1, w_f, b_f):
    t = x2.shape[0]
    row = lambda i: (i, 0)
    const = lambda i: (0, 0)
    return pl.pallas_call(
        _norm1_kernel,
        out_shape=(jax.ShapeDtypeStruct((t, D_MODEL), BF16),
                   jax.ShapeDtypeStruct((t, LANES), F32)),
        grid=(t // TM_NORM,),
        in_specs=[pl.BlockSpec((TM_NORM, D_MODEL), row),
                  pl.BlockSpec((1, D_MODEL), const),
                  pl.BlockSpec((D_MODEL, LANES), const),
                  pl.BlockSpec((1, LANES), const)],
        out_specs=(pl.BlockSpec((TM_NORM, D_MODEL), row), pl.BlockSpec((TM_NORM, LANES), row)),
        compiler_params=pltpu.CompilerParams(
            dimension_semantics=("arbitrary",), vmem_limit_bytes=VMEM_LIMIT),
        name="norm1",
    )(x2, g1, w_f, b_f)


N_SECTIONS = 1 + 3 + N_GATE_SEC
N_CHUNKS = N_MAIN_SEC + N_GATE_SEC
GATE_SECTION0 = 4
WCONV_ROWS = 256


def _inproj_kernel(h_ref, wt_hbm, bg_ref, lng_ref, lnb_ref, wsp_ref, bsp_ref, qg_ref, kg_ref,
                   asgu_ref, q_ref, kt_ref, va_ref, gates_ref,
                   stage, wb, sem):
    s = pl.program_id(0)
    i = pl.program_id(1)
    tm = h_ref.shape[0]

    def chunk_copy(c, slot):
        r0 = pl.multiple_of(c * SEC + jnp.where(c >= N_MAIN_SEC, FOX_HEADS, 0), 8)
        return pltpu.make_async_copy(wt_hbm.at[pl.ds(r0, SEC), :], stage.at[slot], sem.at[slot])

    def convert(slot, k):
        for p in range(SEC // WCONV_ROWS):
            rs = slice(p * WCONV_ROWS, (p + 1) * WCONV_ROWS)
            wb[k, :, rs] = stage[slot, rs, :].T.astype(BF16)

    @pl.when((s == 0) & (i == 0))
    def _():
        chunk_copy(0, 0).start()
        chunk_copy(1, 1).start()
        chunk_copy(0, 0).wait()
        convert(0, 0)
        chunk_copy(2, 0).start(priority=PREFETCH_PRIORITY)
        chunk_copy(1, 1).wait()
        convert(1, 1)

    @pl.when((s > 0) & (i == 0))
    def _():
        c = s + 1
        slot = c % 2
        chunk_copy(c, slot).wait()

        @pl.when(c + 1 < N_CHUNKS)
        def _():
            chunk_copy(c + 1, 1 - slot).start(priority=PREFETCH_PRIORITY)

        convert(slot, 0)

    def section(k=0):
        return jnp.dot(h_ref[...], wb[k], preferred_element_type=F32)

    @pl.when(s == 0)
    def _():
        u = jax.nn.gelu(section(0))
        v = jax.nn.gelu(section(1))
        mu = jnp.mean(v, axis=-1, keepdims=True)
        vc = v - mu
        var = jnp.mean(vc * vc, axis=-1, keepdims=True)
        vn = ((vc * lax.rsqrt(var + EPS)) * lng_ref[...] + lnb_ref[...]).astype(BF16)
        row = lax.broadcasted_iota(I32, (CHUNK, CHUNK), 0)
        col = lax.broadcasted_iota(I32, (CHUNK, CHUNK), 1)
        causal = row >= col
        for g in range(SGU_GROUPS):
            wg = jnp.where(causal, wsp_ref[g], 0.0).astype(BF16)
            bcol = bsp_ref[:, g:g + 1]
            gs = slice(g * LANES, (g + 1) * LANES)
            n_ch = tm // CHUNK
            rhs = jnp.concatenate([vn[c * CHUNK:(c + 1) * CHUNK, gs] for c in range(n_ch)], axis=1)
            sg = jnp.dot(wg, rhs, preferred_element_type=F32) + bcol
            for c in range(n_ch):
                rs = slice(c * CHUNK, (c + 1) * CHUNK)
                asgu_ref[rs, gs] = (u[rs, gs] * sg[:, c * LANES:(c + 1) * LANES]).astype(BF16)

    def _head_norm(z, h, gain_ref):
        zh = z[:, h * HEAD_DIM:(h + 1) * HEAD_DIM]
        ms = jnp.mean(zh * zh, axis=-1, keepdims=True)
        return (zh * lax.rsqrt(ms + EPS)) * gain_ref[...]

    @pl.when(s == 1)
    def _():
        z = section()
        for h in range(FOX_HEADS):
            q_ref[:, h * HEAD_DIM:(h + 1) * HEAD_DIM] = (
                _head_norm(z, h, qg_ref) * Q_SCALE).astype(BF16)

    @pl.when(s == 2)
    def _():
        z = section()
        for h in range(FOX_HEADS):
            kt_ref[h * HEAD_DIM:(h + 1) * HEAD_DIM, :] = _head_norm(z, h, kg_ref).T.astype(BF16)

    @pl.when(s == 3)
    def _():
        va_ref[...] = section().astype(BF16)

    @pl.when(s >= GATE_SECTION0)
    def _():
        gates_ref[...] = (0.5 * jnp.tanh(0.5 * (section() + bg_ref[...])) + 0.5).astype(BF16)


def _inproj(h, wt, b_gate, ln_g, ln_b, w_sp, b_sp_t, q_g, k_g):
    t = h.shape[0]
    n_i = t // TM_IN

    def active(sec):
        return lambda s, i: jnp.where(s < sec, 0, jnp.where(s > sec, n_i - 1, i))

    def gate_blk(s, i):
        g = jnp.clip(s - GATE_SECTION0, 0, N_GATE_SEC - 1)
        return (jnp.where(s < GATE_SECTION0, 0, i), g)

    const2 = lambda s, i: (0, 0)
    out_shape = (
        jax.ShapeDtypeStruct((t, SGU_WIDTH), BF16),
        jax.ShapeDtypeStruct((t, FOX_WIDTH), BF16),
        jax.ShapeDtypeStruct((FOX_WIDTH, t), BF16),
        jax.ShapeDtypeStruct((t, FOX_WIDTH), BF16),
        jax.ShapeDtypeStruct((t, 2 * D_MODEL), BF16),
    )
    return pl.pallas_call(
        _inproj_kernel,
        out_shape=out_shape,
        grid=(N_SECTIONS, n_i),
        in_specs=[
            pl.BlockSpec((TM_IN, D_MODEL), lambda s, i: (i, 0)),
            pl.BlockSpec(memory_space=pl.ANY),
            pl.BlockSpec((1, SEC), lambda s, i: (0, jnp.clip(s - GATE_SECTION0, 0, N_GATE_SEC - 1))),
            pl.BlockSpec((1, SGU_WIDTH), const2),
            pl.BlockSpec((1, SGU_WIDTH), const2),
            pl.BlockSpec((SGU_GROUPS, CHUNK, CHUNK), lambda s, i: (0, 0, 0)),
            pl.BlockSpec((CHUNK, SGU_GROUPS), const2),
            pl.BlockSpec((1, HEAD_DIM), const2),
            pl.BlockSpec((1, HEAD_DIM), const2),
        ],
        out_specs=(
            pl.BlockSpec((TM_IN, SGU_WIDTH), lambda s, i: (active(0)(s, i), 0)),
            pl.BlockSpec((TM_IN, FOX_WIDTH), lambda s, i: (active(1)(s, i), 0)),
            pl.BlockSpec((FOX_WIDTH, TM_IN), lambda s, i: (0, active(2)(s, i))),
            pl.BlockSpec((TM_IN, FOX_WIDTH), lambda s, i: (active(3)(s, i), 0)),
            pl.BlockSpec((TM_IN, SEC), gate_blk),
        ),
        scratch_shapes=[pltpu.VMEM((2, SEC, D_MODEL), F32),
                        pltpu.VMEM((2, D_MODEL, SEC), BF16),
                        pltpu.SemaphoreType.DMA((2,))],
        compiler_params=pltpu.CompilerParams(
            dimension_semantics=("arbitrary", "arbitrary"), vmem_limit_bytes=VMEM_LIMIT),
        name="inproj",
    )(h, wt, b_gate, ln_g, ln_b, w_sp, b_sp_t, q_g, k_g)


def _split3(x):
    x0 = x.astype(BF16)
    r1 = x - x0.astype(F32)
    x1 = r1.astype(BF16)
    r2 = r1 - x1.astype(F32)
    return x0, x1, r2.astype(BF16)


def _cumsum_kernel(lf_ref, negc_ref, c_s):
    seq = lf_ref.shape[0]
    row = lax.broadcasted_iota(I32, (CHUNK, CHUNK), 0)
    col = lax.broadcasted_iota(I32, (CHUNK, CHUNK), 1)
    tri = jnp.where(row >= col, 1.0, 0.0).astype(BF16)
    carry = jnp.zeros((1, LANES), F32)
    for r in range(seq // CHUNK):
        rs = slice(r * CHUNK, (r + 1) * CHUNK)
        x0, x1, x2 = _split3(lf_ref[rs, :])
        cs = (jnp.dot(tri, x0, preferred_element_type=F32)
              + jnp.dot(tri, x1, preferred_element_type=F32)
              + jnp.dot(tri, x2, preferred_element_type=F32)) + carry
        carry = cs[CHUNK - 1:CHUNK, :]
        c_s[rs, :] = cs
    ct = c_s[...].T
    negc_ref[0] = -ct[0:FOX_HEADS, :]


def _forget_cumsum(lf, batch, seq):
    return pl.pallas_call(
        _cumsum_kernel,
        out_shape=jax.ShapeDtypeStruct((batch, FOX_HEADS, seq), F32),
        grid=(batch,),
        in_specs=[pl.BlockSpec((seq, LANES), lambda b: (b, 0))],
        out_specs=pl.BlockSpec((1, FOX_HEADS, seq), lambda b: (b, 0, 0)),
        scratch_shapes=[pltpu.VMEM((seq, LANES), F32)],
        compiler_params=pltpu.CompilerParams(
            dimension_semantics=("arbitrary",), vmem_limit_bytes=VMEM_LIMIT),
        name="forget_cumsum",
    )(lf)


def _attn_kernel(q_ref, kt_ref, v_ref, negc_ref, o_ref, kx, vx):
    seq = q_ref.shape[0]
    sub = lax.broadcasted_iota(I32, (HEAD_DIM, seq), 0)
    for hh in range(HEADS_PER_STEP):
        hs = slice(hh * HEAD_DIM, (hh + 1) * HEAD_DIM)
        c0, c1, c2 = (c.astype(F32) for c in _split3(negc_ref[0, hh] * LOG2E))
        kx[hh, 0:HEAD_DIM, :] = kt_ref[hs, :]
        kx[hh, HEAD_DIM:, :] = jnp.where(
            sub == 0, c0, jnp.where(sub == 1, c1, jnp.where(sub == 2, c2, 0.0))).astype(BF16)
        vx[hh, :, 0:HEAD_DIM] = v_ref[:, hs]
        vx[hh, :, HEAD_DIM:] = jnp.ones((seq, HEAD_DIM), BF16)
    lane = lax.broadcasted_iota(I32, (TQ, HEAD_DIM), 1)
    bias_cols = jnp.where(lane < N_SPLIT, 1.0, 0.0).astype(BF16)
    row = lax.broadcasted_iota(I32, (TQ, TQ), 0)
    col = lax.broadcasted_iota(I32, (TQ, TQ), 1)
    causal = row >= col
    for qi in range(seq // TQ):
        for hh in range(HEADS_PER_STEP):
            hs = slice(hh * HEAD_DIM, (hh + 1) * HEAD_DIM)
            k0 = qi * TQ
            q = jnp.concatenate([q_ref[k0:k0 + TQ, hs], bias_cols], axis=1)
            s_d = jnp.dot(q, kx[hh, :, k0:k0 + TQ], preferred_element_type=F32)
            s_d = jnp.where(causal, s_d, -jnp.inf)
            m = jnp.max(s_d, axis=-1, keepdims=True)
            if qi > 0:
                s_o = jnp.dot(q, kx[hh, :, 0:k0], preferred_element_type=F32)
                m = jnp.maximum(m, jnp.max(s_o, axis=-1, keepdims=True))
            acc = jnp.dot(jnp.exp2(s_d - m).astype(BF16), vx[hh, k0:k0 + TQ, :],
                          preferred_element_type=F32)
            if qi > 0:
                acc = acc + jnp.dot(jnp.exp2(s_o - m).astype(BF16), vx[hh, 0:k0, :],
                                    preferred_element_type=F32)
            inv_l = 1.0 / acc[:, HEAD_DIM:HEAD_DIM + 1]
            o_ref[k0:k0 + TQ, hs] = (acc[:, 0:HEAD_DIM] * inv_l).astype(BF16)


def _attention(q, kt, va, negc, batch, seq):
    t = q.shape[0]
    width = HEADS_PER_STEP * HEAD_DIM
    blk = pl.BlockSpec((seq, width), lambda b, g: (b, g))
    return pl.pallas_call(
        _attn_kernel,
        out_shape=jax.ShapeDtypeStruct((t, FOX_WIDTH), BF16),
        grid=(batch, FOX_HEADS // HEADS_PER_STEP),
        in_specs=[blk,
                  pl.BlockSpec((width, seq), lambda b, g: (g, b)),
                  blk,
                  pl.BlockSpec((1, HEADS_PER_STEP, 1, seq), lambda b, g: (b, g, 0, 0))],
        out_specs=blk,
        scratch_shapes=[pltpu.VMEM((HEADS_PER_STEP, 2 * HEAD_DIM, seq), BF16),
                        pltpu.VMEM((HEADS_PER_STEP, seq, 2 * HEAD_DIM), BF16)],
        compiler_params=pltpu.CompilerParams(
            dimension_semantics=("arbitrary", "arbitrary"), vmem_limit_bytes=VMEM_LIMIT),
        name="fox_attention",
    )(q, kt, va, negc)


MIX_WROWS = 512


def _mix_kernel(as_ref, af_ref, g_ref, x_ref, wps_hbm, wpf_hbm, wo_hbm, g2_ref, wr_ref, br_ref,
                x1_ref, lg_ref, wps_ref, wpf_ref, wo_ref, stage, sem):
    @pl.when(pl.program_id(0) == 0)
    def _():
        pieces = [(src, dst, r0) for src, dst in ((wps_hbm, wps_ref), (wpf_hbm, wpf_ref),
                                                  (wo_hbm, wo_ref))
                  for r0 in range(0, dst.shape[0], MIX_WROWS)]

        def piece_copy(k):
            src, _, r0 = pieces[k]
            return pltpu.make_async_copy(src.at[pl.ds(r0, MIX_WROWS), :], stage.at[k % 2],
                                         sem.at[k % 2])

        piece_copy(0).start()
        for k, (_, dst, r0) in enumerate(pieces):
            if k + 1 < len(pieces):
                piece_copy(k + 1).start()
            piece_copy(k).wait()
            dst[r0:r0 + MIX_WROWS, :] = stage[k % 2].astype(BF16)

    ys = jnp.dot(as_ref[...], wps_ref[...], preferred_element_type=F32)
    yf = jnp.dot(af_ref[...], wpf_ref[...], preferred_element_type=F32)
    m = (g_ref[:, :D_MODEL].astype(F32) * ys + g_ref[:, D_MODEL:].astype(F32) * yf).astype(BF16)
    x1 = x_ref[...] + jnp.dot(m, wo_ref[...], preferred_element_type=F32)
    x1_ref[...] = x1
    ms = jnp.mean(x1 * x1, axis=-1, keepdims=True)
    h2 = ((x1 * lax.rsqrt(ms + EPS)) * g2_ref[...]).astype(BF16)
    lg_ref[...] = jnp.dot(h2, wr_ref[...], preferred_element_type=F32) + br_ref[...]


def _mix(a_sgu, a_fox, gates, x2, wps, wpf, wo, g2, w_r, b_r):
    t = x2.shape[0]
    row = lambda i: (i, 0)
    const = lambda i: (0, 0)
    resident = functools.partial(pl.BlockSpec, index_map=const, pipeline_mode=pl.Buffered(1))
    return pl.pallas_call(
        _mix_kernel,
        out_shape=(jax.ShapeDtypeStruct((t, D_MODEL), F32),
                   jax.ShapeDtypeStruct((t, LANES), F32)),
        grid=(t // TM_MIX,),
        in_specs=[
            pl.BlockSpec((TM_MIX, SGU_WIDTH), row),
            pl.BlockSpec((TM_MIX, FOX_WIDTH), row),
            pl.BlockSpec((TM_MIX, 2 * D_MODEL), row),
            pl.BlockSpec((TM_MIX, D_MODEL), row),
            pl.BlockSpec(memory_space=pl.ANY),
            pl.BlockSpec(memory_space=pl.ANY),
            pl.BlockSpec(memory_space=pl.ANY),
            pl.BlockSpec((1, D_MODEL), const),
            resident((D_MODEL, LANES)),
            pl.BlockSpec((1, LANES), const),
        ],
        out_specs=(pl.BlockSpec((TM_MIX, D_MODEL), row), pl.BlockSpec((TM_MIX, LANES), row)),
        scratch_shapes=[pltpu.VMEM((SGU_WIDTH, D_MODEL), BF16),
                        pltpu.VMEM((FOX_WIDTH, D_MODEL), BF16),
                        pltpu.VMEM((D_MODEL, D_MODEL), BF16),
                        pltpu.VMEM((2, MIX_WROWS, D_MODEL), F32),
                        pltpu.SemaphoreType.DMA((2,))],
        compiler_params=pltpu.CompilerParams(
            dimension_semantics=("arbitrary",), vmem_limit_bytes=VMEM_LIMIT),
        name="mix",
    )(a_sgu, a_fox, gates, x2, wps, wpf, wo, g2, w_r, b_r)


def _route_kernel(lg_ref, dt_ref, mf_ref, bm_ref, em_ref, bt_ref, mi_ref, inv_ref):
    t = lg_ref.shape[0]
    n_chunks = t // RCH
    lane_i = lax.broadcasted_iota(I32, (RCH, LANES), 1)
    lane = lane_i.astype(F32)
    lane_grp = ((lane_i - N_GROUPS) >> EXPERT_GROUP_SHIFT).astype(F32)
    is_grp = lane_i < N_GROUPS
    is_exp = (lane_i >= N_GROUPS) & (lane_i < N_GROUPS + N_EXPERTS)
    r_i = lax.broadcasted_iota(I32, (RCH, RCH), 0)
    c_i = lax.broadcasted_iota(I32, (RCH, RCH), 1)
    strict_lower = jnp.where(r_i > c_i, 1.0, 0.0).astype(BF16)
    neg_inf = -jnp.inf

    def first_max(vals):
        vmax = jnp.max(vals, axis=-1, keepdims=True)
        idx = jnp.min(jnp.where(vals == vmax, lane, float(LANES)), axis=-1, keepdims=True)
        return vmax, idx

    def pick(table, idx):
        return jnp.sum(jnp.where(lane == idx, table, 0.0), axis=-1, keepdims=True)

    def pack(cols):
        out = jnp.zeros((RCH, LANES), F32)
        for n, c in enumerate(cols):
            out = jnp.where(lane_i == n, c, out)
        return out

    def pass1(ci, counts):
        r0 = pl.multiple_of(ci * RCH, RCH)
        lg = lg_ref[pl.ds(r0, RCH), :]
        gmax, grp = first_max(jnp.where(is_grp, lg, neg_inf))
        p_grp = 1.0 / jnp.sum(jnp.where(is_grp, jnp.exp(lg - gmax), 0.0), axis=-1, keepdims=True)
        el = jnp.where(is_exp & (lane_grp == grp), lg, neg_inf)
        v1, i1 = first_max(el)
        v2, i2 = first_max(jnp.where(lane == i1, neg_inf, el))
        e21 = jnp.exp(v2 - v1)
        w1 = p_grp / (1.0 + e21)
        w2 = p_grp * e21 / (1.0 + e21)
        e1 = i1 - float(N_GROUPS)
        e2 = i2 - float(N_GROUPS)
        hot = jnp.where((lane == e1) | (lane == e2), 1.0, 0.0)
        before = jnp.dot(strict_lower, hot.astype(BF16), preferred_element_type=F32) + counts
        mi_ref[pl.ds(r0, RCH), :] = pack([e1, e2, pick(before, e1), pick(before, e2)]).astype(I32)
        mf_ref[pl.ds(r0, RCH), :] = pack([w1, w2])
        return counts + jnp.sum(hot, axis=0, keepdims=True)

    counts = lax.fori_loop(0, n_chunks, pass1, jnp.zeros((1, LANES), F32))

    nblk = jnp.floor((counts + float(EXPERT_BLOCK - 1)) * (1.0 / EXPERT_BLOCK))
    u_r = lax.broadcasted_iota(I32, (LANES, LANES), 0)
    u_c = lax.broadcasted_iota(I32, (LANES, LANES), 1)
    strict_upper = jnp.where(u_r < u_c, 1.0, 0.0).astype(BF16)
    bstart = jnp.dot(jnp.broadcast_to(nblk, (SUBLANES, LANES)).astype(BF16), strict_upper,
                     preferred_element_type=F32)[0:1, :]
    bend = bstart + nblk

    def pass2(ci, _):
        r0 = pl.multiple_of(ci * RCH, RCH)
        mi = mi_ref[pl.ds(r0, RCH), :].astype(F32)
        cols = [jnp.sum(jnp.where(lane_i == n, mi, 0.0), axis=-1, keepdims=True) for n in range(4)]
        d1 = pick(bstart, cols[0]) * float(EXPERT_BLOCK) + cols[2]
        d2 = pick(bstart, cols[1]) * float(EXPERT_BLOCK) + cols[3]
        d_t = pack([d1, d2]).T
        dt_ref[ci] = d_t[0:SUBLANES, :].astype(I32)
        tok = (lax.broadcasted_iota(I32, (RCH, 1), 0) + r0).astype(F32)
        tok_hi = jnp.floor(tok * (1.0 / LANES))
        tok_lo = tok - tok_hi * float(LANES)
        for n, d in enumerate((d1, d2)):
            d_hi = jnp.floor(d * (1.0 / LANES))
            hit = lane == (d - d_hi * float(LANES))
            blk_row = jnp.floor(d_t[n:n + 1, :] * (1.0 / LANES))
            sel = jnp.where(inv_blk == blk_row, 1.0, 0.0).astype(BF16)
            inv_ref[0] += jnp.dot(sel, jnp.where(hit, tok_hi, 0.0).astype(BF16),
                                  preferred_element_type=F32)
            inv_ref[1] += jnp.dot(sel, jnp.where(hit, tok_lo, 0.0).astype(BF16),
                                  preferred_element_type=F32)
        return 0

    nb_rows = bm_ref.shape[0]
    inv_blk = lax.broadcasted_iota(I32, (nb_rows, RCH), 0).astype(F32)
    inv_ref[...] = jnp.zeros_like(inv_ref)
    lax.fori_loop(0, n_chunks, pass2, 0)
    bt_ref[...] = (inv_ref[0] * float(LANES) + inv_ref[1]).astype(I32)

    b_col = lax.broadcasted_iota(I32, (nb_rows, LANES), 0).astype(F32)
    b_lane = lax.broadcasted_iota(I32, (nb_rows, LANES), 1)
    done = jnp.where((bend <= b_col) & (b_lane < N_EXPERTS), 1.0, 0.0)
    blk_e = jnp.minimum(jnp.sum(done, axis=-1, keepdims=True), float(N_EXPERTS - 1))
    n_used = jnp.sum(jnp.where(b_lane[0:1, :] == N_EXPERTS - 1, bend, 0.0), axis=-1, keepdims=True)
    bm_ref[...] = jnp.where(b_lane == 0, blk_e, jnp.where(b_lane == 1, n_used, 0.0)).astype(I32)

    em_ref[...] = jnp.broadcast_to(nblk, (SUBLANES, LANES)).astype(I32)


def _route(logits, nb_rows):
    t = logits.shape[0]
    full = lambda shape: pl.BlockSpec(shape, lambda: (0,) * len(shape))
    return pl.pallas_call(
        _route_kernel,
        out_shape=(jax.ShapeDtypeStruct((t // RCH, SUBLANES, RCH), I32),
                   jax.ShapeDtypeStruct((t, LANES), F32),
                   jax.ShapeDtypeStruct((nb_rows, LANES), I32),
                   jax.ShapeDtypeStruct((SUBLANES, LANES), I32),
                   jax.ShapeDtypeStruct((nb_rows, LANES), I32)),
        in_specs=[full((t, LANES))],
        out_specs=(full((t // RCH, SUBLANES, RCH)), full((t, LANES)), full((nb_rows, LANES)),
                   full((SUBLANES, LANES)), full((nb_rows, LANES))),
        scratch_shapes=[pltpu.VMEM((t, LANES), I32), pltpu.VMEM((2, nb_rows, LANES), F32)],
        compiler_params=pltpu.CompilerParams(vmem_limit_bytes=VMEM_LIMIT),
        name="route",
    )(logits)


GATHER_GROUPS = 4
GATHER_AHEAD = 6
WEIGHT_SLOTS = 3


def _experts_kernel(blk_e_ref, nused_ref, nblk_ref, tok_ref, x_hbm, g2_ref, wg_hbm, wu_hbm, wd_hbm,
                    y_ref, xg, wg_f, wu_f, wd_f, wg_s, wu_s, wd_s, slot_ref, sem, gsem):
    b = pl.program_id(0)
    n_used = nused_ref[0]
    used = b < n_used
    e = blk_e_ref[b]
    new_expert = (b == 0) | (e != blk_e_ref[jnp.maximum(b - 1, 0)])
    n_slots = GATHER_AHEAD + 1
    xslot = b % n_slots

    def gather_rows(block, slot, lo, hi):
        row0 = jnp.minimum(block, n_used - 1) * EXPERT_BLOCK
        for r in range(lo, hi):
            pltpu.make_async_copy(x_hbm.at[pl.ds(tok_ref[row0 + r], 1), :],
                                  xg.at[slot, pl.ds(r, 1), :], gsem.at[slot]).start()

    def gather_wait(slot):
        pltpu.make_async_copy(x_hbm.at[pl.ds(0, EXPERT_BLOCK), :], xg.at[slot], gsem.at[slot]).wait()

    def weight_copies(expert, slot):
        return (pltpu.make_async_copy(wg_hbm.at[expert], wg_f.at[slot], sem.at[slot]),
                pltpu.make_async_copy(wu_hbm.at[expert], wu_f.at[slot], sem.at[slot]),
                pltpu.make_async_copy(wd_hbm.at[expert], wd_f.at[slot], sem.at[slot]))

    def block_expert(blk):
        return blk_e_ref[jnp.minimum(blk, n_used - 1)]

    b_next = b + nblk_ref[e]
    b_next2 = b_next + nblk_ref[block_expert(b_next)]

    @pl.when(b == 0)
    def _():
        slot_ref[0] = 0
        for cp in weight_copies(e, 0):
            cp.start()

        @pl.when(b_next < n_used)
        def _():
            for cp in weight_copies(block_expert(b_next), 1):
                cp.start(priority=PREFETCH_PRIORITY)

        for k in range(GATHER_AHEAD):
            gather_rows(k, k, 0, EXPERT_BLOCK)

    @pl.when(used & new_expert)
    def _():
        slot = slot_ref[0]

        @pl.when(b_next2 < n_used)
        def _():
            for cp in weight_copies(block_expert(b_next2), (slot + 2) % WEIGHT_SLOTS):
                cp.start(priority=PREFETCH_PRIORITY)

        for cp in weight_copies(e, slot):
            cp.wait()
        wg_s[...] = wg_f[slot].astype(BF16)
        wu_s[...] = wu_f[slot].astype(BF16)
        wd_s[...] = wd_f[slot].astype(BF16)
        slot_ref[0] = (slot + 1) % WEIGHT_SLOTS

    @pl.when(used)
    def _():
        gather_wait(xslot)
        ahead = b + GATHER_AHEAD
        aslot = ahead % n_slots
        per_group = EXPERT_BLOCK // GATHER_GROUPS
        x = xg[xslot]
        ms = jnp.mean(x * x, axis=-1, keepdims=True)
        h = ((x * lax.rsqrt(ms + EPS)) * g2_ref[...]).astype(BF16)
        gather_rows(ahead, aslot, 0, per_group)
        a = jnp.dot(h, wg_s[...], preferred_element_type=F32)
        gather_rows(ahead, aslot, per_group, 2 * per_group)
        u = jnp.dot(h, wu_s[...], preferred_element_type=F32)
        gather_rows(ahead, aslot, 2 * per_group, 3 * per_group)
        mid = ((a * _sigmoid(a)) * u).astype(BF16)
        gather_rows(ahead, aslot, 3 * per_group, EXPERT_BLOCK)
        y_ref[...] = jnp.dot(mid, wd_s[...], preferred_element_type=F32)

    @pl.when(b == n_used - 1)
    def _():
        for k in range(1, GATHER_AHEAD + 1):
            gather_wait((b + k) % n_slots)

    @pl.when(jnp.logical_not(used))
    def _():
        y_ref[...] = jnp.zeros_like(y_ref)


def _experts(blk_e, n_used, nblk, row_tok, x1, g2, w_g, w_u, w_d):
    n_rows = row_tok.shape[0]
    nb = n_rows // EXPERT_BLOCK
    hbm = pl.BlockSpec(memory_space=pl.ANY)
    grid_spec = pltpu.PrefetchScalarGridSpec(
        num_scalar_prefetch=4,
        grid=(nb,),
        in_specs=[
            hbm,
            pl.BlockSpec((1, D_MODEL), lambda b, *_: (0, 0)),
            hbm, hbm, hbm,
        ],
        out_specs=pl.BlockSpec((EXPERT_BLOCK, D_MODEL), lambda b, *_: (b, 0)),
        scratch_shapes=[pltpu.VMEM((GATHER_AHEAD + 1, EXPERT_BLOCK, D_MODEL), F32),
                        pltpu.VMEM((WEIGHT_SLOTS, D_MODEL, D_EXPERT), F32),
                        pltpu.VMEM((WEIGHT_SLOTS, D_MODEL, D_EXPERT), F32),
                        pltpu.VMEM((WEIGHT_SLOTS, D_EXPERT, D_MODEL), F32),
                        pltpu.VMEM((D_MODEL, D_EXPERT), BF16),
                        pltpu.VMEM((D_MODEL, D_EXPERT), BF16),
                        pltpu.VMEM((D_EXPERT, D_MODEL), BF16),
                        pltpu.SMEM((1,), I32),
                        pltpu.SemaphoreType.DMA((WEIGHT_SLOTS,)),
                        pltpu.SemaphoreType.DMA((GATHER_AHEAD + 1,))],
    )
    return pl.pallas_call(
        _experts_kernel,
        out_shape=jax.ShapeDtypeStruct((n_rows, D_MODEL), F32),
        grid_spec=grid_spec,
        compiler_params=pltpu.CompilerParams(
            dimension_semantics=("arbitrary",), vmem_limit_bytes=VMEM_LIMIT),
        name="experts",
    )(blk_e, n_used, nblk, row_tok, x1, g2, w_g, w_u, w_d)


def _combine_kernel(d1_ref, d2_ref, x1_ref, w_ref, y_ref, o_ref, gbuf, sem):
    i = pl.program_id(0)
    n = pl.num_programs(0)
    tm = x1_ref.shape[0]

    def request_row(base, slot, r):
        pltpu.make_async_copy(y_ref.at[pl.ds(d1_ref[base + r], 1), :],
                              gbuf.at[slot, pl.ds(r, 1), :], sem.at[slot]).start()
        pltpu.make_async_copy(y_ref.at[pl.ds(d2_ref[base + r], 1), :],
                              gbuf.at[slot, pl.ds(tm + r, 1), :], sem.at[slot]).start()

    def issue(tile, slot, lo, hi):
        base = jnp.minimum(tile, n - 1) * tm
        for r in range(lo, hi):
            request_row(base, slot, r)

    def tile_wait(slot):
        pltpu.make_async_copy(y_ref.at[pl.ds(0, 2 * tm), :], gbuf.at[slot], sem.at[slot]).wait()

    slot = i % 2

    @pl.when(i == 0)
    def _():
        def first_tile(r, _):
            request_row(0, 0, r)
            return 0

        lax.fori_loop(0, tm, first_tile, 0, unroll=SUBLANES)

    tile_wait(slot)
    for c in range(tm // COMBINE_ROWS):
        rs = slice(c * COMBINE_ROWS, (c + 1) * COMBINE_ROWS)
        rs2 = slice(tm + c * COMBINE_ROWS, tm + (c + 1) * COMBINE_ROWS)
        w = w_ref[rs, :]
        o_ref[rs, :] = x1_ref[rs, :] + (w[:, 0:1] * gbuf[slot, rs, :] + w[:, 1:2] * gbuf[slot, rs2, :])
        issue(i + 1, 1 - slot, c * COMBINE_ROWS, (c + 1) * COMBINE_ROWS)

    @pl.when(i == n - 1)
    def _():
        tile_wait(1 - slot)


def _combine(dest1, dest2, x1, w, yb):
    t = x1.shape[0]
    grid_spec = pltpu.PrefetchScalarGridSpec(
        num_scalar_prefetch=2,
        grid=(t // TM_CMB,),
        in_specs=[
            pl.BlockSpec((TM_CMB, D_MODEL), lambda i, a, b: (i, 0)),
            pl.BlockSpec((TM_CMB, LANES), lambda i, a, b: (i, 0)),
            pl.BlockSpec(memory_space=pl.ANY),
        ],
        out_specs=pl.BlockSpec((TM_CMB, D_MODEL), lambda i, a, b: (i, 0)),
        scratch_shapes=[pltpu.VMEM((2, 2 * TM_CMB, D_MODEL), F32),
                        pltpu.SemaphoreType.DMA((2,))],
    )
    return pl.pallas_call(
        _combine_kernel,
        out_shape=jax.ShapeDtypeStruct((t, D_MODEL), F32),
        grid_spec=grid_spec,
        compiler_params=pltpu.CompilerParams(
            dimension_semantics=("arbitrary",), vmem_limit_bytes=VMEM_LIMIT),
        name="combine",
    )(dest1, dest2, x1, w, yb)


def kernel(x, norm1_g, w_in, b_gate, b_forget, sgu_ln_g, sgu_ln_b, w_spatial, b_spatial, q_norm_g, k_norm_g, w_proj_sgu, w_proj_fox, w_out, norm2_g, w_router_group, b_router_group, w_router_expert, b_router_expert, w_expert_gate, w_expert_up, w_expert_down):
    batch, seq, d = x.shape
    t = batch * seq
    l = 0
    x2 = x.reshape(t, d)

    wt = jnp.swapaxes(w_in[l], 0, 1)
    off_f = N_MAIN_SEC * SEC
    w_f = jnp.pad(wt[off_f:off_f + FOX_HEADS].T, ((0, 0), (0, LANES - FOX_HEADS))).astype(BF16)
    b_f = jnp.pad(b_forget[l], (0, LANES - FOX_HEADS)).reshape(1, LANES)
    n_r = N_GROUPS + N_EXPERTS
    w_r = jnp.pad(jnp.concatenate([w_router_group[l], w_router_expert[l]], axis=1),
                  ((0, 0), (0, LANES - n_r))).astype(BF16)
    b_r = jnp.pad(jnp.concatenate([b_router_group[l], b_router_expert[l]]),
                  (0, LANES - n_r)).reshape(1, LANES)

    h, lf = _norm1(x2, norm1_g[l].reshape(1, d), w_f, b_f)
    a_sgu, q, kt, va, gates = _inproj(
        h, wt, b_gate[l].reshape(1, 2 * d),
        sgu_ln_g[l].reshape(1, SGU_WIDTH), sgu_ln_b[l].reshape(1, SGU_WIDTH),
        w_spatial[l], b_spatial[l].T, q_norm_g[l].reshape(1, HEAD_DIM),
        k_norm_g[l].reshape(1, HEAD_DIM))
    negc = _forget_cumsum(lf, batch, seq).reshape(batch, FOX_HEADS, 1, seq)
    a_fox = _attention(q, kt, va, negc, batch, seq)
    x1, logits = _mix(a_sgu, a_fox, gates, x2, w_proj_sgu[l], w_proj_fox[l], w_out[l],
                      norm2_g[l].reshape(1, d), w_r, b_r)

    n_assign = 2 * t
    n_rows = n_assign + N_EXPERTS * EXPERT_BLOCK
    nb = n_rows // EXPERT_BLOCK
    dest_t, meta_f, bmeta, nblk, row_tok = _route(logits, n_rows // LANES)
    dest1, dest2 = dest_t[:, 0, :].reshape(t), dest_t[:, 1, :].reshape(t)
    n_used = bmeta[0:1, 1]
    yb = _experts(bmeta[:nb, 0], n_used, nblk[0, :N_EXPERTS], row_tok.reshape(n_rows), x1,
                  norm2_g[l].reshape(1, d), w_expert_gate[l], w_expert_up[l], w_expert_down[l])
    out = _combine(dest1, dest2, x1, meta_f, yb)
    return out.reshape(batch, seq, d)
```

```python
import functools

import jax
import jax.numpy as jnp
from jax import lax
from jax.experimental import pallas as pl
from jax.experimental.pallas import tpu as pltpu

F32 = jnp.float32
BF16 = jnp.bfloat16
I32 = jnp.int32

D_MODEL = 2048
CHUNK = 128
SGU_GROUPS = 8
SGU_WIDTH = 1024
FOX_HEADS = 8
HEAD_DIM = 128
FOX_WIDTH = 1024
N_GROUPS = 4
EXPERTS_PER_GROUP = 8
N_EXPERTS = 32
D_EXPERT = 512
EXPERT_BLOCK = 128
EPS = 1e-6

LANES = 128
SUBLANES = 8
PREFETCH_PRIORITY = 1
VMEM_LIMIT = 56 * 1024 * 1024

SEC = 1024
N_MAIN_SEC = 5
N_GATE_SEC = 4
TM_NORM = 1024
TM_IN = 512
TQ = 256
HEADS_PER_STEP = 4
TM_MIX = 256
RCH = 512
TM_CMB = 512
COMBINE_ROWS = 64
N_SPLIT = 3
EXPERT_GROUP_SHIFT = EXPERTS_PER_GROUP.bit_length() - 1
LOG2E = 1.4426950408889634
Q_SCALE = HEAD_DIM ** -0.5 * LOG2E


def _sigmoid(x):
    return 1.0 / (1.0 + jnp.exp(-x))


def _log_sigmoid(x):
    return jnp.minimum(x, 0.0) - jnp.log1p(jnp.exp(-jnp.abs(x)))


def _norm1_kernel(x_ref, g1_ref, wf_ref, bf_ref, h_ref, lf_ref):
    x = x_ref[...]
    ms = jnp.mean(x * x, axis=-1, keepdims=True)
    hb = ((x * lax.rsqrt(ms + EPS)) * g1_ref[...]).astype(BF16)
    h_ref[...] = hb
    f = jnp.dot(hb, wf_ref[...], preferred_element_type=F32) + bf_ref[...]
    lf_ref[...] = _log_sigmoid(f)


def _norm1(x2, g1, w_f, b_f):
    t = x2.shape[0]
    row = lambda i: (i, 0)
    const = lambda i: (0, 0)
    return pl.pallas_call(
        _norm1_kernel,
        out_shape=(jax.ShapeDtypeStruct((t, D_MODEL), BF16),
                   jax.ShapeDtypeStruct((t, LANES), F32)),
        grid=(t // TM_NORM,),
        in_specs=[pl.BlockSpec((TM_NORM, D_MODEL), row),
                  pl.BlockSpec((1, D_MODEL), const),
                  pl.BlockSpec((D_MODEL, LANES), const),
                  pl.BlockSpec((1, LANES), const)],
        out_specs=(pl.BlockSpec((TM_NORM, D_MODEL), row), pl.BlockSpec((TM_NORM, LANES), row)),
        compiler_params=pltpu.CompilerParams(
            dimension_semantics=("arbitrary",), vmem_limit_bytes=VMEM_LIMIT),
        name="norm1",
    )(x2, g1, w_f, b_f)


N_SECTIONS = 1 + 3 + N_GATE_SEC
N_CHUNKS = N_MAIN_SEC + N_GATE_SEC
GATE_SECTION0 = 4
WCONV_ROWS = 256


def _inproj_kernel(h_ref, wt_hbm, bg_ref, lng_ref, lnb_ref, wsp_ref, bsp_ref, qg_ref, kg_ref,
                   asgu_ref, q_ref, kt_ref, va_ref, gates_ref,
                   stage, wb, sem):
    s = pl.program_id(0)
    i = pl.program_id(1)
    tm = h_ref.shape[0]

    def chunk_copy(c, slot):
        r0 = pl.multiple_of(c * SEC + jnp.where(c >= N_MAIN_SEC, FOX_HEADS, 0), 8)
        return pltpu.make_async_copy(wt_hbm.at[pl.ds(r0, SEC), :], stage.at[slot], sem.at[slot])

    def convert(slot, k):
        for p in range(SEC // WCONV_ROWS):
            rs = slice(p * WCONV_ROWS, (p + 1) * WCONV_ROWS)
            wb[k, :, rs] = stage[slot, rs, :].T.astype(BF16)

    @pl.when((s == 0) & (i == 0))
    def _():
        chunk_copy(0, 0).start()
        chunk_copy(1, 1).start()
        chunk_copy(0, 0).wait()
        convert(0, 0)
        chunk_copy(2, 0).start(priority=PREFETCH_PRIORITY)
        chunk_copy(1, 1).wait()
        convert(1, 1)

    @pl.when((s > 0) & (i == 0))
    def _():
        c = s + 1
        slot = c % 2
        chunk_copy(c, slot).wait()

        @pl.when(c + 1 < N_CHUNKS)
        def _():
            chunk_copy(c + 1, 1 - slot).start(priority=PREFETCH_PRIORITY)

        convert(slot, 0)

    def section(k=0):
        return jnp.dot(h_ref[...], wb[k], preferred_element_type=F32)

    @pl.when(s == 0)
    def _():
        u = jax.nn.gelu(section(0))
        v = jax.nn.gelu(section(1))
        mu = jnp.mean(v, axis=-1, keepdims=True)
        vc = v - mu
        var = jnp.mean(vc * vc, axis=-1, keepdims=True)
        vn = ((vc * lax.rsqrt(var + EPS)) * lng_ref[...] + lnb_ref[...]).astype(BF16)
        row = lax.broadcasted_iota(I32, (CHUNK, CHUNK), 0)
        col = lax.broadcasted_iota(I32, (CHUNK, CHUNK), 1)
        causal = row >= col
        for g in range(SGU_GROUPS):
            wg = jnp.where(causal, wsp_ref[g], 0.0).astype(BF16)
            bcol = bsp_ref[:, g:g + 1]
            gs = slice(g * LANES, (g + 1) * LANES)
            n_ch = tm // CHUNK
            rhs = jnp.concatenate([vn[c * CHUNK:(c + 1) * CHUNK, gs] for c in range(n_ch)], axis=1)
            sg = jnp.dot(wg, rhs, preferred_element_type=F32) + bcol
            for c in range(n_ch):
                rs = slice(c * CHUNK, (c + 1) * CHUNK)
                asgu_ref[rs, gs] = (u[rs, gs] * sg[:, c * LANES:(c + 1) * LANES]).astype(BF16)

    def _head_norm(z, h, gain_ref):
        zh = z[:, h * HEAD_DIM:(h + 1) * HEAD_DIM]
        ms = jnp.mean(zh * zh, axis=-1, keepdims=True)
        return (zh * lax.rsqrt(ms + EPS)) * gain_ref[...]

    @pl.when(s == 1)
    def _():
        z = section()
        for h in range(FOX_HEADS):
            q_ref[:, h * HEAD_DIM:(h + 1) * HEAD_DIM] = (
                _head_norm(z, h, qg_ref) * Q_SCALE).astype(BF16)

    @pl.when(s == 2)
    def _():
        z = section()
        for h in range(FOX_HEADS):
            kt_ref[h * HEAD_DIM:(h + 1) * HEAD_DIM, :] = _head_norm(z, h, kg_ref).T.astype(BF16)

    @pl.when(s == 3)
    def _():
        va_ref[...] = section().astype(BF16)

    @pl.when(s >= GATE_SECTION0)
    def _():
        gates_ref[...] = (0.5 * jnp.tanh(0.5 * (section() + bg_ref[...])) + 0.5).astype(BF16)


def _inproj(h, wt, b_gate, ln_g, ln_b, w_sp, b_sp_t, q_g, k_g):
    t = h.shape[0]
    n_i = t // TM_IN

    def active(sec):
        return lambda s, i: jnp.where(s < sec, 0, jnp.where(s > sec, n_i - 1, i))

    def gate_blk(s, i):
        g = jnp.clip(s - GATE_SECTION0, 0, N_GATE_SEC - 1)
        return (jnp.where(s < GATE_SECTION0, 0, i), g)

    const2 = lambda s, i: (0, 0)
    out_shape = (
        jax.ShapeDtypeStruct((t, SGU_WIDTH), BF16),
        jax.ShapeDtypeStruct((t, FOX_WIDTH), BF16),
        jax.ShapeDtypeStruct((FOX_WIDTH, t), BF16),
        jax.ShapeDtypeStruct((t, FOX_WIDTH), BF16),
        jax.ShapeDtypeStruct((t, 2 * D_MODEL), BF16),
    )
    return pl.pallas_call(
        _inproj_kernel,
        out_shape=out_shape,
        grid=(N_SECTIONS, n_i),
        in_specs=[
            pl.BlockSpec((TM_IN, D_MODEL), lambda s, i: (i, 0)),
            pl.BlockSpec(memory_space=pl.ANY),
            pl.BlockSpec((1, SEC), lambda s, i: (0, jnp.clip(s - GATE_SECTION0, 0, N_GATE_SEC - 1))),
            pl.BlockSpec((1, SGU_WIDTH), const2),
            pl.BlockSpec((1, SGU_WIDTH), const2),
            pl.BlockSpec((SGU_GROUPS, CHUNK, CHUNK), lambda s, i: (0, 0, 0)),
            pl.BlockSpec((CHUNK, SGU_GROUPS), const2),
            pl.BlockSpec((1, HEAD_DIM), const2),
            pl.BlockSpec((1, HEAD_DIM), const2),
        ],
        out_specs=(
            pl.BlockSpec((TM_IN, SGU_WIDTH), lambda s, i: (active(0)(s, i), 0)),
            pl.BlockSpec((TM_IN, FOX_WIDTH), lambda s, i: (active(1)(s, i), 0)),
            pl.BlockSpec((FOX_WIDTH, TM_IN), lambda s, i: (0, active(2)(s, i))),
            pl.BlockSpec((TM_IN, FOX_WIDTH), lambda s, i: (active(3)(s, i), 0)),
            pl.BlockSpec((TM_IN, SEC), gate_blk),
        ),
        scratch_shapes=[pltpu.VMEM((2, SEC, D_MODEL), F32),
                        pltpu.VMEM((2, D_MODEL, SEC), BF16),
                        pltpu.SemaphoreType.DMA((2,))],
        compiler_params=pltpu.CompilerParams(
            dimension_semantics=("arbitrary", "arbitrary"), vmem_limit_bytes=VMEM_LIMIT),
        name="inproj",
    )(h, wt, b_gate, ln_g, ln_b, w_sp, b_sp_t, q_g, k_g)


def _split3(x):
    x0 = x.astype(BF16)
    r1 = x - x0.astype(F32)
    x1 = r1.astype(BF16)
    r2 = r1 - x1.astype(F32)
    return x0, x1, r2.astype(BF16)


def _cumsum_kernel(lf_ref, negc_ref, c_s):
    seq = lf_ref.shape[0]
    row = lax.broadcasted_iota(I32, (CHUNK, CHUNK), 0)
    col = lax.broadcasted_iota(I32, (CHUNK, CHUNK), 1)
    tri = jnp.where(row >= col, 1.0, 0.0).astype(BF16)
    carry = jnp.zeros((1, LANES), F32)
    for r in range(seq // CHUNK):
        rs = slice(r * CHUNK, (r + 1) * CHUNK)
        x0, x1, x2 = _split3(lf_ref[rs, :])
        cs = (jnp.dot(tri, x0, preferred_element_type=F32)
              + jnp.dot(tri, x1, preferred_element_type=F32)
              + jnp.dot(tri, x2, preferred_element_type=F32)) + carry
        carry = cs[CHUNK - 1:CHUNK, :]
        c_s[rs, :] = cs
    ct = c_s[...].T
    negc_ref[0] = -ct[0:FOX_HEADS, :]


def _forget_cumsum(lf, batch, seq):
    return pl.pallas_call(
        _cumsum_kernel,
        out_shape=jax.ShapeDtypeStruct((batch, FOX_HEADS, seq), F32),
        grid=(batch,),
        in_specs=[pl.BlockSpec((seq, LANES), lambda b: (b, 0))],
        out_specs=pl.BlockSpec((1, FOX_HEADS, seq), lambda b: (b, 0, 0)),
        scratch_shapes=[pltpu.VMEM((seq, LANES), F32)],
        compiler_params=pltpu.CompilerParams(
            dimension_semantics=("arbitrary",), vmem_limit_bytes=VMEM_LIMIT),
        name="forget_cumsum",
    )(lf)


def _attn_kernel(q_ref, kt_ref, v_ref, negc_ref, o_ref, kx, vx):
    seq = q_ref.shape[0]
    sub = lax.broadcasted_iota(I32, (HEAD_DIM, seq), 0)
    for hh in range(HEADS_PER_STEP):
        hs = slice(hh * HEAD_DIM, (hh + 1) * HEAD_DIM)
        c0, c1, c2 = (c.astype(F32) for c in _split3(negc_ref[0, hh] * LOG2E))
        kx[hh, 0:HEAD_DIM, :] = kt_ref[hs, :]
        kx[hh, HEAD_DIM:, :] = jnp.where(
            sub == 0, c0, jnp.where(sub == 1, c1, jnp.where(sub == 2, c2, 0.0))).astype(BF16)
        vx[hh, :, 0:HEAD_DIM] = v_ref[:, hs]
        vx[hh, :, HEAD_DIM:] = jnp.ones((seq, HEAD_DIM), BF16)
    lane = lax.broadcasted_iota(I32, (TQ, HEAD_DIM), 1)
    bias_cols = jnp.where(lane < N_SPLIT, 1.0, 0.0).astype(BF16)
    row = lax.broadcasted_iota(I32, (TQ, TQ), 0)
    col = lax.broadcasted_iota(I32, (TQ, TQ), 1)
    causal = row >= col
    for qi in range(seq // TQ):
        for hh in range(HEADS_PER_STEP):
            hs = slice(hh * HEAD_DIM, (hh + 1) * HEAD_DIM)
            k0 = qi * TQ
            q = jnp.concatenate([q_ref[k0:k0 + TQ, hs], bias_cols], axis=1)
            s_d = jnp.dot(q, kx[hh, :, k0:k0 + TQ], preferred_element_type=F32)
            s_d = jnp.where(causal, s_d, -jnp.inf)
            m = jnp.max(s_d, axis=-1, keepdims=True)
            if qi > 0:
                s_o = jnp.dot(q, kx[hh, :, 0:k0], preferred_element_type=F32)
                m = jnp.maximum(m, jnp.max(s_o, axis=-1, keepdims=True))
            acc = jnp.dot(jnp.exp2(s_d - m).astype(BF16), vx[hh, k0:k0 + TQ, :],
                          preferred_element_type=F32)
            if qi > 0:
                acc = acc + jnp.dot(jnp.exp2(s_o - m).astype(BF16), vx[hh, 0:k0, :],
                                    preferred_element_type=F32)
            inv_l = 1.0 / acc[:, HEAD_DIM:HEAD_DIM + 1]
            o_ref[k0:k0 + TQ, hs] = (acc[:, 0:HEAD_DIM] * inv_l).astype(BF16)


def _attention(q, kt, va, negc, batch, seq):
    t = q.shape[0]
    width = HEADS_PER_STEP * HEAD_DIM
    blk = pl.BlockSpec((seq, width), lambda b, g: (b, g))
    return pl.pallas_call(
        _attn_kernel,
        out_shape=jax.ShapeDtypeStruct((t, FOX_WIDTH), BF16),
        grid=(batch, FOX_HEADS // HEADS_PER_STEP),
        in_specs=[blk,
                  pl.BlockSpec((width, seq), lambda b, g: (g, b)),
                  blk,
                  pl.BlockSpec((1, HEADS_PER_STEP, 1, seq), lambda b, g: (b, g, 0, 0))],
        out_specs=blk,
        scratch_shapes=[pltpu.VMEM((HEADS_PER_STEP, 2 * HEAD_DIM, seq), BF16),
                        pltpu.VMEM((HEADS_PER_STEP, seq, 2 * HEAD_DIM), BF16)],
        compiler_params=pltpu.CompilerParams(
            dimension_semantics=("arbitrary", "arbitrary"), vmem_limit_bytes=VMEM_LIMIT),
        name="fox_attention",
    )(q, kt, va, negc)


MIX_WROWS = 512


def _mix_kernel(as_ref, af_ref, g_ref, x_ref, wps_hbm, wpf_hbm, wo_hbm, g2_ref, wr_ref, br_ref,
                x1_ref, lg_ref, wps_ref, wpf_ref, wo_ref, stage, sem):
    @pl.when(pl.program_id(0) == 0)
    def _():
        pieces = [(src, dst, r0) for src, dst in ((wps_hbm, wps_ref), (wpf_hbm, wpf_ref),
                                                  (wo_hbm, wo_ref))
                  for r0 in range(0, dst.shape[0], MIX_WROWS)]

        def piece_copy(k):
            src, _, r0 = pieces[k]
            return pltpu.make_async_copy(src.at[pl.ds(r0, MIX_WROWS), :], stage.at[k % 2],
                                         sem.at[k % 2])

        piece_copy(0).start()
        for k, (_, dst, r0) in enumerate(pieces):
            if k + 1 < len(pieces):
                piece_copy(k + 1).start()
            piece_copy(k).wait()
            dst[r0:r0 + MIX_WROWS, :] = stage[k % 2].astype(BF16)

    ys = jnp.dot(as_ref[...], wps_ref[...], preferred_element_type=F32)
    yf = jnp.dot(af_ref[...], wpf_ref[...], preferred_element_type=F32)
    m = (g_ref[:, :D_MODEL].astype(F32) * ys + g_ref[:, D_MODEL:].astype(F32) * yf).astype(BF16)
    x1 = x_ref[...] + jnp.dot(m, wo_ref[...], preferred_element_type=F32)
    x1_ref[...] = x1
    ms = jnp.mean(x1 * x1, axis=-1, keepdims=True)
    h2 = ((x1 * lax.rsqrt(ms + EPS)) * g2_ref[...]).astype(BF16)
    lg_ref[...] = jnp.dot(h2, wr_ref[...], preferred_element_type=F32) + br_ref[...]


def _mix(a_sgu, a_fox, gates, x2, wps, wpf, wo, g2, w_r, b_r):
    t = x2.shape[0]
    row = lambda i: (i, 0)
    const = lambda i: (0, 0)
    resident = functools.partial(pl.BlockSpec, index_map=const, pipeline_mode=pl.Buffered(1))
    return pl.pallas_call(
        _mix_kernel,
        out_shape=(jax.ShapeDtypeStruct((t, D_MODEL), F32),
                   jax.ShapeDtypeStruct((t, LANES), F32)),
        grid=(t // TM_MIX,),
        in_specs=[
            pl.BlockSpec((TM_MIX, SGU_WIDTH), row),
            pl.BlockSpec((TM_MIX, FOX_WIDTH), row),
            pl.BlockSpec((TM_MIX, 2 * D_MODEL), row),
            pl.BlockSpec((TM_MIX, D_MODEL), row),
            pl.BlockSpec(memory_space=pl.ANY),
            pl.BlockSpec(memory_space=pl.ANY),
            pl.BlockSpec(memory_space=pl.ANY),
            pl.BlockSpec((1, D_MODEL), const),
            resident((D_MODEL, LANES)),
            pl.BlockSpec((1, LANES), const),
        ],
        out_specs=(pl.BlockSpec((TM_MIX, D_MODEL), row), pl.BlockSpec((TM_MIX, LANES), row)),
        scratch_shapes=[pltpu.VMEM((SGU_WIDTH, D_MODEL), BF16),
                        pltpu.VMEM((FOX_WIDTH, D_MODEL), BF16),
                        pltpu.VMEM((D_MODEL, D_MODEL), BF16),
                        pltpu.VMEM((2, MIX_WROWS, D_MODEL), F32),
                        pltpu.SemaphoreType.DMA((2,))],
        compiler_params=pltpu.CompilerParams(
            dimension_semantics=("arbitrary",), vmem_limit_bytes=VMEM_LIMIT),
        name="mix",
    )(a_sgu, a_fox, gates, x2, wps, wpf, wo, g2, w_r, b_r)


def _route_kernel(lg_ref, dt_ref, mf_ref, bm_ref, em_ref, bt_ref, mi_ref, inv_ref):
    t = lg_ref.shape[0]
    n_chunks = t // RCH
    lane_i = lax.broadcasted_iota(I32, (RCH, LANES), 1)
    lane = lane_i.astype(F32)
    lane_grp = ((lane_i - N_GROUPS) >> EXPERT_GROUP_SHIFT).astype(F32)
    is_grp = lane_i < N_GROUPS
    is_exp = (lane_i >= N_GROUPS) & (lane_i < N_GROUPS + N_EXPERTS)
    r_i = lax.broadcasted_iota(I32, (RCH, RCH), 0)
    c_i = lax.broadcasted_iota(I32, (RCH, RCH), 1)
    strict_lower = jnp.where(r_i > c_i, 1.0, 0.0).astype(BF16)
    neg_inf = -jnp.inf

    def first_max(vals):
        vmax = jnp.max(vals, axis=-1, keepdims=True)
        idx = jnp.min(jnp.where(vals == vmax, lane, float(LANES)), axis=-1, keepdims=True)
        return vmax, idx

    def pick(table, idx):
        return jnp.sum(jnp.where(lane == idx, table, 0.0), axis=-1, keepdims=True)

    def pack(cols):
        out = jnp.zeros((RCH, LANES), F32)
        for n, c in enumerate(cols):
            out = jnp.where(lane_i == n, c, out)
        return out

    def pass1(ci, counts):
        r0 = pl.multiple_of(ci * RCH, RCH)
        lg = lg_ref[pl.ds(r0, RCH), :]
        gmax, grp = first_max(jnp.where(is_grp, lg, neg_inf))
        p_grp = 1.0 / jnp.sum(jnp.where(is_grp, jnp.exp(lg - gmax), 0.0), axis=-1, keepdims=True)
        el = jnp.where(is_exp & (lane_grp == grp), lg, neg_inf)
        v1, i1 = first_max(el)
        v2, i2 = first_max(jnp.where(lane == i1, neg_inf, el))
        e21 = jnp.exp(v2 - v1)
        w1 = p_grp / (1.0 + e21)
        w2 = p_grp * e21 / (1.0 + e21)
        e1 = i1 - float(N_GROUPS)
        e2 = i2 - float(N_GROUPS)
        hot = jnp.where((lane == e1) | (lane == e2), 1.0, 0.0)
        before = jnp.dot(strict_lower, hot.astype(BF16), preferred_element_type=F32) + counts
        mi_ref[pl.ds(r0, RCH), :] = pack([e1, e2, pick(before, e1), pick(before, e2)]).astype(I32)
        mf_ref[pl.ds(r0, RCH), :] = pack([w1, w2])
        return counts + jnp.sum(hot, axis=0, keepdims=True)

    counts = lax.fori_loop(0, n_chunks, pass1, jnp.zeros((1, LANES), F32))

    nblk = jnp.floor((counts + float(EXPERT_BLOCK - 1)) * (1.0 / EXPERT_BLOCK))
    u_r = lax.broadcasted_iota(I32, (LANES, LANES), 0)
    u_c = lax.broadcasted_iota(I32, (LANES, LANES), 1)
    strict_upper = jnp.where(u_r < u_c, 1.0, 0.0).astype(BF16)
    bstart = jnp.dot(jnp.broadcast_to(nblk, (SUBLANES, LANES)).astype(BF16), strict_upper,
                     preferred_element_type=F32)[0:1, :]
    bend = bstart + nblk

    def pass2(ci, _):
        r0 = pl.multiple_of(ci * RCH, RCH)
        mi = mi_ref[pl.ds(r0, RCH), :].astype(F32)
        cols = [jnp.sum(jnp.where(lane_i == n, mi, 0.0), axis=-1, keepdims=True) for n in range(4)]
        d1 = pick(bstart, cols[0]) * float(EXPERT_BLOCK) + cols[2]
        d2 = pick(bstart, cols[1]) * float(EXPERT_BLOCK) + cols[3]
        d_t = pack([d1, d2]).T
        dt_ref[ci] = d_t[0:SUBLANES, :].astype(I32)
        tok = (lax.broadcasted_iota(I32, (RCH, 1), 0) + r0).astype(F32)
        tok_hi = jnp.floor(tok * (1.0 / LANES))
        tok_lo = tok - tok_hi * float(LANES)
        for n, d in enumerate((d1, d2)):
            d_hi = jnp.floor(d * (1.0 / LANES))
            hit = lane == (d - d_hi * float(LANES))
            blk_row = jnp.floor(d_t[n:n + 1, :] * (1.0 / LANES))
            sel = jnp.where(inv_blk == blk_row, 1.0, 0.0).astype(BF16)
            inv_ref[0] += jnp.dot(sel, jnp.where(hit, tok_hi, 0.0).astype(BF16),
                                  preferred_element_type=F32)
            inv_ref[1] += jnp.dot(sel, jnp.where(hit, tok_lo, 0.0).astype(BF16),
                                  preferred_element_type=F32)
        return 0

    nb_rows = bm_ref.shape[0]
    inv_blk = lax.broadcasted_iota(I32, (nb_rows, RCH), 0).astype(F32)
    inv_ref[...] = jnp.zeros_like(inv_ref)
    lax.fori_loop(0, n_chunks, pass2, 0)
    bt_ref[...] = (inv_ref[0] * float(LANES) + inv_ref[1]).astype(I32)

    b_col = lax.broadcasted_iota(I32, (nb_rows, LANES), 0).astype(F32)
    b_lane = lax.broadcasted_iota(I32, (nb_rows, LANES), 1)
    done = jnp.where((bend <= b_col) & (b_lane < N_EXPERTS), 1.0, 0.0)
    blk_e = jnp.minimum(jnp.sum(done, axis=-1, keepdims=True), float(N_EXPERTS - 1))
    n_used = jnp.sum(jnp.where(b_lane[0:1, :] == N_EXPERTS - 1, bend, 0.0), axis=-1, keepdims=True)
    bm_ref[...] = jnp.where(b_lane == 0, blk_e, jnp.where(b_lane == 1, n_used, 0.0)).astype(I32)

    em_ref[...] = jnp.broadcast_to(nblk, (SUBLANES, LANES)).astype(I32)


def _route(logits, nb_rows):
    t = logits.shape[0]
    full = lambda shape: pl.BlockSpec(shape, lambda: (0,) * len(shape))
    return pl.pallas_call(
        _route_kernel,
        out_shape=(jax.ShapeDtypeStruct((t // RCH, SUBLANES, RCH), I32),
                   jax.ShapeDtypeStruct((t, LANES), F32),
                   jax.ShapeDtypeStruct((nb_rows, LANES), I32),
                   jax.ShapeDtypeStruct((SUBLANES, LANES), I32),
                   jax.ShapeDtypeStruct((nb_rows, LANES), I32)),
        in_specs=[full((t, LANES))],
        out_specs=(full((t // RCH, SUBLANES, RCH)), full((t, LANES)), full((nb_rows, LANES)),
                   full((SUBLANES, LANES)), full((nb_rows, LANES))),
        scratch_shapes=[pltpu.VMEM((t, LANES), I32), pltpu.VMEM((2, nb_rows, LANES), F32)],
        compiler_params=pltpu.CompilerParams(vmem_limit_bytes=VMEM_LIMIT),
        name="route",
    )(logits)


GATHER_GROUPS = 4
GATHER_AHEAD = 6
WEIGHT_SLOTS = 3


def _experts_kernel(blk_e_ref, nused_ref, nblk_ref, tok_ref, x_hbm, g2_ref, wg_hbm, wu_hbm, wd_hbm,
                    y_ref, xg, wg_f, wu_f, wd_f, wg_s, wu_s, wd_s, slot_ref, sem, gsem):
    b = pl.program_id(0)
    n_used = nused_ref[0]
    used = b < n_used
    e = blk_e_ref[b]
    new_expert = (b == 0) | (e != blk_e_ref[jnp.maximum(b - 1, 0)])
    n_slots = GATHER_AHEAD + 1
    xslot = b % n_slots

    def gather_rows(block, slot, lo, hi):
        row0 = jnp.minimum(block, n_used - 1) * EXPERT_BLOCK
        for r in range(lo, hi):
            pltpu.make_async_copy(x_hbm.at[pl.ds(tok_ref[row0 + r], 1), :],
                                  xg.at[slot, pl.ds(r, 1), :], gsem.at[slot]).start(priority=1)

    def gather_wait(slot):
        pltpu.make_async_copy(x_hbm.at[pl.ds(0, EXPERT_BLOCK), :], xg.at[slot], gsem.at[slot]).wait()

    @pl.when(b == 0)
    def _():
        for k in range(GATHER_AHEAD):
            gather_rows(k, k, 0, EXPERT_BLOCK)

    def weight_copies(expert, slot):
        return (pltpu.make_async_copy(wg_hbm.at[expert], wg_f.at[slot], sem.at[slot]),
                pltpu.make_async_copy(wu_hbm.at[expert], wu_f.at[slot], sem.at[slot]),
                pltpu.make_async_copy(wd_hbm.at[expert], wd_f.at[slot], sem.at[slot]))

    def block_expert(blk):
        return blk_e_ref[jnp.minimum(blk, n_used - 1)]

    b_next = b + nblk_ref[e]
    b_next2 = b_next + nblk_ref[block_expert(b_next)]

    @pl.when(b == 0)
    def _():
        slot_ref[0] = 0
        for cp in weight_copies(e, 0):
            cp.start()

        @pl.when(b_next < n_used)
        def _():
            for cp in weight_copies(block_expert(b_next), 1):
                cp.start()

    @pl.when(used & new_expert)
    def _():
        slot = slot_ref[0]

        @pl.when(b_next2 < n_used)
        def _():
            for cp in weight_copies(block_expert(b_next2), (slot + 2) % WEIGHT_SLOTS):
                cp.start()

        for cp in weight_copies(e, slot):
            cp.wait()
        wg_s[...] = wg_f[slot].astype(BF16)
        wu_s[...] = wu_f[slot].astype(BF16)
        wd_s[...] = wd_f[slot].astype(BF16)
        slot_ref[0] = (slot + 1) % WEIGHT_SLOTS

    @pl.when(used)
    def _():
        gather_wait(xslot)
        ahead = b + GATHER_AHEAD
        aslot = ahead % n_slots
        per_group = EXPERT_BLOCK // GATHER_GROUPS
        x = xg[xslot]
        ms = jnp.mean(x * x, axis=-1, keepdims=True)
        h = ((x * lax.rsqrt(ms + EPS)) * g2_ref[...]).astype(BF16)
        gather_rows(ahead, aslot, 0, per_group)
        a = jnp.dot(h, wg_s[...], preferred_element_type=F32)
        gather_rows(ahead, aslot, per_group, 2 * per_group)
        u = jnp.dot(h, wu_s[...], preferred_element_type=F32)
        gather_rows(ahead, aslot, 2 * per_group, 3 * per_group)
        mid = ((a * _sigmoid(a)) * u).astype(BF16)
        gather_rows(ahead, aslot, 3 * per_group, EXPERT_BLOCK)
        y_ref[...] = jnp.dot(mid, wd_s[...], preferred_element_type=F32)

    @pl.when(b == n_used - 1)
    def _():
        for k in range(1, GATHER_AHEAD + 1):
            gather_wait((b + k) % n_slots)

    @pl.when(jnp.logical_not(used))
    def _():
        y_ref[...] = jnp.zeros_like(y_ref)


def _experts(blk_e, n_used, nblk, row_tok, x1, g2, w_g, w_u, w_d):
    n_rows = row_tok.shape[0]
    nb = n_rows // EXPERT_BLOCK
    hbm = pl.BlockSpec(memory_space=pl.ANY)
    grid_spec = pltpu.PrefetchScalarGridSpec(
        num_scalar_prefetch=4,
        grid=(nb,),
        in_specs=[
            hbm,
            pl.BlockSpec((1, D_MODEL), lambda b, *_: (0, 0)),
            hbm, hbm, hbm,
        ],
        out_specs=pl.BlockSpec((EXPERT_BLOCK, D_MODEL), lambda b, *_: (b, 0)),
        scratch_shapes=[pltpu.VMEM((GATHER_AHEAD + 1, EXPERT_BLOCK, D_MODEL), F32),
                        pltpu.VMEM((WEIGHT_SLOTS, D_MODEL, D_EXPERT), F32),
                        pltpu.VMEM((WEIGHT_SLOTS, D_MODEL, D_EXPERT), F32),
                        pltpu.VMEM((WEIGHT_SLOTS, D_EXPERT, D_MODEL), F32),
                        pltpu.VMEM((D_MODEL, D_EXPERT), BF16),
                        pltpu.VMEM((D_MODEL, D_EXPERT), BF16),
                        pltpu.VMEM((D_EXPERT, D_MODEL), BF16),
                        pltpu.SMEM((1,), I32),
                        pltpu.SemaphoreType.DMA((WEIGHT_SLOTS,)),
                        pltpu.SemaphoreType.DMA((GATHER_AHEAD + 1,))],
    )
    return pl.pallas_call(
        _experts_kernel,
        out_shape=jax.ShapeDtypeStruct((n_rows, D_MODEL), F32),
        grid_spec=grid_spec,
        compiler_params=pltpu.CompilerParams(
            dimension_semantics=("arbitrary",), vmem_limit_bytes=VMEM_LIMIT),
        name="experts",
    )(blk_e, n_used, nblk, row_tok, x1, g2, w_g, w_u, w_d)


def _combine_kernel(d1_ref, d2_ref, x1_ref, w_ref, y_ref, o_ref, gbuf, sem):
    i = pl.program_id(0)
    n = pl.num_programs(0)
    tm = x1_ref.shape[0]

    def request_row(base, slot, r):
        pltpu.make_async_copy(y_ref.at[pl.ds(d1_ref[base + r], 1), :],
                              gbuf.at[slot, pl.ds(r, 1), :], sem.at[slot]).start()
        pltpu.make_async_copy(y_ref.at[pl.ds(d2_ref[base + r], 1), :],
                              gbuf.at[slot, pl.ds(tm + r, 1), :], sem.at[slot]).start()

    def issue(tile, slot, lo, hi):
        base = jnp.minimum(tile, n - 1) * tm
        for r in range(lo, hi):
            request_row(base, slot, r)

    def tile_wait(slot):
        pltpu.make_async_copy(y_ref.at[pl.ds(0, 2 * tm), :], gbuf.at[slot], sem.at[slot]).wait()

    slot = i % 2

    @pl.when(i == 0)
    def _():
        def first_tile(r, _):
            request_row(0, 0, r)
            return 0

        lax.fori_loop(0, tm, first_tile, 0, unroll=SUBLANES)

    tile_wait(slot)
    for c in range(tm // COMBINE_ROWS):
        rs = slice(c * COMBINE_ROWS, (c + 1) * COMBINE_ROWS)
        rs2 = slice(tm + c * COMBINE_ROWS, tm + (c + 1) * COMBINE_ROWS)
        w = w_ref[rs, :]
        o_ref[rs, :] = x1_ref[rs, :] + (w[:, 0:1] * gbuf[slot, rs, :] + w[:, 1:2] * gbuf[slot, rs2, :])
        issue(i + 1, 1 - slot, c * COMBINE_ROWS, (c + 1) * COMBINE_ROWS)

    @pl.when(i == n - 1)
    def _():
        tile_wait(1 - slot)


def _combine(dest1, dest2, x1, w, yb):
    t = x1.shape[0]
    grid_spec = pltpu.PrefetchScalarGridSpec(
        num_scalar_prefetch=2,
        grid=(t // TM_CMB,),
        in_specs=[
            pl.BlockSpec((TM_CMB, D_MODEL), lambda i, a, b: (i, 0)),
            pl.BlockSpec((TM_CMB, LANES), lambda i, a, b: (i, 0)),
            pl.BlockSpec(memory_space=pl.ANY),
        ],
        out_specs=pl.BlockSpec((TM_CMB, D_MODEL), lambda i, a, b: (i, 0)),
        scratch_shapes=[pltpu.VMEM((2, 2 * TM_CMB, D_MODEL), F32),
                        pltpu.SemaphoreType.DMA((2,))],
    )
    return pl.pallas_call(
        _combine_kernel,
        out_shape=jax.ShapeDtypeStruct((t, D_MODEL), F32),
        grid_spec=grid_spec,
        compiler_params=pltpu.CompilerParams(
            dimension_semantics=("arbitrary",), vmem_limit_bytes=VMEM_LIMIT),
        name="combine",
    )(dest1, dest2, x1, w, yb)


def kernel(x, norm1_g, w_in, b_gate, b_forget, sgu_ln_g, sgu_ln_b, w_spatial, b_spatial, q_norm_g, k_norm_g, w_proj_sgu, w_proj_fox, w_out, norm2_g, w_router_group, b_router_group, w_router_expert, b_router_expert, w_expert_gate, w_expert_up, w_expert_down):
    batch, seq, d = x.shape
    t = batch * seq
    l = 0
    x2 = x.reshape(t, d)

    wt = jnp.swapaxes(w_in[l], 0, 1)
    off_f = N_MAIN_SEC * SEC
    w_f = jnp.pad(wt[off_f:off_f + FOX_HEADS].T, ((0, 0), (0, LANES - FOX_HEADS))).astype(BF16)
    b_f = jnp.pad(b_forget[l], (0, LANES - FOX_HEADS)).reshape(1, LANES)
    n_r = N_GROUPS + N_EXPERTS
    w_r = jnp.pad(jnp.concatenate([w_router_group[l], w_router_expert[l]], axis=1),
                  ((0, 0), (0, LANES - n_r))).astype(BF16)
    b_r = jnp.pad(jnp.concatenate([b_router_group[l], b_router_expert[l]]),
                  (0, LANES - n_r)).reshape(1, LANES)

    h, lf = _norm1(x2, norm1_g[l].reshape(1, d), w_f, b_f)
    a_sgu, q, kt, va, gates = _inproj(
        h, wt, b_gate[l].reshape(1, 2 * d),
        sgu_ln_g[l].reshape(1, SGU_WIDTH), sgu_ln_b[l].reshape(1, SGU_WIDTH),
        w_spatial[l], b_spatial[l].T, q_norm_g[l].reshape(1, HEAD_DIM),
        k_norm_g[l].reshape(1, HEAD_DIM))
    negc = _forget_cumsum(lf, batch, seq).reshape(batch, FOX_HEADS, 1, seq)
    a_fox = _attention(q, kt, va, negc, batch, seq)
    x1, logits = _mix(a_sgu, a_fox, gates, x2, w_proj_sgu[l], w_proj_fox[l], w_out[l],
                      norm2_g[l].reshape(1, d), w_r, b_r)

    n_assign = 2 * t
    n_rows = n_assign + N_EXPERTS * EXPERT_BLOCK
    nb = n_rows // EXPERT_BLOCK
    dest_t, meta_f, bmeta, nblk, row_tok = _route(logits, n_rows // LANES)
    dest1, dest2 = dest_t[:, 0, :].reshape(t), dest_t[:, 1, :].reshape(t)
    n_used = bmeta[0:1, 1]
    yb = _experts(bmeta[:nb, 0], n_used, nblk[0, :N_EXPERTS], row_tok.reshape(n_rows), x1,
                  norm2_g[l].reshape(1, d), w_expert_gate[l], w_expert_up[l], w_expert_down[l])
    out = _combine(dest1, dest2, x1, meta_f, yb)
    return out.reshape(batch, seq, d)
```

```python
import functools

import jax
import jax.numpy as jnp
from jax import lax
from jax.experimental import pallas as pl
from jax.experimental.pallas import tpu as pltpu

F32 = jnp.float32
BF16 = jnp.bfloat16
I32 = jnp.int32

D_MODEL = 2048
CHUNK = 128
SGU_GROUPS = 8
SGU_WIDTH = 1024
FOX_HEADS = 8
HEAD_DIM = 128
FOX_WIDTH = 1024
N_GROUPS = 4
EXPERTS_PER_GROUP = 8
N_EXPERTS = 32
D_EXPERT = 512
EXPERT_BLOCK = 128
EPS = 1e-6

LANES = 128
SUBLANES = 8
PREFETCH_PRIORITY = 1
VMEM_LIMIT = 56 * 1024 * 1024

SEC = 1024
N_MAIN_SEC = 5
N_GATE_SEC = 4
TM_NORM = 1024
TM_IN = 512
TQ = 256
HEADS_PER_STEP = 4
TM_MIX = 256
RCH = 512
TM_CMB = 512
COMBINE_ROWS = 64
N_SPLIT = 3
EXPERT_GROUP_SHIFT = EXPERTS_PER_GROUP.bit_length() - 1
LOG2E = 1.4426950408889634
Q_SCALE = HEAD_DIM ** -0.5 * LOG2E


def _sigmoid(x):
    return 1.0 / (1.0 + jnp.exp(-x))


def _log_sigmoid(x):
    return jnp.minimum(x, 0.0) - jnp.log1p(jnp.exp(-jnp.abs(x)))


def _norm1_kernel(x_ref, g1_ref, wf_ref, bf_ref, h_ref, lf_ref):
    x = x_ref[...]
    ms = jnp.mean(x * x, axis=-1, keepdims=True)
    hb = ((x * lax.rsqrt(ms + EPS)) * g1_ref[...]).astype(BF16)
    h_ref[...] = hb
    f = jnp.dot(hb, wf_ref[...], preferred_element_type=F32) + bf_ref[...]
    lf_ref[...] = _log_sigmoid(f)


def _norm1(x2, g1, w_f, b_f):
    t = x2.shape[0]
    row = lambda i: (i, 0)
    const = lambda i: (0, 0)
    return pl.pallas_call(
        _norm1_kernel,
        out_shape=(jax.ShapeDtypeStruct((t, D_MODEL), BF16),
                   jax.ShapeDtypeStruct((t, LANES), F32)),
        grid=(t // TM_NORM,),
        in_specs=[pl.BlockSpec((TM_NORM, D_MODEL), row),
                  pl.BlockSpec((1, D_MODEL), const),
                  pl.BlockSpec((D_MODEL, LANES), const),
                  pl.BlockSpec((1, LANES), const)],
        out_specs=(pl.BlockSpec((TM_NORM, D_MODEL), row), pl.BlockSpec((TM_NORM, LANES), row)),
        compiler_params=pltpu.CompilerParams(
            dimension_semantics=("arbitrary",), vmem_limit_bytes=VMEM_LIMIT),
        name="norm1",
    )(x2, g1, w_f, b_f)


N_SECTIONS = 1 + 3 + N_GATE_SEC
N_CHUNKS = N_MAIN_SEC + N_GATE_SEC
GATE_SECTION0 = 4
WCONV_ROWS = 256


def _inproj_kernel(h_ref, wt_hbm, bg_ref, lng_ref, lnb_ref, wsp_ref, bsp_ref, qg_ref, kg_ref,
                   asgu_ref, q_ref, kt_ref, va_ref, gates_ref,
                   stage, wb, sem):
    s = pl.program_id(0)
    i = pl.program_id(1)
    tm = h_ref.shape[0]

    def chunk_copy(c, slot):
        r0 = pl.multiple_of(c * SEC + jnp.where(c >= N_MAIN_SEC, FOX_HEADS, 0), 8)
        return pltpu.make_async_copy(wt_hbm.at[pl.ds(r0, SEC), :], stage.at[slot], sem.at[slot])

    def convert(slot, k):
        for p in range(SEC // WCONV_ROWS):
            rs = slice(p * WCONV_ROWS, (p + 1) * WCONV_ROWS)
            wb[k, :, rs] = stage[slot, rs, :].T.astype(BF16)

    @pl.when((s == 0) & (i == 0))
    def _():
        chunk_copy(0, 0).start()
        chunk_copy(1, 1).start()
        chunk_copy(0, 0).wait()
        convert(0, 0)
        chunk_copy(2, 0).start(priority=PREFETCH_PRIORITY)
        chunk_copy(1, 1).wait()
        convert(1, 1)

    @pl.when((s > 0) & (i == 0))
    def _():
        c = s + 1
        slot = c % 2
        chunk_copy(c, slot).wait()

        @pl.when(c + 1 < N_CHUNKS)
        def _():
            chunk_copy(c + 1, 1 - slot).start(priority=PREFETCH_PRIORITY)

        convert(slot, 0)

    def section(k=0):
        return jnp.dot(h_ref[...], wb[k], preferred_element_type=F32)

    @pl.when(s == 0)
    def _():
        u = jax.nn.gelu(section(0))
        v = jax.nn.gelu(section(1))
        mu = jnp.mean(v, axis=-1, keepdims=True)
        vc = v - mu
        var = jnp.mean(vc * vc, axis=-1, keepdims=True)
        vn = ((vc * lax.rsqrt(var + EPS)) * lng_ref[...] + lnb_ref[...]).astype(BF16)
        row = lax.broadcasted_iota(I32, (CHUNK, CHUNK), 0)
        col = lax.broadcasted_iota(I32, (CHUNK, CHUNK), 1)
        causal = row >= col
        for g in range(SGU_GROUPS):
            wg = jnp.where(causal, wsp_ref[g], 0.0).astype(BF16)
            bcol = bsp_ref[:, g:g + 1]
            gs = slice(g * LANES, (g + 1) * LANES)
            n_ch = tm // CHUNK
            rhs = jnp.concatenate([vn[c * CHUNK:(c + 1) * CHUNK, gs] for c in range(n_ch)], axis=1)
            sg = jnp.dot(wg, rhs, preferred_element_type=F32) + bcol
            for c in range(n_ch):
                rs = slice(c * CHUNK, (c + 1) * CHUNK)
                asgu_ref[rs, gs] = (u[rs, gs] * sg[:, c * LANES:(c + 1) * LANES]).astype(BF16)

    def _head_norm(z, h, gain_ref):
        zh = z[:, h * HEAD_DIM:(h + 1) * HEAD_DIM]
        ms = jnp.mean(zh * zh, axis=-1, keepdims=True)
        return (zh * lax.rsqrt(ms + EPS)) * gain_ref[...]

    @pl.when(s == 1)
    def _():
        z = section()
        for h in range(FOX_HEADS):
            q_ref[:, h * HEAD_DIM:(h + 1) * HEAD_DIM] = (
                _head_norm(z, h, qg_ref) * Q_SCALE).astype(BF16)

    @pl.when(s == 2)
    def _():
        z = section()
        for h in range(FOX_HEADS):
            kt_ref[h * HEAD_DIM:(h + 1) * HEAD_DIM, :] = _head_norm(z, h, kg_ref).T.astype(BF16)

    @pl.when(s == 3)
    def _():
        va_ref[...] = section().astype(BF16)

    @pl.when(s >= GATE_SECTION0)
    def _():
        gates_ref[...] = (0.5 * jnp.tanh(0.5 * (section() + bg_ref[...])) + 0.5).astype(BF16)


def _inproj(h, wt, b_gate, ln_g, ln_b, w_sp, b_sp_t, q_g, k_g):
    t = h.shape[0]
    n_i = t // TM_IN

    def active(sec):
        return lambda s, i: jnp.where(s < sec, 0, jnp.where(s > sec, n_i - 1, i))

    def gate_blk(s, i):
        g = jnp.clip(s - GATE_SECTION0, 0, N_GATE_SEC - 1)
        return (jnp.where(s < GATE_SECTION0, 0, i), g)

    const2 = lambda s, i: (0, 0)
    out_shape = (
        jax.ShapeDtypeStruct((t, SGU_WIDTH), BF16),
        jax.ShapeDtypeStruct((t, FOX_WIDTH), BF16),
        jax.ShapeDtypeStruct((FOX_WIDTH, t), BF16),
        jax.ShapeDtypeStruct((t, FOX_WIDTH), BF16),
        jax.ShapeDtypeStruct((t, 2 * D_MODEL), BF16),
    )
    return pl.pallas_call(
        _inproj_kernel,
        out_shape=out_shape,
        grid=(N_SECTIONS, n_i),
        in_specs=[
            pl.BlockSpec((TM_IN, D_MODEL), lambda s, i: (i, 0)),
            pl.BlockSpec(memory_space=pl.ANY),
            pl.BlockSpec((1, SEC), lambda s, i: (0, jnp.clip(s - GATE_SECTION0, 0, N_GATE_SEC - 1))),
            pl.BlockSpec((1, SGU_WIDTH), const2),
            pl.BlockSpec((1, SGU_WIDTH), const2),
            pl.BlockSpec((SGU_GROUPS, CHUNK, CHUNK), lambda s, i: (0, 0, 0)),
            pl.BlockSpec((CHUNK, SGU_GROUPS), const2),
            pl.BlockSpec((1, HEAD_DIM), const2),
            pl.BlockSpec((1, HEAD_DIM), const2),
        ],
        out_specs=(
            pl.BlockSpec((TM_IN, SGU_WIDTH), lambda s, i: (active(0)(s, i), 0)),
            pl.BlockSpec((TM_IN, FOX_WIDTH), lambda s, i: (active(1)(s, i), 0)),
            pl.BlockSpec((FOX_WIDTH, TM_IN), lambda s, i: (0, active(2)(s, i))),
            pl.BlockSpec((TM_IN, FOX_WIDTH), lambda s, i: (active(3)(s, i), 0)),
            pl.BlockSpec((TM_IN, SEC), gate_blk),
        ),
        scratch_shapes=[pltpu.VMEM((2, SEC, D_MODEL), F32),
                        pltpu.VMEM((2, D_MODEL, SEC), BF16),
                        pltpu.SemaphoreType.DMA((2,))],
        compiler_params=pltpu.CompilerParams(
            dimension_semantics=("arbitrary", "arbitrary"), vmem_limit_bytes=VMEM_LIMIT),
        name="inproj",
    )(h, wt, b_gate, ln_g, ln_b, w_sp, b_sp_t, q_g, k_g)


def _split3(x):
    x0 = x.astype(BF16)
    r1 = x - x0.astype(F32)
    x1 = r1.astype(BF16)
    r2 = r1 - x1.astype(F32)
    return x0, x1, r2.astype(BF16)


def _cumsum_kernel(lf_ref, negc_ref, c_s):
    seq = lf_ref.shape[0]
    row = lax.broadcasted_iota(I32, (CHUNK, CHUNK), 0)
    col = lax.broadcasted_iota(I32, (CHUNK, CHUNK), 1)
    tri = jnp.where(row >= col, 1.0, 0.0).astype(BF16)
    carry = jnp.zeros((1, LANES), F32)
    for r in range(seq // CHUNK):
        rs = slice(r * CHUNK, (r + 1) * CHUNK)
        x0, x1, x2 = _split3(lf_ref[rs, :])
        cs = (jnp.dot(tri, x0, preferred_element_type=F32)
              + jnp.dot(tri, x1, preferred_element_type=F32)
              + jnp.dot(tri, x2, preferred_element_type=F32)) + carry
        carry = cs[CHUNK - 1:CHUNK, :]
        c_s[rs, :] = cs
    ct = c_s[...].T
    negc_ref[0] = -ct[0:FOX_HEADS, :]


def _forget_cumsum(lf, batch, seq):
    return pl.pallas_call(
        _cumsum_kernel,
        out_shape=jax.ShapeDtypeStruct((batch, FOX_HEADS, seq), F32),
        grid=(batch,),
        in_specs=[pl.BlockSpec((seq, LANES), lambda b: (b, 0))],
        out_specs=pl.BlockSpec((1, FOX_HEADS, seq), lambda b: (b, 0, 0)),
        scratch_shapes=[pltpu.VMEM((seq, LANES), F32)],
        compiler_params=pltpu.CompilerParams(
            dimension_semantics=("arbitrary",), vmem_limit_bytes=VMEM_LIMIT),
        name="forget_cumsum",
    )(lf)


def _attn_kernel(q_ref, kt_ref, v_ref, negc_ref, o_ref, kx, vx):
    seq = q_ref.shape[0]
    sub = lax.broadcasted_iota(I32, (HEAD_DIM, seq), 0)
    for hh in range(HEADS_PER_STEP):
        hs = slice(hh * HEAD_DIM, (hh + 1) * HEAD_DIM)
        c0, c1, c2 = (c.astype(F32) for c in _split3(negc_ref[0, hh] * LOG2E))
        kx[hh, 0:HEAD_DIM, :] = kt_ref[hs, :]
        kx[hh, HEAD_DIM:, :] = jnp.where(
            sub == 0, c0, jnp.where(sub == 1, c1, jnp.where(sub == 2, c2, 0.0))).astype(BF16)
        vx[hh, :, 0:HEAD_DIM] = v_ref[:, hs]
        vx[hh, :, HEAD_DIM:] = jnp.ones((seq, HEAD_DIM), BF16)
    lane = lax.broadcasted_iota(I32, (TQ, HEAD_DIM), 1)
    bias_cols = jnp.where(lane < N_SPLIT, 1.0, 0.0).astype(BF16)
    row = lax.broadcasted_iota(I32, (TQ, TQ), 0)
    col = lax.broadcasted_iota(I32, (TQ, TQ), 1)
    causal = row >= col
    for qi in range(seq // TQ):
        for hh in range(HEADS_PER_STEP):
            hs = slice(hh * HEAD_DIM, (hh + 1) * HEAD_DIM)
            k0 = qi * TQ
            q = jnp.concatenate([q_ref[k0:k0 + TQ, hs], bias_cols], axis=1)
            s_d = jnp.dot(q, kx[hh, :, k0:k0 + TQ], preferred_element_type=F32)
            s_d = jnp.where(causal, s_d, -jnp.inf)
            m = jnp.max(s_d, axis=-1, keepdims=True)
            if qi > 0:
                s_o = jnp.dot(q, kx[hh, :, 0:k0], preferred_element_type=F32)
                m = jnp.maximum(m, jnp.max(s_o, axis=-1, keepdims=True))
            acc = jnp.dot(jnp.exp2(s_d - m).astype(BF16), vx[hh, k0:k0 + TQ, :],
                          preferred_element_type=F32)
            if qi > 0:
                acc = acc + jnp.dot(jnp.exp2(s_o - m).astype(BF16), vx[hh, 0:k0, :],
                                    preferred_element_type=F32)
            inv_l = 1.0 / acc[:, HEAD_DIM:HEAD_DIM + 1]
            o_ref[k0:k0 + TQ, hs] = (acc[:, 0:HEAD_DIM] * inv_l).astype(BF16)


def _attention(q, kt, va, negc, batch, seq):
    t = q.shape[0]
    width = HEADS_PER_STEP * HEAD_DIM
    blk = pl.BlockSpec((seq, width), lambda b, g: (b, g))
    return pl.pallas_call(
        _attn_kernel,
        out_shape=jax.ShapeDtypeStruct((t, FOX_WIDTH), BF16),
        grid=(batch, FOX_HEADS // HEADS_PER_STEP),
        in_specs=[blk,
                  pl.BlockSpec((width, seq), lambda b, g: (g, b)),
                  blk,
                  pl.BlockSpec((1, HEADS_PER_STEP, 1, seq), lambda b, g: (b, g, 0, 0))],
        out_specs=blk,
        scratch_shapes=[pltpu.VMEM((HEADS_PER_STEP, 2 * HEAD_DIM, seq), BF16),
                        pltpu.VMEM((HEADS_PER_STEP, seq, 2 * HEAD_DIM), BF16)],
        compiler_params=pltpu.CompilerParams(
            dimension_semantics=("arbitrary", "arbitrary"), vmem_limit_bytes=VMEM_LIMIT),
        name="fox_attention",
    )(q, kt, va, negc)


MIX_WROWS = 512


def _mix_kernel(as_ref, af_ref, g_ref, x_ref, wps_hbm, wpf_hbm, wo_hbm, g2_ref, wr_ref, br_ref,
                x1_ref, lg_ref, wps_ref, wpf_ref, wo_ref, stage, sem):
    @pl.when(pl.program_id(0) == 0)
    def _():
        pieces = [(src, dst, r0) for src, dst in ((wps_hbm, wps_ref), (wpf_hbm, wpf_ref),
                                                  (wo_hbm, wo_ref))
                  for r0 in range(0, dst.shape[0], MIX_WROWS)]

        def piece_copy(k):
            src, _, r0 = pieces[k]
            return pltpu.make_async_copy(src.at[pl.ds(r0, MIX_WROWS), :], stage.at[k % 2],
                                         sem.at[k % 2])

        piece_copy(0).start()
        for k, (_, dst, r0) in enumerate(pieces):
            if k + 1 < len(pieces):
                piece_copy(k + 1).start()
            piece_copy(k).wait()
            dst[r0:r0 + MIX_WROWS, :] = stage[k % 2].astype(BF16)

    ys = jnp.dot(as_ref[...], wps_ref[...], preferred_element_type=F32)
    yf = jnp.dot(af_ref[...], wpf_ref[...], preferred_element_type=F32)
    m = (g_ref[:, :D_MODEL].astype(F32) * ys + g_ref[:, D_MODEL:].astype(F32) * yf).astype(BF16)
    x1 = x_ref[...] + jnp.dot(m, wo_ref[...], preferred_element_type=F32)
    x1_ref[...] = x1
    ms = jnp.mean(x1 * x1, axis=-1, keepdims=True)
    h2 = ((x1 * lax.rsqrt(ms + EPS)) * g2_ref[...]).astype(BF16)
    lg_ref[...] = jnp.dot(h2, wr_ref[...], preferred_element_type=F32) + br_ref[...]


def _mix(a_sgu, a_fox, gates, x2, wps, wpf, wo, g2, w_r, b_r):
    t = x2.shape[0]
    row = lambda i: (i, 0)
    const = lambda i: (0, 0)
    resident = functools.partial(pl.BlockSpec, index_map=const, pipeline_mode=pl.Buffered(1))
    return pl.pallas_call(
        _mix_kernel,
        out_shape=(jax.ShapeDtypeStruct((t, D_MODEL), F32),
                   jax.ShapeDtypeStruct((t, LANES), F32)),
        grid=(t // TM_MIX,),
        in_specs=[
            pl.BlockSpec((TM_MIX, SGU_WIDTH), row),
            pl.BlockSpec((TM_MIX, FOX_WIDTH), row),
            pl.BlockSpec((TM_MIX, 2 * D_MODEL), row),
            pl.BlockSpec((TM_MIX, D_MODEL), row),
            pl.BlockSpec(memory_space=pl.ANY),
            pl.BlockSpec(memory_space=pl.ANY),
            pl.BlockSpec(memory_space=pl.ANY),
            pl.BlockSpec((1, D_MODEL), const),
            resident((D_MODEL, LANES)),
            pl.BlockSpec((1, LANES), const),
        ],
        out_specs=(pl.BlockSpec((TM_MIX, D_MODEL), row), pl.BlockSpec((TM_MIX, LANES), row)),
        scratch_shapes=[pltpu.VMEM((SGU_WIDTH, D_MODEL), BF16),
                        pltpu.VMEM((FOX_WIDTH, D_MODEL), BF16),
                        pltpu.VMEM((D_MODEL, D_MODEL), BF16),
                        pltpu.VMEM((2, MIX_WROWS, D_MODEL), F32),
                        pltpu.SemaphoreType.DMA((2,))],
        compiler_params=pltpu.CompilerParams(
            dimension_semantics=("arbitrary",), vmem_limit_bytes=VMEM_LIMIT),
        name="mix",
    )(a_sgu, a_fox, gates, x2, wps, wpf, wo, g2, w_r, b_r)


def _route_kernel(lg_ref, dt_ref, mf_ref, bm_ref, em_ref, bt_ref, mi_ref, inv_ref):
    t = lg_ref.shape[0]
    n_chunks = t // RCH
    lane_i = lax.broadcasted_iota(I32, (RCH, LANES), 1)
    lane = lane_i.astype(F32)
    lane_grp = ((lane_i - N_GROUPS) >> EXPERT_GROUP_SHIFT).astype(F32)
    is_grp = lane_i < N_GROUPS
    is_exp = (lane_i >= N_GROUPS) & (lane_i < N_GROUPS + N_EXPERTS)
    r_i = lax.broadcasted_iota(I32, (RCH, RCH), 0)
    c_i = lax.broadcasted_iota(I32, (RCH, RCH), 1)
    strict_lower = jnp.where(r_i > c_i, 1.0, 0.0).astype(BF16)
    neg_inf = -jnp.inf

    def first_max(vals):
        vmax = jnp.max(vals, axis=-1, keepdims=True)
        idx = jnp.min(jnp.where(vals == vmax, lane, float(LANES)), axis=-1, keepdims=True)
        return vmax, idx

    def pick(table, idx):
        return jnp.sum(jnp.where(lane == idx, table, 0.0), axis=-1, keepdims=True)

    def pack(cols):
        out = jnp.zeros((RCH, LANES), F32)
        for n, c in enumerate(cols):
            out = jnp.where(lane_i == n, c, out)
        return out

    def pass1(ci, counts):
        r0 = pl.multiple_of(ci * RCH, RCH)
        lg = lg_ref[pl.ds(r0, RCH), :]
        gmax, grp = first_max(jnp.where(is_grp, lg, neg_inf))
        p_grp = 1.0 / jnp.sum(jnp.where(is_grp, jnp.exp(lg - gmax), 0.0), axis=-1, keepdims=True)
        el = jnp.where(is_exp & (lane_grp == grp), lg, neg_inf)
        v1, i1 = first_max(el)
        v2, i2 = first_max(jnp.where(lane == i1, neg_inf, el))
        e21 = jnp.exp(v2 - v1)
        w1 = p_grp / (1.0 + e21)
        w2 = p_grp * e21 / (1.0 + e21)
        e1 = i1 - float(N_GROUPS)
        e2 = i2 - float(N_GROUPS)
        hot = jnp.where((lane == e1) | (lane == e2), 1.0, 0.0)
        before = jnp.dot(strict_lower, hot.astype(BF16), preferred_element_type=F32) + counts
        mi_ref[pl.ds(r0, RCH), :] = pack([e1, e2, pick(before, e1), pick(before, e2)]).astype(I32)
        mf_ref[pl.ds(r0, RCH), :] = pack([w1, w2])
        return counts + jnp.sum(hot, axis=0, keepdims=True)

    counts = lax.fori_loop(0, n_chunks, pass1, jnp.zeros((1, LANES), F32))

    nblk = jnp.floor((counts + float(EXPERT_BLOCK - 1)) * (1.0 / EXPERT_BLOCK))
    u_r = lax.broadcasted_iota(I32, (LANES, LANES), 0)
    u_c = lax.broadcasted_iota(I32, (LANES, LANES), 1)
    strict_upper = jnp.where(u_r < u_c, 1.0, 0.0).astype(BF16)
    bstart = jnp.dot(jnp.broadcast_to(nblk, (SUBLANES, LANES)).astype(BF16), strict_upper,
                     preferred_element_type=F32)[0:1, :]
    bend = bstart + nblk

    def pass2(ci, _):
        r0 = pl.multiple_of(ci * RCH, RCH)
        mi = mi_ref[pl.ds(r0, RCH), :].astype(F32)
        cols = [jnp.sum(jnp.where(lane_i == n, mi, 0.0), axis=-1, keepdims=True) for n in range(4)]
        d1 = pick(bstart, cols[0]) * float(EXPERT_BLOCK) + cols[2]
        d2 = pick(bstart, cols[1]) * float(EXPERT_BLOCK) + cols[3]
        d_t = pack([d1, d2]).T
        dt_ref[ci] = d_t[0:SUBLANES, :].astype(I32)
        tok = (lax.broadcasted_iota(I32, (RCH, 1), 0) + r0).astype(F32)
        tok_hi = jnp.floor(tok * (1.0 / LANES))
        tok_lo = tok - tok_hi * float(LANES)
        for n, d in enumerate((d1, d2)):
            d_hi = jnp.floor(d * (1.0 / LANES))
            hit = lane == (d - d_hi * float(LANES))
            blk_row = jnp.floor(d_t[n:n + 1, :] * (1.0 / LANES))
            sel = jnp.where(inv_blk == blk_row, 1.0, 0.0).astype(BF16)
            inv_ref[0] += jnp.dot(sel, jnp.where(hit, tok_hi, 0.0).astype(BF16),
                                  preferred_element_type=F32)
            inv_ref[1] += jnp.dot(sel, jnp.where(hit, tok_lo, 0.0).astype(BF16),
                                  preferred_element_type=F32)
        return 0

    nb_rows = bm_ref.shape[0]
    inv_blk = lax.broadcasted_iota(I32, (nb_rows, RCH), 0).astype(F32)
    inv_ref[...] = jnp.zeros_like(inv_ref)
    lax.fori_loop(0, n_chunks, pass2, 0)
    bt_ref[...] = (inv_ref[0] * float(LANES) + inv_ref[1]).astype(I32)

    b_col = lax.broadcasted_iota(I32, (nb_rows, LANES), 0).astype(F32)
    b_lane = lax.broadcasted_iota(I32, (nb_rows, LANES), 1)
    done = jnp.where((bend <= b_col) & (b_lane < N_EXPERTS), 1.0, 0.0)
    blk_e = jnp.minimum(jnp.sum(done, axis=-1, keepdims=True), float(N_EXPERTS - 1))
    n_used = jnp.sum(jnp.where(b_lane[0:1, :] == N_EXPERTS - 1, bend, 0.0), axis=-1, keepdims=True)
    bm_ref[...] = jnp.where(b_lane == 0, blk_e, jnp.where(b_lane == 1, n_used, 0.0)).astype(I32)

    em_ref[...] = jnp.broadcast_to(nblk, (SUBLANES, LANES)).astype(I32)


def _route(logits, nb_rows):
    t = logits.shape[0]
    full = lambda shape: pl.BlockSpec(shape, lambda: (0,) * len(shape))
    return pl.pallas_call(
        _route_kernel,
        out_shape=(jax.ShapeDtypeStruct((t // RCH, SUBLANES, RCH), I32),
                   jax.ShapeDtypeStruct((t, LANES), F32),
                   jax.ShapeDtypeStruct((nb_rows, LANES), I32),
                   jax.ShapeDtypeStruct((SUBLANES, LANES), I32),
                   jax.ShapeDtypeStruct((nb_rows, LANES), I32)),
        in_specs=[full((t, LANES))],
        out_specs=(full((t // RCH, SUBLANES, RCH)), full((t, LANES)), full((nb_rows, LANES)),
                   full((SUBLANES, LANES)), full((nb_rows, LANES))),
        scratch_shapes=[pltpu.VMEM((t, LANES), I32), pltpu.VMEM((2, nb_rows, LANES), F32)],
        compiler_params=pltpu.CompilerParams(vmem_limit_bytes=VMEM_LIMIT),
        name="route",
    )(logits)


GATHER_GROUPS = 4
GATHER_AHEAD = 8
WEIGHT_SLOTS = 3


def _experts_kernel(blk_e_ref, nused_ref, nblk_ref, tok_ref, x_hbm, g2_ref, wg_hbm, wu_hbm, wd_hbm,
                    y_ref, xg, wg_f, wu_f, wd_f, wg_s, wu_s, wd_s, slot_ref, sem, gsem):
    b = pl.program_id(0)
    n_used = nused_ref[0]
    used = b < n_used
    e = blk_e_ref[b]
    new_expert = (b == 0) | (e != blk_e_ref[jnp.maximum(b - 1, 0)])
    n_slots = GATHER_AHEAD + 1
    xslot = b % n_slots

    def gather_rows(block, slot, lo, hi):
        row0 = jnp.minimum(block, n_used - 1) * EXPERT_BLOCK
        for r in range(lo, hi):
            pltpu.make_async_copy(x_hbm.at[pl.ds(tok_ref[row0 + r], 1), :],
                                  xg.at[slot, pl.ds(r, 1), :], gsem.at[slot]).start()

    def gather_wait(slot):
        pltpu.make_async_copy(x_hbm.at[pl.ds(0, EXPERT_BLOCK), :], xg.at[slot], gsem.at[slot]).wait()

    @pl.when(b == 0)
    def _():
        for k in range(GATHER_AHEAD):
            gather_rows(k, k, 0, EXPERT_BLOCK)

    def weight_copies(expert, slot):
        return (pltpu.make_async_copy(wg_hbm.at[expert], wg_f.at[slot], sem.at[slot]),
                pltpu.make_async_copy(wu_hbm.at[expert], wu_f.at[slot], sem.at[slot]),
                pltpu.make_async_copy(wd_hbm.at[expert], wd_f.at[slot], sem.at[slot]))

    def block_expert(blk):
        return blk_e_ref[jnp.minimum(blk, n_used - 1)]

    b_next = b + nblk_ref[e]
    b_next2 = b_next + nblk_ref[block_expert(b_next)]

    @pl.when(b == 0)
    def _():
        slot_ref[0] = 0
        for cp in weight_copies(e, 0):
            cp.start()

        @pl.when(b_next < n_used)
        def _():
            for cp in weight_copies(block_expert(b_next), 1):
                cp.start(priority=PREFETCH_PRIORITY)

    @pl.when(used & new_expert)
    def _():
        slot = slot_ref[0]

        @pl.when(b_next2 < n_used)
        def _():
            for cp in weight_copies(block_expert(b_next2), (slot + 2) % WEIGHT_SLOTS):
                cp.start(priority=PREFETCH_PRIORITY)

        for cp in weight_copies(e, slot):
            cp.wait()
        wg_s[...] = wg_f[slot].astype(BF16)
        wu_s[...] = wu_f[slot].astype(BF16)
        wd_s[...] = wd_f[slot].astype(BF16)
        slot_ref[0] = (slot + 1) % WEIGHT_SLOTS

    @pl.when(used)
    def _():
        gather_wait(xslot)
        ahead = b + GATHER_AHEAD
        aslot = ahead % n_slots
        per_group = EXPERT_BLOCK // GATHER_GROUPS
        x = xg[xslot]
        ms = jnp.mean(x * x, axis=-1, keepdims=True)
        h = ((x * lax.rsqrt(ms + EPS)) * g2_ref[...]).astype(BF16)
        gather_rows(ahead, aslot, 0, per_group)
        a = jnp.dot(h, wg_s[...], preferred_element_type=F32)
        gather_rows(ahead, aslot, per_group, 2 * per_group)
        u = jnp.dot(h, wu_s[...], preferred_element_type=F32)
        gather_rows(ahead, aslot, 2 * per_group, 3 * per_group)
        mid = ((a * _sigmoid(a)) * u).astype(BF16)
        gather_rows(ahead, aslot, 3 * per_group, EXPERT_BLOCK)
        y_ref[...] = jnp.dot(mid, wd_s[...], preferred_element_type=F32)

    @pl.when(b == n_used - 1)
    def _():
        for k in range(1, GATHER_AHEAD + 1):
            gather_wait((b + k) % n_slots)

    @pl.when(jnp.logical_not(used))
    def _():
        y_ref[...] = jnp.zeros_like(y_ref)


def _experts(blk_e, n_used, nblk, row_tok, x1, g2, w_g, w_u, w_d):
    n_rows = row_tok.shape[0]
    nb = n_rows // EXPERT_BLOCK
    hbm = pl.BlockSpec(memory_space=pl.ANY)
    grid_spec = pltpu.PrefetchScalarGridSpec(
        num_scalar_prefetch=4,
        grid=(nb,),
        in_specs=[
            hbm,
            pl.BlockSpec((1, D_MODEL), lambda b, *_: (0, 0)),
            hbm, hbm, hbm,
        ],
        out_specs=pl.BlockSpec((EXPERT_BLOCK, D_MODEL), lambda b, *_: (b, 0)),
        scratch_shapes=[pltpu.VMEM((GATHER_AHEAD + 1, EXPERT_BLOCK, D_MODEL), F32),
                        pltpu.VMEM((WEIGHT_SLOTS, D_MODEL, D_EXPERT), F32),
                        pltpu.VMEM((WEIGHT_SLOTS, D_MODEL, D_EXPERT), F32),
                        pltpu.VMEM((WEIGHT_SLOTS, D_EXPERT, D_MODEL), F32),
                        pltpu.VMEM((D_MODEL, D_EXPERT), BF16),
                        pltpu.VMEM((D_MODEL, D_EXPERT), BF16),
                        pltpu.VMEM((D_EXPERT, D_MODEL), BF16),
                        pltpu.SMEM((1,), I32),
                        pltpu.SemaphoreType.DMA((WEIGHT_SLOTS,)),
                        pltpu.SemaphoreType.DMA((GATHER_AHEAD + 1,))],
    )
    return pl.pallas_call(
        _experts_kernel,
        out_shape=jax.ShapeDtypeStruct((n_rows, D_MODEL), F32),
        grid_spec=grid_spec,
        compiler_params=pltpu.CompilerParams(
            dimension_semantics=("arbitrary",), vmem_limit_bytes=VMEM_LIMIT),
        name="experts",
    )(blk_e, n_used, nblk, row_tok, x1, g2, w_g, w_u, w_d)


def _combine_kernel(d1_ref, d2_ref, x1_ref, w_ref, y_ref, o_ref, gbuf, sem):
    i = pl.program_id(0)
    n = pl.num_programs(0)
    tm = x1_ref.shape[0]

    def request_row(base, slot, r):
        pltpu.make_async_copy(y_ref.at[pl.ds(d1_ref[base + r], 1), :],
                              gbuf.at[slot, pl.ds(r, 1), :], sem.at[slot]).start()
        pltpu.make_async_copy(y_ref.at[pl.ds(d2_ref[base + r], 1), :],
                              gbuf.at[slot, pl.ds(tm + r, 1), :], sem.at[slot]).start()

    def issue(tile, slot, lo, hi):
        base = jnp.minimum(tile, n - 1) * tm
        for r in range(lo, hi):
            request_row(base, slot, r)

    def tile_wait(slot):
        pltpu.make_async_copy(y_ref.at[pl.ds(0, 2 * tm), :], gbuf.at[slot], sem.at[slot]).wait()

    slot = i % 2

    @pl.when(i == 0)
    def _():
        def first_tile(r, _):
            request_row(0, 0, r)
            return 0

        lax.fori_loop(0, tm, first_tile, 0, unroll=SUBLANES)

    tile_wait(slot)
    for c in range(tm // COMBINE_ROWS):
        rs = slice(c * COMBINE_ROWS, (c + 1) * COMBINE_ROWS)
        rs2 = slice(tm + c * COMBINE_ROWS, tm + (c + 1) * COMBINE_ROWS)
        w = w_ref[rs, :]
        o_ref[rs, :] = x1_ref[rs, :] + (w[:, 0:1] * gbuf[slot, rs, :] + w[:, 1:2] * gbuf[slot, rs2, :])
        issue(i + 1, 1 - slot, c * COMBINE_ROWS, (c + 1) * COMBINE_ROWS)

    @pl.when(i == n - 1)
    def _():
        tile_wait(1 - slot)


def _combine(dest1, dest2, x1, w, yb):
    t = x1.shape[0]
    grid_spec = pltpu.PrefetchScalarGridSpec(
        num_scalar_prefetch=2,
        grid=(t // TM_CMB,),
        in_specs=[
            pl.BlockSpec((TM_CMB, D_MODEL), lambda i, a, b: (i, 0)),
            pl.BlockSpec((TM_CMB, LANES), lambda i, a, b: (i, 0)),
            pl.BlockSpec(memory_space=pl.ANY),
        ],
        out_specs=pl.BlockSpec((TM_CMB, D_MODEL), lambda i, a, b: (i, 0)),
        scratch_shapes=[pltpu.VMEM((2, 2 * TM_CMB, D_MODEL), F32),
                        pltpu.SemaphoreType.DMA((2,))],
    )
    return pl.pallas_call(
        _combine_kernel,
        out_shape=jax.ShapeDtypeStruct((t, D_MODEL), F32),
        grid_spec=grid_spec,
        compiler_params=pltpu.CompilerParams(
            dimension_semantics=("arbitrary",), vmem_limit_bytes=VMEM_LIMIT),
        name="combine",
    )(dest1, dest2, x1, w, yb)


def kernel(x, norm1_g, w_in, b_gate, b_forget, sgu_ln_g, sgu_ln_b, w_spatial, b_spatial, q_norm_g, k_norm_g, w_proj_sgu, w_proj_fox, w_out, norm2_g, w_router_group, b_router_group, w_router_expert, b_router_expert, w_expert_gate, w_expert_up, w_expert_down):
    batch, seq, d = x.shape
    t = batch * seq
    l = 0
    x2 = x.reshape(t, d)

    wt = jnp.swapaxes(w_in[l], 0, 1)
    off_f = N_MAIN_SEC * SEC
    w_f = jnp.pad(wt[off_f:off_f + FOX_HEADS].T, ((0, 0), (0, LANES - FOX_HEADS))).astype(BF16)
    b_f = jnp.pad(b_forget[l], (0, LANES - FOX_HEADS)).reshape(1, LANES)
    n_r = N_GROUPS + N_EXPERTS
    w_r = jnp.pad(jnp.concatenate([w_router_group[l], w_router_expert[l]], axis=1),
                  ((0, 0), (0, LANES - n_r))).astype(BF16)
    b_r = jnp.pad(jnp.concatenate([b_router_group[l], b_router_expert[l]]),
                  (0, LANES - n_r)).reshape(1, LANES)

    h, lf = _norm1(x2, norm1_g[l].reshape(1, d), w_f, b_f)
    a_sgu, q, kt, va, gates = _inproj(
        h, wt, b_gate[l].reshape(1, 2 * d),
        sgu_ln_g[l].reshape(1, SGU_WIDTH), sgu_ln_b[l].reshape(1, SGU_WIDTH),
        w_spatial[l], b_spatial[l].T, q_norm_g[l].reshape(1, HEAD_DIM),
        k_norm_g[l].reshape(1, HEAD_DIM))
    negc = _forget_cumsum(lf, batch, seq).reshape(batch, FOX_HEADS, 1, seq)
    a_fox = _attention(q, kt, va, negc, batch, seq)
    x1, logits = _mix(a_sgu, a_fox, gates, x2, w_proj_sgu[l], w_proj_fox[l], w_out[l],
                      norm2_g[l].reshape(1, d), w_r, b_r)

    n_assign = 2 * t
    n_rows = n_assign + N_EXPERTS * EXPERT_BLOCK
    nb = n_rows // EXPERT_BLOCK
    dest_t, meta_f, bmeta, nblk, row_tok = _route(logits, n_rows // LANES)
    dest1, dest2 = dest_t[:, 0, :].reshape(t), dest_t[:, 1, :].reshape(t)
    n_used = bmeta[0:1, 1]
    yb = _experts(bmeta[:nb, 0], n_used, nblk[0, :N_EXPERTS], row_tok.reshape(n_rows), x1,
                  norm2_g[l].reshape(1, d), w_expert_gate[l], w_expert_up[l], w_expert_down[l])
    out = _combine(dest1, dest2, x1, meta_f, yb)
    return out.reshape(batch, seq, d)
```

```python
import functools

import jax
import jax.numpy as jnp
from jax import lax
from jax.experimental import pallas as pl
from jax.experimental.pallas import tpu as pltpu

F32 = jnp.float32
BF16 = jnp.bfloat16
I32 = jnp.int32

D_MODEL = 2048
CHUNK = 128
SGU_GROUPS = 8
SGU_WIDTH = 1024
FOX_HEADS = 8
HEAD_DIM = 128
FOX_WIDTH = 1024
N_GROUPS = 4
EXPERTS_PER_GROUP = 8
N_EXPERTS = 32
D_EXPERT = 512
EXPERT_BLOCK = 128
EPS = 1e-6

LANES = 128
SUBLANES = 8
PREFETCH_PRIORITY = 1
VMEM_LIMIT = 56 * 1024 * 1024

SEC = 1024
N_MAIN_SEC = 5
N_GATE_SEC = 4
TM_NORM = 1024
TM_IN = 512
TQ = 256
HEADS_PER_STEP = 4
TM_MIX = 256
RCH = 512
TM_CMB = 512
COMBINE_ROWS = 64
COMBINE_AHEAD = 2
N_SPLIT = 3
EXPERT_GROUP_SHIFT = EXPERTS_PER_GROUP.bit_length() - 1
LOG2E = 1.4426950408889634
Q_SCALE = HEAD_DIM ** -0.5 * LOG2E


def _sigmoid(x):
    return 1.0 / (1.0 + jnp.exp(-x))


def _log_sigmoid(x):
    return jnp.minimum(x, 0.0) - jnp.log1p(jnp.exp(-jnp.abs(x)))


def _norm1_kernel(x_ref, g1_ref, wf_ref, bf_ref, h_ref, lf_ref):
    x = x_ref[...]
    ms = jnp.mean(x * x, axis=-1, keepdims=True)
    hb = ((x * lax.rsqrt(ms + EPS)) * g1_ref[...]).astype(BF16)
    h_ref[...] = hb
    f = jnp.dot(hb, wf_ref[...], preferred_element_type=F32) + bf_ref[...]
    lf_ref[...] = _log_sigmoid(f)


def _norm1(x2, g1, w_f, b_f):
    t = x2.shape[0]
    row = lambda i: (i, 0)
    const = lambda i: (0, 0)
    return pl.pallas_call(
        _norm1_kernel,
        out_shape=(jax.ShapeDtypeStruct((t, D_MODEL), BF16),
                   jax.ShapeDtypeStruct((t, LANES), F32)),
        grid=(t // TM_NORM,),
        in_specs=[pl.BlockSpec((TM_NORM, D_MODEL), row),
                  pl.BlockSpec((1, D_MODEL), const),
                  pl.BlockSpec((D_MODEL, LANES), const),
                  pl.BlockSpec((1, LANES), const)],
        out_specs=(pl.BlockSpec((TM_NORM, D_MODEL), row), pl.BlockSpec((TM_NORM, LANES), row)),
        compiler_params=pltpu.CompilerParams(
            dimension_semantics=("arbitrary",), vmem_limit_bytes=VMEM_LIMIT),
        name="norm1",
    )(x2, g1, w_f, b_f)


N_SECTIONS = 1 + 3 + N_GATE_SEC
N_CHUNKS = N_MAIN_SEC + N_GATE_SEC
GATE_SECTION0 = 4
WCONV_ROWS = 256


def _inproj_kernel(h_ref, wt_hbm, bg_ref, lng_ref, lnb_ref, wsp_ref, bsp_ref, qg_ref, kg_ref,
                   asgu_ref, q_ref, kt_ref, va_ref, gates_ref,
                   stage, wb, sem):
    s = pl.program_id(0)
    i = pl.program_id(1)
    tm = h_ref.shape[0]

    def chunk_copy(c, slot):
        r0 = pl.multiple_of(c * SEC + jnp.where(c >= N_MAIN_SEC, FOX_HEADS, 0), 8)
        return pltpu.make_async_copy(wt_hbm.at[pl.ds(r0, SEC), :], stage.at[slot], sem.at[slot])

    def convert(slot, k):
        for p in range(SEC // WCONV_ROWS):
            rs = slice(p * WCONV_ROWS, (p + 1) * WCONV_ROWS)
            wb[k, :, rs] = stage[slot, rs, :].T.astype(BF16)

    @pl.when((s == 0) & (i == 0))
    def _():
        chunk_copy(0, 0).start()
        chunk_copy(1, 1).start()
        chunk_copy(0, 0).wait()
        convert(0, 0)
        chunk_copy(2, 0).start(priority=PREFETCH_PRIORITY)
        chunk_copy(1, 1).wait()
        convert(1, 1)

    @pl.when((s > 0) & (i == 0))
    def _():
        c = s + 1
        slot = c % 2
        chunk_copy(c, slot).wait()

        @pl.when(c + 1 < N_CHUNKS)
        def _():
            chunk_copy(c + 1, 1 - slot).start(priority=PREFETCH_PRIORITY)

        convert(slot, 0)

    def section(k=0):
        return jnp.dot(h_ref[...], wb[k], preferred_element_type=F32)

    @pl.when(s == 0)
    def _():
        u = jax.nn.gelu(section(0))
        v = jax.nn.gelu(section(1))
        mu = jnp.mean(v, axis=-1, keepdims=True)
        vc = v - mu
        var = jnp.mean(vc * vc, axis=-1, keepdims=True)
        vn = ((vc * lax.rsqrt(var + EPS)) * lng_ref[...] + lnb_ref[...]).astype(BF16)
        row = lax.broadcasted_iota(I32, (CHUNK, CHUNK), 0)
        col = lax.broadcasted_iota(I32, (CHUNK, CHUNK), 1)
        causal = row >= col
        for g in range(SGU_GROUPS):
            wg = jnp.where(causal, wsp_ref[g], 0.0).astype(BF16)
            bcol = bsp_ref[:, g:g + 1]
            gs = slice(g * LANES, (g + 1) * LANES)
            n_ch = tm // CHUNK
            rhs = jnp.concatenate([vn[c * CHUNK:(c + 1) * CHUNK, gs] for c in range(n_ch)], axis=1)
            sg = jnp.dot(wg, rhs, preferred_element_type=F32) + bcol
            for c in range(n_ch):
                rs = slice(c * CHUNK, (c + 1) * CHUNK)
                asgu_ref[rs, gs] = (u[rs, gs] * sg[:, c * LANES:(c + 1) * LANES]).astype(BF16)

    def _head_norm(z, h, gain_ref):
        zh = z[:, h * HEAD_DIM:(h + 1) * HEAD_DIM]
        ms = jnp.mean(zh * zh, axis=-1, keepdims=True)
        return (zh * lax.rsqrt(ms + EPS)) * gain_ref[...]

    @pl.when(s == 1)
    def _():
        z = section()
        for h in range(FOX_HEADS):
            q_ref[:, h * HEAD_DIM:(h + 1) * HEAD_DIM] = (
                _head_norm(z, h, qg_ref) * Q_SCALE).astype(BF16)

    @pl.when(s == 2)
    def _():
        z = section()
        for h in range(FOX_HEADS):
            kt_ref[h * HEAD_DIM:(h + 1) * HEAD_DIM, :] = _head_norm(z, h, kg_ref).T.astype(BF16)

    @pl.when(s == 3)
    def _():
        va_ref[...] = section().astype(BF16)

    @pl.when(s >= GATE_SECTION0)
    def _():
        gates_ref[...] = (0.5 * jnp.tanh(0.5 * (section() + bg_ref[...])) + 0.5).astype(BF16)


def _inproj(h, wt, b_gate, ln_g, ln_b, w_sp, b_sp_t, q_g, k_g):
    t = h.shape[0]
    n_i = t // TM_IN

    def active(sec):
        return lambda s, i: jnp.where(s < sec, 0, jnp.where(s > sec, n_i - 1, i))

    def gate_blk(s, i):
        g = jnp.clip(s - GATE_SECTION0, 0, N_GATE_SEC - 1)
        return (jnp.where(s < GATE_SECTION0, 0, i), g)

    const2 = lambda s, i: (0, 0)
    out_shape = (
        jax.ShapeDtypeStruct((t, SGU_WIDTH), BF16),
        jax.ShapeDtypeStruct((t, FOX_WIDTH), BF16),
        jax.ShapeDtypeStruct((FOX_WIDTH, t), BF16),
        jax.ShapeDtypeStruct((t, FOX_WIDTH), BF16),
        jax.ShapeDtypeStruct((t, 2 * D_MODEL), BF16),
    )
    return pl.pallas_call(
        _inproj_kernel,
        out_shape=out_shape,
        grid=(N_SECTIONS, n_i),
        in_specs=[
            pl.BlockSpec((TM_IN, D_MODEL), lambda s, i: (i, 0)),
            pl.BlockSpec(memory_space=pl.ANY),
            pl.BlockSpec((1, SEC), lambda s, i: (0, jnp.clip(s - GATE_SECTION0, 0, N_GATE_SEC - 1))),
            pl.BlockSpec((1, SGU_WIDTH), const2),
            pl.BlockSpec((1, SGU_WIDTH), const2),
            pl.BlockSpec((SGU_GROUPS, CHUNK, CHUNK), lambda s, i: (0, 0, 0)),
            pl.BlockSpec((CHUNK, SGU_GROUPS), const2),
            pl.BlockSpec((1, HEAD_DIM), const2),
            pl.BlockSpec((1, HEAD_DIM), const2),
        ],
        out_specs=(
            pl.BlockSpec((TM_IN, SGU_WIDTH), lambda s, i: (active(0)(s, i), 0)),
            pl.BlockSpec((TM_IN, FOX_WIDTH), lambda s, i: (active(1)(s, i), 0)),
            pl.BlockSpec((FOX_WIDTH, TM_IN), lambda s, i: (0, active(2)(s, i))),
            pl.BlockSpec((TM_IN, FOX_WIDTH), lambda s, i: (active(3)(s, i), 0)),
            pl.BlockSpec((TM_IN, SEC), gate_blk),
        ),
        scratch_shapes=[pltpu.VMEM((2, SEC, D_MODEL), F32),
                        pltpu.VMEM((2, D_MODEL, SEC), BF16),
                        pltpu.SemaphoreType.DMA((2,))],
        compiler_params=pltpu.CompilerParams(
            dimension_semantics=("arbitrary", "arbitrary"), vmem_limit_bytes=VMEM_LIMIT),
        name="inproj",
    )(h, wt, b_gate, ln_g, ln_b, w_sp, b_sp_t, q_g, k_g)


def _split3(x):
    x0 = x.astype(BF16)
    r1 = x - x0.astype(F32)
    x1 = r1.astype(BF16)
    r2 = r1 - x1.astype(F32)
    return x0, x1, r2.astype(BF16)


def _cumsum_kernel(lf_ref, negc_ref, c_s):
    seq = lf_ref.shape[0]
    row = lax.broadcasted_iota(I32, (CHUNK, CHUNK), 0)
    col = lax.broadcasted_iota(I32, (CHUNK, CHUNK), 1)
    tri = jnp.where(row >= col, 1.0, 0.0).astype(BF16)
    carry = jnp.zeros((1, LANES), F32)
    for r in range(seq // CHUNK):
        rs = slice(r * CHUNK, (r + 1) * CHUNK)
        x0, x1, x2 = _split3(lf_ref[rs, :])
        cs = (jnp.dot(tri, x0, preferred_element_type=F32)
              + jnp.dot(tri, x1, preferred_element_type=F32)
              + jnp.dot(tri, x2, preferred_element_type=F32)) + carry
        carry = cs[CHUNK - 1:CHUNK, :]
        c_s[rs, :] = cs
    ct = c_s[...].T
    negc_ref[0] = -ct[0:FOX_HEADS, :]


def _forget_cumsum(lf, batch, seq):
    return pl.pallas_call(
        _cumsum_kernel,
        out_shape=jax.ShapeDtypeStruct((batch, FOX_HEADS, seq), F32),
        grid=(batch,),
        in_specs=[pl.BlockSpec((seq, LANES), lambda b: (b, 0))],
        out_specs=pl.BlockSpec((1, FOX_HEADS, seq), lambda b: (b, 0, 0)),
        scratch_shapes=[pltpu.VMEM((seq, LANES), F32)],
        compiler_params=pltpu.CompilerParams(
            dimension_semantics=("arbitrary",), vmem_limit_bytes=VMEM_LIMIT),
        name="forget_cumsum",
    )(lf)


def _attn_kernel(q_ref, kt_ref, v_ref, negc_ref, o_ref, kx, vx):
    seq = q_ref.shape[0]
    sub = lax.broadcasted_iota(I32, (HEAD_DIM, seq), 0)
    for hh in range(HEADS_PER_STEP):
        hs = slice(hh * HEAD_DIM, (hh + 1) * HEAD_DIM)
        c0, c1, c2 = (c.astype(F32) for c in _split3(negc_ref[0, hh] * LOG2E))
        kx[hh, 0:HEAD_DIM, :] = kt_ref[hs, :]
        kx[hh, HEAD_DIM:, :] = jnp.where(
            sub == 0, c0, jnp.where(sub == 1, c1, jnp.where(sub == 2, c2, 0.0))).astype(BF16)
        vx[hh, :, 0:HEAD_DIM] = v_ref[:, hs]
        vx[hh, :, HEAD_DIM:] = jnp.ones((seq, HEAD_DIM), BF16)
    lane = lax.broadcasted_iota(I32, (TQ, HEAD_DIM), 1)
    bias_cols = jnp.where(lane < N_SPLIT, 1.0, 0.0).astype(BF16)
    row = lax.broadcasted_iota(I32, (TQ, TQ), 0)
    col = lax.broadcasted_iota(I32, (TQ, TQ), 1)
    causal = row >= col
    for qi in range(seq // TQ):
        for hh in range(HEADS_PER_STEP):
            hs = slice(hh * HEAD_DIM, (hh + 1) * HEAD_DIM)
            k0 = qi * TQ
            q = jnp.concatenate([q_ref[k0:k0 + TQ, hs], bias_cols], axis=1)
            s_d = jnp.dot(q, kx[hh, :, k0:k0 + TQ], preferred_element_type=F32)
            s_d = jnp.where(causal, s_d, -jnp.inf)
            m = jnp.max(s_d, axis=-1, keepdims=True)
            if qi > 0:
                s_o = jnp.dot(q, kx[hh, :, 0:k0], preferred_element_type=F32)
                m = jnp.maximum(m, jnp.max(s_o, axis=-1, keepdims=True))
            acc = jnp.dot(jnp.exp2(s_d - m).astype(BF16), vx[hh, k0:k0 + TQ, :],
                          preferred_element_type=F32)
            if qi > 0:
                acc = acc + jnp.dot(jnp.exp2(s_o - m).astype(BF16), vx[hh, 0:k0, :],
                                    preferred_element_type=F32)
            inv_l = 1.0 / acc[:, HEAD_DIM:HEAD_DIM + 1]
            o_ref[k0:k0 + TQ, hs] = (acc[:, 0:HEAD_DIM] * inv_l).astype(BF16)


def _attention(q, kt, va, negc, batch, seq):
    t = q.shape[0]
    width = HEADS_PER_STEP * HEAD_DIM
    blk = pl.BlockSpec((seq, width), lambda b, g: (b, g))
    return pl.pallas_call(
        _attn_kernel,
        out_shape=jax.ShapeDtypeStruct((t, FOX_WIDTH), BF16),
        grid=(batch, FOX_HEADS // HEADS_PER_STEP),
        in_specs=[blk,
                  pl.BlockSpec((width, seq), lambda b, g: (g, b)),
                  blk,
                  pl.BlockSpec((1, HEADS_PER_STEP, 1, seq), lambda b, g: (b, g, 0, 0))],
        out_specs=blk,
        scratch_shapes=[pltpu.VMEM((HEADS_PER_STEP, 2 * HEAD_DIM, seq), BF16),
                        pltpu.VMEM((HEADS_PER_STEP, seq, 2 * HEAD_DIM), BF16)],
        compiler_params=pltpu.CompilerParams(
            dimension_semantics=("arbitrary", "arbitrary"), vmem_limit_bytes=VMEM_LIMIT),
        name="fox_attention",
    )(q, kt, va, negc)


MIX_WROWS = 512


def _mix_kernel(as_ref, af_ref, g_ref, x_ref, wps_hbm, wpf_hbm, wo_hbm, g2_ref, wr_ref, br_ref,
                x1_ref, lg_ref, wps_ref, wpf_ref, wo_ref, stage, sem):
    @pl.when(pl.program_id(0) == 0)
    def _():
        pieces = [(src, dst, r0) for src, dst in ((wps_hbm, wps_ref), (wpf_hbm, wpf_ref),
                                                  (wo_hbm, wo_ref))
                  for r0 in range(0, dst.shape[0], MIX_WROWS)]

        def piece_copy(k):
            src, _, r0 = pieces[k]
            return pltpu.make_async_copy(src.at[pl.ds(r0, MIX_WROWS), :], stage.at[k % 2],
                                         sem.at[k % 2])

        piece_copy(0).start()
        for k, (_, dst, r0) in enumerate(pieces):
            if k + 1 < len(pieces):
                piece_copy(k + 1).start()
            piece_copy(k).wait()
            dst[r0:r0 + MIX_WROWS, :] = stage[k % 2].astype(BF16)

    ys = jnp.dot(as_ref[...], wps_ref[...], preferred_element_type=F32)
    yf = jnp.dot(af_ref[...], wpf_ref[...], preferred_element_type=F32)
    m = (g_ref[:, :D_MODEL].astype(F32) * ys + g_ref[:, D_MODEL:].astype(F32) * yf).astype(BF16)
    x1 = x_ref[...] + jnp.dot(m, wo_ref[...], preferred_element_type=F32)
    x1_ref[...] = x1
    ms = jnp.mean(x1 * x1, axis=-1, keepdims=True)
    h2 = ((x1 * lax.rsqrt(ms + EPS)) * g2_ref[...]).astype(BF16)
    lg_ref[...] = jnp.dot(h2, wr_ref[...], preferred_element_type=F32) + br_ref[...]


def _mix(a_sgu, a_fox, gates, x2, wps, wpf, wo, g2, w_r, b_r):
    t = x2.shape[0]
    row = lambda i: (i, 0)
    const = lambda i: (0, 0)
    resident = functools.partial(pl.BlockSpec, index_map=const, pipeline_mode=pl.Buffered(1))
    return pl.pallas_call(
        _mix_kernel,
        out_shape=(jax.ShapeDtypeStruct((t, D_MODEL), F32),
                   jax.ShapeDtypeStruct((t, LANES), F32)),
        grid=(t // TM_MIX,),
        in_specs=[
            pl.BlockSpec((TM_MIX, SGU_WIDTH), row),
            pl.BlockSpec((TM_MIX, FOX_WIDTH), row),
            pl.BlockSpec((TM_MIX, 2 * D_MODEL), row),
            pl.BlockSpec((TM_MIX, D_MODEL), row),
            pl.BlockSpec(memory_space=pl.ANY),
            pl.BlockSpec(memory_space=pl.ANY),
            pl.BlockSpec(memory_space=pl.ANY),
            pl.BlockSpec((1, D_MODEL), const),
            resident((D_MODEL, LANES)),
            pl.BlockSpec((1, LANES), const),
        ],
        out_specs=(pl.BlockSpec((TM_MIX, D_MODEL), row), pl.BlockSpec((TM_MIX, LANES), row)),
        scratch_shapes=[pltpu.VMEM((SGU_WIDTH, D_MODEL), BF16),
                        pltpu.VMEM((FOX_WIDTH, D_MODEL), BF16),
                        pltpu.VMEM((D_MODEL, D_MODEL), BF16),
                        pltpu.VMEM((2, MIX_WROWS, D_MODEL), F32),
                        pltpu.SemaphoreType.DMA((2,))],
        compiler_params=pltpu.CompilerParams(
            dimension_semantics=("arbitrary",), vmem_limit_bytes=VMEM_LIMIT),
        name="mix",
    )(a_sgu, a_fox, gates, x2, wps, wpf, wo, g2, w_r, b_r)


def _route_kernel(lg_ref, dt_ref, mf_ref, bm_ref, em_ref, bt_ref, mi_ref, inv_ref):
    t = lg_ref.shape[0]
    n_chunks = t // RCH
    lane_i = lax.broadcasted_iota(I32, (RCH, LANES), 1)
    lane = lane_i.astype(F32)
    lane_grp = ((lane_i - N_GROUPS) >> EXPERT_GROUP_SHIFT).astype(F32)
    is_grp = lane_i < N_GROUPS
    is_exp = (lane_i >= N_GROUPS) & (lane_i < N_GROUPS + N_EXPERTS)
    r_i = lax.broadcasted_iota(I32, (RCH, RCH), 0)
    c_i = lax.broadcasted_iota(I32, (RCH, RCH), 1)
    strict_lower = jnp.where(r_i > c_i, 1.0, 0.0).astype(BF16)
    neg_inf = -jnp.inf

    def first_max(vals):
        vmax = jnp.max(vals, axis=-1, keepdims=True)
        idx = jnp.min(jnp.where(vals == vmax, lane, float(LANES)), axis=-1, keepdims=True)
        return vmax, idx

    def pick(table, idx):
        return jnp.sum(jnp.where(lane == idx, table, 0.0), axis=-1, keepdims=True)

    def pack(cols):
        out = jnp.zeros((RCH, LANES), F32)
        for n, c in enumerate(cols):
            out = jnp.where(lane_i == n, c, out)
        return out

    def pass1(ci, counts):
        r0 = pl.multiple_of(ci * RCH, RCH)
        lg = lg_ref[pl.ds(r0, RCH), :]
        gmax, grp = first_max(jnp.where(is_grp, lg, neg_inf))
        p_grp = 1.0 / jnp.sum(jnp.where(is_grp, jnp.exp(lg - gmax), 0.0), axis=-1, keepdims=True)
        el = jnp.where(is_exp & (lane_grp == grp), lg, neg_inf)
        v1, i1 = first_max(el)
        v2, i2 = first_max(jnp.where(lane == i1, neg_inf, el))
        e21 = jnp.exp(v2 - v1)
        w1 = p_grp / (1.0 + e21)
        w2 = p_grp * e21 / (1.0 + e21)
        e1 = i1 - float(N_GROUPS)
        e2 = i2 - float(N_GROUPS)
        hot = jnp.where((lane == e1) | (lane == e2), 1.0, 0.0)
        before = jnp.dot(strict_lower, hot.astype(BF16), preferred_element_type=F32) + counts
        mi_ref[pl.ds(r0, RCH), :] = pack([e1, e2, pick(before, e1), pick(before, e2)]).astype(I32)
        mf_ref[pl.ds(r0, RCH), :] = pack([w1, w2])
        return counts + jnp.sum(hot, axis=0, keepdims=True)

    counts = lax.fori_loop(0, n_chunks, pass1, jnp.zeros((1, LANES), F32))

    nblk = jnp.floor((counts + float(EXPERT_BLOCK - 1)) * (1.0 / EXPERT_BLOCK))
    u_r = lax.broadcasted_iota(I32, (LANES, LANES), 0)
    u_c = lax.broadcasted_iota(I32, (LANES, LANES), 1)
    strict_upper = jnp.where(u_r < u_c, 1.0, 0.0).astype(BF16)
    bstart = jnp.dot(jnp.broadcast_to(nblk, (SUBLANES, LANES)).astype(BF16), strict_upper,
                     preferred_element_type=F32)[0:1, :]
    bend = bstart + nblk

    def pass2(ci, _):
        r0 = pl.multiple_of(ci * RCH, RCH)
        mi = mi_ref[pl.ds(r0, RCH), :].astype(F32)
        cols = [jnp.sum(jnp.where(lane_i == n, mi, 0.0), axis=-1, keepdims=True) for n in range(4)]
        d1 = pick(bstart, cols[0]) * float(EXPERT_BLOCK) + cols[2]
        d2 = pick(bstart, cols[1]) * float(EXPERT_BLOCK) + cols[3]
        d_t = pack([d1, d2]).T
        dt_ref[ci] = d_t[0:SUBLANES, :].astype(I32)
        tok = (lax.broadcasted_iota(I32, (RCH, 1), 0) + r0).astype(F32)
        tok_hi = jnp.floor(tok * (1.0 / LANES))
        tok_lo = tok - tok_hi * float(LANES)
        for n, d in enumerate((d1, d2)):
            d_hi = jnp.floor(d * (1.0 / LANES))
            hit = lane == (d - d_hi * float(LANES))
            blk_row = jnp.floor(d_t[n:n + 1, :] * (1.0 / LANES))
            sel = jnp.where(inv_blk == blk_row, 1.0, 0.0).astype(BF16)
            inv_ref[0] += jnp.dot(sel, jnp.where(hit, tok_hi, 0.0).astype(BF16),
                                  preferred_element_type=F32)
            inv_ref[1] += jnp.dot(sel, jnp.where(hit, tok_lo, 0.0).astype(BF16),
                                  preferred_element_type=F32)
        return 0

    nb_rows = bm_ref.shape[0]
    inv_blk = lax.broadcasted_iota(I32, (nb_rows, RCH), 0).astype(F32)
    inv_ref[...] = jnp.zeros_like(inv_ref)
    lax.fori_loop(0, n_chunks, pass2, 0)
    bt_ref[...] = (inv_ref[0] * float(LANES) + inv_ref[1]).astype(I32)

    b_col = lax.broadcasted_iota(I32, (nb_rows, LANES), 0).astype(F32)
    b_lane = lax.broadcasted_iota(I32, (nb_rows, LANES), 1)
    done = jnp.where((bend <= b_col) & (b_lane < N_EXPERTS), 1.0, 0.0)
    blk_e = jnp.minimum(jnp.sum(done, axis=-1, keepdims=True), float(N_EXPERTS - 1))
    n_used = jnp.sum(jnp.where(b_lane[0:1, :] == N_EXPERTS - 1, bend, 0.0), axis=-1, keepdims=True)
    bm_ref[...] = jnp.where(b_lane == 0, blk_e, jnp.where(b_lane == 1, n_used, 0.0)).astype(I32)

    em_ref[...] = jnp.broadcast_to(nblk, (SUBLANES, LANES)).astype(I32)


def _route(logits, nb_rows):
    t = logits.shape[0]
    full = lambda shape: pl.BlockSpec(shape, lambda: (0,) * len(shape))
    return pl.pallas_call(
        _route_kernel,
        out_shape=(jax.ShapeDtypeStruct((t // RCH, SUBLANES, RCH), I32),
                   jax.ShapeDtypeStruct((t, LANES), F32),
                   jax.ShapeDtypeStruct((nb_rows, LANES), I32),
                   jax.ShapeDtypeStruct((SUBLANES, LANES), I32),
                   jax.ShapeDtypeStruct((nb_rows, LANES), I32)),
        in_specs=[full((t, LANES))],
        out_specs=(full((t // RCH, SUBLANES, RCH)), full((t, LANES)), full((nb_rows, LANES)),
                   full((SUBLANES, LANES)), full((nb_rows, LANES))),
        scratch_shapes=[pltpu.VMEM((t, LANES), I32), pltpu.VMEM((2, nb_rows, LANES), F32)],
        compiler_params=pltpu.CompilerParams(vmem_limit_bytes=VMEM_LIMIT),
        name="route",
    )(logits)


GATHER_GROUPS = 4
GATHER_AHEAD = 6
WEIGHT_SLOTS = 3


def _experts_kernel(blk_e_ref, nused_ref, nblk_ref, tok_ref, x_hbm, g2_ref, wg_hbm, wu_hbm, wd_hbm,
                    y_ref, xg, wg_f, wu_f, wd_f, wg_s, wu_s, wd_s, slot_ref, sem, gsem):
    b = pl.program_id(0)
    n_used = nused_ref[0]
    used = b < n_used
    e = blk_e_ref[b]
    new_expert = (b == 0) | (e != blk_e_ref[jnp.maximum(b - 1, 0)])
    n_slots = GATHER_AHEAD + 1
    xslot = b % n_slots

    def gather_rows(block, slot, lo, hi):
        row0 = jnp.minimum(block, n_used - 1) * EXPERT_BLOCK
        for r in range(lo, hi):
            pltpu.make_async_copy(x_hbm.at[pl.ds(tok_ref[row0 + r], 1), :],
                                  xg.at[slot, pl.ds(r, 1), :], gsem.at[slot]).start()

    def gather_wait(slot):
        pltpu.make_async_copy(x_hbm.at[pl.ds(0, EXPERT_BLOCK), :], xg.at[slot], gsem.at[slot]).wait()

    @pl.when(b == 0)
    def _():
        for k in range(GATHER_AHEAD):
            gather_rows(k, k, 0, EXPERT_BLOCK)

    def weight_copies(expert, slot):
        return (pltpu.make_async_copy(wg_hbm.at[expert], wg_f.at[slot], sem.at[slot]),
                pltpu.make_async_copy(wu_hbm.at[expert], wu_f.at[slot], sem.at[slot]),
                pltpu.make_async_copy(wd_hbm.at[expert], wd_f.at[slot], sem.at[slot]))

    def block_expert(blk):
        return blk_e_ref[jnp.minimum(blk, n_used - 1)]

    b_next = b + nblk_ref[e]
    b_next2 = b_next + nblk_ref[block_expert(b_next)]

    @pl.when(b == 0)
    def _():
        slot_ref[0] = 0
        for cp in weight_copies(e, 0):
            cp.start()

        @pl.when(b_next < n_used)
        def _():
            for cp in weight_copies(block_expert(b_next), 1):
                cp.start(priority=PREFETCH_PRIORITY)

    @pl.when(used & new_expert)
    def _():
        slot = slot_ref[0]

        @pl.when(b_next2 < n_used)
        def _():
            for cp in weight_copies(block_expert(b_next2), (slot + 2) % WEIGHT_SLOTS):
                cp.start(priority=PREFETCH_PRIORITY)

        for cp in weight_copies(e, slot):
            cp.wait()
        wg_s[...] = wg_f[slot].astype(BF16)
        wu_s[...] = wu_f[slot].astype(BF16)
        wd_s[...] = wd_f[slot].astype(BF16)
        slot_ref[0] = (slot + 1) % WEIGHT_SLOTS

    @pl.when(used)
    def _():
        gather_wait(xslot)
        ahead = b + GATHER_AHEAD
        aslot = ahead % n_slots
        per_group = EXPERT_BLOCK // GATHER_GROUPS
        x = xg[xslot]
        ms = jnp.mean(x * x, axis=-1, keepdims=True)
        h = ((x * lax.rsqrt(ms + EPS)) * g2_ref[...]).astype(BF16)
        gather_rows(ahead, aslot, 0, per_group)
        a = jnp.dot(h, wg_s[...], preferred_element_type=F32)
        gather_rows(ahead, aslot, per_group, 2 * per_group)
        u = jnp.dot(h, wu_s[...], preferred_element_type=F32)
        gather_rows(ahead, aslot, 2 * per_group, 3 * per_group)
        mid = ((a * _sigmoid(a)) * u).astype(BF16)
        gather_rows(ahead, aslot, 3 * per_group, EXPERT_BLOCK)
        y_ref[...] = jnp.dot(mid, wd_s[...], preferred_element_type=F32)

    @pl.when(b == n_used - 1)
    def _():
        for k in range(1, GATHER_AHEAD + 1):
            gather_wait((b + k) % n_slots)

    @pl.when(jnp.logical_not(used))
    def _():
        y_ref[...] = jnp.zeros_like(y_ref)


def _experts(blk_e, n_used, nblk, row_tok, x1, g2, w_g, w_u, w_d):
    n_rows = row_tok.shape[0]
    nb = n_rows // EXPERT_BLOCK
    hbm = pl.BlockSpec(memory_space=pl.ANY)
    grid_spec = pltpu.PrefetchScalarGridSpec(
        num_scalar_prefetch=4,
        grid=(nb,),
        in_specs=[
            hbm,
            pl.BlockSpec((1, D_MODEL), lambda b, *_: (0, 0)),
            hbm, hbm, hbm,
        ],
        out_specs=pl.BlockSpec((EXPERT_BLOCK, D_MODEL), lambda b, *_: (b, 0)),
        scratch_shapes=[pltpu.VMEM((GATHER_AHEAD + 1, EXPERT_BLOCK, D_MODEL), F32),
                        pltpu.VMEM((WEIGHT_SLOTS, D_MODEL, D_EXPERT), F32),
                        pltpu.VMEM((WEIGHT_SLOTS, D_MODEL, D_EXPERT), F32),
                        pltpu.VMEM((WEIGHT_SLOTS, D_EXPERT, D_MODEL), F32),
                        pltpu.VMEM((D_MODEL, D_EXPERT), BF16),
                        pltpu.VMEM((D_MODEL, D_EXPERT), BF16),
                        pltpu.VMEM((D_EXPERT, D_MODEL), BF16),
                        pltpu.SMEM((1,), I32),
                        pltpu.SemaphoreType.DMA((WEIGHT_SLOTS,)),
                        pltpu.SemaphoreType.DMA((GATHER_AHEAD + 1,))],
    )
    return pl.pallas_call(
        _experts_kernel,
        out_shape=jax.ShapeDtypeStruct((n_rows, D_MODEL), F32),
        grid_spec=grid_spec,
        compiler_params=pltpu.CompilerParams(
            dimension_semantics=("arbitrary",), vmem_limit_bytes=VMEM_LIMIT),
        name="experts",
    )(blk_e, n_used, nblk, row_tok, x1, g2, w_g, w_u, w_d)


def _combine_kernel(d1_ref, d2_ref, x1_ref, w_ref, y_ref, o_ref, gbuf, sem):
    i = pl.program_id(0)
    n = pl.num_programs(0)
    tm = x1_ref.shape[0]

    def request_row(base, slot, r):
        pltpu.make_async_copy(y_ref.at[pl.ds(d1_ref[base + r], 1), :],
                              gbuf.at[slot, pl.ds(r, 1), :], sem.at[slot]).start()
        pltpu.make_async_copy(y_ref.at[pl.ds(d2_ref[base + r], 1), :],
                              gbuf.at[slot, pl.ds(tm + r, 1), :], sem.at[slot]).start()

    def issue(tile, slot, lo, hi):
        base = jnp.minimum(tile, n - 1) * tm
        for r in range(lo, hi):
            request_row(base, slot, r)

    def tile_wait(slot):
        pltpu.make_async_copy(y_ref.at[pl.ds(0, 2 * tm), :], gbuf.at[slot], sem.at[slot]).wait()

    n_slots = COMBINE_AHEAD + 1
    slot = i % n_slots

    @pl.when(i == 0)
    def _():
        for k in range(COMBINE_AHEAD):
            base = jnp.minimum(k, n - 1) * tm

            def first_tiles(r, _, base=base, k=k):
                request_row(base, k, r)
                return 0

            lax.fori_loop(0, tm, first_tiles, 0, unroll=SUBLANES)

    tile_wait(slot)
    ahead = i + COMBINE_AHEAD
    aslot = ahead % n_slots
    for c in range(tm // COMBINE_ROWS):
        rs = slice(c * COMBINE_ROWS, (c + 1) * COMBINE_ROWS)
        rs2 = slice(tm + c * COMBINE_ROWS, tm + (c + 1) * COMBINE_ROWS)
        w = w_ref[rs, :]
        o_ref[rs, :] = x1_ref[rs, :] + (w[:, 0:1] * gbuf[slot, rs, :] + w[:, 1:2] * gbuf[slot, rs2, :])
        issue(ahead, aslot, c * COMBINE_ROWS, (c + 1) * COMBINE_ROWS)

    @pl.when(i == n - 1)
    def _():
        for k in range(1, COMBINE_AHEAD + 1):
            tile_wait((i + k) % n_slots)


def _combine(dest1, dest2, x1, w, yb):
    t = x1.shape[0]
    grid_spec = pltpu.PrefetchScalarGridSpec(
        num_scalar_prefetch=2,
        grid=(t // TM_CMB,),
        in_specs=[
            pl.BlockSpec((TM_CMB, D_MODEL), lambda i, a, b: (i, 0)),
            pl.BlockSpec((TM_CMB, LANES), lambda i, a, b: (i, 0)),
            pl.BlockSpec(memory_space=pl.ANY),
        ],
        out_specs=pl.BlockSpec((TM_CMB, D_MODEL), lambda i, a, b: (i, 0)),
        scratch_shapes=[pltpu.VMEM((COMBINE_AHEAD + 1, 2 * TM_CMB, D_MODEL), F32),
                        pltpu.SemaphoreType.DMA((COMBINE_AHEAD + 1,))],
    )
    return pl.pallas_call(
        _combine_kernel,
        out_shape=jax.ShapeDtypeStruct((t, D_MODEL), F32),
        grid_spec=grid_spec,
        compiler_params=pltpu.CompilerParams(
            dimension_semantics=("arbitrary",), vmem_limit_bytes=VMEM_LIMIT),
        name="combine",
    )(dest1, dest2, x1, w, yb)


def kernel(x, norm1_g, w_in, b_gate, b_forget, sgu_ln_g, sgu_ln_b, w_spatial, b_spatial, q_norm_g, k_norm_g, w_proj_sgu, w_proj_fox, w_out, norm2_g, w_router_group, b_router_group, w_router_expert, b_router_expert, w_expert_gate, w_expert_up, w_expert_down):
    batch, seq, d = x.shape
    t = batch * seq
    l = 0
    x2 = x.reshape(t, d)

    wt = jnp.swapaxes(w_in[l], 0, 1)
    off_f = N_MAIN_SEC * SEC
    w_f = jnp.pad(wt[off_f:off_f + FOX_HEADS].T, ((0, 0), (0, LANES - FOX_HEADS))).astype(BF16)
    b_f = jnp.pad(b_forget[l], (0, LANES - FOX_HEADS)).reshape(1, LANES)
    n_r = N_GROUPS + N_EXPERTS
    w_r = jnp.pad(jnp.concatenate([w_router_group[l], w_router_expert[l]], axis=1),
                  ((0, 0), (0, LANES - n_r))).astype(BF16)
    b_r = jnp.pad(jnp.concatenate([b_router_group[l], b_router_expert[l]]),
                  (0, LANES - n_r)).reshape(1, LANES)

    h, lf = _norm1(x2, norm1_g[l].reshape(1, d), w_f, b_f)
    a_sgu, q, kt, va, gates = _inproj(
        h, wt, b_gate[l].reshape(1, 2 * d),
        sgu_ln_g[l].reshape(1, SGU_WIDTH), sgu_ln_b[l].reshape(1, SGU_WIDTH),
        w_spatial[l], b_spatial[l].T, q_norm_g[l].reshape(1, HEAD_DIM),
        k_norm_g[l].reshape(1, HEAD_DIM))
    negc = _forget_cumsum(lf, batch, seq).reshape(batch, FOX_HEADS, 1, seq)
    a_fox = _attention(q, kt, va, negc, batch, seq)
    x1, logits = _mix(a_sgu, a_fox, gates, x2, w_proj_sgu[l], w_proj_fox[l], w_out[l],
                      norm2_g[l].reshape(1, d), w_r, b_r)

    n_assign = 2 * t
    n_rows = n_assign + N_EXPERTS * EXPERT_BLOCK
    nb = n_rows // EXPERT_BLOCK
    dest_t, meta_f, bmeta, nblk, row_tok = _route(logits, n_rows // LANES)
    dest1, dest2 = dest_t[:, 0, :].reshape(t), dest_t[:, 1, :].reshape(t)
    n_used = bmeta[0:1, 1]
    yb = _experts(bmeta[:nb, 0], n_used, nblk[0, :N_EXPERTS], row_tok.reshape(n_rows), x1,
                  norm2_g[l].reshape(1, d), w_expert_gate[l], w_expert_up[l], w_expert_down[l])
    out = _combine(dest1, dest2, x1, meta_f, yb)
    return out.reshape(batch, seq, d)
```

```python
import functools

import jax
import jax.numpy as jnp
from jax import lax
from jax.experimental import pallas as pl
from jax.experimental.pallas import tpu as pltpu

F32 = jnp.float32
BF16 = jnp.bfloat16
I32 = jnp.int32

D_MODEL = 2048
CHUNK = 128
SGU_GROUPS = 8
SGU_WIDTH = 1024
FOX_HEADS = 8
HEAD_DIM = 128
FOX_WIDTH = 1024
N_GROUPS = 4
EXPERTS_PER_GROUP = 8
N_EXPERTS = 32
D_EXPERT = 512
EXPERT_BLOCK = 128
EPS = 1e-6

LANES = 128
SUBLANES = 8
PREFETCH_PRIORITY = 1
VMEM_LIMIT = 56 * 1024 * 1024

SEC = 1024
N_MAIN_SEC = 5
N_GATE_SEC = 4
TM_NORM = 1024
TM_IN = 512
TQ = 256
HEADS_PER_STEP = 4
TM_MIX = 256
RCH = 512
TM_CMB = 512
COMBINE_ROWS = 64
N_SPLIT = 3
EXPERT_GROUP_SHIFT = EXPERTS_PER_GROUP.bit_length() - 1
LOG2E = 1.4426950408889634
Q_SCALE = HEAD_DIM ** -0.5 * LOG2E


def _sigmoid(x):
    return 1.0 / (1.0 + jnp.exp(-x))


def _log_sigmoid(x):
    return jnp.minimum(x, 0.0) - jnp.log1p(jnp.exp(-jnp.abs(x)))


def _norm1_kernel(x_ref, g1_ref, wf_ref, bf_ref, h_ref, lf_ref):
    x = x_ref[...]
    ms = jnp.mean(x * x, axis=-1, keepdims=True)
    hb = ((x * lax.rsqrt(ms + EPS)) * g1_ref[...]).astype(BF16)
    h_ref[...] = hb
    f = jnp.dot(hb, wf_ref[...], preferred_element_type=F32) + bf_ref[...]
    lf_ref[...] = _log_sigmoid(f)


def _norm1(x2, g1, w_f, b_f):
    t = x2.shape[0]
    row = lambda i: (i, 0)
    const = lambda i: (0, 0)
    return pl.pallas_call(
        _norm1_kernel,
        out_shape=(jax.ShapeDtypeStruct((t, D_MODEL), BF16),
                   jax.ShapeDtypeStruct((t, LANES), F32)),
        grid=(t // TM_NORM,),
        in_specs=[pl.BlockSpec((TM_NORM, D_MODEL), row),
                  pl.BlockSpec((1, D_MODEL), const),
                  pl.BlockSpec((D_MODEL, LANES), const),
                  pl.BlockSpec((1, LANES), const)],
        out_specs=(pl.BlockSpec((TM_NORM, D_MODEL), row), pl.BlockSpec((TM_NORM, LANES), row)),
        compiler_params=pltpu.CompilerParams(
            dimension_semantics=("arbitrary",), vmem_limit_bytes=VMEM_LIMIT),
        name="norm1",
    )(x2, g1, w_f, b_f)


N_SECTIONS = 1 + 3 + N_GATE_SEC
N_CHUNKS = N_MAIN_SEC + N_GATE_SEC
GATE_SECTION0 = 4
WCONV_ROWS = 256


def _inproj_kernel(h_ref, wt_hbm, bg_ref, lng_ref, lnb_ref, wsp_ref, bsp_ref, qg_ref, kg_ref,
                   asgu_ref, q_ref, kt_ref, va_ref, gates_ref,
                   stage, wb, sem):
    s = pl.program_id(0)
    i = pl.program_id(1)
    tm = h_ref.shape[0]

    def chunk_copy(c, slot):
        r0 = pl.multiple_of(c * SEC + jnp.where(c >= N_MAIN_SEC, FOX_HEADS, 0), 8)
        return pltpu.make_async_copy(wt_hbm.at[pl.ds(r0, SEC), :], stage.at[slot], sem.at[slot])

    def convert(slot, k):
        for p in range(SEC // WCONV_ROWS):
            rs = slice(p * WCONV_ROWS, (p + 1) * WCONV_ROWS)
            wb[k, :, rs] = stage[slot, rs, :].T.astype(BF16)

    @pl.when((s == 0) & (i == 0))
    def _():
        chunk_copy(0, 0).start()
        chunk_copy(1, 1).start()
        chunk_copy(0, 0).wait()
        convert(0, 0)
        chunk_copy(2, 0).start(priority=PREFETCH_PRIORITY)
        chunk_copy(1, 1).wait()
        convert(1, 1)

    @pl.when((s > 0) & (i == 0))
    def _():
        c = s + 1
        slot = c % 2
        chunk_copy(c, slot).wait()

        @pl.when(c + 1 < N_CHUNKS)
        def _():
            chunk_copy(c + 1, 1 - slot).start(priority=PREFETCH_PRIORITY)

        convert(slot, 0)

    def section(k=0):
        return jnp.dot(h_ref[...], wb[k], preferred_element_type=F32)

    @pl.when(s == 0)
    def _():
        u = jax.nn.gelu(section(0))
        v = jax.nn.gelu(section(1))
        mu = jnp.mean(v, axis=-1, keepdims=True)
        vc = v - mu
        var = jnp.mean(vc * vc, axis=-1, keepdims=True)
        vn = ((vc * lax.rsqrt(var + EPS)) * lng_ref[...] + lnb_ref[...]).astype(BF16)
        row = lax.broadcasted_iota(I32, (CHUNK, CHUNK), 0)
        col = lax.broadcasted_iota(I32, (CHUNK, CHUNK), 1)
        causal = row >= col
        for g in range(SGU_GROUPS):
            wg = jnp.where(causal, wsp_ref[g], 0.0).astype(BF16)
            bcol = bsp_ref[:, g:g + 1]
            gs = slice(g * LANES, (g + 1) * LANES)
            n_ch = tm // CHUNK
            rhs = jnp.concatenate([vn[c * CHUNK:(c + 1) * CHUNK, gs] for c in range(n_ch)], axis=1)
            sg = jnp.dot(wg, rhs, preferred_element_type=F32) + bcol
            for c in range(n_ch):
                rs = slice(c * CHUNK, (c + 1) * CHUNK)
                asgu_ref[rs, gs] = (u[rs, gs] * sg[:, c * LANES:(c + 1) * LANES]).astype(BF16)

    def _head_norm(z, h, gain_ref):
        zh = z[:, h * HEAD_DIM:(h + 1) * HEAD_DIM]
        ms = jnp.mean(zh * zh, axis=-1, keepdims=True)
        return (zh * lax.rsqrt(ms + EPS)) * gain_ref[...]

    @pl.when(s == 1)
    def _():
        z = section()
        for h in range(FOX_HEADS):
            q_ref[:, h * HEAD_DIM:(h + 1) * HEAD_DIM] = (
                _head_norm(z, h, qg_ref) * Q_SCALE).astype(BF16)

    @pl.when(s == 2)
    def _():
        z = section()
        for h in range(FOX_HEADS):
            kt_ref[h * HEAD_DIM:(h + 1) * HEAD_DIM, :] = _head_norm(z, h, kg_ref).T.astype(BF16)

    @pl.when(s == 3)
    def _():
        va_ref[...] = section().astype(BF16)

    @pl.when(s >= GATE_SECTION0)
    def _():
        gates_ref[...] = (0.5 * jnp.tanh(0.5 * (section() + bg_ref[...])) + 0.5).astype(BF16)


def _inproj(h, wt, b_gate, ln_g, ln_b, w_sp, b_sp_t, q_g, k_g):
    t = h.shape[0]
    n_i = t // TM_IN

    def active(sec):
        return lambda s, i: jnp.where(s < sec, 0, jnp.where(s > sec, n_i - 1, i))

    def gate_blk(s, i):
        g = jnp.clip(s - GATE_SECTION0, 0, N_GATE_SEC - 1)
        return (jnp.where(s < GATE_SECTION0, 0, i), g)

    const2 = lambda s, i: (0, 0)
    out_shape = (
        jax.ShapeDtypeStruct((t, SGU_WIDTH), BF16),
        jax.ShapeDtypeStruct((t, FOX_WIDTH), BF16),
        jax.ShapeDtypeStruct((FOX_WIDTH, t), BF16),
        jax.ShapeDtypeStruct((t, FOX_WIDTH), BF16),
        jax.ShapeDtypeStruct((t, 2 * D_MODEL), BF16),
    )
    return pl.pallas_call(
        _inproj_kernel,
        out_shape=out_shape,
        grid=(N_SECTIONS, n_i),
        in_specs=[
            pl.BlockSpec((TM_IN, D_MODEL), lambda s, i: (i, 0)),
            pl.BlockSpec(memory_space=pl.ANY),
            pl.BlockSpec((1, SEC), lambda s, i: (0, jnp.clip(s - GATE_SECTION0, 0, N_GATE_SEC - 1))),
            pl.BlockSpec((1, SGU_WIDTH), const2),
            pl.BlockSpec((1, SGU_WIDTH), const2),
            pl.BlockSpec((SGU_GROUPS, CHUNK, CHUNK), lambda s, i: (0, 0, 0)),
            pl.BlockSpec((CHUNK, SGU_GROUPS), const2),
            pl.BlockSpec((1, HEAD_DIM), const2),
            pl.BlockSpec((1, HEAD_DIM), const2),
        ],
        out_specs=(
            pl.BlockSpec((TM_IN, SGU_WIDTH), lambda s, i: (active(0)(s, i), 0)),
            pl.BlockSpec((TM_IN, FOX_WIDTH), lambda s, i: (active(1)(s, i), 0)),
            pl.BlockSpec((FOX_WIDTH, TM_IN), lambda s, i: (0, active(2)(s, i))),
            pl.BlockSpec((TM_IN, FOX_WIDTH), lambda s, i: (active(3)(s, i), 0)),
            pl.BlockSpec((TM_IN, SEC), gate_blk),
        ),
        scratch_shapes=[pltpu.VMEM((2, SEC, D_MODEL), F32),
                        pltpu.VMEM((2, D_MODEL, SEC), BF16),
                        pltpu.SemaphoreType.DMA((2,))],
        compiler_params=pltpu.CompilerParams(
            dimension_semantics=("arbitrary", "arbitrary"), vmem_limit_bytes=VMEM_LIMIT),
        name="inproj",
    )(h, wt, b_gate, ln_g, ln_b, w_sp, b_sp_t, q_g, k_g)


def _split3(x):
    x0 = x.astype(BF16)
    r1 = x - x0.astype(F32)
    x1 = r1.astype(BF16)
    r2 = r1 - x1.astype(F32)
    return x0, x1, r2.astype(BF16)


def _cumsum_kernel(lf_ref, negc_ref, c_s):
    seq = lf_ref.shape[0]
    row = lax.broadcasted_iota(I32, (CHUNK, CHUNK), 0)
    col = lax.broadcasted_iota(I32, (CHUNK, CHUNK), 1)
    tri = jnp.where(row >= col, 1.0, 0.0).astype(BF16)
    carry = jnp.zeros((1, LANES), F32)
    for r in range(seq // CHUNK):
        rs = slice(r * CHUNK, (r + 1) * CHUNK)
        x0, x1, x2 = _split3(lf_ref[rs, :])
        cs = (jnp.dot(tri, x0, preferred_element_type=F32)
              + jnp.dot(tri, x1, preferred_element_type=F32)
              + jnp.dot(tri, x2, preferred_element_type=F32)) + carry
        carry = cs[CHUNK - 1:CHUNK, :]
        c_s[rs, :] = cs
    ct = c_s[...].T
    negc_ref[0] = -ct[0:FOX_HEADS, :]


def _forget_cumsum(lf, batch, seq):
    return pl.pallas_call(
        _cumsum_kernel,
        out_shape=jax.ShapeDtypeStruct((batch, FOX_HEADS, seq), F32),
        grid=(batch,),
        in_specs=[pl.BlockSpec((seq, LANES), lambda b: (b, 0))],
        out_specs=pl.BlockSpec((1, FOX_HEADS, seq), lambda b: (b, 0, 0)),
        scratch_shapes=[pltpu.VMEM((seq, LANES), F32)],
        compiler_params=pltpu.CompilerParams(
            dimension_semantics=("arbitrary",), vmem_limit_bytes=VMEM_LIMIT),
        name="forget_cumsum",
    )(lf)


def _attn_kernel(q_ref, kt_ref, v_ref, negc_ref, o_ref, kx, vx):
    seq = q_ref.shape[0]
    sub = lax.broadcasted_iota(I32, (HEAD_DIM, seq), 0)
    for hh in range(HEADS_PER_STEP):
        hs = slice(hh * HEAD_DIM, (hh + 1) * HEAD_DIM)
        c0, c1, c2 = (c.astype(F32) for c in _split3(negc_ref[0, hh] * LOG2E))
        kx[hh, 0:HEAD_DIM, :] = kt_ref[hs, :]
        kx[hh, HEAD_DIM:, :] = jnp.where(
            sub == 0, c0, jnp.where(sub == 1, c1, jnp.where(sub == 2, c2, 0.0))).astype(BF16)
        vx[hh, :, 0:HEAD_DIM] = v_ref[:, hs]
        vx[hh, :, HEAD_DIM:] = jnp.ones((seq, HEAD_DIM), BF16)
    lane = lax.broadcasted_iota(I32, (TQ, HEAD_DIM), 1)
    bias_cols = jnp.where(lane < N_SPLIT, 1.0, 0.0).astype(BF16)
    row = lax.broadcasted_iota(I32, (TQ, TQ), 0)
    col = lax.broadcasted_iota(I32, (TQ, TQ), 1)
    causal = row >= col
    for qi in range(seq // TQ):
        for hh in range(HEADS_PER_STEP):
            hs = slice(hh * HEAD_DIM, (hh + 1) * HEAD_DIM)
            k0 = qi * TQ
            q = jnp.concatenate([q_ref[k0:k0 + TQ, hs], bias_cols], axis=1)
            s_d = jnp.dot(q, kx[hh, :, k0:k0 + TQ], preferred_element_type=F32)
            s_d = jnp.where(causal, s_d, -jnp.inf)
            m = jnp.max(s_d, axis=-1, keepdims=True)
            if qi > 0:
                s_o = jnp.dot(q, kx[hh, :, 0:k0], preferred_element_type=F32)
                m = jnp.maximum(m, jnp.max(s_o, axis=-1, keepdims=True))
            acc = jnp.dot(jnp.exp2(s_d - m).astype(BF16), vx[hh, k0:k0 + TQ, :],
                          preferred_element_type=F32)
            if qi > 0:
                acc = acc + jnp.dot(jnp.exp2(s_o - m).astype(BF16), vx[hh, 0:k0, :],
                                    preferred_element_type=F32)
            inv_l = 1.0 / acc[:, HEAD_DIM:HEAD_DIM + 1]
            o_ref[k0:k0 + TQ, hs] = (acc[:, 0:HEAD_DIM] * inv_l).astype(BF16)


def _attention(q, kt, va, negc, batch, seq):
    t = q.shape[0]
    width = HEADS_PER_STEP * HEAD_DIM
    blk = pl.BlockSpec((seq, width), lambda b, g: (b, g))
    return pl.pallas_call(
        _attn_kernel,
        out_shape=jax.ShapeDtypeStruct((t, FOX_WIDTH), BF16),
        grid=(batch, FOX_HEADS // HEADS_PER_STEP),
        in_specs=[blk,
                  pl.BlockSpec((width, seq), lambda b, g: (g, b)),
                  blk,
                  pl.BlockSpec((1, HEADS_PER_STEP, 1, seq), lambda b, g: (b, g, 0, 0))],
        out_specs=blk,
        scratch_shapes=[pltpu.VMEM((HEADS_PER_STEP, 2 * HEAD_DIM, seq), BF16),
                        pltpu.VMEM((HEADS_PER_STEP, seq, 2 * HEAD_DIM), BF16)],
        compiler_params=pltpu.CompilerParams(
            dimension_semantics=("arbitrary", "arbitrary"), vmem_limit_bytes=VMEM_LIMIT),
        name="fox_attention",
    )(q, kt, va, negc)


MIX_WROWS = 512


def _mix_kernel(as_ref, af_ref, g_ref, x_ref, wps_hbm, wpf_hbm, wo_hbm, g2_ref, wr_ref, br_ref,
                x1_ref, lg_ref, wps_ref, wpf_ref, wo_ref, stage, sem):
    @pl.when(pl.program_id(0) == 0)
    def _():
        pieces = [(src, dst, r0) for src, dst in ((wps_hbm, wps_ref), (wpf_hbm, wpf_ref),
                                                  (wo_hbm, wo_ref))
                  for r0 in range(0, dst.shape[0], MIX_WROWS)]

        def piece_copy(k):
            src, _, r0 = pieces[k]
            return pltpu.make_async_copy(src.at[pl.ds(r0, MIX_WROWS), :], stage.at[k % 2],
                                         sem.at[k % 2])

        piece_copy(0).start()
        for k, (_, dst, r0) in enumerate(pieces):
            if k + 1 < len(pieces):
                piece_copy(k + 1).start()
            piece_copy(k).wait()
            dst[r0:r0 + MIX_WROWS, :] = stage[k % 2].astype(BF16)

    ys = jnp.dot(as_ref[...], wps_ref[...], preferred_element_type=F32)
    yf = jnp.dot(af_ref[...], wpf_ref[...], preferred_element_type=F32)
    m = (g_ref[:, :D_MODEL].astype(F32) * ys + g_ref[:, D_MODEL:].astype(F32) * yf).astype(BF16)
    x1 = x_ref[...] + jnp.dot(m, wo_ref[...], preferred_element_type=F32)
    x1_ref[...] = x1
    ms = jnp.mean(x1 * x1, axis=-1, keepdims=True)
    h2 = ((x1 * lax.rsqrt(ms + EPS)) * g2_ref[...]).astype(BF16)
    lg_ref[...] = jnp.dot(h2, wr_ref[...], preferred_element_type=F32) + br_ref[...]


def _mix(a_sgu, a_fox, gates, x2, wps, wpf, wo, g2, w_r, b_r):
    t = x2.shape[0]
    row = lambda i: (i, 0)
    const = lambda i: (0, 0)
    resident = functools.partial(pl.BlockSpec, index_map=const, pipeline_mode=pl.Buffered(1))
    return pl.pallas_call(
        _mix_kernel,
        out_shape=(jax.ShapeDtypeStruct((t, D_MODEL), F32),
                   jax.ShapeDtypeStruct((t, LANES), F32)),
        grid=(t // TM_MIX,),
        in_specs=[
            pl.BlockSpec((TM_MIX, SGU_WIDTH), row),
            pl.BlockSpec((TM_MIX, FOX_WIDTH), row),
            pl.BlockSpec((TM_MIX, 2 * D_MODEL), row),
            pl.BlockSpec((TM_MIX, D_MODEL), row),
            pl.BlockSpec(memory_space=pl.ANY),
            pl.BlockSpec(memory_space=pl.ANY),
            pl.BlockSpec(memory_space=pl.ANY),
            pl.BlockSpec((1, D_MODEL), const),
            resident((D_MODEL, LANES)),
            pl.BlockSpec((1, LANES), const),
        ],
        out_specs=(pl.BlockSpec((TM_MIX, D_MODEL), row), pl.BlockSpec((TM_MIX, LANES), row)),
        scratch_shapes=[pltpu.VMEM((SGU_WIDTH, D_MODEL), BF16),
                        pltpu.VMEM((FOX_WIDTH, D_MODEL), BF16),
                        pltpu.VMEM((D_MODEL, D_MODEL), BF16),
                        pltpu.VMEM((2, MIX_WROWS, D_MODEL), F32),
                        pltpu.SemaphoreType.DMA((2,))],
        compiler_params=pltpu.CompilerParams(
            dimension_semantics=("arbitrary",), vmem_limit_bytes=VMEM_LIMIT),
        name="mix",
    )(a_sgu, a_fox, gates, x2, wps, wpf, wo, g2, w_r, b_r)


def _route_kernel(lg_ref, dt_ref, mf_ref, bm_ref, em_ref, bt_ref, mi_ref, inv_ref):
    t = lg_ref.shape[0]
    n_chunks = t // RCH
    lane_i = lax.broadcasted_iota(I32, (RCH, LANES), 1)
    lane = lane_i.astype(F32)
    lane_grp = ((lane_i - N_GROUPS) >> EXPERT_GROUP_SHIFT).astype(F32)
    is_grp = lane_i < N_GROUPS
    is_exp = (lane_i >= N_GROUPS) & (lane_i < N_GROUPS + N_EXPERTS)
    r_i = lax.broadcasted_iota(I32, (RCH, RCH), 0)
    c_i = lax.broadcasted_iota(I32, (RCH, RCH), 1)
    strict_lower = jnp.where(r_i > c_i, 1.0, 0.0).astype(BF16)
    neg_inf = -jnp.inf

    def first_max(vals):
        vmax = jnp.max(vals, axis=-1, keepdims=True)
        idx = jnp.min(jnp.where(vals == vmax, lane, float(LANES)), axis=-1, keepdims=True)
        return vmax, idx

    def pick(table, idx):
        return jnp.sum(jnp.where(lane == idx, table, 0.0), axis=-1, keepdims=True)

    def pack(cols):
        out = jnp.zeros((RCH, LANES), F32)
        for n, c in enumerate(cols):
            out = jnp.where(lane_i == n, c, out)
        return out

    def pass1(ci, counts):
        r0 = pl.multiple_of(ci * RCH, RCH)
        lg = lg_ref[pl.ds(r0, RCH), :]
        gmax, grp = first_max(jnp.where(is_grp, lg, neg_inf))
        p_grp = 1.0 / jnp.sum(jnp.where(is_grp, jnp.exp(lg - gmax), 0.0), axis=-1, keepdims=True)
        el = jnp.where(is_exp & (lane_grp == grp), lg, neg_inf)
        v1, i1 = first_max(el)
        v2, i2 = first_max(jnp.where(lane == i1, neg_inf, el))
        e21 = jnp.exp(v2 - v1)
        w1 = p_grp / (1.0 + e21)
        w2 = p_grp * e21 / (1.0 + e21)
        e1 = i1 - float(N_GROUPS)
        e2 = i2 - float(N_GROUPS)
        hot = jnp.where((lane == e1) | (lane == e2), 1.0, 0.0)
        before = jnp.dot(strict_lower, hot.astype(BF16), preferred_element_type=F32) + counts
        mi_ref[pl.ds(r0, RCH), :] = pack([e1, e2, pick(before, e1), pick(before, e2)]).astype(I32)
        mf_ref[pl.ds(r0, RCH), :] = pack([w1, w2])
        return counts + jnp.sum(hot, axis=0, keepdims=True)

    counts = lax.fori_loop(0, n_chunks, pass1, jnp.zeros((1, LANES), F32))

    nblk = jnp.floor((counts + float(EXPERT_BLOCK - 1)) * (1.0 / EXPERT_BLOCK))
    u_r = lax.broadcasted_iota(I32, (LANES, LANES), 0)
    u_c = lax.broadcasted_iota(I32, (LANES, LANES), 1)
    strict_upper = jnp.where(u_r < u_c, 1.0, 0.0).astype(BF16)
    bstart = jnp.dot(jnp.broadcast_to(nblk, (SUBLANES, LANES)).astype(BF16), strict_upper,
                     preferred_element_type=F32)[0:1, :]
    bend = bstart + nblk

    def pass2(ci, _):
        r0 = pl.multiple_of(ci * RCH, RCH)
        mi = mi_ref[pl.ds(r0, RCH), :].astype(F32)
        cols = [jnp.sum(jnp.where(lane_i == n, mi, 0.0), axis=-1, keepdims=True) for n in range(4)]
        d1 = pick(bstart, cols[0]) * float(EXPERT_BLOCK) + cols[2]
        d2 = pick(bstart, cols[1]) * float(EXPERT_BLOCK) + cols[3]
        d_t = pack([d1, d2]).T
        dt_ref[ci] = d_t[0:SUBLANES, :].astype(I32)
        tok = (lax.broadcasted_iota(I32, (RCH, 1), 0) + r0).astype(F32)
        tok_hi = jnp.floor(tok * (1.0 / LANES))
        tok_lo = tok - tok_hi * float(LANES)
        for n, d in enumerate((d1, d2)):
            d_hi = jnp.floor(d * (1.0 / LANES))
            hit = lane == (d - d_hi * float(LANES))
            blk_row = jnp.floor(d_t[n:n + 1, :] * (1.0 / LANES))
            sel = jnp.where(inv_blk == blk_row, 1.0, 0.0).astype(BF16)
            inv_ref[0] += jnp.dot(sel, jnp.where(hit, tok_hi, 0.0).astype(BF16),
                                  preferred_element_type=F32)
            inv_ref[1] += jnp.dot(sel, jnp.where(hit, tok_lo, 0.0).astype(BF16),
                                  preferred_element_type=F32)
        return 0

    nb_rows = bm_ref.shape[0]
    inv_blk = lax.broadcasted_iota(I32, (nb_rows, RCH), 0).astype(F32)
    inv_ref[...] = jnp.zeros_like(inv_ref)
    lax.fori_loop(0, n_chunks, pass2, 0)
    bt_ref[...] = (inv_ref[0] * float(LANES) + inv_ref[1]).astype(I32)

    b_col = lax.broadcasted_iota(I32, (nb_rows, LANES), 0).astype(F32)
    b_lane = lax.broadcasted_iota(I32, (nb_rows, LANES), 1)
    done = jnp.where((bend <= b_col) & (b_lane < N_EXPERTS), 1.0, 0.0)
    blk_e = jnp.minimum(jnp.sum(done, axis=-1, keepdims=True), float(N_EXPERTS - 1))
    n_used = jnp.sum(jnp.where(b_lane[0:1, :] == N_EXPERTS - 1, bend, 0.0), axis=-1, keepdims=True)
    bm_ref[...] = jnp.where(b_lane == 0, blk_e, jnp.where(b_lane == 1, n_used, 0.0)).astype(I32)

    em_ref[...] = jnp.broadcast_to(nblk, (SUBLANES, LANES)).astype(I32)


def _route(logits, nb_rows):
    t = logits.shape[0]
    full = lambda shape: pl.BlockSpec(shape, lambda: (0,) * len(shape))
    return pl.pallas_call(
        _route_kernel,
        out_shape=(jax.ShapeDtypeStruct((t // RCH, SUBLANES, RCH), I32),
                   jax.ShapeDtypeStruct((t, LANES), F32),
                   jax.ShapeDtypeStruct((nb_rows, LANES), I32),
                   jax.ShapeDtypeStruct((SUBLANES, LANES), I32),
                   jax.ShapeDtypeStruct((nb_rows, LANES), I32)),
        in_specs=[full((t, LANES))],
        out_specs=(full((t // RCH, SUBLANES, RCH)), full((t, LANES)), full((nb_rows, LANES)),
                   full((SUBLANES, LANES)), full((nb_rows, LANES))),
        scratch_shapes=[pltpu.VMEM((t, LANES), I32), pltpu.VMEM((2, nb_rows, LANES), F32)],
        compiler_params=pltpu.CompilerParams(vmem_limit_bytes=VMEM_LIMIT),
        name="route",
    )(logits)


GATHER_GROUPS = 4
GATHER_AHEAD = 6
WEIGHT_SLOTS = 3


def _experts_kernel(blk_e_ref, nused_ref, nblk_ref, tok_ref, x_hbm, g2_ref, wg_hbm, wu_hbm, wd_hbm,
                    y_ref, xg, wg_f, wu_f, wd_f, wg_s, wu_s, wd_s, slot_ref, sem, gsem):
    b = pl.program_id(0)
    n_used = nused_ref[0]
    used = b < n_used
    e = blk_e_ref[b]
    new_expert = (b == 0) | (e != blk_e_ref[jnp.maximum(b - 1, 0)])
    n_slots = GATHER_AHEAD + 1
    xslot = b % n_slots

    def gather_rows(block, slot, lo, hi):
        row0 = jnp.minimum(block, n_used - 1) * EXPERT_BLOCK
        for r in range(lo, hi):
            pltpu.make_async_copy(x_hbm.at[pl.ds(tok_ref[row0 + r], 1), :],
                                  xg.at[slot, pl.ds(r, 1), :], gsem.at[slot]).start()

    def gather_wait(slot):
        pltpu.make_async_copy(x_hbm.at[pl.ds(0, EXPERT_BLOCK), :], xg.at[slot], gsem.at[slot]).wait()

    @pl.when(b == 0)
    def _():
        for k in range(GATHER_AHEAD):
            gather_rows(k, k, 0, EXPERT_BLOCK)

    def weight_copies(expert, slot):
        return (pltpu.make_async_copy(wg_hbm.at[expert], wg_f.at[slot], sem.at[slot]),
                pltpu.make_async_copy(wu_hbm.at[expert], wu_f.at[slot], sem.at[slot]),
                pltpu.make_async_copy(wd_hbm.at[expert], wd_f.at[slot], sem.at[slot]))

    def block_expert(blk):
        return blk_e_ref[jnp.minimum(blk, n_used - 1)]

    b_next = b + nblk_ref[e]
    b_next2 = b_next + nblk_ref[block_expert(b_next)]

    @pl.when(b == 0)
    def _():
        slot_ref[0] = 0
        for cp in weight_copies(e, 0):
            cp.start()

        @pl.when(b_next < n_used)
        def _():
            for cp in weight_copies(block_expert(b_next), 1):
                cp.start(priority=PREFETCH_PRIORITY)

    @pl.when(used & new_expert)
    def _():
        slot = slot_ref[0]

        @pl.when(b_next2 < n_used)
        def _():
            for cp in weight_copies(block_expert(b_next2), (slot + 2) % WEIGHT_SLOTS):
                cp.start(priority=PREFETCH_PRIORITY)

        for cp in weight_copies(e, slot):
            cp.wait()
        wg_s[...] = wg_f[slot].astype(BF16)
        wu_s[...] = wu_f[slot].astype(BF16)
        wd_s[...] = wd_f[slot].astype(BF16)
        slot_ref[0] = (slot + 1) % WEIGHT_SLOTS

    @pl.when(used)
    def _():
        gather_wait(xslot)
        ahead = b + GATHER_AHEAD
        aslot = ahead % n_slots
        per_group = EXPERT_BLOCK // GATHER_GROUPS
        x = xg[xslot]
        ms = jnp.mean(x * x, axis=-1, keepdims=True)
        h = ((x * lax.rsqrt(ms + EPS)) * g2_ref[...]).astype(BF16)
        a = jnp.dot(h, wg_s[...], preferred_element_type=F32)
        gather_rows(ahead, aslot, 0, per_group)
        u = jnp.dot(h, wu_s[...], preferred_element_type=F32)
        gather_rows(ahead, aslot, per_group, 2 * per_group)
        mid = ((a * _sigmoid(a)) * u).astype(BF16)
        gather_rows(ahead, aslot, 2 * per_group, 3 * per_group)
        y = jnp.dot(mid, wd_s[...], preferred_element_type=F32)
        gather_rows(ahead, aslot, 3 * per_group, EXPERT_BLOCK)
        y_ref[...] = y

    @pl.when(b == n_used - 1)
    def _():
        for k in range(1, GATHER_AHEAD + 1):
            gather_wait((b + k) % n_slots)

    @pl.when(jnp.logical_not(used))
    def _():
        y_ref[...] = jnp.zeros_like(y_ref)


def _experts(blk_e, n_used, nblk, row_tok, x1, g2, w_g, w_u, w_d):
    n_rows = row_tok.shape[0]
    nb = n_rows // EXPERT_BLOCK
    hbm = pl.BlockSpec(memory_space=pl.ANY)
    grid_spec = pltpu.PrefetchScalarGridSpec(
        num_scalar_prefetch=4,
        grid=(nb,),
        in_specs=[
            hbm,
            pl.BlockSpec((1, D_MODEL), lambda b, *_: (0, 0)),
            hbm, hbm, hbm,
        ],
        out_specs=pl.BlockSpec((EXPERT_BLOCK, D_MODEL), lambda b, *_: (b, 0)),
        scratch_shapes=[pltpu.VMEM((GATHER_AHEAD + 1, EXPERT_BLOCK, D_MODEL), F32),
                        pltpu.VMEM((WEIGHT_SLOTS, D_MODEL, D_EXPERT), F32),
                        pltpu.VMEM((WEIGHT_SLOTS, D_MODEL, D_EXPERT), F32),
                        pltpu.VMEM((WEIGHT_SLOTS, D_EXPERT, D_MODEL), F32),
                        pltpu.VMEM((D_MODEL, D_EXPERT), BF16),
                        pltpu.VMEM((D_MODEL, D_EXPERT), BF16),
                        pltpu.VMEM((D_EXPERT, D_MODEL), BF16),
                        pltpu.SMEM((1,), I32),
                        pltpu.SemaphoreType.DMA((WEIGHT_SLOTS,)),
                        pltpu.SemaphoreType.DMA((GATHER_AHEAD + 1,))],
    )
    return pl.pallas_call(
        _experts_kernel,
        out_shape=jax.ShapeDtypeStruct((n_rows, D_MODEL), F32),
        grid_spec=grid_spec,
        compiler_params=pltpu.CompilerParams(
            dimension_semantics=("arbitrary",), vmem_limit_bytes=VMEM_LIMIT),
        name="experts",
    )(blk_e, n_used, nblk, row_tok, x1, g2, w_g, w_u, w_d)


def _combine_kernel(d1_ref, d2_ref, x1_ref, w_ref, y_ref, o_ref, gbuf, sem):
    i = pl.program_id(0)
    n = pl.num_programs(0)
    tm = x1_ref.shape[0]

    def request_row(base, slot, r):
        pltpu.make_async_copy(y_ref.at[pl.ds(d1_ref[base + r], 1), :],
                              gbuf.at[slot, pl.ds(r, 1), :], sem.at[slot]).start()
        pltpu.make_async_copy(y_ref.at[pl.ds(d2_ref[base + r], 1), :],
                              gbuf.at[slot, pl.ds(tm + r, 1), :], sem.at[slot]).start()

    def issue(tile, slot, lo, hi):
        base = jnp.minimum(tile, n - 1) * tm
        for r in range(lo, hi):
            request_row(base, slot, r)

    def tile_wait(slot):
        pltpu.make_async_copy(y_ref.at[pl.ds(0, 2 * tm), :], gbuf.at[slot], sem.at[slot]).wait()

    slot = i % 2

    @pl.when(i == 0)
    def _():
        def first_tile(r, _):
            request_row(0, 0, r)
            return 0

        lax.fori_loop(0, tm, first_tile, 0, unroll=SUBLANES)

    tile_wait(slot)
    for c in range(tm // COMBINE_ROWS):
        rs = slice(c * COMBINE_ROWS, (c + 1) * COMBINE_ROWS)
        rs2 = slice(tm + c * COMBINE_ROWS, tm + (c + 1) * COMBINE_ROWS)
        w = w_ref[rs, :]
        o_ref[rs, :] = x1_ref[rs, :] + (w[:, 0:1] * gbuf[slot, rs, :] + w[:, 1:2] * gbuf[slot, rs2, :])
        issue(i + 1, 1 - slot, c * COMBINE_ROWS, (c + 1) * COMBINE_ROWS)

    @pl.when(i == n - 1)
    def _():
        tile_wait(1 - slot)


def _combine(dest1, dest2, x1, w, yb):
    t = x1.shape[0]
    grid_spec = pltpu.PrefetchScalarGridSpec(
        num_scalar_prefetch=2,
        grid=(t // TM_CMB,),
        in_specs=[
            pl.BlockSpec((TM_CMB, D_MODEL), lambda i, a, b: (i, 0)),
            pl.BlockSpec((TM_CMB, LANES), lambda i, a, b: (i, 0)),
            pl.BlockSpec(memory_space=pl.ANY),
        ],
        out_specs=pl.BlockSpec((TM_CMB, D_MODEL), lambda i, a, b: (i, 0)),
        scratch_shapes=[pltpu.VMEM((2, 2 * TM_CMB, D_MODEL), F32),
                        pltpu.SemaphoreType.DMA((2,))],
    )
    return pl.pallas_call(
        _combine_kernel,
        out_shape=jax.ShapeDtypeStruct((t, D_MODEL), F32),
        grid_spec=grid_spec,
        compiler_params=pltpu.CompilerParams(
            dimension_semantics=("arbitrary",), vmem_limit_bytes=VMEM_LIMIT),
        name="combine",
    )(dest1, dest2, x1, w, yb)


def kernel(x, norm1_g, w_in, b_gate, b_forget, sgu_ln_g, sgu_ln_b, w_spatial, b_spatial, q_norm_g, k_norm_g, w_proj_sgu, w_proj_fox, w_out, norm2_g, w_router_group, b_router_group, w_router_expert, b_router_expert, w_expert_gate, w_expert_up, w_expert_down):
    batch, seq, d = x.shape
    t = batch * seq
    l = 0
    x2 = x.reshape(t, d)

    wt = jnp.swapaxes(w_in[l], 0, 1)
    off_f = N_MAIN_SEC * SEC
    w_f = jnp.pad(wt[off_f:off_f + FOX_HEADS].T, ((0, 0), (0, LANES - FOX_HEADS))).astype(BF16)
    b_f = jnp.pad(b_forget[l], (0, LANES - FOX_HEADS)).reshape(1, LANES)
    n_r = N_GROUPS + N_EXPERTS
    w_r = jnp.pad(jnp.concatenate([w_router_group[l], w_router_expert[l]], axis=1),
                  ((0, 0), (0, LANES - n_r))).astype(BF16)
    b_r = jnp.pad(jnp.concatenate([b_router_group[l], b_router_expert[l]]),
                  (0, LANES - n_r)).reshape(1, LANES)

    h, lf = _norm1(x2, norm1_g[l].reshape(1, d), w_f, b_f)
    a_sgu, q, kt, va, gates = _inproj(
        h, wt, b_gate[l].reshape(1, 2 * d),
        sgu_ln_g[l].reshape(1, SGU_WIDTH), sgu_ln_b[l].reshape(1, SGU_WIDTH),
        w_spatial[l], b_spatial[l].T, q_norm_g[l].reshape(1, HEAD_DIM),
        k_norm_g[l].reshape(1, HEAD_DIM))
    negc = _forget_cumsum(lf, batch, seq).reshape(batch, FOX_HEADS, 1, seq)
    a_fox = _attention(q, kt, va, negc, batch, seq)
    x1, logits = _mix(a_sgu, a_fox, gates, x2, w_proj_sgu[l], w_proj_fox[l], w_out[l],
                      norm2_g[l].reshape(1, d), w_r, b_r)

    n_assign = 2 * t
    n_rows = n_assign + N_EXPERTS * EXPERT_BLOCK
    nb = n_rows // EXPERT_BLOCK
    dest_t, meta_f, bmeta, nblk, row_tok = _route(logits, n_rows // LANES)
    dest1, dest2 = dest_t[:, 0, :].reshape(t), dest_t[:, 1, :].reshape(t)
    n_used = bmeta[0:1, 1]
    yb = _experts(bmeta[:nb, 0], n_used, nblk[0, :N_EXPERTS], row_tok.reshape(n_rows), x1,
                  norm2_g[l].reshape(1, d), w_expert_gate[l], w_expert_up[l], w_expert_down[l])
    out = _combine(dest1, dest2, x1, meta_f, yb)
    return out.reshape(batch, seq, d)
```
